```python
import jax, jax.numpy as jnp
from jax import lax
import numpy as np

D_MODEL = 1024
BATCH = 8
SEQ = 2048
DEPTH = 1

GRID_W = 64
CTX_LEN = 256
M_HEADS = 4
M_HEAD_DIM = D_MODEL // M_HEADS
M_WIDTH = M_HEADS * M_HEAD_DIM
P_GROUPS = 4
P_WIDTH = D_MODEL // 2
P_GROUP_DIM = P_WIDTH // P_GROUPS
POOL_WINDOWS = (2, 4, 8, 16)
CHUNK = 128
N_DIR = 2
N_GATE_COLS = N_DIR * 2 * M_HEADS
N_BRANCH = 2
IN_SIZES = (M_WIDTH, M_WIDTH, M_WIDTH, M_WIDTH, M_WIDTH, N_GATE_COLS, P_WIDTH, P_WIDTH, N_BRANCH * D_MODEL)
IN_WIDTH = sum(IN_SIZES)
EPS = 1e-6
M_INIT = -1e30

kernel_name = 'hybrid_mlstm_pool_flow_block'


def _split_proj(p):
    names = ('q', 'k', 'v', 'o', 'z_m', 'gates', 'p', 'z_p', 'g_branch')
    offs = [int(o) for o in np.cumsum(IN_SIZES)[:-1]]
    return dict(zip(names, jnp.split(p, offs, axis=-1)))


def _rmsnorm(x, w):
    xf = x.astype(jnp.float32)
    y = xf * lax.rsqrt(jnp.mean(xf * xf, axis=-1, keepdims=True) + EPS)
    return (y * w.astype(jnp.float32)).astype(x.dtype)


def _adaln(cvec, w, b):
    m = jax.nn.silu(cvec) @ w + b
    return jnp.split(m, 3, axis=-1)


def _zero_state(b):
    return (jnp.zeros((b, M_HEADS, M_HEAD_DIM, M_HEAD_DIM), jnp.float32),
            jnp.zeros((b, M_HEADS, M_HEAD_DIM), jnp.float32),
            jnp.full((b, M_HEADS), M_INIT, jnp.float32))


def _mlstm_inputs(parts, gate_b):
    b, t, _ = parts['q'].shape
    def heads(a):
        return a.reshape(b, t, M_HEADS, M_HEAD_DIM).transpose(0, 2, 1, 3).astype(jnp.float32)
    q = heads(parts['q'])
    k = heads(parts['k']) * (M_HEAD_DIM ** -0.5)
    v = heads(parts['v'])
    pre = parts['gates'].astype(jnp.float32) + gate_b.astype(jnp.float32)
    pre = pre.reshape(b, t, N_DIR, 2, M_HEADS).transpose(2, 3, 0, 4, 1)
    log_i = pre[:, 0]
    log_f = jax.nn.log_sigmoid(pre[:, 1])
    return q, k, v, log_i, log_f


def _mlstm_chunk(carry, xs):
    C, n, m = carry
    q, k, v, b, li = xs
    L = q.shape[2]
    causal = jnp.tril(jnp.ones((L, L), dtype=bool))
    logw = jnp.where(causal, b[..., :, None] - b[..., None, :] + li[..., None, :], -jnp.inf)
    inter = b + m[..., None]
    m_t = jnp.maximum(inter, jnp.max(logw, axis=-1))
    w_intra = jnp.exp(logw - m_t[..., None])
    w_inter = jnp.exp(inter - m_t)
    s = jnp.einsum('bhtk,bhsk->bhts', q, k) * w_intra
    num = w_inter[..., None] * jnp.einsum('bhvk,bhtk->bhtv', C, q) + jnp.einsum('bhts,bhsv->bhtv', s, v)
    den = w_inter * jnp.einsum('bhk,bhtk->bht', n, q) + jnp.sum(s, axis=-1)
    h = num / jnp.maximum(jnp.abs(den), jnp.exp(-m_t))[..., None]
    b_end = b[..., -1]
    w_end = b_end[..., None] - b + li
    m_new = jnp.maximum(b_end + m, jnp.max(w_end, axis=-1))
    decay = jnp.exp(b_end + m - m_new)
    e = jnp.exp(w_end - m_new[..., None])
    C_new = decay[..., None, None] * C + jnp.einsum('bhs,bhsv,bhsk->bhvk', e, v, k)
    n_new = decay[..., None] * n + jnp.einsum('bhs,bhsk->bhk', e, k)
    return (C_new, n_new, m_new), h


def _mlstm_scan(q, k, v, log_i, log_f, state):
    b, h, t, d = q.shape
    nc = t // CHUNK
    def chunks(a):
        return jnp.moveaxis(a.reshape(a.shape[:2] + (nc, CHUNK) + a.shape[3:]), 2, 0)
    bcum = jnp.cumsum(chunks(log_f), axis=-1)
    state, out = lax.scan(_mlstm_chunk, state, (chunks(q), chunks(k), chunks(v), bcum, chunks(log_i)))
    return jnp.moveaxis(out, 0, 2).reshape(b, h, t, d), state


def _bi_mlstm(q, k, v, log_i, log_f, st_f, st_b):
    rev = lambda a: jnp.flip(a, axis=2)
    h_f, st_f = _mlstm_scan(q, k, v, log_i[0], log_f[0], st_f)
    h_b, st_b = _mlstm_scan(rev(q), rev(k), rev(v), rev(log_i[1]), rev(log_f[1]), st_b)
    return h_f + rev(h_b), st_f, st_b


def _mlstm_branch(h, parts, head_norm_w, branch_m_w):
    b, nh, t, d = h.shape
    hn = h * lax.rsqrt(jnp.mean(h * h, axis=-1, keepdims=True) + EPS)
    hn = hn.transpose(0, 2, 1, 3).reshape(b, t, M_WIDTH).astype(parts['o'].dtype) * head_norm_w
    y = hn * jax.nn.sigmoid(parts['o']) * jax.nn.silu(parts['z_m'])
    return y @ branch_m_w


def _box_mean(x, w, axis):
    L = x.shape[axis]
    cs = jnp.cumsum(x.astype(jnp.float32), axis=axis)
    cs = jnp.concatenate([jnp.zeros_like(lax.slice_in_dim(cs, 0, 1, axis=axis)), cs], axis=axis)
    pos = np.arange(L)
    lo = np.clip(pos - w // 2, 0, L)
    hi = np.clip(pos + w - w // 2, 0, L)
    s = jnp.take(cs, hi, axis=axis) - jnp.take(cs, lo, axis=axis)
    shape = [1] * x.ndim
    shape[axis] = L
    cnt = (hi - lo).astype(np.float32).reshape(shape)
    return (s / cnt).astype(x.dtype)


def _pool_branch(parts, grid_shape, axes, pool_w, pool_scale, branch_p_w):
    p = parts['p']
    b, t, _ = p.shape
    pg = p.reshape((b,) + grid_shape + (P_GROUPS, P_GROUP_DIM))
    groups = []
    for g, w in enumerate(POOL_WINDOWS):
        a = pg[..., g, :]
        for ax in axes:
            a = _box_mean(a, w, ax)
        groups.append(a)
    mixed = (jnp.stack(groups, axis=-2) - pg).reshape(b, t, P_GROUPS, P_GROUP_DIM)
    mixed = jnp.einsum('btgc,gcd->btgd', mixed, pool_w).reshape(b, t, P_WIDTH)
    y = mixed * pool_scale * jax.nn.silu(parts['z_p'])
    return y @ branch_p_w


def _merge(parts, y_m, y_p, out_w):
    g_m, g_p = jnp.split(parts['g_branch'], 2, axis=-1)
    return (jax.nn.sigmoid(g_m) * y_m + jax.nn.sigmoid(g_p) * y_p) @ out_w


def _hybrid_layer(x, xc, mod_x, mod_c, norm_w, in_w, gate_b, head_norm_w, pool_w, pool_scale,
                  branch_m_w, branch_p_w, out_w, update_ctx):
    b, t, _ = x.shape
    rows = t // GRID_W
    shift, scale, gate = [m[:, None, :] for m in mod_x]
    shift_c, scale_c, gate_c = mod_c
    parts_x = _split_proj((_rmsnorm(x, norm_w) * (1 + scale) + shift) @ in_w)
    parts_c = _split_proj((_rmsnorm(xc, norm_w) * (1 + scale_c) + shift_c) @ in_w)
    zero = _zero_state(b)
    qc, kc, vc, lic, lfc = _mlstm_inputs(parts_c, gate_b)
    h_c, st_f, st_b = _bi_mlstm(qc, kc, vc, lic, lfc, zero, zero)
    qx, kx, vx, lix, lfx = _mlstm_inputs(parts_x, gate_b)
    h_x, _, _ = _bi_mlstm(qx, kx, vx, lix, lfx, st_f, st_b)
    y_m = _mlstm_branch(h_x, parts_x, head_norm_w, branch_m_w)
    y_p = _pool_branch(parts_x, (rows, GRID_W), (1, 2), pool_w, pool_scale, branch_p_w)
    x = x + gate * _merge(parts_x, y_m, y_p, out_w)
    if update_ctx:
        y_mc = _mlstm_branch(h_c, parts_c, head_norm_w, branch_m_w)
        y_pc = _pool_branch(parts_c, (xc.shape[1],), (1,), pool_w, pool_scale, branch_p_w)
        xc = xc + gate_c * _merge(parts_c, y_mc, y_pc, out_w)
    return x, xc


def setup_inputs(seed: int = 0) -> dict:
    key = jax.random.key(seed)
    ks = jax.random.split(key, 20)
    def nrm(k, shape, s):
        return jax.random.normal(k, shape, jnp.float32) * s
    i_b = nrm(ks[8], (DEPTH, N_DIR, 1, M_HEADS), 0.1)
    f_b = jnp.linspace(3.0, 6.0, M_HEADS, dtype=jnp.float32) + nrm(ks[9], (DEPTH, N_DIR, 1, M_HEADS), 0.1)
    return {
        'x': nrm(ks[0], (BATCH, SEQ, D_MODEL), 1.0),
        'c': nrm(ks[1], (BATCH, D_MODEL), 1.0),
        'ctx': nrm(ks[2], (BATCH, CTX_LEN, D_MODEL), 1.0),
        'c_ctx': nrm(ks[3], (D_MODEL,), 1.0),
        'norm_w': 1.0 + nrm(ks[4], (DEPTH, D_MODEL), 0.02),
        'ada_w': nrm(ks[5], (DEPTH, D_MODEL, 3 * D_MODEL), 0.5 * D_MODEL ** -0.5),
        'ada_b': nrm(ks[6], (DEPTH, 3 * D_MODEL), 0.01),
        'in_w': nrm(ks[7], (DEPTH, D_MODEL, IN_WIDTH), D_MODEL ** -0.5),
        'gate_b': jnp.concatenate([i_b, f_b], axis=2).reshape(DEPTH, N_GATE_COLS),
        'head_norm_w': 1.0 + nrm(ks[10], (DEPTH, M_WIDTH), 0.02),
        'pool_w': nrm(ks[11], (DEPTH, P_GROUPS, P_GROUP_DIM, P_GROUP_DIM), P_GROUP_DIM ** -0.5),
        'pool_scale': 1.0 + nrm(ks[12], (DEPTH, P_WIDTH), 0.1),
        'branch_m_w': nrm(ks[13], (DEPTH, M_WIDTH, D_MODEL), M_WIDTH ** -0.5),
        'branch_p_w': nrm(ks[14], (DEPTH, P_WIDTH, D_MODEL), P_WIDTH ** -0.5),
        'out_w': nrm(ks[15], (DEPTH, D_MODEL, D_MODEL), D_MODEL ** -0.5),
        'final_norm_w': 1.0 + nrm(ks[16], (D_MODEL,), 0.02),
    }


def reference(x, c, ctx, c_ctx, norm_w, ada_w, ada_b, in_w, gate_b, head_norm_w, pool_w, pool_scale,
              branch_m_w, branch_p_w, out_w, final_norm_w):
    xc = ctx
    for l in range(DEPTH):
        mod_x = _adaln(c, ada_w[l], ada_b[l])
        mod_c = _adaln(c_ctx, ada_w[l], ada_b[l])
        x, xc = _hybrid_layer(x, xc, mod_x, mod_c, norm_w[l], in_w[l], gate_b[l], head_norm_w[l],
                              pool_w[l], pool_scale[l], branch_m_w[l], branch_p_w[l], out_w[l],
                              l < DEPTH - 1)
    return _rmsnorm(x, final_norm_w)
```

```python
import functools

import numpy as np
import jax
import jax.numpy as jnp
from jax import lax
from jax.experimental import pallas as pl
from jax.experimental.pallas import tpu as pltpu

F32 = jnp.float32
BF16 = jnp.bfloat16

EPS = 1e-6
M_INIT = -1e30
NEG_BIG = -1e30
M_HEADS = 4
N_DIR = 2
N_GATE_COLS = N_DIR * 2 * M_HEADS
P_GROUPS = 4
POOL_WINDOWS = (2, 4, 8, 16)
GRID_W = 64
LANES = 128
MXU_DIM = 256
CHUNK = 256
MIB = 1024 * 1024


def _cparams(sem, vmem_mib):
    return pltpu.CompilerParams(dimension_semantics=sem, vmem_limit_bytes=vmem_mib * MIB)


def _sigmoid(x):
    return 1.0 / (1.0 + jnp.exp(-x))


def _silu(x):
    return x * _sigmoid(x)


def _log_sigmoid(x):
    return jnp.minimum(x, 0.0) - jnp.log(1.0 + jnp.exp(-jnp.abs(x)))


def _adaln_kernel(c_ref, w_ref, b_ref, o_ref):
    o_ref[...] = jnp.dot(_silu(c_ref[...]), w_ref[...], preferred_element_type=F32) + b_ref[...]


def _adaln(cc, ada_w, ada_b):
    rows, d = cc.shape
    n = ada_w.shape[1]
    tn = d
    return pl.pallas_call(
        _adaln_kernel,
        grid=(n // tn,),
        in_specs=[pl.BlockSpec((rows, d), lambda j: (0, 0)),
                  pl.BlockSpec((d, tn), lambda j: (0, j)),
                  pl.BlockSpec((1, tn), lambda j: (0, j))],
        out_specs=pl.BlockSpec((rows, tn), lambda j: (0, j)),
        out_shape=jax.ShapeDtypeStruct((rows, n), F32),
        compiler_params=_cparams(("arbitrary",), 32),
        name="adaln",
    )(cc, ada_w, ada_b.reshape(1, n))


def _inproj_kernel(x_ref, sc_ref, sh_ref, nw_ref, w_ref, wg_ref, y_ref, g_ref, xn_ref):
    @pl.when(pl.program_id(1) == 0)
    def _():
        x = x_ref[...]
        ms = jnp.mean(x * x, axis=-1, keepdims=True)
        xn = x * lax.rsqrt(ms + EPS) * nw_ref[...]
        xn = xn * (1.0 + sc_ref[0]) + sh_ref[0]
        xn_ref[...] = xn.astype(BF16)
        g_ref[...] = jnp.dot(xn_ref[...], wg_ref[...], preferred_element_type=F32)

    y_ref[...] = jnp.dot(xn_ref[...], w_ref[...], preferred_element_type=F32).astype(BF16)


def _in_proj(x2, scale, shift, norm_w, w, wg, rows_per_mod, name):
    n, d = x2.shape
    nc = w.shape[1]
    tm = min(1024, rows_per_mod)
    tn = 1024
    per = rows_per_mod // tm
    return pl.pallas_call(
        _inproj_kernel,
        grid=(n // tm, nc // tn),
        in_specs=[pl.BlockSpec((tm, d), lambda i, j: (i, 0)),
                  pl.BlockSpec((1, 1, d), lambda i, j: (i // per, 0, 0)),
                  pl.BlockSpec((1, 1, d), lambda i, j: (i // per, 0, 0)),
                  pl.BlockSpec((1, d), lambda i, j: (0, 0)),
                  pl.BlockSpec((d, tn), lambda i, j: (0, j)),
                  pl.BlockSpec((d, LANES), lambda i, j: (0, 0))],
        out_specs=[pl.BlockSpec((tm, tn), lambda i, j: (i, j)),
                   pl.BlockSpec((tm, LANES), lambda i, j: (i, 0))],
        out_shape=[jax.ShapeDtypeStruct((n, nc), BF16),
                   jax.ShapeDtypeStruct((n, LANES), F32)],
        scratch_shapes=[pltpu.VMEM((tm, d), BF16)],
        compiler_params=_cparams(("arbitrary", "arbitrary"), 48),
        name=name,
    )(x2, scale, shift, norm_w, w, wg)


def _gates_kernel(g_ref, b_ref, o_ref, *, chunk):
    pre = g_ref[0] + b_ref[...]
    rows, t = pre.shape
    lf = _log_sigmoid(pre)
    pos = lax.broadcasted_iota(jnp.int32, (rows, t), 1) % chunk
    row = lax.broadcasted_iota(jnp.int32, (rows, t), 0)
    fwd = lf
    bwd = lf
    k = 1
    while k < chunk:
        fwd = fwd + jnp.where(pos >= k, pltpu.roll(fwd, k, 1), 0.0)
        bwd = bwd + jnp.where(pos < chunk - k, pltpu.roll(bwd, t - k, 1), 0.0)
        k *= 2
    is_forget = (row // M_HEADS) % 2 == 1
    is_rev = row >= 2 * M_HEADS
    o_ref[0] = jnp.where(is_forget, jnp.where(is_rev, bwd, fwd), pre)


def _gates(graw_t, gate_b, chunk):
    b, rows, t = graw_t.shape
    return pl.pallas_call(
        functools.partial(_gates_kernel, chunk=chunk),
        grid=(b,),
        in_specs=[pl.BlockSpec((1, rows, t), lambda i: (i, 0, 0)),
                  pl.BlockSpec((rows, 1), lambda i: (0, 0))],
        out_specs=pl.BlockSpec((1, rows, t), lambda i: (i, 0, 0)),
        out_shape=jax.ShapeDtypeStruct((b, rows, t), F32),
        compiler_params=_cparams(("arbitrary",), 32),
        name="gates",
    )(graw_t, gate_b.reshape(rows, 1))


def _gate_layouts(gates_tok, gate_b, chunk):
    b, t, _ = gates_tok.shape
    graw_t = jnp.transpose(gates_tok[:, :, :N_GATE_COLS], (0, 2, 1))
    g = _gates(graw_t, gate_b, chunk).reshape(b, N_DIR, 2, M_HEADS, t)
    grow = jnp.transpose(g, (0, 3, 1, 2, 4)).reshape(b, M_HEADS, 2 * N_DIR, t)
    gcol = jnp.transpose(grow, (0, 1, 3, 2))
    return grow, gcol


def _state_update(k, v, a_col, b_end, ct_ref, n_ref, m_ref, d):
    m = m_ref[d]
    m_new = jnp.maximum(b_end + m, jnp.max(a_col, axis=0, keepdims=True) + b_end)
    decay = jnp.exp(b_end + m - m_new)
    e_col = jnp.exp(b_end + a_col - m_new)
    ke = k.astype(F32) * e_col
    upd = lax.dot_general(ke.astype(BF16), v, (((0,), (0,)), ((), ())), preferred_element_type=F32)
    ct_ref[d] = decay * ct_ref[d] + upd
    n_ref[d] = decay * n_ref[d] + jnp.sum(ke, axis=0, keepdims=True)
    m_ref[d] = m_new


def _chunk_out(q, k, v, li_col, b_col, a_row, b_end, ct_ref, n_ref, m_ref, d):
    L = q.shape[0]
    m = m_ref[d]
    t_idx = lax.broadcasted_iota(jnp.int32, (L, L), 0)
    s_idx = lax.broadcasted_iota(jnp.int32, (L, L), 1)
    mask = (s_idx <= t_idx) if d == 0 else (s_idx >= t_idx)
    logw = jnp.where(mask, b_col + a_row, NEG_BIG)
    inter = b_col + m
    m_t = jnp.maximum(inter, jnp.max(logw, axis=-1, keepdims=True))
    w_intra = jnp.exp(logw - m_t)
    w_inter = jnp.exp(inter - m_t)
    s = lax.dot_general(q, k, (((1,), (1,)), ((), ())), preferred_element_type=F32) * w_intra
    num = (w_inter * jnp.dot(q, ct_ref[d].astype(BF16), preferred_element_type=F32)
           + jnp.dot(s.astype(BF16), v, preferred_element_type=F32))
    den = (w_inter * jnp.sum(q.astype(F32) * n_ref[d], axis=-1, keepdims=True)
           + jnp.sum(s, axis=-1, keepdims=True))
    h = num / jnp.maximum(jnp.abs(den), jnp.exp(-m_t))
    _state_update(k, v, li_col - b_col, b_end, ct_ref, n_ref, m_ref, d)
    return h


def _mlstm_kernel(q_ref, k_ref, v_ref, o_ref, z_ref, grow_ref, gcol_ref,
                  kc_ref, vc_ref, gcolc_ref, hnw_ref, out_ref,
                  ct_ref, n_ref, m_ref, hs_ref, *, chunk):
    t = q_ref.shape[1]
    tc = kc_ref.shape[1]
    nc = t // chunk
    ncc = tc // chunk
    dh = q_ref.shape[2]

    ct_ref[...] = jnp.zeros(ct_ref.shape, F32)
    n_ref[...] = jnp.zeros(n_ref.shape, F32)
    m_ref[...] = jnp.full(m_ref.shape, M_INIT, F32)

    def gate_cols(ref, start, d):
        g = ref[0, 0, pl.ds(start, chunk), :]
        return g[:, 2 * d:2 * d + 1], g[:, 2 * d + 1:2 * d + 2]

    def b_end_of(ref, start, d):
        pos = start + chunk - 1 if d == 0 else start
        return ref[0, 0, pl.ds(pos, 1), :][:, 2 * d + 1:2 * d + 2]

    for d in range(N_DIR):
        for i in range(ncc):
            c = i if d == 0 else ncc - 1 - i
            start = c * chunk
            li_col, b_col = gate_cols(gcolc_ref, start, d)
            _state_update(kc_ref[0, pl.ds(start, chunk), :], vc_ref[0, pl.ds(start, chunk), :],
                          li_col - b_col, b_end_of(gcolc_ref, start, d), ct_ref, n_ref, m_ref, d)

    def run_chunk(c, d):
        start = pl.multiple_of(c * chunk, chunk)
        li_col, b_col = gate_cols(gcol_ref, start, d)
        g = grow_ref[0, 0, c]
        a_row = g[2 * d:2 * d + 1, :] - g[2 * d + 1:2 * d + 2, :]
        h = _chunk_out(q_ref[0, pl.ds(start, chunk), :], k_ref[0, pl.ds(start, chunk), :],
                       v_ref[0, pl.ds(start, chunk), :], li_col, b_col, a_row,
                       b_end_of(gcol_ref, start, d), ct_ref, n_ref, m_ref, d)
        return start, h

    def finalize(start, h):
        hn = h * lax.rsqrt(jnp.mean(h * h, axis=-1, keepdims=True) + EPS) * hnw_ref[...]
        o = o_ref[0, pl.ds(start, chunk), :].astype(F32)
        z = z_ref[0, pl.ds(start, chunk), :].astype(F32)
        out_ref[0, pl.ds(start, chunk), :] = (hn * _sigmoid(o) * _silu(z)).astype(out_ref.dtype)

    def first(i, carry):
        for d in range(N_DIR):
            start, h = run_chunk(i if d == 0 else nc - 1 - i, d)
            hs_ref[pl.ds(start, chunk), :] = h
        return carry

    def second(i, carry):
        for d in range(N_DIR):
            start, h = run_chunk(i if d == 0 else nc - 1 - i, d)
            finalize(start, hs_ref[pl.ds(start, chunk), :] + h)
        return carry

    lax.fori_loop(0, nc // 2, first, 0)
    lax.fori_loop(nc // 2, nc, second, 0)


def _mlstm(y, yc, grow, gcol, gcolc, head_norm_w, chunk):
    b, t, _ = y.shape
    tc = yc.shape[1]
    dh = MXU_DIM
    assert t % (2 * chunk) == 0 and tc % chunk == 0
    grow = grow.reshape(b, M_HEADS, 2 * N_DIR, t // chunk, chunk).transpose(0, 1, 3, 2, 4)
    blk = lambda col0: pl.BlockSpec((1, t, dh), lambda i, h, c=col0: (i, 0, c + h))
    blkc = lambda col0: pl.BlockSpec((1, tc, dh), lambda i, h, c=col0: (i, 0, c + h))
    return pl.pallas_call(
        functools.partial(_mlstm_kernel, chunk=chunk),
        grid=(b, M_HEADS),
        in_specs=[blk(0), blk(M_HEADS), blk(2 * M_HEADS), blk(3 * M_HEADS), blk(4 * M_HEADS),
                  pl.BlockSpec((1, 1, t // chunk, 2 * N_DIR, chunk), lambda i, h: (i, h, 0, 0, 0)),
                  pl.BlockSpec((1, 1, t, 2 * N_DIR), lambda i, h: (i, h, 0, 0)),
                  blkc(0), blkc(M_HEADS),
                  pl.BlockSpec((1, 1, tc, 2 * N_DIR), lambda i, h: (i, h, 0, 0)),
                  pl.BlockSpec((1, dh), lambda i, h: (0, h))],
        out_specs=pl.BlockSpec((1, t, dh), lambda i, h: (i, 0, h)),
        out_shape=jax.ShapeDtypeStruct((b, t, M_HEADS * dh), BF16),
        scratch_shapes=[pltpu.VMEM((N_DIR, dh, dh), F32),
                        pltpu.VMEM((N_DIR, 1, dh), F32),
                        pltpu.VMEM((N_DIR, 1, 1), F32),
                        pltpu.VMEM((t, dh), F32)],
        compiler_params=_cparams(("arbitrary", "arbitrary"), 48),
        name="mlstm",
    )(y, y, y, y, y, grow, gcol, yc, yc, gcolc, head_norm_w)


def _window_bounds(pos, w, length):
    return np.clip(pos - w // 2, 0, length), np.clip(pos + w - w // 2, 0, length)


def _col_pool_matrices(tile):
    pos = np.arange(GRID_W)
    mats = []
    for w in POOL_WINDOWS:
        lo, hi = _window_bounds(pos, w, GRID_W)
        band = ((pos[None, :] >= lo[:, None]) & (pos[None, :] < hi[:, None])).astype(np.float32)
        mats.append(np.kron(np.eye(tile // GRID_W, dtype=np.float32), band))
    return jnp.asarray(np.stack(mats), dtype=BF16)


def _pool_kernel(p_ref, z_ref, mc_ref, pw_ref, ps_ref, out_ref, pad_ref, scale_ref, *, tile, rows):
    t = p_ref.shape[1]
    gd = LANES
    pad = (max(POOL_WINDOWS) // 2) * GRID_W
    n_tiles = t // tile

    @pl.when(pl.program_id(0) == 0)
    def _():
        pad_ref[...] = jnp.zeros(pad_ref.shape, F32)
        tok = lax.broadcasted_iota(jnp.int32, (t, gd), 0)
        r = tok // GRID_W
        c = tok % GRID_W
        for g, w in enumerate(POOL_WINDOWS):
            cnt_r = jnp.minimum(r + (w - w // 2), rows) - jnp.maximum(r - w // 2, 0)
            cnt_c = jnp.minimum(c + (w - w // 2), GRID_W) - jnp.maximum(c - w // 2, 0)
            scale_ref[g] = 1.0 / (cnt_r.astype(F32) * cnt_c.astype(F32))

    for g, w in enumerate(POOL_WINDOWS):
        cols = pl.ds(g * gd, gd)

        def col_sum(i, carry):
            start = pl.multiple_of(i * tile, tile)
            pad_ref[pl.ds(pad + start, tile), :] = jnp.dot(
                mc_ref[g], p_ref[0, pl.ds(start, tile), cols], preferred_element_type=F32)
            return carry

        lax.fori_loop(0, n_tiles, col_sum, 0)

        def row_sum(i, carry):
            start = pl.multiple_of(i * tile, tile)
            acc = pad_ref[pl.ds(pad + start - (w // 2) * GRID_W, tile), :]
            for j in range(1 - w // 2, w - w // 2):
                acc = acc + pad_ref[pl.ds(pad + start + j * GRID_W, tile), :]
            mixed = acc * scale_ref[g, pl.ds(start, tile), :] - p_ref[0, pl.ds(start, tile), cols].astype(F32)
            mm = jnp.dot(mixed.astype(BF16), pw_ref[g], preferred_element_type=F32)
            zz = z_ref[0, pl.ds(start, tile), cols].astype(F32)
            out_ref[0, pl.ds(start, tile), cols] = (mm * ps_ref[:, cols] * _silu(zz)).astype(out_ref.dtype)
            return carry

        lax.fori_loop(0, n_tiles, row_sum, 0)


def _pool(y, pool_w, pool_scale, p_col0):
    b, t, _ = y.shape
    pw = P_GROUPS * LANES
    tile = MXU_DIM
    pad = (max(POOL_WINDOWS) // 2) * GRID_W
    return pl.pallas_call(
        functools.partial(_pool_kernel, tile=tile, rows=t // GRID_W),
        grid=(b,),
        in_specs=[pl.BlockSpec((1, t, pw), lambda i: (i, 0, p_col0 // pw)),
                  pl.BlockSpec((1, t, pw), lambda i: (i, 0, p_col0 // pw + 1)),
                  pl.BlockSpec((P_GROUPS, tile, tile), lambda i: (0, 0, 0)),
                  pl.BlockSpec((P_GROUPS, LANES, LANES), lambda i: (0, 0, 0)),
                  pl.BlockSpec((1, pw), lambda i: (0, 0))],
        out_specs=pl.BlockSpec((1, t, pw), lambda i: (i, 0, 0)),
        out_shape=jax.ShapeDtypeStruct((b, t, pw), BF16),
        scratch_shapes=[pltpu.VMEM((t + 2 * pad, LANES), F32),
                        pltpu.VMEM((P_GROUPS, t, LANES), F32)],
        compiler_params=_cparams(("arbitrary",), 48),
        name="pool",
    )(y, y, _col_pool_matrices(tile), pool_w, pool_scale)


def _merge_kernel(x_ref, ym_ref, yp_ref, gm_ref, gp_ref, gate_ref, wm_ref, wp_ref, wo_ref, fw_ref, o_ref):
    y_m = jnp.dot(ym_ref[...], wm_ref[...], preferred_element_type=F32)
    y_p = jnp.dot(yp_ref[...], wp_ref[...], preferred_element_type=F32)
    merged = _sigmoid(gm_ref[...].astype(F32)) * y_m + _sigmoid(gp_ref[...].astype(F32)) * y_p
    upd = jnp.dot(merged.astype(BF16), wo_ref[...], preferred_element_type=F32)
    xo = x_ref[...] + gate_ref[0] * upd
    o_ref[...] = xo * lax.rsqrt(jnp.mean(xo * xo, axis=-1, keepdims=True) + EPS) * fw_ref[...]


def _merge(x2, ym, yp, y, gate, wm, wp, wo, final_w, rows_per_mod, g_col0):
    n, d = x2.shape
    tm = 512
    per = rows_per_mod // tm
    pw = yp.shape[1]
    full = lambda shape: pl.BlockSpec(shape, lambda i: (0,) * len(shape))
    return pl.pallas_call(
        _merge_kernel,
        grid=(n // tm,),
        in_specs=[pl.BlockSpec((tm, d), lambda i: (i, 0)),
                  pl.BlockSpec((tm, d), lambda i: (i, 0)),
                  pl.BlockSpec((tm, pw), lambda i: (i, 0)),
                  pl.BlockSpec((tm, d), lambda i: (i, g_col0 // d)),
                  pl.BlockSpec((tm, d), lambda i: (i, g_col0 // d + 1)),
                  pl.BlockSpec((1, 1, d), lambda i: (i // per, 0, 0)),
                  full((d, d)), full((pw, d)), full((d, d)), full((1, d))],
        out_specs=pl.BlockSpec((tm, d), lambda i: (i, 0)),
        out_shape=jax.ShapeDtypeStruct((n, d), F32),
        compiler_params=_cparams(("arbitrary",), 48),
        name="merge",
    )(x2, ym, yp, y, y, gate, wm, wp, wo, final_w)


def kernel(x, c, ctx, c_ctx, norm_w, ada_w, ada_b, in_w, gate_b, head_norm_w, pool_w, pool_scale,
           branch_m_w, branch_p_w, out_w, final_norm_w):
    b, t, d = x.shape
    tc = ctx.shape[1]
    depth = norm_w.shape[0]
    assert depth == 1, "context stream update between layers is not implemented"
    mw = M_HEADS * MXU_DIM
    assert d == mw
    g0 = 5 * mw

    cc = jnp.zeros((2 * 8, d), F32).at[:b].set(c).at[b].set(c_ctx)
    mod = _adaln(cc, ada_w[0], ada_b[0])
    shift, scale, gate = mod[:, :d], mod[:, d:2 * d], mod[:, 2 * d:]

    kscale = jnp.concatenate([jnp.ones((mw,), F32), jnp.full((mw,), MXU_DIM ** -0.5, F32),
                              jnp.ones((in_w.shape[2] - 2 * mw,), F32)])
    w_all = in_w[0] * kscale
    w_main = jnp.concatenate([w_all[:, :g0], w_all[:, g0 + N_GATE_COLS:]], axis=1).astype(BF16)
    w_gate = jnp.pad(w_all[:, g0:g0 + N_GATE_COLS], ((0, 0), (0, LANES - N_GATE_COLS))).astype(BF16)
    nw = norm_w[0].reshape(1, d)

    y, gates_tok = _in_proj(x.reshape(b * t, d), scale[:b, None, :], shift[:b, None, :], nw,
                            w_main, w_gate, t, "in_proj")
    yc, gates_tok_c = _in_proj(ctx.reshape(b * tc, d), scale[b:b + 1, None, :], shift[b:b + 1, None, :], nw,
                               w_main[:, mw:3 * mw], w_gate, b * tc, "in_proj_ctx")
    y = y.reshape(b, t, -1)
    yc = yc.reshape(b, tc, -1)

    grow, gcol = _gate_layouts(gates_tok.reshape(b, t, LANES), gate_b[0], CHUNK)
    _, gcolc = _gate_layouts(gates_tok_c.reshape(b, tc, LANES), gate_b[0], CHUNK)

    ym = _mlstm(y, yc, grow, gcol, gcolc, head_norm_w[0].reshape(1, mw), CHUNK)
    yp = _pool(y, pool_w[0].astype(BF16), pool_scale[0].reshape(1, -1), g0)

    out = _merge(x.reshape(b * t, d), ym.reshape(b * t, mw), yp.reshape(b * t, -1), y.reshape(b * t, -1),
                 gate[:b, None, :], branch_m_w[0].astype(BF16), branch_p_w[0].astype(BF16),
                 out_w[0].astype(BF16), final_norm_w.reshape(1, d), t, g0 + 2 * P_GROUPS * LANES)
    return out.reshape(b, t, d)
```

```python
import functools

import numpy as np
import jax
import jax.numpy as jnp
from jax import lax
from jax.experimental import pallas as pl
from jax.experimental.pallas import tpu as pltpu

F32 = jnp.float32
BF16 = jnp.bfloat16

EPS = 1e-6
M_INIT = -1e30
NEG_BIG = -1e30
M_HEADS = 4
N_DIR = 2
N_GATE_COLS = N_DIR * 2 * M_HEADS
P_GROUPS = 4
POOL_WINDOWS = (2, 4, 8, 16)
GRID_W = 64
LANES = 128
MXU_DIM = 256
CHUNK = 256
MIB = 1024 * 1024


def _cparams(sem, vmem_mib):
    return pltpu.CompilerParams(dimension_semantics=sem, vmem_limit_bytes=vmem_mib * MIB)


def _sigmoid(x):
    return 1.0 / (1.0 + jnp.exp(-x))


def _silu(x):
    return x * _sigmoid(x)


def _log_sigmoid(x):
    return jnp.minimum(x, 0.0) - jnp.log(1.0 + jnp.exp(-jnp.abs(x)))


def _adaln_kernel(c_ref, w_ref, b_ref, o_ref):
    o_ref[...] = jnp.dot(_silu(c_ref[...]), w_ref[...], preferred_element_type=F32) + b_ref[...]


def _adaln(cc, ada_w, ada_b):
    rows, d = cc.shape
    n = ada_w.shape[1]
    tn = d
    return pl.pallas_call(
        _adaln_kernel,
        grid=(n // tn,),
        in_specs=[pl.BlockSpec((rows, d), lambda j: (0, 0)),
                  pl.BlockSpec((d, tn), lambda j: (0, j)),
                  pl.BlockSpec((1, tn), lambda j: (0, j))],
        out_specs=pl.BlockSpec((rows, tn), lambda j: (0, j)),
        out_shape=jax.ShapeDtypeStruct((rows, n), F32),
        compiler_params=_cparams(("arbitrary",), 32),
        name="adaln",
    )(cc, ada_w, ada_b.reshape(1, n))


def _inproj_kernel(x_ref, sc_ref, sh_ref, nw_ref, w_ref, wg_ref, y_ref, g_ref, xn_ref):
    @pl.when(pl.program_id(1) == 0)
    def _():
        x = x_ref[...]
        ms = jnp.mean(x * x, axis=-1, keepdims=True)
        xn = x * lax.rsqrt(ms + EPS) * nw_ref[...]
        xn = xn * (1.0 + sc_ref[0]) + sh_ref[0]
        xn_ref[...] = xn.astype(BF16)
        g_ref[...] = jnp.dot(xn_ref[...], wg_ref[...], preferred_element_type=F32)

    y_ref[...] = jnp.dot(xn_ref[...], w_ref[...], preferred_element_type=F32).astype(BF16)


def _in_proj(x2, scale, shift, norm_w, w, wg, rows_per_mod, name):
    n, d = x2.shape
    nc = w.shape[1]
    tm = min(1024, rows_per_mod)
    tn = 1024
    per = rows_per_mod // tm
    return pl.pallas_call(
        _inproj_kernel,
        grid=(n // tm, nc // tn),
        in_specs=[pl.BlockSpec((tm, d), lambda i, j: (i, 0)),
                  pl.BlockSpec((1, 1, d), lambda i, j: (i // per, 0, 0)),
                  pl.BlockSpec((1, 1, d), lambda i, j: (i // per, 0, 0)),
                  pl.BlockSpec((1, d), lambda i, j: (0, 0)),
                  pl.BlockSpec((d, tn), lambda i, j: (0, j)),
                  pl.BlockSpec((d, LANES), lambda i, j: (0, 0))],
        out_specs=[pl.BlockSpec((tm, tn), lambda i, j: (i, j)),
                   pl.BlockSpec((tm, LANES), lambda i, j: (i, 0))],
        out_shape=[jax.ShapeDtypeStruct((n, nc), BF16),
                   jax.ShapeDtypeStruct((n, LANES), F32)],
        scratch_shapes=[pltpu.VMEM((tm, d), BF16)],
        compiler_params=_cparams(("arbitrary", "arbitrary"), 48),
        name=name,
    )(x2, scale, shift, norm_w, w, wg)


TERM_ROW0 = 16
TERM_PIECES = (3, 2, 2, 3, 2)
N_DH = N_DIR * M_HEADS


def _split(x, pieces):
    out = []
    for _ in range(pieces):
        p = x.astype(BF16)
        out.append(p)
        x = x - p.astype(F32)
    return out


def _chunk_scan(x, pos, chunk, op, reverse):
    t = x.shape[1]
    k = 1
    while k < chunk:
        if reverse:
            x = jnp.where(pos < chunk - k, op(x, pltpu.roll(x, t - k, 1)), x)
        else:
            x = jnp.where(pos >= k, op(x, pltpu.roll(x, k, 1)), x)
        k *= 2
    return x


def _gate_terms(li, pre_f, m0, chunk):
    rows, t = li.shape
    nc = t // chunk
    pos = lax.broadcasted_iota(jnp.int32, (rows, t), 1) % chunk
    rev = lax.broadcasted_iota(jnp.int32, (rows, t), 0) >= M_HEADS
    rev1 = lax.broadcasted_iota(jnp.int32, (rows, 1), 0) >= M_HEADS
    lf = _log_sigmoid(pre_f)
    b = jnp.where(rev, _chunk_scan(lf, pos, chunk, jnp.add, True), _chunk_scan(lf, pos, chunk, jnp.add, False))
    a = li - b
    cmax = jnp.where(rev, _chunk_scan(a, pos, chunk, jnp.maximum, True),
                     _chunk_scan(a, pos, chunk, jnp.maximum, False))

    def at_scan_end(x, c):
        lo = c * chunk
        return jnp.where(rev1, x[:, lo:lo + 1], x[:, lo + chunk - 1:lo + chunk])

    b_end = [at_scan_end(b, c) for c in range(nc)]
    a_max = [at_scan_end(cmax, c) for c in range(nc)]
    m_f, m_b = m0, m0
    m_in_f, m_in_b = [None] * nc, [None] * nc
    for i in range(nc):
        j = nc - 1 - i
        m_in_f[i] = m_f
        m_f = b_end[i] + jnp.maximum(m_f, a_max[i])
        m_in_b[j] = m_b
        m_b = b_end[j] + jnp.maximum(m_b, a_max[j])
    per_chunk = lambda vals: jnp.concatenate([jnp.broadcast_to(v, (rows, chunk)) for v in vals], axis=1)
    m_in = per_chunk([jnp.where(rev1, m_in_b[c], m_in_f[c]) for c in range(nc)])
    g = jnp.maximum(m_in, cmax)
    g_end = jnp.maximum(m_in, per_chunk(a_max))
    terms = (-g, jnp.exp(m_in - g), jnp.exp(a - g_end), -(b + g), jnp.exp(m_in - g_end))
    return terms, a, jnp.where(rev1, m_b, m_f)


def _term_rows(terms, t):
    ones = (lax.broadcasted_iota(jnp.int32, (TERM_ROW0, t), 0) < 3).astype(BF16)
    rows = [ones]
    for term, pieces in zip(terms, TERM_PIECES):
        rows += _split(term, pieces)
    used = TERM_ROW0 + N_DH * sum(TERM_PIECES)
    rows.append(jnp.zeros((LANES - used, t), BF16))
    return jnp.concatenate(rows, axis=0)


def _gates_kernel(ic_ref, fc_ref, i_ref, f_ref, bi_ref, bf_ref, rc_ref, r_ref, ra_ref, *, chunk):
    m0 = jnp.full((N_DH, 1), M_INIT, F32)
    terms_c, _, m_ctx = _gate_terms(ic_ref[0] + bi_ref[...], fc_ref[0] + bf_ref[...], m0, chunk)
    rc_ref[0] = _term_rows(terms_c, ic_ref.shape[2])
    terms, a, _ = _gate_terms(i_ref[0] + bi_ref[...], f_ref[0] + bf_ref[...], m_ctx, chunk)
    t = i_ref.shape[2]
    r_ref[0] = _term_rows(terms, t)
    a_pieces = [p.astype(F32) for p in _split(a, 3)]
    ra_ref[0] = jnp.zeros(ra_ref.shape[1:], F32)
    for dh in range(N_DH):
        for j, p in enumerate(a_pieces):
            ra_ref[0, dh, j:j + 1, :] = p[dh:dh + 1, :]


def _gates(gates_tok, gates_tok_c, gate_b, chunk):
    b, t, _ = gates_tok.shape
    tc = gates_tok_c.shape[1]

    def rows(g):
        g = jnp.transpose(g[:, :, :N_GATE_COLS], (0, 2, 1)).reshape(b, N_DIR, 2, M_HEADS, -1)
        return g[:, :, 0].reshape(b, N_DH, -1), g[:, :, 1].reshape(b, N_DH, -1)

    gb = gate_b.reshape(N_DIR, 2, M_HEADS)
    seq = lambda n: pl.BlockSpec((1, N_DH, n), lambda i: (i, 0, 0))
    col = pl.BlockSpec((N_DH, 1), lambda i: (0, 0))
    rc, r, ra = pl.pallas_call(
        functools.partial(_gates_kernel, chunk=chunk),
        grid=(b,),
        in_specs=[seq(tc), seq(tc), seq(t), seq(t), col, col],
        out_specs=[pl.BlockSpec((1, LANES, tc), lambda i: (i, 0, 0)),
                   pl.BlockSpec((1, LANES, t), lambda i: (i, 0, 0)),
                   pl.BlockSpec((1, N_DH, 8, t), lambda i: (i, 0, 0, 0))],
        out_shape=[jax.ShapeDtypeStruct((b, LANES, tc), BF16),
                   jax.ShapeDtypeStruct((b, LANES, t), BF16),
                   jax.ShapeDtypeStruct((b, N_DH, 8, t), F32)],
        compiler_params=_cparams(("arbitrary",), 32),
        name="gates",
    )(*rows(gates_tok_c), *rows(gates_tok), gb[:, 0].reshape(N_DH, 1), gb[:, 1].reshape(N_DH, 1))
    ra = ra.reshape(b, N_DIR, M_HEADS, 8, t // chunk, chunk).transpose(0, 1, 2, 4, 3, 5)
    return jnp.transpose(rc, (0, 2, 1)), jnp.transpose(r, (0, 2, 1)), ra


def _term_selectors(chunk):
    seld = np.zeros((M_HEADS, N_DIR, LANES, chunk), np.float32)
    selw = np.zeros((M_HEADS, N_DIR, LANES, 4 * LANES), np.float32)
    first = np.concatenate([[0], np.cumsum(TERM_PIECES)])
    for h in range(M_HEADS):
        for d in range(N_DIR):
            lane = lambda j: TERM_ROW0 + j * N_DH + d * M_HEADS + h
            for j in range(first[0], first[1]):
                seld[h, d, lane(j), :] = 1.0
            for q in range(4):
                for j in range(first[q + 1], first[q + 2]):
                    selw[h, d, lane(j), q * LANES:(q + 1) * LANES] = 1.0
    return jnp.asarray(seld, BF16), jnp.asarray(selw, BF16)


def _twice(x):
    return jnp.concatenate([x, x], axis=1)


def _mlstm_kernel(q_ref, k_ref, v_ref, o_ref, z_ref, p_ref, ra_ref, kc_ref, vc_ref, pc_ref,
                  seld_ref, selw_ref, hnw_ref, out_ref, st_ref, hs_ref, *, chunk):
    t = q_ref.shape[1]
    nc = t // chunk
    ncc = kc_ref.shape[1] // chunk
    dh = q_ref.shape[2]
    ones_col = jnp.ones((chunk, LANES), BF16)
    mean_col = jnp.full((dh, LANES), 1.0 / dh, BF16)
    t_idx = lax.broadcasted_iota(jnp.int32, (chunk, chunk), 0)
    s_idx = lax.broadcasted_iota(jnp.int32, (chunk, chunk), 1)

    st_ref[...] = jnp.zeros(st_ref.shape, F32)

    def row_terms(p, d):
        w = jnp.dot(p, selw_ref[0, d], preferred_element_type=F32)
        return [w[:, i * LANES:(i + 1) * LANES] for i in range(4)]

    def update(k, v, e_b, decay_b, d):
        ke = (k.astype(F32) * _twice(e_b)).astype(BF16)
        upd = lax.dot_general(ke, jnp.concatenate([v, ones_col], axis=1), (((0,), (0,)), ((), ())),
                              preferred_element_type=F32)
        st_ref[d] = jnp.concatenate([decay_b] * 3, axis=1) * st_ref[d] + upd

    for d in range(N_DIR):
        for i in range(ncc):
            rows = pl.ds((i if d == 0 else ncc - 1 - i) * chunk, chunk)
            _, e_b, _, decay_b = row_terms(pc_ref[0, rows, :], d)
            update(kc_ref[0, rows, :], vc_ref[0, rows, :], e_b, decay_b, d)

    def run_chunk(c, d):
        start = pl.multiple_of(c * chunk, chunk)
        rows = pl.ds(start, chunk)
        p, q, k, v = p_ref[0, rows, :], q_ref[0, rows, :], k_ref[0, rows, :], v_ref[0, rows, :]
        a_rows = jnp.concatenate([ra_ref[0, d, 0, c], jnp.zeros((8, chunk), F32)], axis=0).astype(BF16)
        sel = jnp.concatenate([a_rows, seld_ref[0, d, TERM_ROW0:, :]], axis=0)
        logw = jnp.dot(p, sel, preferred_element_type=F32)
        mask = (s_idx <= t_idx) if d == 0 else (s_idx >= t_idx)
        w_intra = jnp.exp(jnp.where(mask, logw, NEG_BIG))
        s = lax.dot_general(q, k, (((1,), (1,)), ((), ())), preferred_element_type=F32) * w_intra
        w_inter_b, e_b, neg_mt_b, decay_b = row_terms(p, d)
        xa = jnp.dot(q, st_ref[d].astype(BF16), preferred_element_type=F32)
        ya = jnp.dot(s.astype(BF16), jnp.concatenate([v, ones_col], axis=1), preferred_element_type=F32)
        den_b = w_inter_b * xa[:, dh:] + ya[:, dh:]
        inv_b = 1.0 / jnp.maximum(jnp.abs(den_b), jnp.exp(neg_mt_b))
        h = (_twice(w_inter_b) * xa[:, :dh] + ya[:, :dh]) * _twice(inv_b)
        update(k, v, e_b, decay_b, d)
        return rows, h

    def finalize(rows, h):
        ms_b = jnp.dot((h * h).astype(BF16), mean_col, preferred_element_type=F32)
        hn = h * _twice(lax.rsqrt(ms_b + EPS)) * hnw_ref[...]
        o = o_ref[0, rows, :].astype(F32)
        z = z_ref[0, rows, :].astype(F32)
        out_ref[0, rows, :] = (hn * _sigmoid(o) * _silu(z)).astype(out_ref.dtype)

    def first(i, carry):
        for d in range(N_DIR):
            rows, h = run_chunk(i if d == 0 else nc - 1 - i, d)
            hs_ref[rows, :] = h
        return carry

    def second(i, carry):
        for d in range(N_DIR):
            rows, h = run_chunk(i if d == 0 else nc - 1 - i, d)
            finalize(rows, hs_ref[rows, :] + h)
        return carry

    lax.fori_loop(0, nc // 2, first, 0)
    lax.fori_loop(nc // 2, nc, second, 0)


def _mlstm(y, yc, p, ra, pc, head_norm_w, chunk):
    b, t, _ = y.shape
    tc = yc.shape[1]
    dh = MXU_DIM
    assert chunk == dh and t % (2 * chunk) == 0 and tc % chunk == 0
    seld, selw = _term_selectors(chunk)
    blk = lambda col0: pl.BlockSpec((1, t, dh), lambda i, h, c=col0: (i, 0, c + h))
    blkc = lambda col0: pl.BlockSpec((1, tc, dh), lambda i, h, c=col0: (i, 0, c + h))
    return pl.pallas_call(
        functools.partial(_mlstm_kernel, chunk=chunk),
        grid=(b, M_HEADS),
        in_specs=[blk(0), blk(M_HEADS), blk(2 * M_HEADS), blk(3 * M_HEADS), blk(4 * M_HEADS),
                  pl.BlockSpec((1, t, LANES), lambda i, h: (i, 0, 0)),
                  pl.BlockSpec((1, N_DIR, 1, t // chunk, 8, chunk), lambda i, h: (i, 0, h, 0, 0, 0)),
                  blkc(0), blkc(M_HEADS),
                  pl.BlockSpec((1, tc, LANES), lambda i, h: (i, 0, 0)),
                  pl.BlockSpec((1, N_DIR, LANES, chunk), lambda i, h: (h, 0, 0, 0)),
                  pl.BlockSpec((1, N_DIR, LANES, 4 * LANES), lambda i, h: (h, 0, 0, 0)),
                  pl.BlockSpec((1, dh), lambda i, h: (0, h))],
        out_specs=pl.BlockSpec((1, t, dh), lambda i, h: (i, 0, h)),
        out_shape=jax.ShapeDtypeStruct((b, t, M_HEADS * dh), BF16),
        scratch_shapes=[pltpu.VMEM((N_DIR, dh, dh + LANES), F32),
                        pltpu.VMEM((t, dh), F32)],
        compiler_params=_cparams(("arbitrary", "arbitrary"), 48),
        name="mlstm",
    )(y, y, y, y, y, p, ra, yc, yc, pc, seld, selw, head_norm_w)


def _window_bounds(pos, w, length):
    return np.clip(pos - w // 2, 0, length), np.clip(pos + w - w // 2, 0, length)


def _col_pool_matrices(tile):
    pos = np.arange(GRID_W)
    mats = []
    for w in POOL_WINDOWS:
        lo, hi = _window_bounds(pos, w, GRID_W)
        band = ((pos[None, :] >= lo[:, None]) & (pos[None, :] < hi[:, None])).astype(np.float32)
        mats.append(np.kron(np.eye(tile // GRID_W, dtype=np.float32), band))
    return jnp.asarray(np.stack(mats), dtype=BF16)


def _pool_kernel(p_ref, z_ref, mc_ref, pw_ref, ps_ref, out_ref, pad_ref, scale_ref, *, tile, rows):
    t = p_ref.shape[1]
    gd = LANES
    pad = (max(POOL_WINDOWS) // 2) * GRID_W
    n_tiles = t // tile

    @pl.when(pl.program_id(0) == 0)
    def _():
        pad_ref[...] = jnp.zeros(pad_ref.shape, F32)
        tok = lax.broadcasted_iota(jnp.int32, (t, gd), 0)
        r = tok // GRID_W
        c = tok % GRID_W
        for g, w in enumerate(POOL_WINDOWS):
            cnt_r = jnp.minimum(r + (w - w // 2), rows) - jnp.maximum(r - w // 2, 0)
            cnt_c = jnp.minimum(c + (w - w // 2), GRID_W) - jnp.maximum(c - w // 2, 0)
            scale_ref[g] = 1.0 / (cnt_r.astype(F32) * cnt_c.astype(F32))

    for g, w in enumerate(POOL_WINDOWS):
        cols = pl.ds(g * gd, gd)

        def col_sum(i, carry):
            start = pl.multiple_of(i * tile, tile)
            pad_ref[pl.ds(pad + start, tile), :] = jnp.dot(
                mc_ref[g], p_ref[0, pl.ds(start, tile), cols], preferred_element_type=F32)
            return carry

        lax.fori_loop(0, n_tiles, col_sum, 0)

        def row_sum(i, carry):
            start = pl.multiple_of(i * tile, tile)
            acc = pad_ref[pl.ds(pad + start - (w // 2) * GRID_W, tile), :]
            for j in range(1 - w // 2, w - w // 2):
                acc = acc + pad_ref[pl.ds(pad + start + j * GRID_W, tile), :]
            mixed = acc * scale_ref[g, pl.ds(start, tile), :] - p_ref[0, pl.ds(start, tile), cols].astype(F32)
            mm = jnp.dot(mixed.astype(BF16), pw_ref[g], preferred_element_type=F32)
            zz = z_ref[0, pl.ds(start, tile), cols].astype(F32)
            out_ref[0, pl.ds(start, tile), cols] = (mm * ps_ref[:, cols] * _silu(zz)).astype(out_ref.dtype)
            return carry

        lax.fori_loop(0, n_tiles, row_sum, 0)


def _pool(y, pool_w, pool_scale, p_col0):
    b, t, _ = y.shape
    pw = P_GROUPS * LANES
    tile = MXU_DIM
    pad = (max(POOL_WINDOWS) // 2) * GRID_W
    return pl.pallas_call(
        functools.partial(_pool_kernel, tile=tile, rows=t // GRID_W),
        grid=(b,),
        in_specs=[pl.BlockSpec((1, t, pw), lambda i: (i, 0, p_col0 // pw)),
                  pl.BlockSpec((1, t, pw), lambda i: (i, 0, p_col0 // pw + 1)),
                  pl.BlockSpec((P_GROUPS, tile, tile), lambda i: (0, 0, 0)),
                  pl.BlockSpec((P_GROUPS, LANES, LANES), lambda i: (0, 0, 0)),
                  pl.BlockSpec((1, pw), lambda i: (0, 0))],
        out_specs=pl.BlockSpec((1, t, pw), lambda i: (i, 0, 0)),
        out_shape=jax.ShapeDtypeStruct((b, t, pw), BF16),
        scratch_shapes=[pltpu.VMEM((t + 2 * pad, LANES), F32),
                        pltpu.VMEM((P_GROUPS, t, LANES), F32)],
        compiler_params=_cparams(("arbitrary",), 48),
        name="pool",
    )(y, y, _col_pool_matrices(tile), pool_w, pool_scale)


def _merge_kernel(x_ref, ym_ref, yp_ref, gm_ref, gp_ref, gate_ref, wm_ref, wp_ref, wo_ref, fw_ref, o_ref):
    y_m = jnp.dot(ym_ref[...], wm_ref[...], preferred_element_type=F32)
    y_p = jnp.dot(yp_ref[...], wp_ref[...], preferred_element_type=F32)
    merged = _sigmoid(gm_ref[...].astype(F32)) * y_m + _sigmoid(gp_ref[...].astype(F32)) * y_p
    upd = jnp.dot(merged.astype(BF16), wo_ref[...], preferred_element_type=F32)
    xo = x_ref[...] + gate_ref[0] * upd
    o_ref[...] = xo * lax.rsqrt(jnp.mean(xo * xo, axis=-1, keepdims=True) + EPS) * fw_ref[...]


def _merge(x2, ym, yp, y, gate, wm, wp, wo, final_w, rows_per_mod, g_col0):
    n, d = x2.shape
    tm = 512
    per = rows_per_mod // tm
    pw = yp.shape[1]
    full = lambda shape: pl.BlockSpec(shape, lambda i: (0,) * len(shape))
    return pl.pallas_call(
        _merge_kernel,
        grid=(n // tm,),
        in_specs=[pl.BlockSpec((tm, d), lambda i: (i, 0)),
                  pl.BlockSpec((tm, d), lambda i: (i, 0)),
                  pl.BlockSpec((tm, pw), lambda i: (i, 0)),
                  pl.BlockSpec((tm, d), lambda i: (i, g_col0 // d)),
                  pl.BlockSpec((tm, d), lambda i: (i, g_col0 // d + 1)),
                  pl.BlockSpec((1, 1, d), lambda i: (i // per, 0, 0)),
                  full((d, d)), full((pw, d)), full((d, d)), full((1, d))],
        out_specs=pl.BlockSpec((tm, d), lambda i: (i, 0)),
        out_shape=jax.ShapeDtypeStruct((n, d), F32),
        compiler_params=_cparams(("arbitrary",), 48),
        name="merge",
    )(x2, ym, yp, y, y, gate, wm, wp, wo, final_w)


def kernel(x, c, ctx, c_ctx, norm_w, ada_w, ada_b, in_w, gate_b, head_norm_w, pool_w, pool_scale,
           branch_m_w, branch_p_w, out_w, final_norm_w):
    b, t, d = x.shape
    tc = ctx.shape[1]
    depth = norm_w.shape[0]
    assert depth == 1, "context stream update between layers is not implemented"
    mw = M_HEADS * MXU_DIM
    assert d == mw
    g0 = 5 * mw

    cc = jnp.zeros((2 * 8, d), F32).at[:b].set(c).at[b].set(c_ctx)
    mod = _adaln(cc, ada_w[0], ada_b[0])
    shift, scale, gate = mod[:, :d], mod[:, d:2 * d], mod[:, 2 * d:]

    kscale = jnp.concatenate([jnp.ones((mw,), F32), jnp.full((mw,), MXU_DIM ** -0.5, F32),
                              jnp.ones((in_w.shape[2] - 2 * mw,), F32)])
    w_all = in_w[0] * kscale
    w_main = jnp.concatenate([w_all[:, :g0], w_all[:, g0 + N_GATE_COLS:]], axis=1).astype(BF16)
    w_gate = jnp.pad(w_all[:, g0:g0 + N_GATE_COLS], ((0, 0), (0, LANES - N_GATE_COLS))).astype(BF16)
    nw = norm_w[0].reshape(1, d)

    y, gates_tok = _in_proj(x.reshape(b * t, d), scale[:b, None, :], shift[:b, None, :], nw,
                            w_main, w_gate, t, "in_proj")
    yc, gates_tok_c = _in_proj(ctx.reshape(b * tc, d), scale[b:b + 1, None, :], shift[b:b + 1, None, :], nw,
                               w_main[:, mw:3 * mw], w_gate, b * tc, "in_proj_ctx")
    y = y.reshape(b, t, -1)
    yc = yc.reshape(b, tc, -1)

    pc, p, ra = _gates(gates_tok.reshape(b, t, LANES), gates_tok_c.reshape(b, tc, LANES), gate_b[0], CHUNK)
    ym = _mlstm(y, yc, p, ra, pc, head_norm_w[0].reshape(1, mw), CHUNK)
    yp = _pool(y, pool_w[0].astype(BF16), pool_scale[0].reshape(1, -1), g0)

    out = _merge(x.reshape(b * t, d), ym.reshape(b * t, mw), yp.reshape(b * t, -1), y.reshape(b * t, -1),
                 gate[:b, None, :], branch_m_w[0].astype(BF16), branch_p_w[0].astype(BF16),
                 out_w[0].astype(BF16), final_norm_w.reshape(1, d), t, g0 + 2 * P_GROUPS * LANES)
    return out.reshape(b, t, d)
```

```python
import functools

import numpy as np
import jax
import jax.numpy as jnp
from jax import lax
from jax.experimental import pallas as pl
from jax.experimental.pallas import tpu as pltpu

F32 = jnp.float32
BF16 = jnp.bfloat16

EPS = 1e-6
M_INIT = -1e30
NEG_BIG = -1e30
M_HEADS = 4
N_DIR = 2
N_GATE_COLS = N_DIR * 2 * M_HEADS
P_GROUPS = 4
POOL_WINDOWS = (2, 4, 8, 16)
GRID_W = 64
LANES = 128
MXU_DIM = 256
CHUNK = 256
POOL_UNROLL = 4
MIB = 1024 * 1024


def _cparams(sem, vmem_mib):
    return pltpu.CompilerParams(dimension_semantics=sem, vmem_limit_bytes=vmem_mib * MIB)


def _sigmoid(x):
    return 1.0 / (1.0 + jnp.exp(-x))


def _silu(x):
    return x * _sigmoid(x)


def _log_sigmoid(x):
    return jnp.minimum(x, 0.0) - jnp.log(1.0 + jnp.exp(-jnp.abs(x)))


def _dot_nt(a, b):
    return lax.dot_general(a, b, (((1,), (1,)), ((), ())), preferred_element_type=F32)


def _adaln_kernel(c_ref, w_ref, b_ref, o_ref):
    o_ref[...] = jnp.dot(_silu(c_ref[...]), w_ref[...], preferred_element_type=F32) + b_ref[...]


def _adaln(cc, ada_w, ada_b):
    rows, d = cc.shape
    n = ada_w.shape[1]
    tn = d
    return pl.pallas_call(
        _adaln_kernel,
        grid=(n // tn,),
        in_specs=[pl.BlockSpec((rows, d), lambda j: (0, 0)),
                  pl.BlockSpec((d, tn), lambda j: (0, j)),
                  pl.BlockSpec((1, tn), lambda j: (0, j))],
        out_specs=pl.BlockSpec((rows, tn), lambda j: (0, j)),
        out_shape=jax.ShapeDtypeStruct((rows, n), F32),
        compiler_params=_cparams(("arbitrary",), 32),
        name="adaln",
    )(cc, ada_w, ada_b.reshape(1, n))


def _inproj_kernel(x_ref, sc_ref, sh_ref, nw_ref, w_ref, wg_ref, y_ref, g_ref, xn_ref):
    @pl.when(pl.program_id(1) == 0)
    def _():
        x = x_ref[...]
        ms = jnp.mean(x * x, axis=-1, keepdims=True)
        xn = x * lax.rsqrt(ms + EPS) * nw_ref[...]
        xn = xn * (1.0 + sc_ref[0]) + sh_ref[0]
        xn_ref[...] = xn.astype(BF16)
        g_ref[...] = _dot_nt(xn_ref[...], wg_ref[...])

    y_ref[...] = _dot_nt(xn_ref[...], w_ref[...]).astype(BF16)


def _in_proj(x2, scale, shift, norm_w, w, wg, rows_per_mod, name, out0=0, nc=None):
    n, d = x2.shape
    nc = w.shape[0] if nc is None else nc
    tm = min(1024, rows_per_mod)
    tn = 1024
    per = rows_per_mod // tm
    j0 = out0 // tn
    return pl.pallas_call(
        _inproj_kernel,
        grid=(n // tm, nc // tn),
        in_specs=[pl.BlockSpec((tm, d), lambda i, j: (i, 0)),
                  pl.BlockSpec((1, 1, d), lambda i, j: (i // per, 0, 0)),
                  pl.BlockSpec((1, 1, d), lambda i, j: (i // per, 0, 0)),
                  pl.BlockSpec((1, d), lambda i, j: (0, 0)),
                  pl.BlockSpec((tn, d), lambda i, j: (j + j0, 0)),
                  pl.BlockSpec((LANES, d), lambda i, j: (0, 0))],
        out_specs=[pl.BlockSpec((tm, tn), lambda i, j: (i, j)),
                   pl.BlockSpec((tm, LANES), lambda i, j: (i, 0))],
        out_shape=[jax.ShapeDtypeStruct((n, nc), BF16),
                   jax.ShapeDtypeStruct((n, LANES), F32)],
        scratch_shapes=[pltpu.VMEM((tm, d), BF16)],
        compiler_params=_cparams(("arbitrary", "arbitrary"), 48),
        name=name,
    )(x2, scale, shift, norm_w, w, wg)


TERM_ROW0 = 16
TERM_PIECES = (3, 2, 2, 3, 2)
N_DH = N_DIR * M_HEADS


def _split(x, pieces):
    out = []
    for _ in range(pieces):
        p = x.astype(BF16)
        out.append(p)
        x = x - p.astype(F32)
    return out


def _chunk_scan(x, pos, chunk, op, reverse):
    t = x.shape[1]
    k = 1
    while k < chunk:
        if reverse:
            x = jnp.where(pos < chunk - k, op(x, pltpu.roll(x, t - k, 1)), x)
        else:
            x = jnp.where(pos >= k, op(x, pltpu.roll(x, k, 1)), x)
        k *= 2
    return x


def _gate_terms(li, pre_f, m0, chunk):
    rows, t = li.shape
    nc = t // chunk
    pos = lax.broadcasted_iota(jnp.int32, (rows, t), 1) % chunk
    rev = lax.broadcasted_iota(jnp.int32, (rows, t), 0) >= M_HEADS
    rev1 = lax.broadcasted_iota(jnp.int32, (rows, 1), 0) >= M_HEADS
    lf = _log_sigmoid(pre_f)
    b = jnp.where(rev, _chunk_scan(lf, pos, chunk, jnp.add, True), _chunk_scan(lf, pos, chunk, jnp.add, False))
    a = li - b
    cmax = jnp.where(rev, _chunk_scan(a, pos, chunk, jnp.maximum, True),
                     _chunk_scan(a, pos, chunk, jnp.maximum, False))

    def at_scan_end(x, c):
        lo = c * chunk
        return jnp.where(rev1, x[:, lo:lo + 1], x[:, lo + chunk - 1:lo + chunk])

    b_end = [at_scan_end(b, c) for c in range(nc)]
    a_max = [at_scan_end(cmax, c) for c in range(nc)]
    m_f, m_b = m0, m0
    m_in_f, m_in_b = [None] * nc, [None] * nc
    for i in range(nc):
        j = nc - 1 - i
        m_in_f[i] = m_f
        m_f = b_end[i] + jnp.maximum(m_f, a_max[i])
        m_in_b[j] = m_b
        m_b = b_end[j] + jnp.maximum(m_b, a_max[j])
    per_chunk = lambda vals: jnp.concatenate([jnp.broadcast_to(v, (rows, chunk)) for v in vals], axis=1)
    m_in = per_chunk([jnp.where(rev1, m_in_b[c], m_in_f[c]) for c in range(nc)])
    g = jnp.maximum(m_in, cmax)
    g_end = jnp.maximum(m_in, per_chunk(a_max))
    terms = (-g, jnp.exp(m_in - g), jnp.exp(a - g_end), -(b + g), jnp.exp(m_in - g_end))
    return terms, a, jnp.where(rev1, m_b, m_f)


def _term_rows(terms, t):
    ones = (lax.broadcasted_iota(jnp.int32, (TERM_ROW0, t), 0) < 3).astype(BF16)
    rows = [ones]
    for term, pieces in zip(terms, TERM_PIECES):
        rows += _split(term, pieces)
    used = TERM_ROW0 + N_DH * sum(TERM_PIECES)
    rows.append(jnp.zeros((LANES - used, t), BF16))
    return jnp.concatenate(rows, axis=0)


def _gates_kernel(ic_ref, fc_ref, i_ref, f_ref, bi_ref, bf_ref, rc_ref, r_ref, ra_ref, *, chunk):
    m0 = jnp.full((N_DH, 1), M_INIT, F32)
    terms_c, _, m_ctx = _gate_terms(ic_ref[0] + bi_ref[...], fc_ref[0] + bf_ref[...], m0, chunk)
    rc_ref[0] = _term_rows(terms_c, ic_ref.shape[2])
    terms, a, _ = _gate_terms(i_ref[0] + bi_ref[...], f_ref[0] + bf_ref[...], m_ctx, chunk)
    t = i_ref.shape[2]
    r_ref[0] = _term_rows(terms, t)
    a_pieces = [p.astype(F32) for p in _split(a, 3)]
    ra_ref[0] = jnp.zeros(ra_ref.shape[1:], F32)
    for dh in range(N_DH):
        for j, p in enumerate(a_pieces):
            ra_ref[0, dh, j:j + 1, :] = p[dh:dh + 1, :]


def _gates(gates_tok, gates_tok_c, gate_b, chunk):
    b, t, _ = gates_tok.shape
    tc = gates_tok_c.shape[1]

    def rows(g):
        g = jnp.transpose(g[:, :, :N_GATE_COLS], (0, 2, 1)).reshape(b, N_DIR, 2, M_HEADS, -1)
        return g[:, :, 0].reshape(b, N_DH, -1), g[:, :, 1].reshape(b, N_DH, -1)

    gb = gate_b.reshape(N_DIR, 2, M_HEADS)
    seq = lambda n: pl.BlockSpec((1, N_DH, n), lambda i: (i, 0, 0))
    col = pl.BlockSpec((N_DH, 1), lambda i: (0, 0))
    rc, r, ra = pl.pallas_call(
        functools.partial(_gates_kernel, chunk=chunk),
        grid=(b,),
        in_specs=[seq(tc), seq(tc), seq(t), seq(t), col, col],
        out_specs=[pl.BlockSpec((1, LANES, tc), lambda i: (i, 0, 0)),
                   pl.BlockSpec((1, LANES, t), lambda i: (i, 0, 0)),
                   pl.BlockSpec((1, N_DH, 8, t), lambda i: (i, 0, 0, 0))],
        out_shape=[jax.ShapeDtypeStruct((b, LANES, tc), BF16),
                   jax.ShapeDtypeStruct((b, LANES, t), BF16),
                   jax.ShapeDtypeStruct((b, N_DH, 8, t), F32)],
        compiler_params=_cparams(("arbitrary",), 32),
        name="gates",
    )(*rows(gates_tok_c), *rows(gates_tok), gb[:, 0].reshape(N_DH, 1), gb[:, 1].reshape(N_DH, 1))
    ra = ra.reshape(b, N_DIR, M_HEADS, 8, t // chunk, chunk).transpose(0, 1, 2, 4, 3, 5)
    return jnp.transpose(rc, (0, 2, 1)), jnp.transpose(r, (0, 2, 1)), ra


def _term_selectors(chunk):
    seld = np.zeros((M_HEADS, N_DIR, LANES, chunk), np.float32)
    selw = np.zeros((M_HEADS, N_DIR, LANES, 4 * LANES), np.float32)
    first = np.concatenate([[0], np.cumsum(TERM_PIECES)])
    for h in range(M_HEADS):
        for d in range(N_DIR):
            lane = lambda j: TERM_ROW0 + j * N_DH + d * M_HEADS + h
            for j in range(first[0], first[1]):
                seld[h, d, lane(j), :] = 1.0
            for q in range(4):
                for j in range(first[q + 1], first[q + 2]):
                    selw[h, d, lane(j), q * LANES:(q + 1) * LANES] = 1.0
    return jnp.asarray(seld, BF16), jnp.asarray(selw, BF16)


def _twice(x):
    return jnp.concatenate([x, x], axis=1)


def _mlstm_kernel(q_ref, k_ref, v_ref, o_ref, z_ref, p_ref, ra_ref, kc_ref, vc_ref, pc_ref,
                  seld_ref, selw_ref, hnw_ref, out_ref, st_ref, hs_ref, *, chunk):
    t = q_ref.shape[1]
    nc = t // chunk
    ncc = kc_ref.shape[1] // chunk
    dh = q_ref.shape[2]
    ones_col = jnp.ones((chunk, LANES), BF16)
    mean_col = jnp.full((dh, LANES), 1.0 / dh, BF16)
    t_idx = lax.broadcasted_iota(jnp.int32, (chunk, chunk), 0)
    s_idx = lax.broadcasted_iota(jnp.int32, (chunk, chunk), 1)

    st_ref[...] = jnp.zeros(st_ref.shape, F32)

    def row_terms(p, d):
        w = jnp.dot(p, selw_ref[0, d], preferred_element_type=F32)
        return [w[:, i * LANES:(i + 1) * LANES] for i in range(4)]

    def update(k, v, e_b, decay_b, d):
        ke = (k.astype(F32) * _twice(e_b)).astype(BF16)
        upd = lax.dot_general(ke, jnp.concatenate([v, ones_col], axis=1), (((0,), (0,)), ((), ())),
                              preferred_element_type=F32)
        st_ref[d] = jnp.concatenate([decay_b] * 3, axis=1) * st_ref[d] + upd

    for d in range(N_DIR):
        for i in range(ncc):
            rows = pl.ds((i if d == 0 else ncc - 1 - i) * chunk, chunk)
            _, e_b, _, decay_b = row_terms(pc_ref[0, rows, :], d)
            update(kc_ref[0, rows, :], vc_ref[0, rows, :], e_b, decay_b, d)

    def run_chunk(c, d):
        start = pl.multiple_of(c * chunk, chunk)
        rows = pl.ds(start, chunk)
        p, q, k, v = p_ref[0, rows, :], q_ref[0, rows, :], k_ref[0, rows, :], v_ref[0, rows, :]
        a_rows = jnp.concatenate([ra_ref[0, d, 0, c], jnp.zeros((8, chunk), F32)], axis=0).astype(BF16)
        sel = jnp.concatenate([a_rows, seld_ref[0, d, TERM_ROW0:, :]], axis=0)
        logw = jnp.dot(p, sel, preferred_element_type=F32)
        mask = (s_idx <= t_idx) if d == 0 else (s_idx >= t_idx)
        w_intra = jnp.exp(jnp.where(mask, logw, NEG_BIG))
        s = lax.dot_general(q, k, (((1,), (1,)), ((), ())), preferred_element_type=F32) * w_intra
        w_inter_b, e_b, neg_mt_b, decay_b = row_terms(p, d)
        xa = jnp.dot(q, st_ref[d].astype(BF16), preferred_element_type=F32)
        ya = jnp.dot(s.astype(BF16), jnp.concatenate([v, ones_col], axis=1), preferred_element_type=F32)
        den_b = w_inter_b * xa[:, dh:] + ya[:, dh:]
        inv_b = 1.0 / jnp.maximum(jnp.abs(den_b), jnp.exp(neg_mt_b))
        h = (_twice(w_inter_b) * xa[:, :dh] + ya[:, :dh]) * _twice(inv_b)
        update(k, v, e_b, decay_b, d)
        return rows, h

    def finalize(rows, h):
        ms_b = jnp.dot((h * h).astype(BF16), mean_col, preferred_element_type=F32)
        hn = h * _twice(lax.rsqrt(ms_b + EPS)) * hnw_ref[...]
        o = o_ref[0, rows, :].astype(F32)
        z = z_ref[0, rows, :].astype(F32)
        out_ref[0, rows, :] = (hn * _sigmoid(o) * _silu(z)).astype(out_ref.dtype)

    def first(i, carry):
        for d in range(N_DIR):
            rows, h = run_chunk(i if d == 0 else nc - 1 - i, d)
            hs_ref[rows, :] = h
        return carry

    def second(i, carry):
        for d in range(N_DIR):
            rows, h = run_chunk(i if d == 0 else nc - 1 - i, d)
            finalize(rows, hs_ref[rows, :] + h)
        return carry

    lax.fori_loop(0, nc // 2, first, 0)
    lax.fori_loop(nc // 2, nc, second, 0)


def _mlstm(y, yc, p, ra, pc, head_norm_w, chunk):
    b, t, _ = y.shape
    tc = yc.shape[1]
    dh = MXU_DIM
    assert chunk == dh and t % (2 * chunk) == 0 and tc % chunk == 0
    seld, selw = _term_selectors(chunk)
    blk = lambda col0: pl.BlockSpec((1, t, dh), lambda i, h, c=col0: (i, 0, c + h))
    blkc = lambda col0: pl.BlockSpec((1, tc, dh), lambda i, h, c=col0: (i, 0, c + h))
    return pl.pallas_call(
        functools.partial(_mlstm_kernel, chunk=chunk),
        grid=(b, M_HEADS),
        in_specs=[blk(0), blk(M_HEADS), blk(2 * M_HEADS), blk(3 * M_HEADS), blk(4 * M_HEADS),
                  pl.BlockSpec((1, t, LANES), lambda i, h: (i, 0, 0)),
                  pl.BlockSpec((1, N_DIR, 1, t // chunk, 8, chunk), lambda i, h: (i, 0, h, 0, 0, 0)),
                  blkc(0), blkc(M_HEADS),
                  pl.BlockSpec((1, tc, LANES), lambda i, h: (i, 0, 0)),
                  pl.BlockSpec((1, N_DIR, LANES, chunk), lambda i, h: (h, 0, 0, 0)),
                  pl.BlockSpec((1, N_DIR, LANES, 4 * LANES), lambda i, h: (h, 0, 0, 0)),
                  pl.BlockSpec((1, dh), lambda i, h: (0, h))],
        out_specs=pl.BlockSpec((1, t, dh), lambda i, h: (i, 0, h)),
        out_shape=jax.ShapeDtypeStruct((b, t, M_HEADS * dh), BF16),
        scratch_shapes=[pltpu.VMEM((N_DIR, dh, dh + LANES), F32),
                        pltpu.VMEM((t, dh), F32)],
        compiler_params=_cparams(("arbitrary", "arbitrary"), 48),
        name="mlstm",
    )(y, y, y, y, y, p, ra, yc, yc, pc, seld, selw, head_norm_w)


def _window_bounds(pos, w, length):
    return np.clip(pos - w // 2, 0, length), np.clip(pos + w - w // 2, 0, length)


def _col_pool_matrices(tile):
    pos = np.arange(GRID_W)
    mats = []
    for w in POOL_WINDOWS:
        lo, hi = _window_bounds(pos, w, GRID_W)
        band = ((pos[None, :] >= lo[:, None]) & (pos[None, :] < hi[:, None])).astype(np.float32)
        mats.append(np.kron(np.eye(tile // GRID_W, dtype=np.float32), band))
    return jnp.asarray(np.stack(mats), dtype=BF16)


def _pool_kernel(p_ref, z_ref, mc_ref, pw_ref, ps_ref, out_ref, pad_ref, scale_ref, *, tile, rows):
    t = p_ref.shape[1]
    gd = LANES
    pad = (max(POOL_WINDOWS) // 2) * GRID_W
    n_tiles = t // tile

    @pl.when(pl.program_id(0) == 0)
    def _():
        pad_ref[...] = jnp.zeros(pad_ref.shape, F32)
        tok = lax.broadcasted_iota(jnp.int32, (t, gd), 0)
        r = tok // GRID_W
        c = tok % GRID_W
        for g, w in enumerate(POOL_WINDOWS):
            cnt_r = jnp.minimum(r + (w - w // 2), rows) - jnp.maximum(r - w // 2, 0)
            cnt_c = jnp.minimum(c + (w - w // 2), GRID_W) - jnp.maximum(c - w // 2, 0)
            scale_ref[g] = 1.0 / (cnt_r.astype(F32) * cnt_c.astype(F32))

    for g, w in enumerate(POOL_WINDOWS):
        cols = pl.ds(g * gd, gd)

        def col_sum(i, carry):
            start = pl.multiple_of(i * tile, tile)
            pad_ref[pl.ds(pad + start, tile), :] = jnp.dot(
                mc_ref[g], p_ref[0, pl.ds(start, tile), cols], preferred_element_type=F32)
            return carry

        lax.fori_loop(0, n_tiles, col_sum, 0, unroll=POOL_UNROLL)

        def row_sum(i, carry):
            start = pl.multiple_of(i * tile, tile)
            acc = pad_ref[pl.ds(pad + start - (w // 2) * GRID_W, tile), :]
            for j in range(1 - w // 2, w - w // 2):
                acc = acc + pad_ref[pl.ds(pad + start + j * GRID_W, tile), :]
            mixed = acc * scale_ref[g, pl.ds(start, tile), :] - p_ref[0, pl.ds(start, tile), cols].astype(F32)
            mm = jnp.dot(mixed.astype(BF16), pw_ref[g], preferred_element_type=F32)
            zz = z_ref[0, pl.ds(start, tile), cols].astype(F32)
            out_ref[0, pl.ds(start, tile), cols] = (mm * ps_ref[:, cols] * _silu(zz)).astype(out_ref.dtype)
            return carry

        lax.fori_loop(0, n_tiles, row_sum, 0, unroll=POOL_UNROLL)


def _pool(y, pool_w, pool_scale, p_col0):
    b, t, _ = y.shape
    pw = P_GROUPS * LANES
    tile = MXU_DIM
    pad = (max(POOL_WINDOWS) // 2) * GRID_W
    return pl.pallas_call(
        functools.partial(_pool_kernel, tile=tile, rows=t // GRID_W),
        grid=(b,),
        in_specs=[pl.BlockSpec((1, t, pw), lambda i: (i, 0, p_col0 // pw)),
                  pl.BlockSpec((1, t, pw), lambda i: (i, 0, p_col0 // pw + 1)),
                  pl.BlockSpec((P_GROUPS, tile, tile), lambda i: (0, 0, 0)),
                  pl.BlockSpec((P_GROUPS, LANES, LANES), lambda i: (0, 0, 0)),
                  pl.BlockSpec((1, pw), lambda i: (0, 0))],
        out_specs=pl.BlockSpec((1, t, pw), lambda i: (i, 0, 0)),
        out_shape=jax.ShapeDtypeStruct((b, t, pw), BF16),
        scratch_shapes=[pltpu.VMEM((t + 2 * pad, LANES), F32),
                        pltpu.VMEM((P_GROUPS, t, LANES), F32)],
        compiler_params=_cparams(("arbitrary",), 48),
        name="pool",
    )(y, y, _col_pool_matrices(tile), pool_w, pool_scale)


def _merge_kernel(x_ref, ym_ref, yp_ref, gm_ref, gp_ref, gate_ref, wm_ref, wp_ref, wo_ref, fw_ref, o_ref):
    y_m = jnp.dot(ym_ref[...], wm_ref[...], preferred_element_type=F32)
    y_p = jnp.dot(yp_ref[...], wp_ref[...], preferred_element_type=F32)
    merged = _sigmoid(gm_ref[...].astype(F32)) * y_m + _sigmoid(gp_ref[...].astype(F32)) * y_p
    upd = jnp.dot(merged.astype(BF16), wo_ref[...], preferred_element_type=F32)
    xo = x_ref[...] + gate_ref[0] * upd
    o_ref[...] = xo * lax.rsqrt(jnp.mean(xo * xo, axis=-1, keepdims=True) + EPS) * fw_ref[...]


def _merge(x2, ym, yp, y, gate, wm, wp, wo, final_w, rows_per_mod, g_col0):
    n, d = x2.shape
    tm = 512
    per = rows_per_mod // tm
    pw = yp.shape[1]
    full = lambda shape: pl.BlockSpec(shape, lambda i: (0,) * len(shape))
    return pl.pallas_call(
        _merge_kernel,
        grid=(n // tm,),
        in_specs=[pl.BlockSpec((tm, d), lambda i: (i, 0)),
                  pl.BlockSpec((tm, d), lambda i: (i, 0)),
                  pl.BlockSpec((tm, pw), lambda i: (i, 0)),
                  pl.BlockSpec((tm, d), lambda i: (i, g_col0 // d)),
                  pl.BlockSpec((tm, d), lambda i: (i, g_col0 // d + 1)),
                  pl.BlockSpec((1, 1, d), lambda i: (i // per, 0, 0)),
                  full((d, d)), full((pw, d)), full((d, d)), full((1, d))],
        out_specs=pl.BlockSpec((tm, d), lambda i: (i, 0)),
        out_shape=jax.ShapeDtypeStruct((n, d), F32),
        compiler_params=_cparams(("arbitrary",), 48),
        name="merge",
    )(x2, ym, yp, y, y, gate, wm, wp, wo, final_w)


def kernel(x, c, ctx, c_ctx, norm_w, ada_w, ada_b, in_w, gate_b, head_norm_w, pool_w, pool_scale,
           branch_m_w, branch_p_w, out_w, final_norm_w):
    b, t, d = x.shape
    tc = ctx.shape[1]
    depth = norm_w.shape[0]
    assert depth == 1, "context stream update between layers is not implemented"
    mw = M_HEADS * MXU_DIM
    assert d == mw
    g0 = 5 * mw

    cc = jnp.zeros((2 * 8, d), F32).at[:b].set(c).at[b].set(c_ctx)
    mod = _adaln(cc, ada_w[0], ada_b[0])
    shift, scale, gate = mod[:, :d], mod[:, d:2 * d], mod[:, 2 * d:]

    wt = jnp.transpose(in_w[0])
    kscale = jnp.concatenate([jnp.ones((mw,), F32), jnp.full((mw,), MXU_DIM ** -0.5, F32),
                              jnp.ones((wt.shape[0] - N_GATE_COLS - 2 * mw,), F32)])[:, None]
    w_main = (jnp.concatenate([wt[:g0], wt[g0 + N_GATE_COLS:]], axis=0) * kscale).astype(BF16)
    w_gate = jnp.pad(wt[g0:g0 + N_GATE_COLS], ((0, LANES - N_GATE_COLS), (0, 0))).astype(BF16)
    nw = norm_w[0].reshape(1, d)

    y, gates_tok = _in_proj(x.reshape(b * t, d), scale[:b, None, :], shift[:b, None, :], nw,
                            w_main, w_gate, t, "in_proj")
    yc, gates_tok_c = _in_proj(ctx.reshape(b * tc, d), scale[b:b + 1, None, :], shift[b:b + 1, None, :], nw,
                               w_main, w_gate, b * tc, "in_proj_ctx", out0=mw, nc=2 * mw)
    y = y.reshape(b, t, -1)
    yc = yc.reshape(b, tc, -1)

    pc, p, ra = _gates(gates_tok.reshape(b, t, LANES), gates_tok_c.reshape(b, tc, LANES), gate_b[0], CHUNK)
    ym = _mlstm(y, yc, p, ra, pc, head_norm_w[0].reshape(1, mw), CHUNK)
    yp = _pool(y, pool_w[0].astype(BF16), pool_scale[0].reshape(1, -1), g0)

    out = _merge(x.reshape(b * t, d), ym.reshape(b * t, mw), yp.reshape(b * t, -1), y.reshape(b * t, -1),
                 gate[:b, None, :], branch_m_w[0].astype(BF16), branch_p_w[0].astype(BF16),
                 out_w[0].astype(BF16), final_norm_w.reshape(1, d), t, g0 + 2 * P_GROUPS * LANES)
    return out.reshape(b, t, d)
```

```python
import functools

import numpy as np
import jax
import jax.numpy as jnp
from jax import lax
from jax.experimental import pallas as pl
from jax.experimental.pallas import tpu as pltpu

F32 = jnp.float32
BF16 = jnp.bfloat16

EPS = 1e-6
M_INIT = -1e30
NEG_BIG = -1e30
M_HEADS = 4
N_DIR = 2
N_GATE_COLS = N_DIR * 2 * M_HEADS
P_GROUPS = 4
POOL_WINDOWS = (2, 4, 8, 16)
GRID_W = 64
LANES = 128
MXU_DIM = 256
CHUNK = 256
POOL_UNROLL = 4
MIB = 1024 * 1024


def _cparams(sem, vmem_mib):
    return pltpu.CompilerParams(dimension_semantics=sem, vmem_limit_bytes=vmem_mib * MIB)


def _sigmoid(x):
    return 1.0 / (1.0 + jnp.exp(-x))


def _silu(x):
    return x * _sigmoid(x)


def _log_sigmoid(x):
    return jnp.minimum(x, 0.0) - jnp.log(1.0 + jnp.exp(-jnp.abs(x)))


def _dot_nt(a, b):
    return lax.dot_general(a, b, (((1,), (1,)), ((), ())), preferred_element_type=F32)


def _adaln_kernel(c_ref, w_ref, b_ref, o_ref):
    o_ref[...] = jnp.dot(_silu(c_ref[...]), w_ref[...], preferred_element_type=F32) + b_ref[...]


def _adaln(cc, ada_w, ada_b):
    rows, d = cc.shape
    n = ada_w.shape[1]
    tn = d
    return pl.pallas_call(
        _adaln_kernel,
        grid=(n // tn,),
        in_specs=[pl.BlockSpec((rows, d), lambda j: (0, 0)),
                  pl.BlockSpec((d, tn), lambda j: (0, j)),
                  pl.BlockSpec((1, tn), lambda j: (0, j))],
        out_specs=pl.BlockSpec((rows, tn), lambda j: (0, j)),
        out_shape=jax.ShapeDtypeStruct((rows, n), F32),
        compiler_params=_cparams(("arbitrary",), 32),
        name="adaln",
    )(cc, ada_w, ada_b.reshape(1, n))


def _inproj_kernel(x_ref, sc_ref, sh_ref, nw_ref, w_ref, wg_ref, y_ref, g_ref, xn_ref):
    @pl.when(pl.program_id(1) == 0)
    def _():
        x = x_ref[...]
        ms = jnp.mean(x * x, axis=-1, keepdims=True)
        xn = x * lax.rsqrt(ms + EPS) * nw_ref[...]
        xn = xn * (1.0 + sc_ref[0]) + sh_ref[0]
        xn_ref[...] = xn.astype(BF16)
        wg = wg_ref[...]
        wg = jnp.concatenate([wg, jnp.zeros((LANES - wg.shape[0], wg.shape[1]), wg.dtype)], axis=0)
        g_ref[...] = _dot_nt(xn_ref[...], wg)

    y_ref[...] = _dot_nt(xn_ref[...], w_ref[...]).astype(BF16)


def _in_proj(x2, scale, shift, norm_w, w, gate_row0, rows_per_mod, name, out0=0, nc=None):
    n, d = x2.shape
    nc = w.shape[0] - N_GATE_COLS if nc is None else nc
    tm = min(2048, rows_per_mod)
    tn = 1024
    per = rows_per_mod // tm
    assert out0 % tn == 0 and gate_row0 % tn == 0

    def w_row(i, j):
        r = out0 + j * tn
        return (pl.multiple_of(r + jnp.where(r >= gate_row0, N_GATE_COLS, 0), N_GATE_COLS), 0)

    return pl.pallas_call(
        _inproj_kernel,
        grid=(n // tm, nc // tn),
        in_specs=[pl.BlockSpec((tm, d), lambda i, j: (i, 0)),
                  pl.BlockSpec((1, 1, d), lambda i, j: (i // per, 0, 0)),
                  pl.BlockSpec((1, 1, d), lambda i, j: (i // per, 0, 0)),
                  pl.BlockSpec((1, d), lambda i, j: (0, 0)),
                  pl.BlockSpec((pl.Element(tn), pl.Element(d)), w_row),
                  pl.BlockSpec((pl.Element(N_GATE_COLS), pl.Element(d)), lambda i, j: (gate_row0, 0))],
        out_specs=[pl.BlockSpec((tm, tn), lambda i, j: (i, j)),
                   pl.BlockSpec((tm, LANES), lambda i, j: (i, 0))],
        out_shape=[jax.ShapeDtypeStruct((n, nc), BF16),
                   jax.ShapeDtypeStruct((n, LANES), F32)],
        scratch_shapes=[pltpu.VMEM((tm, d), BF16)],
        compiler_params=_cparams(("arbitrary", "arbitrary"), 56),
        name=name,
    )(x2, scale, shift, norm_w, w, w)


TERM_ROW0 = 16
TERM_PIECES = (3, 2, 2, 3, 2)
N_DH = N_DIR * M_HEADS


def _split(x, pieces):
    out = []
    for _ in range(pieces):
        p = x.astype(BF16)
        out.append(p)
        x = x - p.astype(F32)
    return out


def _chunk_scan(x, pos, chunk, op, reverse):
    t = x.shape[1]
    k = 1
    while k < chunk:
        if reverse:
            x = jnp.where(pos < chunk - k, op(x, pltpu.roll(x, t - k, 1)), x)
        else:
            x = jnp.where(pos >= k, op(x, pltpu.roll(x, k, 1)), x)
        k *= 2
    return x


def _gate_terms(li, pre_f, m0, chunk):
    rows, t = li.shape
    nc = t // chunk
    pos = lax.broadcasted_iota(jnp.int32, (rows, t), 1) % chunk
    rev = lax.broadcasted_iota(jnp.int32, (rows, t), 0) >= M_HEADS
    rev1 = lax.broadcasted_iota(jnp.int32, (rows, 1), 0) >= M_HEADS
    lf = _log_sigmoid(pre_f)
    b = jnp.where(rev, _chunk_scan(lf, pos, chunk, jnp.add, True), _chunk_scan(lf, pos, chunk, jnp.add, False))
    a = li - b
    cmax = jnp.where(rev, _chunk_scan(a, pos, chunk, jnp.maximum, True),
                     _chunk_scan(a, pos, chunk, jnp.maximum, False))

    def at_scan_end(x, c):
        lo = c * chunk
        return jnp.where(rev1, x[:, lo:lo + 1], x[:, lo + chunk - 1:lo + chunk])

    b_end = [at_scan_end(b, c) for c in range(nc)]
    a_max = [at_scan_end(cmax, c) for c in range(nc)]
    m_f, m_b = m0, m0
    m_in_f, m_in_b = [None] * nc, [None] * nc
    for i in range(nc):
        j = nc - 1 - i
        m_in_f[i] = m_f
        m_f = b_end[i] + jnp.maximum(m_f, a_max[i])
        m_in_b[j] = m_b
        m_b = b_end[j] + jnp.maximum(m_b, a_max[j])
    per_chunk = lambda vals: jnp.concatenate([jnp.broadcast_to(v, (rows, chunk)) for v in vals], axis=1)
    m_in = per_chunk([jnp.where(rev1, m_in_b[c], m_in_f[c]) for c in range(nc)])
    g = jnp.maximum(m_in, cmax)
    g_end = jnp.maximum(m_in, per_chunk(a_max))
    terms = (-g, jnp.exp(m_in - g), jnp.exp(a - g_end), -(b + g), jnp.exp(m_in - g_end))
    return terms, a, jnp.where(rev1, m_b, m_f)


def _term_rows(terms, t):
    ones = (lax.broadcasted_iota(jnp.int32, (TERM_ROW0, t), 0) < 3).astype(BF16)
    rows = [ones]
    for term, pieces in zip(terms, TERM_PIECES):
        rows += _split(term, pieces)
    used = TERM_ROW0 + N_DH * sum(TERM_PIECES)
    rows.append(jnp.zeros((LANES - used, t), BF16))
    return jnp.concatenate(rows, axis=0)


def _gates_kernel(ic_ref, fc_ref, i_ref, f_ref, bi_ref, bf_ref, rc_ref, r_ref, ra_ref, *, chunk):
    m0 = jnp.full((N_DH, 1), M_INIT, F32)
    terms_c, _, m_ctx = _gate_terms(ic_ref[0] + bi_ref[...], fc_ref[0] + bf_ref[...], m0, chunk)
    rc_ref[0] = _term_rows(terms_c, ic_ref.shape[2])
    terms, a, _ = _gate_terms(i_ref[0] + bi_ref[...], f_ref[0] + bf_ref[...], m_ctx, chunk)
    t = i_ref.shape[2]
    r_ref[0] = _term_rows(terms, t)
    a_pieces = [p.astype(F32) for p in _split(a, 3)]
    ra_ref[0] = jnp.zeros(ra_ref.shape[1:], F32)
    for dh in range(N_DH):
        for j, p in enumerate(a_pieces):
            ra_ref[0, dh, j:j + 1, :] = p[dh:dh + 1, :]


def _gates(gates_tok, gates_tok_c, gate_b, chunk):
    b, t, _ = gates_tok.shape
    tc = gates_tok_c.shape[1]

    def rows(g):
        g = jnp.transpose(g[:, :, :N_GATE_COLS], (0, 2, 1)).reshape(b, N_DIR, 2, M_HEADS, -1)
        return g[:, :, 0].reshape(b, N_DH, -1), g[:, :, 1].reshape(b, N_DH, -1)

    gb = gate_b.reshape(N_DIR, 2, M_HEADS)
    seq = lambda n: pl.BlockSpec((1, N_DH, n), lambda i: (i, 0, 0))
    col = pl.BlockSpec((N_DH, 1), lambda i: (0, 0))
    rc, r, ra = pl.pallas_call(
        functools.partial(_gates_kernel, chunk=chunk),
        grid=(b,),
        in_specs=[seq(tc), seq(tc), seq(t), seq(t), col, col],
        out_specs=[pl.BlockSpec((1, LANES, tc), lambda i: (i, 0, 0)),
                   pl.BlockSpec((1, LANES, t), lambda i: (i, 0, 0)),
                   pl.BlockSpec((1, N_DH, 8, t), lambda i: (i, 0, 0, 0))],
        out_shape=[jax.ShapeDtypeStruct((b, LANES, tc), BF16),
                   jax.ShapeDtypeStruct((b, LANES, t), BF16),
                   jax.ShapeDtypeStruct((b, N_DH, 8, t), F32)],
        compiler_params=_cparams(("arbitrary",), 32),
        name="gates",
    )(*rows(gates_tok_c), *rows(gates_tok), gb[:, 0].reshape(N_DH, 1), gb[:, 1].reshape(N_DH, 1))
    ra = ra.reshape(b, N_DIR, M_HEADS, 8, t // chunk, chunk).transpose(0, 1, 2, 4, 3, 5)
    return jnp.transpose(rc, (0, 2, 1)), jnp.transpose(r, (0, 2, 1)), ra


def _term_selectors(chunk):
    seld = np.zeros((M_HEADS, N_DIR, LANES, chunk), np.float32)
    selw = np.zeros((M_HEADS, N_DIR, LANES, 4 * LANES), np.float32)
    first = np.concatenate([[0], np.cumsum(TERM_PIECES)])
    for h in range(M_HEADS):
        for d in range(N_DIR):
            lane = lambda j: TERM_ROW0 + j * N_DH + d * M_HEADS + h
            for j in range(first[0], first[1]):
                seld[h, d, lane(j), :] = 1.0
            for q in range(4):
                for j in range(first[q + 1], first[q + 2]):
                    selw[h, d, lane(j), q * LANES:(q + 1) * LANES] = 1.0
    return jnp.asarray(seld, BF16), jnp.asarray(selw, BF16)


def _twice(x):
    return jnp.concatenate([x, x], axis=1)


def _mlstm_kernel(q_ref, k_ref, v_ref, o_ref, z_ref, p_ref, ra_ref, kc_ref, vc_ref, pc_ref,
                  seld_ref, selw_ref, hnw_ref, out_ref, st_ref, hs_ref, *, chunk):
    t = q_ref.shape[1]
    nc = t // chunk
    ncc = kc_ref.shape[1] // chunk
    dh = q_ref.shape[2]
    ones_col = jnp.ones((chunk, LANES), BF16)
    mean_col = jnp.full((dh, LANES), 1.0 / dh, BF16)
    t_idx = lax.broadcasted_iota(jnp.int32, (chunk, chunk), 0)
    s_idx = lax.broadcasted_iota(jnp.int32, (chunk, chunk), 1)

    st_ref[...] = jnp.zeros(st_ref.shape, F32)

    def row_terms(p, d):
        w = jnp.dot(p, selw_ref[0, d], preferred_element_type=F32)
        return [w[:, i * LANES:(i + 1) * LANES] for i in range(4)]

    def update(k, v, e_b, decay_b, d):
        ke = (k.astype(F32) * _twice(e_b)).astype(BF16)
        upd = lax.dot_general(ke, jnp.concatenate([v, ones_col], axis=1), (((0,), (0,)), ((), ())),
                              preferred_element_type=F32)
        st_ref[d] = jnp.concatenate([decay_b] * 3, axis=1) * st_ref[d] + upd

    for d in range(N_DIR):
        for i in range(ncc):
            rows = pl.ds((i if d == 0 else ncc - 1 - i) * chunk, chunk)
            _, e_b, _, decay_b = row_terms(pc_ref[0, rows, :], d)
            update(kc_ref[0, rows, :], vc_ref[0, rows, :], e_b, decay_b, d)

    def run_chunk(c, d):
        start = pl.multiple_of(c * chunk, chunk)
        rows = pl.ds(start, chunk)
        p, q, k, v = p_ref[0, rows, :], q_ref[0, rows, :], k_ref[0, rows, :], v_ref[0, rows, :]
        a_rows = jnp.concatenate([ra_ref[0, d, 0, c], jnp.zeros((8, chunk), F32)], axis=0).astype(BF16)
        sel = jnp.concatenate([a_rows, seld_ref[0, d, TERM_ROW0:, :]], axis=0)
        logw = jnp.dot(p, sel, preferred_element_type=F32)
        mask = (s_idx <= t_idx) if d == 0 else (s_idx >= t_idx)
        w_intra = jnp.exp(jnp.where(mask, logw, NEG_BIG))
        s = lax.dot_general(q, k, (((1,), (1,)), ((), ())), preferred_element_type=F32) * w_intra
        w_inter_b, e_b, neg_mt_b, decay_b = row_terms(p, d)
        xa = jnp.dot(q, st_ref[d].astype(BF16), preferred_element_type=F32)
        ya = jnp.dot(s.astype(BF16), jnp.concatenate([v, ones_col], axis=1), preferred_element_type=F32)
        den_b = w_inter_b * xa[:, dh:] + ya[:, dh:]
        inv_b = 1.0 / jnp.maximum(jnp.abs(den_b), jnp.exp(neg_mt_b))
        h = (_twice(w_inter_b) * xa[:, :dh] + ya[:, :dh]) * _twice(inv_b)
        update(k, v, e_b, decay_b, d)
        return rows, h

    def finalize(rows, h):
        ms_b = jnp.dot((h * h).astype(BF16), mean_col, preferred_element_type=F32)
        hn = h * _twice(lax.rsqrt(ms_b + EPS)) * hnw_ref[...]
        o = o_ref[0, rows, :].astype(F32)
        z = z_ref[0, rows, :].astype(F32)
        out_ref[0, rows, :] = (hn * _sigmoid(o) * _silu(z)).astype(out_ref.dtype)

    def first(i, carry):
        for d in range(N_DIR):
            rows, h = run_chunk(i if d == 0 else nc - 1 - i, d)
            hs_ref[rows, :] = h
        return carry

    def second(i, carry):
        for d in range(N_DIR):
            rows, h = run_chunk(i if d == 0 else nc - 1 - i, d)
            finalize(rows, hs_ref[rows, :] + h)
        return carry

    lax.fori_loop(0, nc // 2, first, 0)
    lax.fori_loop(nc // 2, nc, second, 0)


def _mlstm(y, yc, p, ra, pc, head_norm_w, chunk):
    b, t, _ = y.shape
    tc = yc.shape[1]
    dh = MXU_DIM
    assert chunk == dh and t % (2 * chunk) == 0 and tc % chunk == 0
    seld, selw = _term_selectors(chunk)
    blk = lambda col0: pl.BlockSpec((1, t, dh), lambda i, h, c=col0: (i, 0, c + h))
    blkc = lambda col0: pl.BlockSpec((1, tc, dh), lambda i, h, c=col0: (i, 0, c + h))
    return pl.pallas_call(
        functools.partial(_mlstm_kernel, chunk=chunk),
        grid=(b, M_HEADS),
        in_specs=[blk(0), blk(M_HEADS), blk(2 * M_HEADS), blk(3 * M_HEADS), blk(4 * M_HEADS),
                  pl.BlockSpec((1, t, LANES), lambda i, h: (i, 0, 0)),
                  pl.BlockSpec((1, N_DIR, 1, t // chunk, 8, chunk), lambda i, h: (i, 0, h, 0, 0, 0)),
                  blkc(0), blkc(M_HEADS),
                  pl.BlockSpec((1, tc, LANES), lambda i, h: (i, 0, 0)),
                  pl.BlockSpec((1, N_DIR, LANES, chunk), lambda i, h: (h, 0, 0, 0)),
                  pl.BlockSpec((1, N_DIR, LANES, 4 * LANES), lambda i, h: (h, 0, 0, 0)),
                  pl.BlockSpec((1, dh), lambda i, h: (0, h))],
        out_specs=pl.BlockSpec((1, t, dh), lambda i, h: (i, 0, h)),
        out_shape=jax.ShapeDtypeStruct((b, t, M_HEADS * dh), BF16),
        scratch_shapes=[pltpu.VMEM((N_DIR, dh, dh + LANES), F32),
                        pltpu.VMEM((t, dh), F32)],
        compiler_params=_cparams(("arbitrary", "arbitrary"), 48),
        name="mlstm",
    )(y, y, y, y, y, p, ra, yc, yc, pc, seld, selw, head_norm_w)


def _window_bounds(pos, w, length):
    return np.clip(pos - w // 2, 0, length), np.clip(pos + w - w // 2, 0, length)


def _col_pool_matrices(tile):
    pos = np.arange(GRID_W)
    mats = []
    for w in POOL_WINDOWS:
        lo, hi = _window_bounds(pos, w, GRID_W)
        band = ((pos[None, :] >= lo[:, None]) & (pos[None, :] < hi[:, None])).astype(np.float32)
        mats.append(np.kron(np.eye(tile // GRID_W, dtype=np.float32), band))
    return jnp.asarray(np.stack(mats), dtype=BF16)


def _pool_kernel(p_ref, z_ref, mc_ref, pw_ref, ps_ref, out_ref, pad_ref, scale_ref, *, tile, rows):
    t = p_ref.shape[1]
    gd = LANES
    pad = (max(POOL_WINDOWS) // 2) * GRID_W
    n_tiles = t // tile

    @pl.when(pl.program_id(0) == 0)
    def _():
        pad_ref[...] = jnp.zeros(pad_ref.shape, F32)
        tok = lax.broadcasted_iota(jnp.int32, (t, gd), 0)
        r = tok // GRID_W
        c = tok % GRID_W
        for g, w in enumerate(POOL_WINDOWS):
            cnt_r = jnp.minimum(r + (w - w // 2), rows) - jnp.maximum(r - w // 2, 0)
            cnt_c = jnp.minimum(c + (w - w // 2), GRID_W) - jnp.maximum(c - w // 2, 0)
            scale_ref[g] = 1.0 / (cnt_r.astype(F32) * cnt_c.astype(F32))

    for g, w in enumerate(POOL_WINDOWS):
        cols = pl.ds(g * gd, gd)

        def col_sum(i, carry):
            start = pl.multiple_of(i * tile, tile)
            pad_ref[pl.ds(pad + start, tile), :] = jnp.dot(
                mc_ref[g], p_ref[0, pl.ds(start, tile), cols], preferred_element_type=F32)
            return carry

        lax.fori_loop(0, n_tiles, col_sum, 0, unroll=POOL_UNROLL)

        def row_sum(i, carry):
            start = pl.multiple_of(i * tile, tile)
            acc = pad_ref[pl.ds(pad + start - (w // 2) * GRID_W, tile), :]
            for j in range(1 - w // 2, w - w // 2):
                acc = acc + pad_ref[pl.ds(pad + start + j * GRID_W, tile), :]
            mixed = acc * scale_ref[g, pl.ds(start, tile), :] - p_ref[0, pl.ds(start, tile), cols].astype(F32)
            mm = jnp.dot(mixed.astype(BF16), pw_ref[g], preferred_element_type=F32)
            zz = z_ref[0, pl.ds(start, tile), cols].astype(F32)
            out_ref[0, pl.ds(start, tile), cols] = (mm * ps_ref[:, cols] * _silu(zz)).astype(out_ref.dtype)
            return carry

        lax.fori_loop(0, n_tiles, row_sum, 0, unroll=POOL_UNROLL)


def _pool(y, pool_w, pool_scale, p_col0):
    b, t, _ = y.shape
    pw = P_GROUPS * LANES
    tile = MXU_DIM
    pad = (max(POOL_WINDOWS) // 2) * GRID_W
    return pl.pallas_call(
        functools.partial(_pool_kernel, tile=tile, rows=t // GRID_W),
        grid=(b,),
        in_specs=[pl.BlockSpec((1, t, pw), lambda i: (i, 0, p_col0 // pw)),
                  pl.BlockSpec((1, t, pw), lambda i: (i, 0, p_col0 // pw + 1)),
                  pl.BlockSpec((P_GROUPS, tile, tile), lambda i: (0, 0, 0)),
                  pl.BlockSpec((P_GROUPS, LANES, LANES), lambda i: (0, 0, 0)),
                  pl.BlockSpec((1, pw), lambda i: (0, 0))],
        out_specs=pl.BlockSpec((1, t, pw), lambda i: (i, 0, 0)),
        out_shape=jax.ShapeDtypeStruct((b, t, pw), BF16),
        scratch_shapes=[pltpu.VMEM((t + 2 * pad, LANES), F32),
                        pltpu.VMEM((P_GROUPS, t, LANES), F32)],
        compiler_params=_cparams(("arbitrary",), 48),
        name="pool",
    )(y, y, _col_pool_matrices(tile), pool_w, pool_scale)


def _merge_kernel(x_ref, ym_ref, yp_ref, gm_ref, gp_ref, gate_ref, wm_ref, wp_ref, wo_ref, fw_ref, o_ref):
    y_m = jnp.dot(ym_ref[...], wm_ref[...], preferred_element_type=F32)
    y_p = jnp.dot(yp_ref[...], wp_ref[...], preferred_element_type=F32)
    merged = _sigmoid(gm_ref[...].astype(F32)) * y_m + _sigmoid(gp_ref[...].astype(F32)) * y_p
    upd = jnp.dot(merged.astype(BF16), wo_ref[...], preferred_element_type=F32)
    xo = x_ref[...] + gate_ref[0] * upd
    o_ref[...] = xo * lax.rsqrt(jnp.mean(xo * xo, axis=-1, keepdims=True) + EPS) * fw_ref[...]


def _merge(x2, ym, yp, y, gate, wm, wp, wo, final_w, rows_per_mod, g_col0):
    n, d = x2.shape
    tm = 512
    per = rows_per_mod // tm
    pw = yp.shape[1]
    full = lambda shape: pl.BlockSpec(shape, lambda i: (0,) * len(shape))
    return pl.pallas_call(
        _merge_kernel,
        grid=(n // tm,),
        in_specs=[pl.BlockSpec((tm, d), lambda i: (i, 0)),
                  pl.BlockSpec((tm, d), lambda i: (i, 0)),
                  pl.BlockSpec((tm, pw), lambda i: (i, 0)),
                  pl.BlockSpec((tm, d), lambda i: (i, g_col0 // d)),
                  pl.BlockSpec((tm, d), lambda i: (i, g_col0 // d + 1)),
                  pl.BlockSpec((1, 1, d), lambda i: (i // per, 0, 0)),
                  full((d, d)), full((pw, d)), full((d, d)), full((1, d))],
        out_specs=pl.BlockSpec((tm, d), lambda i: (i, 0)),
        out_shape=jax.ShapeDtypeStruct((n, d), F32),
        compiler_params=_cparams(("arbitrary",), 48),
        name="merge",
    )(x2, ym, yp, y, y, gate, wm, wp, wo, final_w)


def kernel(x, c, ctx, c_ctx, norm_w, ada_w, ada_b, in_w, gate_b, head_norm_w, pool_w, pool_scale,
           branch_m_w, branch_p_w, out_w, final_norm_w):
    b, t, d = x.shape
    tc = ctx.shape[1]
    depth = norm_w.shape[0]
    assert depth == 1, "context stream update between layers is not implemented"
    mw = M_HEADS * MXU_DIM
    assert d == mw
    g0 = 5 * mw

    cc = jnp.zeros((2 * 8, d), F32).at[:b].set(c).at[b].set(c_ctx)
    mod = _adaln(cc, ada_w[0], ada_b[0])
    shift, scale, gate = mod[:, :d], mod[:, d:2 * d], mod[:, 2 * d:]

    wt = jnp.transpose(in_w[0])
    row = jnp.arange(wt.shape[0])[:, None]
    wt = (wt * jnp.where((row >= mw) & (row < 2 * mw), MXU_DIM ** -0.5, 1.0)).astype(BF16)
    nw = norm_w[0].reshape(1, d)

    y, gates_tok = _in_proj(x.reshape(b * t, d), scale[:b, None, :], shift[:b, None, :], nw,
                            wt, g0, t, "in_proj")
    yc, gates_tok_c = _in_proj(ctx.reshape(b * tc, d), scale[b:b + 1, None, :], shift[b:b + 1, None, :], nw,
                               wt, g0, b * tc, "in_proj_ctx", out0=mw, nc=2 * mw)
    y = y.reshape(b, t, -1)
    yc = yc.reshape(b, tc, -1)

    pc, p, ra = _gates(gates_tok.reshape(b, t, LANES), gates_tok_c.reshape(b, tc, LANES), gate_b[0], CHUNK)
    ym = _mlstm(y, yc, p, ra, pc, head_norm_w[0].reshape(1, mw), CHUNK)
    yp = _pool(y, pool_w[0].astype(BF16), pool_scale[0].reshape(1, -1), g0)

    out = _merge(x.reshape(b * t, d), ym.reshape(b * t, mw), yp.reshape(b * t, -1), y.reshape(b * t, -1),
                 gate[:b, None, :], branch_m_w[0].astype(BF16), branch_p_w[0].astype(BF16),
                 out_w[0].astype(BF16), final_norm_w.reshape(1, d), t, g0 + 2 * P_GROUPS * LANES)
    return out.reshape(b, t, d)
```

```python
import functools

import numpy as np
import jax
import jax.numpy as jnp
from jax import lax
from jax.experimental import pallas as pl
from jax.experimental.pallas import tpu as pltpu

F32 = jnp.float32
BF16 = jnp.bfloat16

EPS = 1e-6
M_INIT = -1e30
NEG_BIG = -1e30
M_HEADS = 4
N_DIR = 2
N_GATE_COLS = N_DIR * 2 * M_HEADS
P_GROUPS = 4
POOL_WINDOWS = (2, 4, 8, 16)
GRID_W = 64
LANES = 128
MXU_DIM = 256
CHUNK = 256
POOL_UNROLL = 4
SCAN_UNROLL = 2
MIB = 1024 * 1024


def _cparams(sem, vmem_mib):
    return pltpu.CompilerParams(dimension_semantics=sem, vmem_limit_bytes=vmem_mib * MIB)


def _sigmoid(x):
    return 1.0 / (1.0 + jnp.exp(-x))


def _silu(x):
    return x * _sigmoid(x)


def _log_sigmoid(x):
    return jnp.minimum(x, 0.0) - jnp.log(1.0 + jnp.exp(-jnp.abs(x)))


def _dot_nt(a, b):
    return lax.dot_general(a, b, (((1,), (1,)), ((), ())), preferred_element_type=F32)


def _adaln_kernel(c_ref, w_ref, b_ref, o_ref):
    o_ref[...] = jnp.dot(_silu(c_ref[...]), w_ref[...], preferred_element_type=F32) + b_ref[...]


def _adaln(cc, ada_w, ada_b):
    rows, d = cc.shape
    n = ada_w.shape[1]
    tn = d
    return pl.pallas_call(
        _adaln_kernel,
        grid=(n // tn,),
        in_specs=[pl.BlockSpec((rows, d), lambda j: (0, 0)),
                  pl.BlockSpec((d, tn), lambda j: (0, j)),
                  pl.BlockSpec((1, tn), lambda j: (0, j))],
        out_specs=pl.BlockSpec((rows, tn), lambda j: (0, j)),
        out_shape=jax.ShapeDtypeStruct((rows, n), F32),
        compiler_params=_cparams(("arbitrary",), 32),
        name="adaln",
    )(cc, ada_w, ada_b.reshape(1, n))


def _inproj_kernel(x_ref, sc_ref, sh_ref, nw_ref, w_ref, wg_ref, y_ref, g_ref, xn_ref):
    @pl.when(pl.program_id(1) == 0)
    def _():
        x = x_ref[...]
        ms = jnp.mean(x * x, axis=-1, keepdims=True)
        xn = x * lax.rsqrt(ms + EPS) * nw_ref[...]
        xn = xn * (1.0 + sc_ref[0]) + sh_ref[0]
        xn_ref[...] = xn.astype(BF16)
        wg = wg_ref[...]
        wg = jnp.concatenate([wg, jnp.zeros((LANES - wg.shape[0], wg.shape[1]), wg.dtype)], axis=0)
        g_ref[...] = _dot_nt(xn_ref[...], wg)

    y_ref[...] = _dot_nt(xn_ref[...], w_ref[...]).astype(BF16)


def _in_proj(x2, scale, shift, norm_w, w, gate_row0, rows_per_mod, name, out0=0, nc=None):
    n, d = x2.shape
    nc = w.shape[0] - N_GATE_COLS if nc is None else nc
    tm = min(2048, rows_per_mod)
    tn = 1024
    per = rows_per_mod // tm
    assert out0 % tn == 0 and gate_row0 % tn == 0

    def w_row(i, j):
        r = out0 + j * tn
        return (pl.multiple_of(r + jnp.where(r >= gate_row0, N_GATE_COLS, 0), N_GATE_COLS), 0)

    return pl.pallas_call(
        _inproj_kernel,
        grid=(n // tm, nc // tn),
        in_specs=[pl.BlockSpec((tm, d), lambda i, j: (i, 0)),
                  pl.BlockSpec((1, 1, d), lambda i, j: (i // per, 0, 0)),
                  pl.BlockSpec((1, 1, d), lambda i, j: (i // per, 0, 0)),
                  pl.BlockSpec((1, d), lambda i, j: (0, 0)),
                  pl.BlockSpec((pl.Element(tn), pl.Element(d)), w_row),
                  pl.BlockSpec((pl.Element(N_GATE_COLS), pl.Element(d)), lambda i, j: (gate_row0, 0))],
        out_specs=[pl.BlockSpec((tm, tn), lambda i, j: (i, j)),
                   pl.BlockSpec((tm, LANES), lambda i, j: (i, 0))],
        out_shape=[jax.ShapeDtypeStruct((n, nc), BF16),
                   jax.ShapeDtypeStruct((n, LANES), F32)],
        scratch_shapes=[pltpu.VMEM((tm, d), BF16)],
        compiler_params=_cparams(("arbitrary", "arbitrary"), 56),
        name=name,
    )(x2, scale, shift, norm_w, w, w)


N_DH = N_DIR * M_HEADS
TERM_PIECES = (3, 2, 2, 3)


def _split(x, pieces):
    out = []
    for _ in range(pieces):
        p = x.astype(BF16)
        out.append(p)
        x = x - p.astype(F32)
    return out


def _chunk_scan(x, pos, chunk, op, reverse):
    t = x.shape[1]
    k = 1
    while k < chunk:
        if reverse:
            x = jnp.where(pos < chunk - k, op(x, pltpu.roll(x, t - k, 1)), x)
        else:
            x = jnp.where(pos >= k, op(x, pltpu.roll(x, k, 1)), x)
        k *= 2
    return x


def _gate_terms(li, pre_f, m0, chunk):
    rows, t = li.shape
    nc = t // chunk
    pos = lax.broadcasted_iota(jnp.int32, (rows, t), 1) % chunk
    rev = lax.broadcasted_iota(jnp.int32, (rows, t), 0) >= M_HEADS
    rev1 = lax.broadcasted_iota(jnp.int32, (rows, 1), 0) >= M_HEADS
    lf = _log_sigmoid(pre_f)
    b = jnp.where(rev, _chunk_scan(lf, pos, chunk, jnp.add, True), _chunk_scan(lf, pos, chunk, jnp.add, False))
    a = li - b
    cmax = jnp.where(rev, _chunk_scan(a, pos, chunk, jnp.maximum, True),
                     _chunk_scan(a, pos, chunk, jnp.maximum, False))

    def at_scan_end(x, c):
        lo = c * chunk
        return jnp.where(rev1, x[:, lo:lo + 1], x[:, lo + chunk - 1:lo + chunk])

    b_end = [at_scan_end(b, c) for c in range(nc)]
    a_max = [at_scan_end(cmax, c) for c in range(nc)]
    m_f, m_b = m0, m0
    m_in_f, m_in_b = [None] * nc, [None] * nc
    for i in range(nc):
        j = nc - 1 - i
        m_in_f[i] = m_f
        m_f = b_end[i] + jnp.maximum(m_f, a_max[i])
        m_in_b[j] = m_b
        m_b = b_end[j] + jnp.maximum(m_b, a_max[j])
    m_in = [jnp.where(rev1, m_in_b[c], m_in_f[c]) for c in range(nc)]
    per_chunk = lambda vals: jnp.concatenate([jnp.broadcast_to(v, (rows, chunk)) for v in vals], axis=1)
    m_in_t = per_chunk(m_in)
    g = jnp.maximum(m_in_t, cmax)
    g_end = [jnp.maximum(m_in[c], a_max[c]) for c in range(nc)]
    pieces = []
    for term, n in zip((-g, jnp.exp(m_in_t - g), jnp.exp(a - per_chunk(g_end)), -(b + g)), TERM_PIECES):
        pieces += _split(term, n)
    pieces.append(jnp.zeros((LANES - rows * sum(TERM_PIECES), t), BF16))
    decay = jnp.concatenate([jnp.exp(m_in[c] - g_end[c]) for c in range(nc)], axis=1)
    return jnp.concatenate(pieces, axis=0), a, decay, jnp.where(rev1, m_b, m_f)


def _gates_kernel(ic_ref, fc_ref, i_ref, f_ref, bi_ref, bf_ref, rc_ref, dc_ref, r_ref, a_ref, d_ref, *, chunk):
    m0 = jnp.full((N_DH, 1), M_INIT, F32)
    rc_ref[0], _, dc_ref[0], m_ctx = _gate_terms(ic_ref[0] + bi_ref[...], fc_ref[0] + bf_ref[...], m0, chunk)
    r_ref[0], a_ref[0], d_ref[0], _ = _gate_terms(i_ref[0] + bi_ref[...], f_ref[0] + bf_ref[...], m_ctx, chunk)


def _gates(gates_tok, gates_tok_c, gate_b, chunk):
    b, t, _ = gates_tok.shape
    tc = gates_tok_c.shape[1]
    nc, ncc = t // chunk, tc // chunk

    def rows(g):
        g = jnp.transpose(g[:, :, :N_GATE_COLS], (0, 2, 1)).reshape(b, N_DIR, 2, M_HEADS, -1)
        return g[:, :, 0].reshape(b, N_DH, -1), g[:, :, 1].reshape(b, N_DH, -1)

    gb = gate_b.reshape(N_DIR, 2, M_HEADS)
    spec = lambda r, n: pl.BlockSpec((1, r, n), lambda i: (i, 0, 0))
    col = pl.BlockSpec((N_DH, 1), lambda i: (0, 0))
    rc, dc, r, a, dec = pl.pallas_call(
        functools.partial(_gates_kernel, chunk=chunk),
        grid=(b,),
        in_specs=[spec(N_DH, tc), spec(N_DH, tc), spec(N_DH, t), spec(N_DH, t), col, col],
        out_specs=[spec(LANES, tc), spec(N_DH, ncc), spec(LANES, t), spec(N_DH, t), spec(N_DH, nc)],
        out_shape=[jax.ShapeDtypeStruct((b, LANES, tc), BF16), jax.ShapeDtypeStruct((b, N_DH, ncc), F32),
                   jax.ShapeDtypeStruct((b, LANES, t), BF16), jax.ShapeDtypeStruct((b, N_DH, t), F32),
                   jax.ShapeDtypeStruct((b, N_DH, nc), F32)],
        compiler_params=_cparams(("arbitrary",), 32),
        name="gates",
    )(*rows(gates_tok_c), *rows(gates_tok), gb[:, 0].reshape(N_DH, 1), gb[:, 1].reshape(N_DH, 1))
    a = a.reshape(b, N_DH, nc, chunk).transpose(0, 2, 1, 3)
    decay = jnp.concatenate([dc, dec], axis=2).reshape(b * N_DH, ncc + nc)
    return jnp.transpose(rc, (0, 2, 1)), jnp.transpose(r, (0, 2, 1)), a, decay


def _twice(x):
    return jnp.concatenate([x, x], axis=1)


def _mlstm_kernel(dec_ref, q_ref, k_ref, v_ref, o_ref, z_ref, p_ref, a_ref, kc_ref, vc_ref, pc_ref,
                  sel_ref, hnw_ref, out_ref, ct_ref, n_ref, hs_ref, *, chunk):
    t = q_ref.shape[1]
    nc = t // chunk
    ncc = kc_ref.shape[1] // chunk
    bi, hi = pl.program_id(0), pl.program_id(1)
    t_idx = lax.broadcasted_iota(jnp.int32, (chunk, chunk), 0)
    s_idx = lax.broadcasted_iota(jnp.int32, (chunk, chunk), 1)

    ct_ref[...] = jnp.zeros(ct_ref.shape, F32)
    n_ref[...] = jnp.zeros(n_ref.shape, F32)

    def row_terms(p, d):
        w = jnp.dot(p, sel_ref[0, d], preferred_element_type=F32)
        return [w[:, i * LANES:(i + 1) * LANES] for i in range(len(TERM_PIECES))]

    def update(k, v, e_b, decay, d):
        ke = k.astype(F32) * _twice(e_b)
        upd = lax.dot_general(ke.astype(BF16), v, (((0,), (0,)), ((), ())), preferred_element_type=F32)
        ct_ref[d] = decay * ct_ref[d] + upd
        n_ref[d] = decay * n_ref[d] + jnp.sum(ke, axis=0, keepdims=True)

    def decay_of(c, d):
        return dec_ref[bi * N_DH + d * M_HEADS + hi, c]

    for d in range(N_DIR):
        for i in range(ncc):
            c = i if d == 0 else ncc - 1 - i
            rows = pl.ds(c * chunk, chunk)
            e_b = row_terms(pc_ref[0, rows, :], d)[2]
            update(kc_ref[0, rows, :], vc_ref[0, rows, :], e_b, decay_of(c, d), d)

    def run_chunk(c, d):
        rows = pl.ds(pl.multiple_of(c * chunk, chunk), chunk)
        q, k, v = q_ref[0, rows, :], k_ref[0, rows, :], v_ref[0, rows, :]
        neg_g_b, w_inter_b, e_b, neg_mt_b = row_terms(p_ref[0, rows, :], d)
        a_row = a_ref[0, c, pl.ds(d * M_HEADS + hi, 1), :]
        mask = (s_idx <= t_idx) if d == 0 else (s_idx >= t_idx)
        w_intra = jnp.exp(jnp.where(mask, a_row + _twice(neg_g_b), NEG_BIG))
        s = _dot_nt(q, k) * w_intra
        num = (_twice(w_inter_b) * jnp.dot(q, ct_ref[d].astype(BF16), preferred_element_type=F32)
               + jnp.dot(s.astype(BF16), v, preferred_element_type=F32))
        den_b = (w_inter_b * jnp.sum(q.astype(F32) * n_ref[d], axis=-1, keepdims=True)
                 + jnp.sum(s, axis=-1, keepdims=True))
        h = num * _twice(1.0 / jnp.maximum(jnp.abs(den_b), jnp.exp(neg_mt_b)))
        update(k, v, e_b, decay_of(ncc + c, d), d)
        return rows, h

    def finalize(rows, h):
        hn = h * lax.rsqrt(jnp.mean(h * h, axis=-1, keepdims=True) + EPS) * hnw_ref[...]
        o = o_ref[0, rows, :].astype(F32)
        z = z_ref[0, rows, :].astype(F32)
        out_ref[0, rows, :] = (hn * _sigmoid(o) * _silu(z)).astype(out_ref.dtype)

    def first(i, carry):
        for d in range(N_DIR):
            rows, h = run_chunk(i if d == 0 else nc - 1 - i, d)
            hs_ref[rows, :] = h
        return carry

    def second(i, carry):
        for d in range(N_DIR):
            rows, h = run_chunk(i if d == 0 else nc - 1 - i, d)
            finalize(rows, hs_ref[rows, :] + h)
        return carry

    lax.fori_loop(0, nc // 2, first, 0, unroll=SCAN_UNROLL)
    lax.fori_loop(nc // 2, nc, second, 0, unroll=SCAN_UNROLL)


def _term_selectors():
    sel = np.zeros((M_HEADS, N_DIR, LANES, len(TERM_PIECES) * LANES), np.float32)
    first = np.concatenate([[0], np.cumsum(TERM_PIECES)])
    for h in range(M_HEADS):
        for d in range(N_DIR):
            for q in range(len(TERM_PIECES)):
                for j in range(first[q], first[q + 1]):
                    sel[h, d, j * N_DH + d * M_HEADS + h, q * LANES:(q + 1) * LANES] = 1.0
    return jnp.asarray(sel, BF16)


def _mlstm(y, yc, p, a, pc, decay, head_norm_w, chunk):
    b, t, _ = y.shape
    tc = yc.shape[1]
    dh = MXU_DIM
    sel = _term_selectors()
    assert t % (2 * chunk) == 0 and tc % chunk == 0
    blk = lambda col0: pl.BlockSpec((1, t, dh), lambda i, h, c=col0: (i, 0, c + h))
    blkc = lambda col0: pl.BlockSpec((1, tc, dh), lambda i, h, c=col0: (i, 0, c + h))
    return pl.pallas_call(
        functools.partial(_mlstm_kernel, chunk=chunk),
        grid=(b, M_HEADS),
        in_specs=[pl.BlockSpec(memory_space=pltpu.SMEM),
                  blk(0), blk(M_HEADS), blk(2 * M_HEADS), blk(3 * M_HEADS), blk(4 * M_HEADS),
                  pl.BlockSpec((1, t, LANES), lambda i, h: (i, 0, 0)),
                  pl.BlockSpec((1, t // chunk, N_DH, chunk), lambda i, h: (i, 0, 0, 0)),
                  blkc(0), blkc(M_HEADS),
                  pl.BlockSpec((1, tc, LANES), lambda i, h: (i, 0, 0)),
                  pl.BlockSpec((1, N_DIR) + sel.shape[2:], lambda i, h: (h, 0, 0, 0)),
                  pl.BlockSpec((1, dh), lambda i, h: (0, h))],
        out_specs=pl.BlockSpec((1, t, dh), lambda i, h: (i, 0, h)),
        out_shape=jax.ShapeDtypeStruct((b, t, M_HEADS * dh), BF16),
        scratch_shapes=[pltpu.VMEM((N_DIR, dh, dh), F32),
                        pltpu.VMEM((N_DIR, 1, dh), F32),
                        pltpu.VMEM((t, dh), F32)],
        compiler_params=_cparams(("arbitrary", "arbitrary"), 48),
        name="mlstm",
    )(decay, y, y, y, y, y, p, a, yc, yc, pc, sel, head_norm_w)


def _window_bounds(pos, w, length):
    return np.clip(pos - w // 2, 0, length), np.clip(pos + w - w // 2, 0, length)


def _col_pool_matrices(tile):
    pos = np.arange(GRID_W)
    mats = []
    for w in POOL_WINDOWS:
        lo, hi = _window_bounds(pos, w, GRID_W)
        band = ((pos[None, :] >= lo[:, None]) & (pos[None, :] < hi[:, None])).astype(np.float32)
        mats.append(np.kron(np.eye(tile // GRID_W, dtype=np.float32), band))
    return jnp.asarray(np.stack(mats), dtype=BF16)


def _pool_kernel(p_ref, z_ref, mc_ref, pw_ref, ps_ref, out_ref, pad_ref, scale_ref, *, tile, rows):
    t = p_ref.shape[1]
    gd = LANES
    pad = (max(POOL_WINDOWS) // 2) * GRID_W
    n_tiles = t // tile

    @pl.when(pl.program_id(0) == 0)
    def _():
        pad_ref[...] = jnp.zeros(pad_ref.shape, F32)
        tok = lax.broadcasted_iota(jnp.int32, (t, gd), 0)
        r = tok // GRID_W
        c = tok % GRID_W
        for g, w in enumerate(POOL_WINDOWS):
            cnt_r = jnp.minimum(r + (w - w // 2), rows) - jnp.maximum(r - w // 2, 0)
            cnt_c = jnp.minimum(c + (w - w // 2), GRID_W) - jnp.maximum(c - w // 2, 0)
            scale_ref[g] = 1.0 / (cnt_r.astype(F32) * cnt_c.astype(F32))

    for g, w in enumerate(POOL_WINDOWS):
        cols = pl.ds(g * gd, gd)

        def col_sum(i, carry):
            start = pl.multiple_of(i * tile, tile)
            pad_ref[pl.ds(pad + start, tile), :] = jnp.dot(
                mc_ref[g], p_ref[0, pl.ds(start, tile), cols], preferred_element_type=F32)
            return carry

        lax.fori_loop(0, n_tiles, col_sum, 0, unroll=POOL_UNROLL)

        def row_sum(i, carry):
            start = pl.multiple_of(i * tile, tile)
            acc = pad_ref[pl.ds(pad + start - (w // 2) * GRID_W, tile), :]
            for j in range(1 - w // 2, w - w // 2):
                acc = acc + pad_ref[pl.ds(pad + start + j * GRID_W, tile), :]
            mixed = acc * scale_ref[g, pl.ds(start, tile), :] - p_ref[0, pl.ds(start, tile), cols].astype(F32)
            mm = jnp.dot(mixed.astype(BF16), pw_ref[g], preferred_element_type=F32)
            zz = z_ref[0, pl.ds(start, tile), cols].astype(F32)
            out_ref[0, pl.ds(start, tile), cols] = (mm * ps_ref[:, cols] * _silu(zz)).astype(out_ref.dtype)
            return carry

        lax.fori_loop(0, n_tiles, row_sum, 0, unroll=POOL_UNROLL)


def _pool(y, pool_w, pool_scale, p_col0):
    b, t, _ = y.shape
    pw = P_GROUPS * LANES
    tile = MXU_DIM
    pad = (max(POOL_WINDOWS) // 2) * GRID_W
    return pl.pallas_call(
        functools.partial(_pool_kernel, tile=tile, rows=t // GRID_W),
        grid=(b,),
        in_specs=[pl.BlockSpec((1, t, pw), lambda i: (i, 0, p_col0 // pw)),
                  pl.BlockSpec((1, t, pw), lambda i: (i, 0, p_col0 // pw + 1)),
                  pl.BlockSpec((P_GROUPS, tile, tile), lambda i: (0, 0, 0)),
                  pl.BlockSpec((P_GROUPS, LANES, LANES), lambda i: (0, 0, 0)),
                  pl.BlockSpec((1, pw), lambda i: (0, 0))],
        out_specs=pl.BlockSpec((1, t, pw), lambda i: (i, 0, 0)),
        out_shape=jax.ShapeDtypeStruct((b, t, pw), BF16),
        scratch_shapes=[pltpu.VMEM((t + 2 * pad, LANES), F32),
                        pltpu.VMEM((P_GROUPS, t, LANES), F32)],
        compiler_params=_cparams(("arbitrary",), 48),
        name="pool",
    )(y, y, _col_pool_matrices(tile), pool_w, pool_scale)


def _merge_kernel(x_ref, ym_ref, yp_ref, gm_ref, gp_ref, gate_ref, wm_ref, wp_ref, wo_ref, fw_ref, o_ref):
    y_m = jnp.dot(ym_ref[...], wm_ref[...], preferred_element_type=F32)
    y_p = jnp.dot(yp_ref[...], wp_ref[...], preferred_element_type=F32)
    merged = _sigmoid(gm_ref[...].astype(F32)) * y_m + _sigmoid(gp_ref[...].astype(F32)) * y_p
    upd = jnp.dot(merged.astype(BF16), wo_ref[...], preferred_element_type=F32)
    xo = x_ref[...] + gate_ref[0] * upd
    o_ref[...] = xo * lax.rsqrt(jnp.mean(xo * xo, axis=-1, keepdims=True) + EPS) * fw_ref[...]


def _merge(x2, ym, yp, y, gate, wm, wp, wo, final_w, rows_per_mod, g_col0):
    n, d = x2.shape
    tm = 512
    per = rows_per_mod // tm
    pw = yp.shape[1]
    full = lambda shape: pl.BlockSpec(shape, lambda i: (0,) * len(shape))
    return pl.pallas_call(
        _merge_kernel,
        grid=(n // tm,),
        in_specs=[pl.BlockSpec((tm, d), lambda i: (i, 0)),
                  pl.BlockSpec((tm, d), lambda i: (i, 0)),
                  pl.BlockSpec((tm, pw), lambda i: (i, 0)),
                  pl.BlockSpec((tm, d), lambda i: (i, g_col0 // d)),
                  pl.BlockSpec((tm, d), lambda i: (i, g_col0 // d + 1)),
                  pl.BlockSpec((1, 1, d), lambda i: (i // per, 0, 0)),
                  full((d, d)), full((pw, d)), full((d, d)), full((1, d))],
        out_specs=pl.BlockSpec((tm, d), lambda i: (i, 0)),
        out_shape=jax.ShapeDtypeStruct((n, d), F32),
        compiler_params=_cparams(("arbitrary",), 48),
        name="merge",
    )(x2, ym, yp, y, y, gate, wm, wp, wo, final_w)


def kernel(x, c, ctx, c_ctx, norm_w, ada_w, ada_b, in_w, gate_b, head_norm_w, pool_w, pool_scale,
           branch_m_w, branch_p_w, out_w, final_norm_w):
    b, t, d = x.shape
    tc = ctx.shape[1]
    depth = norm_w.shape[0]
    assert depth == 1, "context stream update between layers is not implemented"
    mw = M_HEADS * MXU_DIM
    assert d == mw
    g0 = 5 * mw

    cc = jnp.zeros((2 * 8, d), F32).at[:b].set(c).at[b].set(c_ctx)
    mod = _adaln(cc, ada_w[0], ada_b[0])
    shift, scale, gate = mod[:, :d], mod[:, d:2 * d], mod[:, 2 * d:]

    wt = jnp.transpose(in_w[0])
    row = jnp.arange(wt.shape[0])[:, None]
    wt = (wt * jnp.where((row >= mw) & (row < 2 * mw), MXU_DIM ** -0.5, 1.0)).astype(BF16)
    nw = norm_w[0].reshape(1, d)

    y, gates_tok = _in_proj(x.reshape(b * t, d), scale[:b, None, :], shift[:b, None, :], nw,
                            wt, g0, t, "in_proj")
    yc, gates_tok_c = _in_proj(ctx.reshape(b * tc, d), scale[b:b + 1, None, :], shift[b:b + 1, None, :], nw,
                               wt, g0, b * tc, "in_proj_ctx", out0=mw, nc=2 * mw)
    y = y.reshape(b, t, -1)
    yc = yc.reshape(b, tc, -1)

    pc, p, a, decay = _gates(gates_tok.reshape(b, t, LANES), gates_tok_c.reshape(b, tc, LANES), gate_b[0], CHUNK)
    ym = _mlstm(y, yc, p, a, pc, decay, head_norm_w[0].reshape(1, mw), CHUNK)
    yp = _pool(y, pool_w[0].astype(BF16), pool_scale[0].reshape(1, -1), g0)

    out = _merge(x.reshape(b * t, d), ym.reshape(b * t, mw), yp.reshape(b * t, -1), y.reshape(b * t, -1),
                 gate[:b, None, :], branch_m_w[0].astype(BF16), branch_p_w[0].astype(BF16),
                 out_w[0].astype(BF16), final_norm_w.reshape(1, d), t, g0 + 2 * P_GROUPS * LANES)
    return out.reshape(b, t, d)
```

```python
import functools

import numpy as np
import jax
import jax.numpy as jnp
from jax import lax
from jax.experimental import pallas as pl
from jax.experimental.pallas import tpu as pltpu

F32 = jnp.float32
BF16 = jnp.bfloat16

EPS = 1e-6
LOG2E = 1.4426950408889634
M_INIT = -1e30
NEG_BIG = -1e30
M_HEADS = 4
N_DIR = 2
N_GATE_COLS = N_DIR * 2 * M_HEADS
P_GROUPS = 4
POOL_WINDOWS = (2, 4, 8, 16)
GRID_W = 64
LANES = 128
MXU_DIM = 256
CHUNK = 256
POOL_UNROLL = 4
SCAN_UNROLL = 2
MIB = 1024 * 1024


def _cparams(sem, vmem_mib):
    return pltpu.CompilerParams(dimension_semantics=sem, vmem_limit_bytes=vmem_mib * MIB)


def _sigmoid(x):
    return 1.0 / (1.0 + jnp.exp(-x))


def _silu(x):
    return x * _sigmoid(x)


def _log_sigmoid(x):
    return jnp.minimum(x, 0.0) - jnp.log(1.0 + jnp.exp(-jnp.abs(x)))


def _dot_nt(a, b):
    return lax.dot_general(a, b, (((1,), (1,)), ((), ())), preferred_element_type=F32)


def _adaln_kernel(c_ref, w_ref, b_ref, o_ref):
    o_ref[...] = jnp.dot(_silu(c_ref[...]), w_ref[...], preferred_element_type=F32) + b_ref[...]


def _adaln(cc, ada_w, ada_b):
    rows, d = cc.shape
    n = ada_w.shape[1]
    tn = d
    return pl.pallas_call(
        _adaln_kernel,
        grid=(n // tn,),
        in_specs=[pl.BlockSpec((rows, d), lambda j: (0, 0)),
                  pl.BlockSpec((d, tn), lambda j: (0, j)),
                  pl.BlockSpec((1, tn), lambda j: (0, j))],
        out_specs=pl.BlockSpec((rows, tn), lambda j: (0, j)),
        out_shape=jax.ShapeDtypeStruct((rows, n), F32),
        compiler_params=_cparams(("arbitrary",), 32),
        name="adaln",
    )(cc, ada_w, ada_b.reshape(1, n))


def _sigmoid_t(x):
    return 0.5 * jnp.tanh(0.5 * x) + 0.5


MAIN_TILE_ACTS = ("id", "id", "id", "gate_keep", "gate_apply", "half_silu", "sigmoid", "sigmoid")


def _inproj_kernel(x_ref, sc_ref, sh_ref, nw_ref, w_ref, wg_ref, y_ref, g_ref, xn_ref, keep_ref, *, acts):
    j = pl.program_id(1)

    @pl.when(j == 0)
    def _():
        x = x_ref[...]
        ms = jnp.mean(x * x, axis=-1, keepdims=True)
        xn = x * lax.rsqrt(ms + EPS) * nw_ref[...]
        xn = xn * (1.0 + sc_ref[0]) + sh_ref[0]
        xn_ref[...] = xn.astype(BF16)
        wg = wg_ref[...]
        wg = jnp.concatenate([wg, jnp.zeros((LANES - wg.shape[0], wg.shape[1]), wg.dtype)], axis=0)
        g_ref[...] = _dot_nt(xn_ref[...], wg)

    def tile(act):
        acc = _dot_nt(xn_ref[...], w_ref[...])
        if act == "gate_keep":
            keep_ref[...] = _sigmoid_t(acc).astype(BF16)
        elif act == "gate_apply":
            acc = keep_ref[...].astype(F32) * (acc * _sigmoid_t(acc))
        elif act == "half_silu":
            half = acc.shape[1] // 2
            acc = jnp.concatenate([acc[:, :half], acc[:, half:] * _sigmoid_t(acc[:, half:])], axis=1)
        elif act == "sigmoid":
            acc = _sigmoid_t(acc)
        y_ref[...] = acc.astype(BF16)

    for act in sorted(set(acts)):
        steps = [s for s, a in enumerate(acts) if a == act]
        lo, hi = steps[0], steps[-1]
        assert steps == list(range(lo, hi + 1))
        pl.when((j >= lo) & (j <= hi))(functools.partial(tile, act))


def _in_proj(x2, scale, shift, norm_w, w, gate_row0, rows_per_mod, name, out0=0, nc=None):
    n, d = x2.shape
    nc = w.shape[0] - N_GATE_COLS if nc is None else nc
    tm = min(2048, rows_per_mod)
    tn = 1024
    per = rows_per_mod // tm
    assert out0 % tn == 0 and gate_row0 % tn == 0
    acts = MAIN_TILE_ACTS[out0 // tn:(out0 + nc) // tn]
    keep_shape = (tm, tn) if "gate_keep" in acts else (16, LANES)

    def w_row(i, j):
        r = out0 + j * tn
        return (pl.multiple_of(r + jnp.where(r >= gate_row0, N_GATE_COLS, 0), N_GATE_COLS), 0)

    return pl.pallas_call(
        functools.partial(_inproj_kernel, acts=acts),
        grid=(n // tm, nc // tn),
        in_specs=[pl.BlockSpec((tm, d), lambda i, j: (i, 0)),
                  pl.BlockSpec((1, 1, d), lambda i, j: (i // per, 0, 0)),
                  pl.BlockSpec((1, 1, d), lambda i, j: (i // per, 0, 0)),
                  pl.BlockSpec((1, d), lambda i, j: (0, 0)),
                  pl.BlockSpec((pl.Element(tn), pl.Element(d)), w_row),
                  pl.BlockSpec((pl.Element(N_GATE_COLS), pl.Element(d)), lambda i, j: (gate_row0, 0))],
        out_specs=[pl.BlockSpec((tm, tn), lambda i, j: (i, j)),
                   pl.BlockSpec((tm, LANES), lambda i, j: (i, 0))],
        out_shape=[jax.ShapeDtypeStruct((n, nc), BF16),
                   jax.ShapeDtypeStruct((n, LANES), F32)],
        scratch_shapes=[pltpu.VMEM((tm, d), BF16), pltpu.VMEM(keep_shape, BF16)],
        compiler_params=_cparams(("arbitrary", "arbitrary"), 56),
        name=name,
    )(x2, scale, shift, norm_w, w, w)


N_DH = N_DIR * M_HEADS
TERM_PIECES = (3, 2, 2, 3)


def _split(x, pieces):
    out = []
    for _ in range(pieces):
        p = x.astype(BF16)
        out.append(p)
        x = x - p.astype(F32)
    return out


def _chunk_scan(x, pos, chunk, op, reverse):
    t = x.shape[1]
    k = 1
    while k < chunk:
        if reverse:
            x = jnp.where(pos < chunk - k, op(x, pltpu.roll(x, t - k, 1)), x)
        else:
            x = jnp.where(pos >= k, op(x, pltpu.roll(x, k, 1)), x)
        k *= 2
    return x


def _gate_terms(li, pre_f, m0, chunk):
    rows, t = li.shape
    nc = t // chunk
    pos = lax.broadcasted_iota(jnp.int32, (rows, t), 1) % chunk
    rev = lax.broadcasted_iota(jnp.int32, (rows, t), 0) >= M_HEADS
    rev1 = lax.broadcasted_iota(jnp.int32, (rows, 1), 0) >= M_HEADS
    lf = _log_sigmoid(pre_f)
    b = jnp.where(rev, _chunk_scan(lf, pos, chunk, jnp.add, True), _chunk_scan(lf, pos, chunk, jnp.add, False))
    a = li - b
    cmax = jnp.where(rev, _chunk_scan(a, pos, chunk, jnp.maximum, True),
                     _chunk_scan(a, pos, chunk, jnp.maximum, False))

    def at_scan_end(x, c):
        lo = c * chunk
        return jnp.where(rev1, x[:, lo:lo + 1], x[:, lo + chunk - 1:lo + chunk])

    b_end = [at_scan_end(b, c) for c in range(nc)]
    a_max = [at_scan_end(cmax, c) for c in range(nc)]
    m_f, m_b = m0, m0
    m_in_f, m_in_b = [None] * nc, [None] * nc
    for i in range(nc):
        j = nc - 1 - i
        m_in_f[i] = m_f
        m_f = b_end[i] + jnp.maximum(m_f, a_max[i])
        m_in_b[j] = m_b
        m_b = b_end[j] + jnp.maximum(m_b, a_max[j])
    m_in = [jnp.where(rev1, m_in_b[c], m_in_f[c]) for c in range(nc)]
    per_chunk = lambda vals: jnp.concatenate([jnp.broadcast_to(v, (rows, chunk)) for v in vals], axis=1)
    m_in_t = per_chunk(m_in)
    g = jnp.maximum(m_in_t, cmax)
    g_end = [jnp.maximum(m_in[c], a_max[c]) for c in range(nc)]
    pieces = []
    terms = (-g * LOG2E, jnp.exp(m_in_t - g), jnp.exp(a - per_chunk(g_end)), -(b + g) * LOG2E)
    for term, n in zip(terms, TERM_PIECES):
        pieces += _split(term, n)
    pieces.append(jnp.zeros((LANES - rows * sum(TERM_PIECES), t), BF16))
    decay = jnp.concatenate([jnp.exp(m_in[c] - g_end[c]) for c in range(nc)], axis=1)
    return jnp.concatenate(pieces, axis=0), a * LOG2E, decay, jnp.where(rev1, m_b, m_f)


def _gates_kernel(ic_ref, fc_ref, i_ref, f_ref, bi_ref, bf_ref, rc_ref, dc_ref, r_ref, a_ref, d_ref, *, chunk):
    m0 = jnp.full((N_DH, 1), M_INIT, F32)
    rc_ref[0], _, dc_ref[0], m_ctx = _gate_terms(ic_ref[0] + bi_ref[...], fc_ref[0] + bf_ref[...], m0, chunk)
    r_ref[0], a_ref[0], d_ref[0], _ = _gate_terms(i_ref[0] + bi_ref[...], f_ref[0] + bf_ref[...], m_ctx, chunk)


def _gates(gates_tok, gates_tok_c, gate_b, chunk):
    b, t, _ = gates_tok.shape
    tc = gates_tok_c.shape[1]
    nc, ncc = t // chunk, tc // chunk

    def rows(g):
        g = jnp.transpose(g[:, :, :N_GATE_COLS], (0, 2, 1)).reshape(b, N_DIR, 2, M_HEADS, -1)
        return g[:, :, 0].reshape(b, N_DH, -1), g[:, :, 1].reshape(b, N_DH, -1)

    gb = gate_b.reshape(N_DIR, 2, M_HEADS)
    spec = lambda r, n: pl.BlockSpec((1, r, n), lambda i: (i, 0, 0))
    col = pl.BlockSpec((N_DH, 1), lambda i: (0, 0))
    rc, dc, r, a, dec = pl.pallas_call(
        functools.partial(_gates_kernel, chunk=chunk),
        grid=(b,),
        in_specs=[spec(N_DH, tc), spec(N_DH, tc), spec(N_DH, t), spec(N_DH, t), col, col],
        out_specs=[spec(LANES, tc), spec(N_DH, ncc), spec(LANES, t), spec(N_DH, t), spec(N_DH, nc)],
        out_shape=[jax.ShapeDtypeStruct((b, LANES, tc), BF16), jax.ShapeDtypeStruct((b, N_DH, ncc), F32),
                   jax.ShapeDtypeStruct((b, LANES, t), BF16), jax.ShapeDtypeStruct((b, N_DH, t), F32),
                   jax.ShapeDtypeStruct((b, N_DH, nc), F32)],
        compiler_params=_cparams(("arbitrary",), 32),
        name="gates",
    )(*rows(gates_tok_c), *rows(gates_tok), gb[:, 0].reshape(N_DH, 1), gb[:, 1].reshape(N_DH, 1))
    a = a.reshape(b, N_DH, nc, chunk).transpose(0, 2, 1, 3)
    decay = jnp.concatenate([dc, dec], axis=2).reshape(b * N_DH, ncc + nc)
    return jnp.transpose(rc, (0, 2, 1)), jnp.transpose(r, (0, 2, 1)), a, decay


def _twice(x):
    return jnp.concatenate([x, x], axis=1)


def _mlstm_kernel(dec_ref, q_ref, k_ref, v_ref, oz_ref, p_ref, a_ref, kc_ref, vc_ref, pc_ref,
                  sel_ref, hnw_ref, out_ref, ct_ref, n_ref, hs_ref, *, chunk):
    t = q_ref.shape[1]
    nc = t // chunk
    ncc = kc_ref.shape[1] // chunk
    bi, hi = pl.program_id(0), pl.program_id(1)
    t_idx = lax.broadcasted_iota(jnp.int32, (chunk, chunk), 0)
    s_idx = lax.broadcasted_iota(jnp.int32, (chunk, chunk), 1)

    ct_ref[...] = jnp.zeros(ct_ref.shape, F32)
    n_ref[...] = jnp.zeros(n_ref.shape, F32)

    def row_terms(p, d):
        w = jnp.dot(p, sel_ref[0, d], preferred_element_type=F32)
        return [w[:, i * LANES:(i + 1) * LANES] for i in range(len(TERM_PIECES))]

    def update(k, v, e_b, decay, d):
        ke = k.astype(F32) * _twice(e_b)
        upd = lax.dot_general(ke.astype(BF16), v, (((0,), (0,)), ((), ())), preferred_element_type=F32)
        ct_ref[d] = decay * ct_ref[d] + upd
        n_ref[d] = decay * n_ref[d] + jnp.sum(ke, axis=0, keepdims=True)

    def decay_of(c, d):
        return dec_ref[bi * N_DH + d * M_HEADS + hi, c]

    for d in range(N_DIR):
        for i in range(ncc):
            c = i if d == 0 else ncc - 1 - i
            rows = pl.ds(c * chunk, chunk)
            e_b = row_terms(pc_ref[0, rows, :], d)[2]
            update(kc_ref[0, rows, :], vc_ref[0, rows, :], e_b, decay_of(c, d), d)

    def run_chunk(c, d):
        rows = pl.ds(pl.multiple_of(c * chunk, chunk), chunk)
        q, k, v = q_ref[0, rows, :], k_ref[0, rows, :], v_ref[0, rows, :]
        neg_g_b, w_inter_b, e_b, neg_mt_b = row_terms(p_ref[0, rows, :], d)
        a_row = a_ref[0, c, pl.ds(d * M_HEADS + hi, 1), :]
        mask = (s_idx <= t_idx) if d == 0 else (s_idx >= t_idx)
        w_intra = jnp.exp2(jnp.where(mask, a_row + _twice(neg_g_b), NEG_BIG))
        s = _dot_nt(q, k) * w_intra
        num = (_twice(w_inter_b) * jnp.dot(q, ct_ref[d].astype(BF16), preferred_element_type=F32)
               + jnp.dot(s.astype(BF16), v, preferred_element_type=F32))
        den_b = (w_inter_b * jnp.sum(q.astype(F32) * n_ref[d], axis=-1, keepdims=True)
                 + jnp.sum(s, axis=-1, keepdims=True))
        h = num * _twice(1.0 / jnp.maximum(jnp.abs(den_b), jnp.exp2(neg_mt_b)))
        update(k, v, e_b, decay_of(ncc + c, d), d)
        return rows, h

    def finalize(rows, h):
        hn = h * lax.rsqrt(jnp.mean(h * h, axis=-1, keepdims=True) + EPS) * hnw_ref[...]
        out_ref[0, rows, :] = (hn * oz_ref[0, rows, :].astype(F32)).astype(out_ref.dtype)

    def first(i, carry):
        for d in range(N_DIR):
            rows, h = run_chunk(i if d == 0 else nc - 1 - i, d)
            hs_ref[rows, :] = h
        return carry

    def second(i, carry):
        for d in range(N_DIR):
            rows, h = run_chunk(i if d == 0 else nc - 1 - i, d)
            finalize(rows, hs_ref[rows, :] + h)
        return carry

    lax.fori_loop(0, nc // 2, first, 0, unroll=SCAN_UNROLL)
    lax.fori_loop(nc // 2, nc, second, 0, unroll=SCAN_UNROLL)


def _term_selectors():
    sel = np.zeros((M_HEADS, N_DIR, LANES, len(TERM_PIECES) * LANES), np.float32)
    first = np.concatenate([[0], np.cumsum(TERM_PIECES)])
    for h in range(M_HEADS):
        for d in range(N_DIR):
            for q in range(len(TERM_PIECES)):
                for j in range(first[q], first[q + 1]):
                    sel[h, d, j * N_DH + d * M_HEADS + h, q * LANES:(q + 1) * LANES] = 1.0
    return jnp.asarray(sel, BF16)


def _mlstm(y, yc, p, a, pc, decay, head_norm_w, chunk):
    b, t, _ = y.shape
    tc = yc.shape[1]
    dh = MXU_DIM
    sel = _term_selectors()
    assert t % (2 * chunk) == 0 and tc % chunk == 0
    blk = lambda col0: pl.BlockSpec((1, t, dh), lambda i, h, c=col0: (i, 0, c + h))
    blkc = lambda col0: pl.BlockSpec((1, tc, dh), lambda i, h, c=col0: (i, 0, c + h))
    return pl.pallas_call(
        functools.partial(_mlstm_kernel, chunk=chunk),
        grid=(b, M_HEADS),
        in_specs=[pl.BlockSpec(memory_space=pltpu.SMEM),
                  blk(0), blk(M_HEADS), blk(2 * M_HEADS), blk(4 * M_HEADS),
                  pl.BlockSpec((1, t, LANES), lambda i, h: (i, 0, 0)),
                  pl.BlockSpec((1, t // chunk, N_DH, chunk), lambda i, h: (i, 0, 0, 0)),
                  blkc(0), blkc(M_HEADS),
                  pl.BlockSpec((1, tc, LANES), lambda i, h: (i, 0, 0)),
                  pl.BlockSpec((1, N_DIR) + sel.shape[2:], lambda i, h: (h, 0, 0, 0)),
                  pl.BlockSpec((1, dh), lambda i, h: (0, h))],
        out_specs=pl.BlockSpec((1, t, dh), lambda i, h: (i, 0, h)),
        out_shape=jax.ShapeDtypeStruct((b, t, M_HEADS * dh), BF16),
        scratch_shapes=[pltpu.VMEM((N_DIR, dh, dh), F32),
                        pltpu.VMEM((N_DIR, 1, dh), F32),
                        pltpu.VMEM((t, dh), F32)],
        compiler_params=_cparams(("arbitrary", "arbitrary"), 48),
        name="mlstm",
    )(decay, y, y, y, y, p, a, yc, yc, pc, sel, head_norm_w)


def _window_bounds(pos, w, length):
    return np.clip(pos - w // 2, 0, length), np.clip(pos + w - w // 2, 0, length)


def _col_pool_matrices(tile):
    pos = np.arange(GRID_W)
    mats = []
    for w in POOL_WINDOWS:
        lo, hi = _window_bounds(pos, w, GRID_W)
        band = ((pos[None, :] >= lo[:, None]) & (pos[None, :] < hi[:, None])).astype(np.float32)
        mats.append(np.kron(np.eye(tile // GRID_W, dtype=np.float32), band))
    return jnp.asarray(np.stack(mats), dtype=BF16)


def _pool_kernel(p_ref, z_ref, mc_ref, pw_ref, ps_ref, out_ref, pad_ref, scale_ref, *, tile, rows):
    t = p_ref.shape[1]
    gd = LANES
    pad = (max(POOL_WINDOWS) // 2) * GRID_W
    n_tiles = t // tile

    @pl.when(pl.program_id(0) == 0)
    def _():
        pad_ref[...] = jnp.zeros(pad_ref.shape, F32)
        tok = lax.broadcasted_iota(jnp.int32, (t, gd), 0)
        r = tok // GRID_W
        c = tok % GRID_W
        for g, w in enumerate(POOL_WINDOWS):
            cnt_r = jnp.minimum(r + (w - w // 2), rows) - jnp.maximum(r - w // 2, 0)
            cnt_c = jnp.minimum(c + (w - w // 2), GRID_W) - jnp.maximum(c - w // 2, 0)
            scale_ref[g] = 1.0 / (cnt_r.astype(F32) * cnt_c.astype(F32))

    for g, w in enumerate(POOL_WINDOWS):
        cols = pl.ds(g * gd, gd)

        def col_sum(i, carry):
            start = pl.multiple_of(i * tile, tile)
            pad_ref[pl.ds(pad + start, tile), :] = jnp.dot(
                mc_ref[g], p_ref[0, pl.ds(start, tile), cols], preferred_element_type=F32)
            return carry

        lax.fori_loop(0, n_tiles, col_sum, 0, unroll=POOL_UNROLL)

        def row_sum(i, carry):
            start = pl.multiple_of(i * tile, tile)
            acc = pad_ref[pl.ds(pad + start - (w // 2) * GRID_W, tile), :]
            for j in range(1 - w // 2, w - w // 2):
                acc = acc + pad_ref[pl.ds(pad + start + j * GRID_W, tile), :]
            mixed = acc * scale_ref[g, pl.ds(start, tile), :] - p_ref[0, pl.ds(start, tile), cols].astype(F32)
            mm = jnp.dot(mixed.astype(BF16), pw_ref[g], preferred_element_type=F32)
            zz = z_ref[0, pl.ds(start, tile), cols].astype(F32)
            out_ref[0, pl.ds(start, tile), cols] = (mm * ps_ref[:, cols] * zz).astype(out_ref.dtype)
            return carry

        lax.fori_loop(0, n_tiles, row_sum, 0, unroll=POOL_UNROLL)


def _pool(y, pool_w, pool_scale, p_col0):
    b, t, _ = y.shape
    pw = P_GROUPS * LANES
    tile = MXU_DIM
    pad = (max(POOL_WINDOWS) // 2) * GRID_W
    return pl.pallas_call(
        functools.partial(_pool_kernel, tile=tile, rows=t // GRID_W),
        grid=(b,),
        in_specs=[pl.BlockSpec((1, t, pw), lambda i: (i, 0, p_col0 // pw)),
                  pl.BlockSpec((1, t, pw), lambda i: (i, 0, p_col0 // pw + 1)),
                  pl.BlockSpec((P_GROUPS, tile, tile), lambda i: (0, 0, 0)),
                  pl.BlockSpec((P_GROUPS, LANES, LANES), lambda i: (0, 0, 0)),
                  pl.BlockSpec((1, pw), lambda i: (0, 0))],
        out_specs=pl.BlockSpec((1, t, pw), lambda i: (i, 0, 0)),
        out_shape=jax.ShapeDtypeStruct((b, t, pw), BF16),
        scratch_shapes=[pltpu.VMEM((t + 2 * pad, LANES), F32),
                        pltpu.VMEM((P_GROUPS, t, LANES), F32)],
        compiler_params=_cparams(("arbitrary",), 48),
        name="pool",
    )(y, y, _col_pool_matrices(tile), pool_w, pool_scale)


def _merge_kernel(x_ref, ym_ref, yp_ref, gm_ref, gp_ref, gate_ref, wm_ref, wp_ref, wo_ref, fw_ref, o_ref):
    y_m = jnp.dot(ym_ref[...], wm_ref[...], preferred_element_type=F32)
    y_p = jnp.dot(yp_ref[...], wp_ref[...], preferred_element_type=F32)
    merged = gm_ref[...].astype(F32) * y_m + gp_ref[...].astype(F32) * y_p
    upd = jnp.dot(merged.astype(BF16), wo_ref[...], preferred_element_type=F32)
    xo = x_ref[...] + gate_ref[0] * upd
    o_ref[...] = xo * lax.rsqrt(jnp.mean(xo * xo, axis=-1, keepdims=True) + EPS) * fw_ref[...]


def _merge(x2, ym, yp, y, gate, wm, wp, wo, final_w, rows_per_mod, g_col0):
    n, d = x2.shape
    tm = 512
    per = rows_per_mod // tm
    pw = yp.shape[1]
    full = lambda shape: pl.BlockSpec(shape, lambda i: (0,) * len(shape))
    return pl.pallas_call(
        _merge_kernel,
        grid=(n // tm,),
        in_specs=[pl.BlockSpec((tm, d), lambda i: (i, 0)),
                  pl.BlockSpec((tm, d), lambda i: (i, 0)),
                  pl.BlockSpec((tm, pw), lambda i: (i, 0)),
                  pl.BlockSpec((tm, d), lambda i: (i, g_col0 // d)),
                  pl.BlockSpec((tm, d), lambda i: (i, g_col0 // d + 1)),
                  pl.BlockSpec((1, 1, d), lambda i: (i // per, 0, 0)),
                  full((d, d)), full((pw, d)), full((d, d)), full((1, d))],
        out_specs=pl.BlockSpec((tm, d), lambda i: (i, 0)),
        out_shape=jax.ShapeDtypeStruct((n, d), F32),
        compiler_params=_cparams(("arbitrary",), 48),
        name="merge",
    )(x2, ym, yp, y, y, gate, wm, wp, wo, final_w)


def kernel(x, c, ctx, c_ctx, norm_w, ada_w, ada_b, in_w, gate_b, head_norm_w, pool_w, pool_scale,
           branch_m_w, branch_p_w, out_w, final_norm_w):
    b, t, d = x.shape
    tc = ctx.shape[1]
    depth = norm_w.shape[0]
    assert depth == 1, "context stream update between layers is not implemented"
    mw = M_HEADS * MXU_DIM
    assert d == mw
    g0 = 5 * mw

    cc = jnp.zeros((2 * 8, d), F32).at[:b].set(c).at[b].set(c_ctx)
    mod = _adaln(cc, ada_w[0], ada_b[0])
    shift, scale, gate = mod[:, :d], mod[:, d:2 * d], mod[:, 2 * d:]

    wt = jnp.transpose(in_w[0])
    row = jnp.arange(wt.shape[0])[:, None]
    wt = (wt * jnp.where((row >= mw) & (row < 2 * mw), MXU_DIM ** -0.5, 1.0)).astype(BF16)
    nw = norm_w[0].reshape(1, d)

    y, gates_tok = _in_proj(x.reshape(b * t, d), scale[:b, None, :], shift[:b, None, :], nw,
                            wt, g0, t, "in_proj")
    yc, gates_tok_c = _in_proj(ctx.reshape(b * tc, d), scale[b:b + 1, None, :], shift[b:b + 1, None, :], nw,
                               wt, g0, b * tc, "in_proj_ctx", out0=mw, nc=2 * mw)
    y = y.reshape(b, t, -1)
    yc = yc.reshape(b, tc, -1)

    pc, p, a, decay = _gates(gates_tok.reshape(b, t, LANES), gates_tok_c.reshape(b, tc, LANES), gate_b[0], CHUNK)
    ym = _mlstm(y, yc, p, a, pc, decay, head_norm_w[0].reshape(1, mw), CHUNK)
    yp = _pool(y, pool_w[0].astype(BF16), pool_scale[0].reshape(1, -1), g0)

    out = _merge(x.reshape(b * t, d), ym.reshape(b * t, mw), yp.reshape(b * t, -1), y.reshape(b * t, -1),
                 gate[:b, None, :], branch_m_w[0].astype(BF16), branch_p_w[0].astype(BF16),
                 out_w[0].astype(BF16), final_norm_w.reshape(1, d), t, g0 + 2 * P_GROUPS * LANES)
    return out.reshape(b, t, d)
```

```python
import functools

import numpy as np
import jax
import jax.numpy as jnp
from jax import lax
from jax.experimental import pallas as pl
from jax.experimental.pallas import tpu as pltpu

F32 = jnp.float32
BF16 = jnp.bfloat16

EPS = 1e-6
LOG2E = 1.4426950408889634
M_INIT = -1e30
NEG_BIG = -1e30
M_HEADS = 4
N_DIR = 2
N_GATE_COLS = N_DIR * 2 * M_HEADS
P_GROUPS = 4
POOL_WINDOWS = (2, 4, 8, 16)
GRID_W = 64
LANES = 128
MXU_DIM = 256
CHUNK = 256
POOL_UNROLL = 4
SCAN_UNROLL = 4
MIB = 1024 * 1024


def _cparams(sem, vmem_mib):
    return pltpu.CompilerParams(dimension_semantics=sem, vmem_limit_bytes=vmem_mib * MIB)


def _sigmoid(x):
    return 1.0 / (1.0 + jnp.exp(-x))


def _silu(x):
    return x * _sigmoid(x)


def _log_sigmoid(x):
    return jnp.minimum(x, 0.0) - jnp.log(1.0 + jnp.exp(-jnp.abs(x)))


def _dot_nt(a, b):
    return lax.dot_general(a, b, (((1,), (1,)), ((), ())), preferred_element_type=F32)


def _adaln_kernel(c_ref, w_ref, b_ref, o_ref):
    o_ref[...] = jnp.dot(_silu(c_ref[...]), w_ref[...], preferred_element_type=F32) + b_ref[...]


def _adaln(cc, ada_w, ada_b):
    rows, d = cc.shape
    n = ada_w.shape[1]
    tn = d
    return pl.pallas_call(
        _adaln_kernel,
        grid=(n // tn,),
        in_specs=[pl.BlockSpec((rows, d), lambda j: (0, 0)),
                  pl.BlockSpec((d, tn), lambda j: (0, j)),
                  pl.BlockSpec((1, tn), lambda j: (0, j))],
        out_specs=pl.BlockSpec((rows, tn), lambda j: (0, j)),
        out_shape=jax.ShapeDtypeStruct((rows, n), F32),
        compiler_params=_cparams(("arbitrary",), 32),
        name="adaln",
    )(cc, ada_w, ada_b.reshape(1, n))


def _sigmoid_t(x):
    return 0.5 * jnp.tanh(0.5 * x) + 0.5


MAIN_TILE_ACTS = ("id", "id", "id", "gate_keep", "gate_apply", "half_silu", "id", "id")


def _inproj_kernel(x_ref, sc_ref, sh_ref, nw_ref, w_ref, wg_ref, y_ref, g_ref, xn_ref, keep_ref, *, acts):
    j = pl.program_id(1)

    @pl.when(j == 0)
    def _():
        x = x_ref[...]
        ms = jnp.mean(x * x, axis=-1, keepdims=True)
        xn = x * lax.rsqrt(ms + EPS) * nw_ref[...]
        xn = xn * (1.0 + sc_ref[0]) + sh_ref[0]
        xn_ref[...] = xn.astype(BF16)
        wg = wg_ref[...]
        wg = jnp.concatenate([wg, jnp.zeros((LANES - wg.shape[0], wg.shape[1]), wg.dtype)], axis=0)
        g_ref[...] = _dot_nt(xn_ref[...], wg)

    def tile(act):
        acc = _dot_nt(xn_ref[...], w_ref[...])
        if act == "gate_keep":
            keep_ref[...] = _sigmoid_t(acc).astype(BF16)
        elif act == "gate_apply":
            acc = keep_ref[...].astype(F32) * (acc * _sigmoid_t(acc))
        elif act == "half_silu":
            half = acc.shape[1] // 2
            acc = jnp.concatenate([acc[:, :half], acc[:, half:] * _sigmoid_t(acc[:, half:])], axis=1)
        y_ref[...] = acc.astype(BF16)

    for act in sorted(set(acts)):
        hit = functools.reduce(jnp.logical_or, [j == s for s, a in enumerate(acts) if a == act])
        pl.when(hit)(functools.partial(tile, act))


def _in_proj(x2, scale, shift, norm_w, w, gate_row0, rows_per_mod, name, out0=0, nc=None):
    n, d = x2.shape
    nc = w.shape[0] - N_GATE_COLS if nc is None else nc
    tm = min(2048, rows_per_mod)
    tn = 1024
    per = rows_per_mod // tm
    assert out0 % tn == 0 and gate_row0 % tn == 0
    acts = MAIN_TILE_ACTS[out0 // tn:(out0 + nc) // tn]
    keep_shape = (tm, tn) if "gate_keep" in acts else (16, LANES)

    def w_row(i, j):
        r = out0 + j * tn
        return (pl.multiple_of(r + jnp.where(r >= gate_row0, N_GATE_COLS, 0), N_GATE_COLS), 0)

    return pl.pallas_call(
        functools.partial(_inproj_kernel, acts=acts),
        grid=(n // tm, nc // tn),
        in_specs=[pl.BlockSpec((tm, d), lambda i, j: (i, 0)),
                  pl.BlockSpec((1, 1, d), lambda i, j: (i // per, 0, 0)),
                  pl.BlockSpec((1, 1, d), lambda i, j: (i // per, 0, 0)),
                  pl.BlockSpec((1, d), lambda i, j: (0, 0)),
                  pl.BlockSpec((pl.Element(tn), pl.Element(d)), w_row),
                  pl.BlockSpec((pl.Element(N_GATE_COLS), pl.Element(d)), lambda i, j: (gate_row0, 0))],
        out_specs=[pl.BlockSpec((tm, tn), lambda i, j: (i, j)),
                   pl.BlockSpec((tm, LANES), lambda i, j: (i, 0))],
        out_shape=[jax.ShapeDtypeStruct((n, nc), BF16),
                   jax.ShapeDtypeStruct((n, LANES), F32)],
        scratch_shapes=[pltpu.VMEM((tm, d), BF16), pltpu.VMEM(keep_shape, BF16)],
        compiler_params=_cparams(("arbitrary", "arbitrary"), 56),
        name=name,
    )(x2, scale, shift, norm_w, w, w)


N_DH = N_DIR * M_HEADS
TERM_PIECES = (3, 2, 2, 3)


def _split(x, pieces):
    out = []
    for _ in range(pieces):
        p = x.astype(BF16)
        out.append(p)
        x = x - p.astype(F32)
    return out


def _chunk_scan(x, pos, chunk, op, reverse):
    t = x.shape[1]
    k = 1
    while k < chunk:
        if reverse:
            x = jnp.where(pos < chunk - k, op(x, pltpu.roll(x, t - k, 1)), x)
        else:
            x = jnp.where(pos >= k, op(x, pltpu.roll(x, k, 1)), x)
        k *= 2
    return x


def _gate_terms(li, pre_f, m0, chunk):
    rows, t = li.shape
    nc = t // chunk
    pos = lax.broadcasted_iota(jnp.int32, (rows, t), 1) % chunk
    rev = lax.broadcasted_iota(jnp.int32, (rows, t), 0) >= M_HEADS
    rev1 = lax.broadcasted_iota(jnp.int32, (rows, 1), 0) >= M_HEADS
    lf = _log_sigmoid(pre_f)
    b = jnp.where(rev, _chunk_scan(lf, pos, chunk, jnp.add, True), _chunk_scan(lf, pos, chunk, jnp.add, False))
    a = li - b
    cmax = jnp.where(rev, _chunk_scan(a, pos, chunk, jnp.maximum, True),
                     _chunk_scan(a, pos, chunk, jnp.maximum, False))

    def at_scan_end(x, c):
        lo = c * chunk
        return jnp.where(rev1, x[:, lo:lo + 1], x[:, lo + chunk - 1:lo + chunk])

    b_end = [at_scan_end(b, c) for c in range(nc)]
    a_max = [at_scan_end(cmax, c) for c in range(nc)]
    m_f, m_b = m0, m0
    m_in_f, m_in_b = [None] * nc, [None] * nc
    for i in range(nc):
        j = nc - 1 - i
        m_in_f[i] = m_f
        m_f = b_end[i] + jnp.maximum(m_f, a_max[i])
        m_in_b[j] = m_b
        m_b = b_end[j] + jnp.maximum(m_b, a_max[j])
    m_in = [jnp.where(rev1, m_in_b[c], m_in_f[c]) for c in range(nc)]
    per_chunk = lambda vals: jnp.concatenate([jnp.broadcast_to(v, (rows, chunk)) for v in vals], axis=1)
    m_in_t = per_chunk(m_in)
    g = jnp.maximum(m_in_t, cmax)
    g_end = [jnp.maximum(m_in[c], a_max[c]) for c in range(nc)]
    pieces = []
    terms = (-g * LOG2E, jnp.exp(m_in_t - g), jnp.exp(a - per_chunk(g_end)), -(b + g) * LOG2E)
    for term, n in zip(terms, TERM_PIECES):
        pieces += _split(term, n)
    pieces.append(jnp.zeros((LANES - rows * sum(TERM_PIECES), t), BF16))
    decay = jnp.concatenate([jnp.exp(m_in[c] - g_end[c]) for c in range(nc)], axis=1)
    return jnp.concatenate(pieces, axis=0), a * LOG2E, decay, jnp.where(rev1, m_b, m_f)


def _gates_kernel(ic_ref, fc_ref, i_ref, f_ref, bi_ref, bf_ref, rc_ref, dc_ref, r_ref, a_ref, d_ref, *, chunk):
    m0 = jnp.full((N_DH, 1), M_INIT, F32)
    rc_ref[0], _, dc_ref[0], m_ctx = _gate_terms(ic_ref[0] + bi_ref[...], fc_ref[0] + bf_ref[...], m0, chunk)
    r_ref[0], a_ref[0], d_ref[0], _ = _gate_terms(i_ref[0] + bi_ref[...], f_ref[0] + bf_ref[...], m_ctx, chunk)


def _gates(gates_tok, gates_tok_c, gate_b, chunk):
    b, t, _ = gates_tok.shape
    tc = gates_tok_c.shape[1]
    nc, ncc = t // chunk, tc // chunk

    def rows(g):
        g = jnp.transpose(g[:, :, :N_GATE_COLS], (0, 2, 1)).reshape(b, N_DIR, 2, M_HEADS, -1)
        return g[:, :, 0].reshape(b, N_DH, -1), g[:, :, 1].reshape(b, N_DH, -1)

    gb = gate_b.reshape(N_DIR, 2, M_HEADS)
    spec = lambda r, n: pl.BlockSpec((1, r, n), lambda i: (i, 0, 0))
    col = pl.BlockSpec((N_DH, 1), lambda i: (0, 0))
    rc, dc, r, a, dec = pl.pallas_call(
        functools.partial(_gates_kernel, chunk=chunk),
        grid=(b,),
        in_specs=[spec(N_DH, tc), spec(N_DH, tc), spec(N_DH, t), spec(N_DH, t), col, col],
        out_specs=[spec(LANES, tc), spec(N_DH, ncc), spec(LANES, t), spec(N_DH, t), spec(N_DH, nc)],
        out_shape=[jax.ShapeDtypeStruct((b, LANES, tc), BF16), jax.ShapeDtypeStruct((b, N_DH, ncc), F32),
                   jax.ShapeDtypeStruct((b, LANES, t), BF16), jax.ShapeDtypeStruct((b, N_DH, t), F32),
                   jax.ShapeDtypeStruct((b, N_DH, nc), F32)],
        compiler_params=_cparams(("arbitrary",), 32),
        name="gates",
    )(*rows(gates_tok_c), *rows(gates_tok), gb[:, 0].reshape(N_DH, 1), gb[:, 1].reshape(N_DH, 1))
    a = a.reshape(b, N_DH, nc, chunk).transpose(0, 2, 1, 3)
    decay = jnp.concatenate([dc, dec], axis=2).reshape(b * N_DH, ncc + nc)
    return jnp.transpose(rc, (0, 2, 1)), jnp.transpose(r, (0, 2, 1)), a, decay


def _twice(x):
    return jnp.concatenate([x, x], axis=1)


def _mlstm_kernel(dec_ref, q_ref, k_ref, v_ref, oz_ref, p_ref, a_ref, kc_ref, vc_ref, pc_ref,
                  sel_ref, hnw_ref, out_ref, ct_ref, n_ref, hs_ref, *, chunk):
    t = q_ref.shape[1]
    nc = t // chunk
    ncc = kc_ref.shape[1] // chunk
    bi, hi = pl.program_id(0), pl.program_id(1)
    t_idx = lax.broadcasted_iota(jnp.int32, (chunk, chunk), 0)
    s_idx = lax.broadcasted_iota(jnp.int32, (chunk, chunk), 1)

    ct_ref[...] = jnp.zeros(ct_ref.shape, F32)
    n_ref[...] = jnp.zeros(n_ref.shape, F32)

    def row_terms(p, d):
        w = jnp.dot(p, sel_ref[0, d], preferred_element_type=F32)
        return [w[:, i * LANES:(i + 1) * LANES] for i in range(len(TERM_PIECES))]

    def update(k, v, e_b, decay, d):
        ke = k.astype(F32) * _twice(e_b)
        upd = lax.dot_general(ke.astype(BF16), v, (((0,), (0,)), ((), ())), preferred_element_type=F32)
        ct_ref[d] = decay * ct_ref[d] + upd
        n_ref[d] = decay * n_ref[d] + jnp.sum(ke, axis=0, keepdims=True)

    def decay_of(c, d):
        return dec_ref[bi * N_DH + d * M_HEADS + hi, c]

    for d in range(N_DIR):
        for i in range(ncc):
            c = i if d == 0 else ncc - 1 - i
            rows = pl.ds(c * chunk, chunk)
            e_b = row_terms(pc_ref[0, rows, :], d)[2]
            update(kc_ref[0, rows, :], vc_ref[0, rows, :], e_b, decay_of(c, d), d)

    def run_chunk(c, d):
        rows = pl.ds(pl.multiple_of(c * chunk, chunk), chunk)
        q, k, v = q_ref[0, rows, :], k_ref[0, rows, :], v_ref[0, rows, :]
        neg_g_b, w_inter_b, e_b, neg_mt_b = row_terms(p_ref[0, rows, :], d)
        a_row = a_ref[0, c, pl.ds(d * M_HEADS + hi, 1), :]
        mask = (s_idx <= t_idx) if d == 0 else (s_idx >= t_idx)
        w_intra = jnp.exp2(jnp.where(mask, a_row + _twice(neg_g_b), NEG_BIG))
        s = _dot_nt(q, k) * w_intra
        num = (_twice(w_inter_b) * jnp.dot(q, ct_ref[d].astype(BF16), preferred_element_type=F32)
               + jnp.dot(s.astype(BF16), v, preferred_element_type=F32))
        den_b = (w_inter_b * jnp.sum(q.astype(F32) * n_ref[d], axis=-1, keepdims=True)
                 + jnp.sum(s, axis=-1, keepdims=True))
        h = num * _twice(1.0 / jnp.maximum(jnp.abs(den_b), jnp.exp2(neg_mt_b)))
        update(k, v, e_b, decay_of(ncc + c, d), d)
        return rows, h

    def finalize(rows, h):
        hn = h * lax.rsqrt(jnp.mean(h * h, axis=-1, keepdims=True) + EPS) * hnw_ref[...]
        out_ref[0, rows, :] = (hn * oz_ref[0, rows, :].astype(F32)).astype(out_ref.dtype)

    def first(i, carry):
        for d in range(N_DIR):
            rows, h = run_chunk(i if d == 0 else nc - 1 - i, d)
            hs_ref[rows, :] = h
        return carry

    def second(i, carry):
        for d in range(N_DIR):
            rows, h = run_chunk(i if d == 0 else nc - 1 - i, d)
            finalize(rows, hs_ref[rows, :] + h)
        return carry

    lax.fori_loop(0, nc // 2, first, 0, unroll=SCAN_UNROLL)
    lax.fori_loop(nc // 2, nc, second, 0, unroll=SCAN_UNROLL)


def _term_selectors():
    sel = np.zeros((M_HEADS, N_DIR, LANES, len(TERM_PIECES) * LANES), np.float32)
    first = np.concatenate([[0], np.cumsum(TERM_PIECES)])
    for h in range(M_HEADS):
        for d in range(N_DIR):
            for q in range(len(TERM_PIECES)):
                for j in range(first[q], first[q + 1]):
                    sel[h, d, j * N_DH + d * M_HEADS + h, q * LANES:(q + 1) * LANES] = 1.0
    return jnp.asarray(sel, BF16)


def _mlstm(y, yc, p, a, pc, decay, head_norm_w, chunk):
    b, t, _ = y.shape
    tc = yc.shape[1]
    dh = MXU_DIM
    sel = _term_selectors()
    assert t % (2 * chunk) == 0 and tc % chunk == 0
    blk = lambda col0: pl.BlockSpec((1, t, dh), lambda i, h, c=col0: (i, 0, c + h))
    blkc = lambda col0: pl.BlockSpec((1, tc, dh), lambda i, h, c=col0: (i, 0, c + h))
    return pl.pallas_call(
        functools.partial(_mlstm_kernel, chunk=chunk),
        grid=(b, M_HEADS),
        in_specs=[pl.BlockSpec(memory_space=pltpu.SMEM),
                  blk(0), blk(M_HEADS), blk(2 * M_HEADS), blk(4 * M_HEADS),
                  pl.BlockSpec((1, t, LANES), lambda i, h: (i, 0, 0)),
                  pl.BlockSpec((1, t // chunk, N_DH, chunk), lambda i, h: (i, 0, 0, 0)),
                  blkc(0), blkc(M_HEADS),
                  pl.BlockSpec((1, tc, LANES), lambda i, h: (i, 0, 0)),
                  pl.BlockSpec((1, N_DIR) + sel.shape[2:], lambda i, h: (h, 0, 0, 0)),
                  pl.BlockSpec((1, dh), lambda i, h: (0, h))],
        out_specs=pl.BlockSpec((1, t, dh), lambda i, h: (i, 0, h)),
        out_shape=jax.ShapeDtypeStruct((b, t, M_HEADS * dh), BF16),
        scratch_shapes=[pltpu.VMEM((N_DIR, dh, dh), F32),
                        pltpu.VMEM((N_DIR, 1, dh), F32),
                        pltpu.VMEM((t, dh), F32)],
        compiler_params=_cparams(("arbitrary", "arbitrary"), 48),
        name="mlstm",
    )(decay, y, y, y, y, p, a, yc, yc, pc, sel, head_norm_w)


def _window_bounds(pos, w, length):
    return np.clip(pos - w // 2, 0, length), np.clip(pos + w - w // 2, 0, length)


def _col_pool_matrices(tile):
    pos = np.arange(GRID_W)
    mats = []
    for w in POOL_WINDOWS:
        lo, hi = _window_bounds(pos, w, GRID_W)
        band = ((pos[None, :] >= lo[:, None]) & (pos[None, :] < hi[:, None])).astype(np.float32)
        mats.append(np.kron(np.eye(tile // GRID_W, dtype=np.float32), band))
    return jnp.asarray(np.stack(mats), dtype=BF16)


def _pool_kernel(p_ref, z_ref, mc_ref, pw_ref, ps_ref, out_ref, pad_ref, scale_ref, *, tile, rows):
    t = p_ref.shape[1]
    gd = LANES
    pad = (max(POOL_WINDOWS) // 2) * GRID_W
    n_tiles = t // tile

    @pl.when(pl.program_id(0) == 0)
    def _():
        pad_ref[...] = jnp.zeros(pad_ref.shape, F32)
        tok = lax.broadcasted_iota(jnp.int32, (t, gd), 0)
        r = tok // GRID_W
        c = tok % GRID_W
        for g, w in enumerate(POOL_WINDOWS):
            cnt_r = jnp.minimum(r + (w - w // 2), rows) - jnp.maximum(r - w // 2, 0)
            cnt_c = jnp.minimum(c + (w - w // 2), GRID_W) - jnp.maximum(c - w // 2, 0)
            scale_ref[g] = 1.0 / (cnt_r.astype(F32) * cnt_c.astype(F32))

    for g, w in enumerate(POOL_WINDOWS):
        cols = pl.ds(g * gd, gd)

        def col_sum(i, carry):
            start = pl.multiple_of(i * tile, tile)
            pad_ref[pl.ds(pad + start, tile), :] = jnp.dot(
                mc_ref[g], p_ref[0, pl.ds(start, tile), cols], preferred_element_type=F32)
            return carry

        lax.fori_loop(0, n_tiles, col_sum, 0, unroll=POOL_UNROLL)

        def row_sum(i, carry):
            start = pl.multiple_of(i * tile, tile)
            acc = pad_ref[pl.ds(pad + start - (w // 2) * GRID_W, tile), :]
            for j in range(1 - w // 2, w - w // 2):
                acc = acc + pad_ref[pl.ds(pad + start + j * GRID_W, tile), :]
            mixed = acc * scale_ref[g, pl.ds(start, tile), :] - p_ref[0, pl.ds(start, tile), cols].astype(F32)
            mm = jnp.dot(mixed.astype(BF16), pw_ref[g], preferred_element_type=F32)
            zz = z_ref[0, pl.ds(start, tile), cols].astype(F32)
            out_ref[0, pl.ds(start, tile), cols] = (mm * ps_ref[:, cols] * zz).astype(out_ref.dtype)
            return carry

        lax.fori_loop(0, n_tiles, row_sum, 0, unroll=POOL_UNROLL)


def _pool(y, pool_w, pool_scale, p_col0):
    b, t, _ = y.shape
    pw = P_GROUPS * LANES
    tile = MXU_DIM
    pad = (max(POOL_WINDOWS) // 2) * GRID_W
    return pl.pallas_call(
        functools.partial(_pool_kernel, tile=tile, rows=t // GRID_W),
        grid=(b,),
        in_specs=[pl.BlockSpec((1, t, pw), lambda i: (i, 0, p_col0 // pw)),
                  pl.BlockSpec((1, t, pw), lambda i: (i, 0, p_col0 // pw + 1)),
                  pl.BlockSpec((P_GROUPS, tile, tile), lambda i: (0, 0, 0)),
                  pl.BlockSpec((P_GROUPS, LANES, LANES), lambda i: (0, 0, 0)),
                  pl.BlockSpec((1, pw), lambda i: (0, 0))],
        out_specs=pl.BlockSpec((1, t, pw), lambda i: (i, 0, 0)),
        out_shape=jax.ShapeDtypeStruct((b, t, pw), BF16),
        scratch_shapes=[pltpu.VMEM((t + 2 * pad, LANES), F32),
                        pltpu.VMEM((P_GROUPS, t, LANES), F32)],
        compiler_params=_cparams(("arbitrary",), 48),
        name="pool",
    )(y, y, _col_pool_matrices(tile), pool_w, pool_scale)


def _merge_kernel(x_ref, ym_ref, yp_ref, gm_ref, gp_ref, gate_ref, wm_ref, wp_ref, wo_ref, fw_ref, o_ref):
    y_m = jnp.dot(ym_ref[...], wm_ref[...], preferred_element_type=F32)
    y_p = jnp.dot(yp_ref[...], wp_ref[...], preferred_element_type=F32)
    merged = _sigmoid_t(gm_ref[...].astype(F32)) * y_m + _sigmoid_t(gp_ref[...].astype(F32)) * y_p
    upd = jnp.dot(merged.astype(BF16), wo_ref[...], preferred_element_type=F32)
    xo = x_ref[...] + gate_ref[0] * upd
    o_ref[...] = xo * lax.rsqrt(jnp.mean(xo * xo, axis=-1, keepdims=True) + EPS) * fw_ref[...]


def _merge(x2, ym, yp, y, gate, wm, wp, wo, final_w, rows_per_mod, g_col0):
    n, d = x2.shape
    tm = 512
    per = rows_per_mod // tm
    pw = yp.shape[1]
    full = lambda shape: pl.BlockSpec(shape, lambda i: (0,) * len(shape))
    return pl.pallas_call(
        _merge_kernel,
        grid=(n // tm,),
        in_specs=[pl.BlockSpec((tm, d), lambda i: (i, 0)),
                  pl.BlockSpec((tm, d), lambda i: (i, 0)),
                  pl.BlockSpec((tm, pw), lambda i: (i, 0)),
                  pl.BlockSpec((tm, d), lambda i: (i, g_col0 // d)),
                  pl.BlockSpec((tm, d), lambda i: (i, g_col0 // d + 1)),
                  pl.BlockSpec((1, 1, d), lambda i: (i // per, 0, 0)),
                  full((d, d)), full((pw, d)), full((d, d)), full((1, d))],
        out_specs=pl.BlockSpec((tm, d), lambda i: (i, 0)),
        out_shape=jax.ShapeDtypeStruct((n, d), F32),
        compiler_params=_cparams(("arbitrary",), 48),
        name="merge",
    )(x2, ym, yp, y, y, gate, wm, wp, wo, final_w)


def kernel(x, c, ctx, c_ctx, norm_w, ada_w, ada_b, in_w, gate_b, head_norm_w, pool_w, pool_scale,
           branch_m_w, branch_p_w, out_w, final_norm_w):
    b, t, d = x.shape
    tc = ctx.shape[1]
    depth = norm_w.shape[0]
    assert depth == 1, "context stream update between layers is not implemented"
    mw = M_HEADS * MXU_DIM
    assert d == mw
    g0 = 5 * mw

    cc = jnp.zeros((2 * 8, d), F32).at[:b].set(c).at[b].set(c_ctx)
    mod = _adaln(cc, ada_w[0], ada_b[0])
    shift, scale, gate = mod[:, :d], mod[:, d:2 * d], mod[:, 2 * d:]

    wt = jnp.transpose(in_w[0])
    row = jnp.arange(wt.shape[0])[:, None]
    wt = (wt * jnp.where((row >= mw) & (row < 2 * mw), MXU_DIM ** -0.5, 1.0)).astype(BF16)
    nw = norm_w[0].reshape(1, d)

    y, gates_tok = _in_proj(x.reshape(b * t, d), scale[:b, None, :], shift[:b, None, :], nw,
                            wt, g0, t, "in_proj")
    yc, gates_tok_c = _in_proj(ctx.reshape(b * tc, d), scale[b:b + 1, None, :], shift[b:b + 1, None, :], nw,
                               wt, g0, b * tc, "in_proj_ctx", out0=mw, nc=2 * mw)
    y = y.reshape(b, t, -1)
    yc = yc.reshape(b, tc, -1)

    pc, p, a, decay = _gates(gates_tok.reshape(b, t, LANES), gates_tok_c.reshape(b, tc, LANES), gate_b[0], CHUNK)
    ym = _mlstm(y, yc, p, a, pc, decay, head_norm_w[0].reshape(1, mw), CHUNK)
    yp = _pool(y, pool_w[0].astype(BF16), pool_scale[0].reshape(1, -1), g0)

    out = _merge(x.reshape(b * t, d), ym.reshape(b * t, mw), yp.reshape(b * t, -1), y.reshape(b * t, -1),
                 gate[:b, None, :], branch_m_w[0].astype(BF16), branch_p_w[0].astype(BF16),
                 out_w[0].astype(BF16), final_norm_w.reshape(1, d), t, g0 + 2 * P_GROUPS * LANES)
    return out.reshape(b, t, d)
```

```python
import functools

import numpy as np
import jax
import jax.numpy as jnp
from jax import lax
from jax.experimental import pallas as pl
from jax.experimental.pallas import tpu as pltpu

F32 = jnp.float32
BF16 = jnp.bfloat16

EPS = 1e-6
LOG2E = 1.4426950408889634
M_INIT = -1e30
NEG_BIG = -1e30
M_HEADS = 4
N_DIR = 2
N_GATE_COLS = N_DIR * 2 * M_HEADS
P_GROUPS = 4
POOL_WINDOWS = (2, 4, 8, 16)
GRID_W = 64
LANES = 128
MXU_DIM = 256
CHUNK = 256
POOL_UNROLL = 4
SCAN_UNROLL = 4
MIB = 1024 * 1024


def _cparams(sem, vmem_mib):
    return pltpu.CompilerParams(dimension_semantics=sem, vmem_limit_bytes=vmem_mib * MIB)


def _sigmoid(x):
    return 1.0 / (1.0 + jnp.exp(-x))


def _silu(x):
    return x * _sigmoid(x)


def _log_sigmoid(x):
    return jnp.minimum(x, 0.0) - jnp.log(1.0 + jnp.exp(-jnp.abs(x)))


def _dot_nt(a, b):
    return lax.dot_general(a, b, (((1,), (1,)), ((), ())), preferred_element_type=F32)


def _adaln_kernel(c_ref, w_ref, b_ref, o_ref):
    o_ref[...] = jnp.dot(_silu(c_ref[...]), w_ref[...], preferred_element_type=F32) + b_ref[...]


def _adaln(cc, ada_w, ada_b):
    rows, d = cc.shape
    n = ada_w.shape[1]
    tn = d
    return pl.pallas_call(
        _adaln_kernel,
        grid=(n // tn,),
        in_specs=[pl.BlockSpec((rows, d), lambda j: (0, 0)),
                  pl.BlockSpec((d, tn), lambda j: (0, j)),
                  pl.BlockSpec((1, tn), lambda j: (0, j))],
        out_specs=pl.BlockSpec((rows, tn), lambda j: (0, j)),
        out_shape=jax.ShapeDtypeStruct((rows, n), F32),
        compiler_params=_cparams(("arbitrary",), 32),
        name="adaln",
    )(cc, ada_w, ada_b.reshape(1, n))


def _sigmoid_t(x):
    return 0.5 * jnp.tanh(0.5 * x) + 0.5


IN_TILE = 1024
GATE_SUB = 256


def _inproj_kernel(x_ref, sc_ref, sh_ref, nw_ref, wa_ref, wb_ref, wg_ref, y_ref, g_ref, xn_ref, *, acts):
    j = pl.program_id(1)

    @pl.when(j == 0)
    def _():
        x = x_ref[...]
        ms = jnp.mean(x * x, axis=-1, keepdims=True)
        xn = x * lax.rsqrt(ms + EPS) * nw_ref[...]
        xn = xn * (1.0 + sc_ref[0]) + sh_ref[0]
        xn_ref[...] = xn.astype(BF16)
        wg = wg_ref[...]
        wg = jnp.concatenate([wg, jnp.zeros((LANES - wg.shape[0], wg.shape[1]), wg.dtype)], axis=0)
        g_ref[...] = _dot_nt(xn_ref[...], wg)

    def pair(act):
        tn = IN_TILE
        xn = xn_ref[...]
        if act == "gate":
            for c in range(0, tn, GATE_SUB):
                a = _dot_nt(xn, wa_ref[c:c + GATE_SUB, :])
                b = _dot_nt(xn, wb_ref[c:c + GATE_SUB, :])
                y_ref[:, c:c + GATE_SUB] = a.astype(BF16)
                y_ref[:, tn + c:tn + c + GATE_SUB] = (_sigmoid_t(a) * (b * _sigmoid_t(b))).astype(BF16)
            return
        y_ref[:, :tn] = _dot_nt(xn, wa_ref[...]).astype(BF16)
        b = _dot_nt(xn, wb_ref[...])
        if act == "half_silu":
            half = tn // 2
            b = jnp.concatenate([b[:, :half], b[:, half:] * _sigmoid_t(b[:, half:])], axis=1)
        y_ref[:, tn:] = b.astype(BF16)

    for act in sorted(set(acts)):
        hit = functools.reduce(jnp.logical_or, [j == s for s, a in enumerate(acts) if a == act])
        pl.when(hit)(functools.partial(pair, act))


def _in_proj(x2, scale, shift, norm_w, w, gate_row0, rows_per_mod, name, pairs):
    n, d = x2.shape
    tm = min(2048, rows_per_mod)
    tn = IN_TILE
    per = rows_per_mod // tm
    acts = tuple(p[2] for p in pairs)

    def w_rows(which):
        def index_map(i, j):
            r = sum(jnp.where(j == s, p[which], 0) for s, p in enumerate(pairs))
            return (pl.multiple_of(r, N_GATE_COLS), 0)
        return index_map

    return pl.pallas_call(
        functools.partial(_inproj_kernel, acts=acts),
        grid=(n // tm, len(pairs)),
        in_specs=[pl.BlockSpec((tm, d), lambda i, j: (i, 0)),
                  pl.BlockSpec((1, 1, d), lambda i, j: (i // per, 0, 0)),
                  pl.BlockSpec((1, 1, d), lambda i, j: (i // per, 0, 0)),
                  pl.BlockSpec((1, d), lambda i, j: (0, 0)),
                  pl.BlockSpec((pl.Element(tn), pl.Element(d)), w_rows(0)),
                  pl.BlockSpec((pl.Element(tn), pl.Element(d)), w_rows(1)),
                  pl.BlockSpec((pl.Element(N_GATE_COLS), pl.Element(d)), lambda i, j: (gate_row0, 0))],
        out_specs=[pl.BlockSpec((tm, 2 * tn), lambda i, j: (i, j)),
                   pl.BlockSpec((tm, LANES), lambda i, j: (i, 0))],
        out_shape=[jax.ShapeDtypeStruct((n, 2 * tn * len(pairs)), BF16),
                   jax.ShapeDtypeStruct((n, LANES), F32)],
        scratch_shapes=[pltpu.VMEM((tm, d), BF16)],
        compiler_params=_cparams(("arbitrary", "arbitrary"), 56),
        name=name,
    )(x2, scale, shift, norm_w, w, w, w)


N_DH = N_DIR * M_HEADS
TERM_PIECES = (3, 2, 2, 3)


def _split(x, pieces):
    out = []
    for _ in range(pieces):
        p = x.astype(BF16)
        out.append(p)
        x = x - p.astype(F32)
    return out


def _chunk_scan(x, pos, chunk, op, reverse):
    t = x.shape[1]
    k = 1
    while k < chunk:
        if reverse:
            x = jnp.where(pos < chunk - k, op(x, pltpu.roll(x, t - k, 1)), x)
        else:
            x = jnp.where(pos >= k, op(x, pltpu.roll(x, k, 1)), x)
        k *= 2
    return x


def _gate_terms(li, pre_f, m0, chunk):
    rows, t = li.shape
    nc = t // chunk
    pos = lax.broadcasted_iota(jnp.int32, (rows, t), 1) % chunk
    rev = lax.broadcasted_iota(jnp.int32, (rows, t), 0) >= M_HEADS
    rev1 = lax.broadcasted_iota(jnp.int32, (rows, 1), 0) >= M_HEADS
    lf = _log_sigmoid(pre_f)
    b = jnp.where(rev, _chunk_scan(lf, pos, chunk, jnp.add, True), _chunk_scan(lf, pos, chunk, jnp.add, False))
    a = li - b
    cmax = jnp.where(rev, _chunk_scan(a, pos, chunk, jnp.maximum, True),
                     _chunk_scan(a, pos, chunk, jnp.maximum, False))

    def at_scan_end(x, c):
        lo = c * chunk
        return jnp.where(rev1, x[:, lo:lo + 1], x[:, lo + chunk - 1:lo + chunk])

    b_end = [at_scan_end(b, c) for c in range(nc)]
    a_max = [at_scan_end(cmax, c) for c in range(nc)]
    m_f, m_b = m0, m0
    m_in_f, m_in_b = [None] * nc, [None] * nc
    for i in range(nc):
        j = nc - 1 - i
        m_in_f[i] = m_f
        m_f = b_end[i] + jnp.maximum(m_f, a_max[i])
        m_in_b[j] = m_b
        m_b = b_end[j] + jnp.maximum(m_b, a_max[j])
    m_in = [jnp.where(rev1, m_in_b[c], m_in_f[c]) for c in range(nc)]
    per_chunk = lambda vals: jnp.concatenate([jnp.broadcast_to(v, (rows, chunk)) for v in vals], axis=1)
    m_in_t = per_chunk(m_in)
    g = jnp.maximum(m_in_t, cmax)
    g_end = [jnp.maximum(m_in[c], a_max[c]) for c in range(nc)]
    pieces = []
    terms = (-g * LOG2E, jnp.exp(m_in_t - g), jnp.exp(a - per_chunk(g_end)), -(b + g) * LOG2E)
    for term, n in zip(terms, TERM_PIECES):
        pieces += _split(term, n)
    pieces.append(jnp.zeros((LANES - rows * sum(TERM_PIECES), t), BF16))
    decay = jnp.concatenate([jnp.exp(m_in[c] - g_end[c]) for c in range(nc)], axis=1)
    return jnp.concatenate(pieces, axis=0), a * LOG2E, decay, jnp.where(rev1, m_b, m_f)


def _gates_kernel(ic_ref, fc_ref, i_ref, f_ref, bi_ref, bf_ref, rc_ref, dc_ref, r_ref, a_ref, d_ref, *, chunk):
    m0 = jnp.full((N_DH, 1), M_INIT, F32)
    rc_ref[0], _, dc_ref[0], m_ctx = _gate_terms(ic_ref[0] + bi_ref[...], fc_ref[0] + bf_ref[...], m0, chunk)
    r_ref[0], a_ref[0], d_ref[0], _ = _gate_terms(i_ref[0] + bi_ref[...], f_ref[0] + bf_ref[...], m_ctx, chunk)


def _gates(gates_tok, gates_tok_c, gate_b, chunk):
    b, t, _ = gates_tok.shape
    tc = gates_tok_c.shape[1]
    nc, ncc = t // chunk, tc // chunk

    def rows(g):
        g = jnp.transpose(g[:, :, :N_GATE_COLS], (0, 2, 1)).reshape(b, N_DIR, 2, M_HEADS, -1)
        return g[:, :, 0].reshape(b, N_DH, -1), g[:, :, 1].reshape(b, N_DH, -1)

    gb = gate_b.reshape(N_DIR, 2, M_HEADS)
    spec = lambda r, n: pl.BlockSpec((1, r, n), lambda i: (i, 0, 0))
    col = pl.BlockSpec((N_DH, 1), lambda i: (0, 0))
    rc, dc, r, a, dec = pl.pallas_call(
        functools.partial(_gates_kernel, chunk=chunk),
        grid=(b,),
        in_specs=[spec(N_DH, tc), spec(N_DH, tc), spec(N_DH, t), spec(N_DH, t), col, col],
        out_specs=[spec(LANES, tc), spec(N_DH, ncc), spec(LANES, t), spec(N_DH, t), spec(N_DH, nc)],
        out_shape=[jax.ShapeDtypeStruct((b, LANES, tc), BF16), jax.ShapeDtypeStruct((b, N_DH, ncc), F32),
                   jax.ShapeDtypeStruct((b, LANES, t), BF16), jax.ShapeDtypeStruct((b, N_DH, t), F32),
                   jax.ShapeDtypeStruct((b, N_DH, nc), F32)],
        compiler_params=_cparams(("arbitrary",), 32),
        name="gates",
    )(*rows(gates_tok_c), *rows(gates_tok), gb[:, 0].reshape(N_DH, 1), gb[:, 1].reshape(N_DH, 1))
    a = a.reshape(b, N_DH, nc, chunk).transpose(0, 2, 1, 3)
    decay = jnp.concatenate([dc, dec], axis=2).reshape(b * N_DH, ncc + nc)
    return jnp.transpose(rc, (0, 2, 1)), jnp.transpose(r, (0, 2, 1)), a, decay


def _twice(x):
    return jnp.concatenate([x, x], axis=1)


def _mlstm_kernel(dec_ref, q_ref, k_ref, v_ref, oz_ref, p_ref, a_ref, kc_ref, vc_ref, pc_ref,
                  sel_ref, hnw_ref, out_ref, ct_ref, n_ref, hs_ref, *, chunk):
    t = q_ref.shape[1]
    nc = t // chunk
    ncc = kc_ref.shape[1] // chunk
    bi, hi = pl.program_id(0), pl.program_id(1)
    t_idx = lax.broadcasted_iota(jnp.int32, (chunk, chunk), 0)
    s_idx = lax.broadcasted_iota(jnp.int32, (chunk, chunk), 1)

    ct_ref[...] = jnp.zeros(ct_ref.shape, F32)
    n_ref[...] = jnp.zeros(n_ref.shape, F32)

    def row_terms(p, d):
        w = jnp.dot(p, sel_ref[0, d], preferred_element_type=F32)
        return [w[:, i * LANES:(i + 1) * LANES] for i in range(len(TERM_PIECES))]

    def update(k, v, e_b, decay, d):
        ke = k.astype(F32) * _twice(e_b)
        upd = lax.dot_general(ke.astype(BF16), v, (((0,), (0,)), ((), ())), preferred_element_type=F32)
        ct_ref[d] = decay * ct_ref[d] + upd
        n_ref[d] = decay * n_ref[d] + jnp.sum(ke, axis=0, keepdims=True)

    def decay_of(c, d):
        return dec_ref[bi * N_DH + d * M_HEADS + hi, c]

    for d in range(N_DIR):
        for i in range(ncc):
            c = i if d == 0 else ncc - 1 - i
            rows = pl.ds(c * chunk, chunk)
            e_b = row_terms(pc_ref[0, rows, :], d)[2]
            update(kc_ref[0, rows, :], vc_ref[0, rows, :], e_b, decay_of(c, d), d)

    def run_chunk(c, d):
        rows = pl.ds(pl.multiple_of(c * chunk, chunk), chunk)
        q, k, v = q_ref[0, rows, :], k_ref[0, rows, :], v_ref[0, rows, :]
        neg_g_b, w_inter_b, e_b, neg_mt_b = row_terms(p_ref[0, rows, :], d)
        a_row = a_ref[0, c, pl.ds(d * M_HEADS + hi, 1), :]
        mask = (s_idx <= t_idx) if d == 0 else (s_idx >= t_idx)
        w_intra = jnp.exp2(jnp.where(mask, a_row + _twice(neg_g_b), NEG_BIG))
        s = _dot_nt(q, k) * w_intra
        num = (_twice(w_inter_b) * jnp.dot(q, ct_ref[d].astype(BF16), preferred_element_type=F32)
               + jnp.dot(s.astype(BF16), v, preferred_element_type=F32))
        den_b = (w_inter_b * jnp.sum(q.astype(F32) * n_ref[d], axis=-1, keepdims=True)
                 + jnp.sum(s, axis=-1, keepdims=True))
        h = num * _twice(1.0 / jnp.maximum(jnp.abs(den_b), jnp.exp2(neg_mt_b)))
        update(k, v, e_b, decay_of(ncc + c, d), d)
        return rows, h

    def finalize(rows, h):
        hn = h * lax.rsqrt(jnp.mean(h * h, axis=-1, keepdims=True) + EPS) * hnw_ref[...]
        out_ref[0, rows, :] = (hn * oz_ref[0, rows, :].astype(F32)).astype(out_ref.dtype)

    def first(i, carry):
        for d in range(N_DIR):
            rows, h = run_chunk(i if d == 0 else nc - 1 - i, d)
            hs_ref[rows, :] = h
        return carry

    def second(i, carry):
        for d in range(N_DIR):
            rows, h = run_chunk(i if d == 0 else nc - 1 - i, d)
            finalize(rows, hs_ref[rows, :] + h)
        return carry

    lax.fori_loop(0, nc // 2, first, 0, unroll=SCAN_UNROLL)
    lax.fori_loop(nc // 2, nc, second, 0, unroll=SCAN_UNROLL)


def _term_selectors():
    sel = np.zeros((M_HEADS, N_DIR, LANES, len(TERM_PIECES) * LANES), np.float32)
    first = np.concatenate([[0], np.cumsum(TERM_PIECES)])
    for h in range(M_HEADS):
        for d in range(N_DIR):
            for q in range(len(TERM_PIECES)):
                for j in range(first[q], first[q + 1]):
                    sel[h, d, j * N_DH + d * M_HEADS + h, q * LANES:(q + 1) * LANES] = 1.0
    return jnp.asarray(sel, BF16)


def _mlstm(y, yc, p, a, pc, decay, head_norm_w, chunk, col):
    b, t, _ = y.shape
    tc = yc.shape[1]
    dh = MXU_DIM
    sel = _term_selectors()
    assert t % (2 * chunk) == 0 and tc % chunk == 0
    blk = lambda name: pl.BlockSpec((1, t, dh), lambda i, h, c=col[name] // dh: (i, 0, c + h))
    blkc = lambda col0: pl.BlockSpec((1, tc, dh), lambda i, h, c=col0: (i, 0, c + h))
    return pl.pallas_call(
        functools.partial(_mlstm_kernel, chunk=chunk),
        grid=(b, M_HEADS),
        in_specs=[pl.BlockSpec(memory_space=pltpu.SMEM),
                  blk("q"), blk("k"), blk("v"), blk("oz"),
                  pl.BlockSpec((1, t, LANES), lambda i, h: (i, 0, 0)),
                  pl.BlockSpec((1, t // chunk, N_DH, chunk), lambda i, h: (i, 0, 0, 0)),
                  blkc(0), blkc(M_HEADS),
                  pl.BlockSpec((1, tc, LANES), lambda i, h: (i, 0, 0)),
                  pl.BlockSpec((1, N_DIR) + sel.shape[2:], lambda i, h: (h, 0, 0, 0)),
                  pl.BlockSpec((1, dh), lambda i, h: (0, h))],
        out_specs=pl.BlockSpec((1, t, dh), lambda i, h: (i, 0, h)),
        out_shape=jax.ShapeDtypeStruct((b, t, M_HEADS * dh), BF16),
        scratch_shapes=[pltpu.VMEM((N_DIR, dh, dh), F32),
                        pltpu.VMEM((N_DIR, 1, dh), F32),
                        pltpu.VMEM((t, dh), F32)],
        compiler_params=_cparams(("arbitrary", "arbitrary"), 48),
        name="mlstm",
    )(decay, y, y, y, y, p, a, yc, yc, pc, sel, head_norm_w)


def _window_bounds(pos, w, length):
    return np.clip(pos - w // 2, 0, length), np.clip(pos + w - w // 2, 0, length)


def _col_pool_matrices(tile):
    pos = np.arange(GRID_W)
    mats = []
    for w in POOL_WINDOWS:
        lo, hi = _window_bounds(pos, w, GRID_W)
        band = ((pos[None, :] >= lo[:, None]) & (pos[None, :] < hi[:, None])).astype(np.float32)
        mats.append(np.kron(np.eye(tile // GRID_W, dtype=np.float32), band))
    return jnp.asarray(np.stack(mats), dtype=BF16)


def _pool_kernel(p_ref, z_ref, mc_ref, pw_ref, ps_ref, out_ref, pad_ref, scale_ref, *, tile, rows):
    t = p_ref.shape[1]
    gd = LANES
    pad = (max(POOL_WINDOWS) // 2) * GRID_W
    n_tiles = t // tile

    @pl.when(pl.program_id(0) == 0)
    def _():
        pad_ref[...] = jnp.zeros(pad_ref.shape, F32)
        tok = lax.broadcasted_iota(jnp.int32, (t, gd), 0)
        r = tok // GRID_W
        c = tok % GRID_W
        for g, w in enumerate(POOL_WINDOWS):
            cnt_r = jnp.minimum(r + (w - w // 2), rows) - jnp.maximum(r - w // 2, 0)
            cnt_c = jnp.minimum(c + (w - w // 2), GRID_W) - jnp.maximum(c - w // 2, 0)
            scale_ref[g] = 1.0 / (cnt_r.astype(F32) * cnt_c.astype(F32))

    for g, w in enumerate(POOL_WINDOWS):
        cols = pl.ds(g * gd, gd)

        def col_sum(i, carry):
            start = pl.multiple_of(i * tile, tile)
            pad_ref[pl.ds(pad + start, tile), :] = jnp.dot(
                mc_ref[g], p_ref[0, pl.ds(start, tile), cols], preferred_element_type=F32)
            return carry

        lax.fori_loop(0, n_tiles, col_sum, 0, unroll=POOL_UNROLL)

        def row_sum(i, carry):
            start = pl.multiple_of(i * tile, tile)
            acc = pad_ref[pl.ds(pad + start - (w // 2) * GRID_W, tile), :]
            for j in range(1 - w // 2, w - w // 2):
                acc = acc + pad_ref[pl.ds(pad + start + j * GRID_W, tile), :]
            mixed = acc * scale_ref[g, pl.ds(start, tile), :] - p_ref[0, pl.ds(start, tile), cols].astype(F32)
            mm = jnp.dot(mixed.astype(BF16), pw_ref[g], preferred_element_type=F32)
            zz = z_ref[0, pl.ds(start, tile), cols].astype(F32)
            out_ref[0, pl.ds(start, tile), cols] = (mm * ps_ref[:, cols] * zz).astype(out_ref.dtype)
            return carry

        lax.fori_loop(0, n_tiles, row_sum, 0, unroll=POOL_UNROLL)


def _pool(y, pool_w, pool_scale, p_col0):
    b, t, _ = y.shape
    pw = P_GROUPS * LANES
    tile = MXU_DIM
    pad = (max(POOL_WINDOWS) // 2) * GRID_W
    return pl.pallas_call(
        functools.partial(_pool_kernel, tile=tile, rows=t // GRID_W),
        grid=(b,),
        in_specs=[pl.BlockSpec((1, t, pw), lambda i: (i, 0, p_col0 // pw)),
                  pl.BlockSpec((1, t, pw), lambda i: (i, 0, p_col0 // pw + 1)),
                  pl.BlockSpec((P_GROUPS, tile, tile), lambda i: (0, 0, 0)),
                  pl.BlockSpec((P_GROUPS, LANES, LANES), lambda i: (0, 0, 0)),
                  pl.BlockSpec((1, pw), lambda i: (0, 0))],
        out_specs=pl.BlockSpec((1, t, pw), lambda i: (i, 0, 0)),
        out_shape=jax.ShapeDtypeStruct((b, t, pw), BF16),
        scratch_shapes=[pltpu.VMEM((t + 2 * pad, LANES), F32),
                        pltpu.VMEM((P_GROUPS, t, LANES), F32)],
        compiler_params=_cparams(("arbitrary",), 48),
        name="pool",
    )(y, y, _col_pool_matrices(tile), pool_w, pool_scale)


def _merge_kernel(x_ref, ym_ref, yp_ref, gm_ref, gp_ref, gate_ref, wm_ref, wp_ref, wo_ref, fw_ref, o_ref):
    y_m = jnp.dot(ym_ref[...], wm_ref[...], preferred_element_type=F32)
    y_p = jnp.dot(yp_ref[...], wp_ref[...], preferred_element_type=F32)
    merged = _sigmoid_t(gm_ref[...].astype(F32)) * y_m + _sigmoid_t(gp_ref[...].astype(F32)) * y_p
    upd = jnp.dot(merged.astype(BF16), wo_ref[...], preferred_element_type=F32)
    xo = x_ref[...] + gate_ref[0] * upd
    o_ref[...] = xo * lax.rsqrt(jnp.mean(xo * xo, axis=-1, keepdims=True) + EPS) * fw_ref[...]


def _merge(x2, ym, yp, y, gate, wm, wp, wo, final_w, rows_per_mod, g_col0):
    n, d = x2.shape
    tm = 512
    per = rows_per_mod // tm
    pw = yp.shape[1]
    full = lambda shape: pl.BlockSpec(shape, lambda i: (0,) * len(shape))
    return pl.pallas_call(
        _merge_kernel,
        grid=(n // tm,),
        in_specs=[pl.BlockSpec((tm, d), lambda i: (i, 0)),
                  pl.BlockSpec((tm, d), lambda i: (i, 0)),
                  pl.BlockSpec((tm, pw), lambda i: (i, 0)),
                  pl.BlockSpec((tm, d), lambda i: (i, g_col0 // d)),
                  pl.BlockSpec((tm, d), lambda i: (i, g_col0 // d + 1)),
                  pl.BlockSpec((1, 1, d), lambda i: (i // per, 0, 0)),
                  full((d, d)), full((pw, d)), full((d, d)), full((1, d))],
        out_specs=pl.BlockSpec((tm, d), lambda i: (i, 0)),
        out_shape=jax.ShapeDtypeStruct((n, d), F32),
        compiler_params=_cparams(("arbitrary",), 48),
        name="merge",
    )(x2, ym, yp, y, y, gate, wm, wp, wo, final_w)


def kernel(x, c, ctx, c_ctx, norm_w, ada_w, ada_b, in_w, gate_b, head_norm_w, pool_w, pool_scale,
           branch_m_w, branch_p_w, out_w, final_norm_w):
    b, t, d = x.shape
    tc = ctx.shape[1]
    depth = norm_w.shape[0]
    assert depth == 1, "context stream update between layers is not implemented"
    mw = M_HEADS * MXU_DIM
    assert d == mw
    g0 = 5 * mw

    cc = jnp.zeros((2 * 8, d), F32).at[:b].set(c).at[b].set(c_ctx)
    mod = _adaln(cc, ada_w[0], ada_b[0])
    shift, scale, gate = mod[:, :d], mod[:, d:2 * d], mod[:, 2 * d:]

    wt = jnp.transpose(in_w[0])
    row = jnp.arange(wt.shape[0])[:, None]
    wt = (wt * jnp.where((row >= mw) & (row < 2 * mw), MXU_DIM ** -0.5, 1.0)).astype(BF16)
    nw = norm_w[0].reshape(1, d)

    r_q, r_k, r_v, r_o, r_z = (i * mw for i in range(5))
    r_p = g0 + N_GATE_COLS
    r_gm, r_gp = r_p + mw, r_p + 2 * mw
    pairs = ((r_q, r_k, "id"), (r_v, r_p, "half_silu"), (r_o, r_z, "gate"), (r_gm, r_gp, "id"))
    col = {name: i * mw for i, name in enumerate(("q", "k", "v", "p", "o", "oz", "g_m", "g_p"))}

    y, gates_tok = _in_proj(x.reshape(b * t, d), scale[:b, None, :], shift[:b, None, :], nw,
                            wt, g0, t, "in_proj", pairs)
    yc, gates_tok_c = _in_proj(ctx.reshape(b * tc, d), scale[b:b + 1, None, :], shift[b:b + 1, None, :], nw,
                               wt, g0, b * tc, "in_proj_ctx", ((r_k, r_v, "id"),))
    y = y.reshape(b, t, -1)
    yc = yc.reshape(b, tc, -1)

    pc, p, a, decay = _gates(gates_tok.reshape(b, t, LANES), gates_tok_c.reshape(b, tc, LANES), gate_b[0], CHUNK)
    ym = _mlstm(y, yc, p, a, pc, decay, head_norm_w[0].reshape(1, mw), CHUNK, col)
    yp = _pool(y, pool_w[0].astype(BF16), pool_scale[0].reshape(1, -1), col["p"])

    out = _merge(x.reshape(b * t, d), ym.reshape(b * t, mw), yp.reshape(b * t, -1), y.reshape(b * t, -1),
                 gate[:b, None, :], branch_m_w[0].astype(BF16), branch_p_w[0].astype(BF16),
                 out_w[0].astype(BF16), final_norm_w.reshape(1, d), t, col["g_m"])
    return out.reshape(b, t, d)
```

```python
import functools

import numpy as np
import jax
import jax.numpy as jnp
from jax import lax
from jax.experimental import pallas as pl
from jax.experimental.pallas import tpu as pltpu

F32 = jnp.float32
BF16 = jnp.bfloat16

EPS = 1e-6
LOG2E = 1.4426950408889634
M_INIT = -1e30
NEG_BIG = -1e30
M_HEADS = 4
N_DIR = 2
N_GATE_COLS = N_DIR * 2 * M_HEADS
P_GROUPS = 4
POOL_WINDOWS = (2, 4, 8, 16)
GRID_W = 64
LANES = 128
MXU_DIM = 256
CHUNK = 256
POOL_UNROLL = 4
SCAN_UNROLL = 4
MIB = 1024 * 1024


def _cparams(sem, vmem_mib):
    return pltpu.CompilerParams(dimension_semantics=sem, vmem_limit_bytes=vmem_mib * MIB)


def _sigmoid(x):
    return 1.0 / (1.0 + jnp.exp(-x))


def _silu(x):
    return x * _sigmoid(x)


def _log_sigmoid(x):
    return jnp.minimum(x, 0.0) - jnp.log(1.0 + jnp.exp(-jnp.abs(x)))


def _dot_nt(a, b):
    return lax.dot_general(a, b, (((1,), (1,)), ((), ())), preferred_element_type=F32)


def _adaln_kernel(c_ref, w_ref, b_ref, o_ref):
    o_ref[...] = jnp.dot(_silu(c_ref[...]), w_ref[...], preferred_element_type=F32) + b_ref[...]


def _adaln(cc, ada_w, ada_b):
    rows, d = cc.shape
    n = ada_w.shape[1]
    tn = d
    return pl.pallas_call(
        _adaln_kernel,
        grid=(n // tn,),
        in_specs=[pl.BlockSpec((rows, d), lambda j: (0, 0)),
                  pl.BlockSpec((d, tn), lambda j: (0, j)),
                  pl.BlockSpec((1, tn), lambda j: (0, j))],
        out_specs=pl.BlockSpec((rows, tn), lambda j: (0, j)),
        out_shape=jax.ShapeDtypeStruct((rows, n), F32),
        compiler_params=_cparams(("arbitrary",), 32),
        name="adaln",
    )(cc, ada_w, ada_b.reshape(1, n))


def _sigmoid_t(x):
    return 0.5 * jnp.tanh(0.5 * x) + 0.5


IN_TILE = 1024
GATE_SUB = 256


def _inproj_kernel(x_ref, sc_ref, sh_ref, nw_ref, wa_ref, wb_ref, wg_ref, y_ref, g_ref, xn_ref, *, acts):
    j = pl.program_id(1)

    @pl.when(j == 0)
    def _():
        x = x_ref[...]
        ms = jnp.mean(x * x, axis=-1, keepdims=True)
        xn = x * lax.rsqrt(ms + EPS) * (nw_ref[...] * (1.0 + sc_ref[0])) + sh_ref[0]
        xn_ref[...] = xn.astype(BF16)
        wg = wg_ref[...]
        wg = jnp.concatenate([wg, jnp.zeros((LANES - wg.shape[0], wg.shape[1]), wg.dtype)], axis=0)
        g_ref[...] = _dot_nt(xn_ref[...], wg)

    def pair(act):
        tn = IN_TILE
        xn = xn_ref[...]
        if act == "gate":
            for c in range(0, tn, GATE_SUB):
                a = _dot_nt(xn, wa_ref[c:c + GATE_SUB, :])
                b = _dot_nt(xn, wb_ref[c:c + GATE_SUB, :])
                y_ref[:, c:c + GATE_SUB] = a.astype(BF16)
                y_ref[:, tn + c:tn + c + GATE_SUB] = (_sigmoid_t(a) * (b * _sigmoid_t(b))).astype(BF16)
            return
        y_ref[:, :tn] = _dot_nt(xn, wa_ref[...]).astype(BF16)
        b = _dot_nt(xn, wb_ref[...])
        if act == "half_silu":
            half = tn // 2
            b = jnp.concatenate([b[:, :half], b[:, half:] * _sigmoid_t(b[:, half:])], axis=1)
        y_ref[:, tn:] = b.astype(BF16)

    for act in sorted(set(acts)):
        hit = functools.reduce(jnp.logical_or, [j == s for s, a in enumerate(acts) if a == act])
        pl.when(hit)(functools.partial(pair, act))


def _in_proj(x2, scale, shift, norm_w, w, gate_row0, rows_per_mod, name, pairs):
    n, d = x2.shape
    tm = min(2048, rows_per_mod)
    tn = IN_TILE
    per = rows_per_mod // tm
    acts = tuple(p[2] for p in pairs)

    def w_rows(which):
        def index_map(i, j):
            r = sum(jnp.where(j == s, p[which], 0) for s, p in enumerate(pairs))
            return (pl.multiple_of(r, N_GATE_COLS), 0)
        return index_map

    return pl.pallas_call(
        functools.partial(_inproj_kernel, acts=acts),
        grid=(n // tm, len(pairs)),
        in_specs=[pl.BlockSpec((tm, d), lambda i, j: (i, 0)),
                  pl.BlockSpec((1, 1, d), lambda i, j: (i // per, 0, 0)),
                  pl.BlockSpec((1, 1, d), lambda i, j: (i // per, 0, 0)),
                  pl.BlockSpec((1, d), lambda i, j: (0, 0)),
                  pl.BlockSpec((pl.Element(tn), pl.Element(d)), w_rows(0)),
                  pl.BlockSpec((pl.Element(tn), pl.Element(d)), w_rows(1)),
                  pl.BlockSpec((pl.Element(N_GATE_COLS), pl.Element(d)), lambda i, j: (gate_row0, 0))],
        out_specs=[pl.BlockSpec((tm, 2 * tn), lambda i, j: (i, j)),
                   pl.BlockSpec((tm, LANES), lambda i, j: (i, 0))],
        out_shape=[jax.ShapeDtypeStruct((n, 2 * tn * len(pairs)), BF16),
                   jax.ShapeDtypeStruct((n, LANES), F32)],
        scratch_shapes=[pltpu.VMEM((tm, d), BF16)],
        compiler_params=_cparams(("arbitrary", "arbitrary"), 56),
        name=name,
    )(x2, scale, shift, norm_w, w, w, w)


def _ctx_proj_kernel(x_ref, sc_ref, sh_ref, nw_ref, wsrc_ref, wk_ref, wv_ref, wg_ref,
                     wdst_ref, y_ref, g_ref, xn_ref, *, n_cast, steps, k_rows):
    i, j = pl.program_id(0), pl.program_id(1)
    slab = i * steps + j
    kscale = MXU_DIM ** -0.5

    @pl.when(slab < n_cast)
    def _():
        rows = wsrc_ref.shape[0]
        r = slab * rows + lax.broadcasted_iota(jnp.int32, (rows, 1), 0)
        scale = jnp.where((r >= k_rows[0]) & (r < k_rows[1]), kscale, 1.0)
        wdst_ref[...] = (wsrc_ref[...] * scale).astype(BF16)

    @pl.when(j == 0)
    def _():
        x = x_ref[...]
        ms = jnp.mean(x * x, axis=-1, keepdims=True)
        xn = x * lax.rsqrt(ms + EPS) * (nw_ref[...] * (1.0 + sc_ref[0])) + sh_ref[0]
        xn_ref[...] = xn.astype(BF16)
        wg = wg_ref[...].astype(BF16)
        wg = jnp.concatenate([wg, jnp.zeros((LANES - wg.shape[0], wg.shape[1]), wg.dtype)], axis=0)
        g_ref[...] = _dot_nt(xn_ref[...], wg)

    @pl.when(j == 1)
    def _():
        y_ref[:, :IN_TILE] = _dot_nt(xn_ref[...], (wk_ref[...] * kscale).astype(BF16)).astype(BF16)

    @pl.when(j == 2)
    def _():
        y_ref[:, IN_TILE:] = _dot_nt(xn_ref[...], wv_ref[...].astype(BF16)).astype(BF16)


def _ctx_proj(xc, scale, shift, norm_w, wt, gate_row0, r_k, r_v):
    n, d = xc.shape
    tm, tn = 1024, IN_TILE
    n_cast = 9
    slab = wt.shape[0] // n_cast
    assert slab * n_cast == wt.shape[0] and slab % 16 == 0 and n % tm == 0
    steps = -(-n_cast // (n // tm))
    assert steps >= 3
    cast_idx = lambda i, j: (jnp.minimum(i * steps + j, n_cast - 1), 0)
    elem = lambda rows, r0: pl.BlockSpec((pl.Element(rows), pl.Element(d)), lambda i, j: (r0, 0))
    return pl.pallas_call(
        functools.partial(_ctx_proj_kernel, n_cast=n_cast, steps=steps, k_rows=(r_k, r_k + tn)),
        grid=(n // tm, steps),
        in_specs=[pl.BlockSpec((tm, d), lambda i, j: (i, 0)),
                  pl.BlockSpec((1, 1, d), lambda i, j: (0, 0, 0)),
                  pl.BlockSpec((1, 1, d), lambda i, j: (0, 0, 0)),
                  pl.BlockSpec((1, d), lambda i, j: (0, 0)),
                  pl.BlockSpec((slab, d), cast_idx),
                  elem(tn, r_k), elem(tn, r_v), elem(N_GATE_COLS, gate_row0)],
        out_specs=[pl.BlockSpec((slab, d), cast_idx),
                   pl.BlockSpec((tm, 2 * tn), lambda i, j: (i, 0)),
                   pl.BlockSpec((tm, LANES), lambda i, j: (i, 0))],
        out_shape=[jax.ShapeDtypeStruct(wt.shape, BF16),
                   jax.ShapeDtypeStruct((n, 2 * tn), BF16),
                   jax.ShapeDtypeStruct((n, LANES), F32)],
        scratch_shapes=[pltpu.VMEM((tm, d), BF16)],
        compiler_params=_cparams(("arbitrary", "arbitrary"), 56),
        name="ctx_proj",
    )(xc, scale, shift, norm_w, wt, wt, wt, wt)


N_DH = N_DIR * M_HEADS
TERM_PIECES = (3, 2, 2, 3)


def _split(x, pieces):
    out = []
    for _ in range(pieces):
        p = x.astype(BF16)
        out.append(p)
        x = x - p.astype(F32)
    return out


def _chunk_scan(x, pos, chunk, op, reverse):
    t = x.shape[1]
    k = 1
    while k < chunk:
        if reverse:
            x = jnp.where(pos < chunk - k, op(x, pltpu.roll(x, t - k, 1)), x)
        else:
            x = jnp.where(pos >= k, op(x, pltpu.roll(x, k, 1)), x)
        k *= 2
    return x


def _gate_terms(li, pre_f, m0, chunk):
    rows, t = li.shape
    nc = t // chunk
    pos = lax.broadcasted_iota(jnp.int32, (rows, t), 1) % chunk
    rev = lax.broadcasted_iota(jnp.int32, (rows, t), 0) >= M_HEADS
    rev1 = lax.broadcasted_iota(jnp.int32, (rows, 1), 0) >= M_HEADS
    lf = _log_sigmoid(pre_f)
    b = jnp.where(rev, _chunk_scan(lf, pos, chunk, jnp.add, True), _chunk_scan(lf, pos, chunk, jnp.add, False))
    a = li - b
    cmax = jnp.where(rev, _chunk_scan(a, pos, chunk, jnp.maximum, True),
                     _chunk_scan(a, pos, chunk, jnp.maximum, False))

    def at_scan_end(x, c):
        lo = c * chunk
        return jnp.where(rev1, x[:, lo:lo + 1], x[:, lo + chunk - 1:lo + chunk])

    b_end = [at_scan_end(b, c) for c in range(nc)]
    a_max = [at_scan_end(cmax, c) for c in range(nc)]
    m_f, m_b = m0, m0
    m_in_f, m_in_b = [None] * nc, [None] * nc
    for i in range(nc):
        j = nc - 1 - i
        m_in_f[i] = m_f
        m_f = b_end[i] + jnp.maximum(m_f, a_max[i])
        m_in_b[j] = m_b
        m_b = b_end[j] + jnp.maximum(m_b, a_max[j])
    m_in = [jnp.where(rev1, m_in_b[c], m_in_f[c]) for c in range(nc)]
    per_chunk = lambda vals: jnp.concatenate([jnp.broadcast_to(v, (rows, chunk)) for v in vals], axis=1)
    m_in_t = per_chunk(m_in)
    g = jnp.maximum(m_in_t, cmax)
    g_end = [jnp.maximum(m_in[c], a_max[c]) for c in range(nc)]
    pieces = []
    terms = (-g * LOG2E, jnp.exp(m_in_t - g), jnp.exp(a - per_chunk(g_end)), -(b + g) * LOG2E)
    for term, n in zip(terms, TERM_PIECES):
        pieces += _split(term, n)
    pieces.append(jnp.zeros((LANES - rows * sum(TERM_PIECES), t), BF16))
    decay = jnp.concatenate([jnp.exp(m_in[c] - g_end[c]) for c in range(nc)], axis=1)
    return jnp.concatenate(pieces, axis=0), a * LOG2E, decay, jnp.where(rev1, m_b, m_f)


def _gates_kernel(ic_ref, fc_ref, i_ref, f_ref, bi_ref, bf_ref, rc_ref, dc_ref, r_ref, a_ref, d_ref, *, chunk):
    m0 = jnp.full((N_DH, 1), M_INIT, F32)
    rc_ref[0], _, dc_ref[0], m_ctx = _gate_terms(ic_ref[0] + bi_ref[...], fc_ref[0] + bf_ref[...], m0, chunk)
    r_ref[0], a_ref[0], d_ref[0], _ = _gate_terms(i_ref[0] + bi_ref[...], f_ref[0] + bf_ref[...], m_ctx, chunk)


def _gates(gates_tok, gates_tok_c, gate_b, chunk):
    b, t, _ = gates_tok.shape
    tc = gates_tok_c.shape[1]
    nc, ncc = t // chunk, tc // chunk

    def rows(g):
        g = jnp.transpose(g[:, :, :N_GATE_COLS], (0, 2, 1)).reshape(b, N_DIR, 2, M_HEADS, -1)
        return g[:, :, 0].reshape(b, N_DH, -1), g[:, :, 1].reshape(b, N_DH, -1)

    gb = gate_b.reshape(N_DIR, 2, M_HEADS)
    spec = lambda r, n: pl.BlockSpec((1, r, n), lambda i: (i, 0, 0))
    col = pl.BlockSpec((N_DH, 1), lambda i: (0, 0))
    rc, dc, r, a, dec = pl.pallas_call(
        functools.partial(_gates_kernel, chunk=chunk),
        grid=(b,),
        in_specs=[spec(N_DH, tc), spec(N_DH, tc), spec(N_DH, t), spec(N_DH, t), col, col],
        out_specs=[spec(LANES, tc), spec(N_DH, ncc), spec(LANES, t), spec(N_DH, t), spec(N_DH, nc)],
        out_shape=[jax.ShapeDtypeStruct((b, LANES, tc), BF16), jax.ShapeDtypeStruct((b, N_DH, ncc), F32),
                   jax.ShapeDtypeStruct((b, LANES, t), BF16), jax.ShapeDtypeStruct((b, N_DH, t), F32),
                   jax.ShapeDtypeStruct((b, N_DH, nc), F32)],
        compiler_params=_cparams(("arbitrary",), 32),
        name="gates",
    )(*rows(gates_tok_c), *rows(gates_tok), gb[:, 0].reshape(N_DH, 1), gb[:, 1].reshape(N_DH, 1))
    a = a.reshape(b, N_DH, nc, chunk).transpose(0, 2, 1, 3)
    decay = jnp.concatenate([dc, dec], axis=2).reshape(b * N_DH, ncc + nc)
    return jnp.transpose(rc, (0, 2, 1)), jnp.transpose(r, (0, 2, 1)), a, decay


def _twice(x):
    return jnp.concatenate([x, x], axis=1)


def _mlstm_kernel(dec_ref, q_ref, k_ref, v_ref, oz_ref, p_ref, a_ref, kc_ref, vc_ref, pc_ref,
                  sel_ref, hnw_ref, out_ref, ct_ref, n_ref, hs_ref, *, chunk):
    t = q_ref.shape[1]
    nc = t // chunk
    ncc = kc_ref.shape[1] // chunk
    bi, hi = pl.program_id(0), pl.program_id(1)
    t_idx = lax.broadcasted_iota(jnp.int32, (chunk, chunk), 0)
    s_idx = lax.broadcasted_iota(jnp.int32, (chunk, chunk), 1)

    ct_ref[...] = jnp.zeros(ct_ref.shape, F32)
    n_ref[...] = jnp.zeros(n_ref.shape, F32)

    def row_terms(p, d):
        w = jnp.dot(p, sel_ref[0, d], preferred_element_type=F32)
        return [w[:, i * LANES:(i + 1) * LANES] for i in range(len(TERM_PIECES))]

    def update(k, v, e_b, decay, d):
        ke = k.astype(F32) * _twice(e_b)
        upd = lax.dot_general(ke.astype(BF16), v, (((0,), (0,)), ((), ())), preferred_element_type=F32)
        ct_ref[d] = decay * ct_ref[d] + upd
        n_ref[d] = decay * n_ref[d] + jnp.sum(ke, axis=0, keepdims=True)

    def decay_of(c, d):
        return dec_ref[bi * N_DH + d * M_HEADS + hi, c]

    for d in range(N_DIR):
        for i in range(ncc):
            c = i if d == 0 else ncc - 1 - i
            rows = pl.ds(c * chunk, chunk)
            e_b = row_terms(pc_ref[0, rows, :], d)[2]
            update(kc_ref[0, rows, :], vc_ref[0, rows, :], e_b, decay_of(c, d), d)

    def run_chunk(c, d):
        rows = pl.ds(pl.multiple_of(c * chunk, chunk), chunk)
        q, k, v = q_ref[0, rows, :], k_ref[0, rows, :], v_ref[0, rows, :]
        neg_g_b, w_inter_b, e_b, neg_mt_b = row_terms(p_ref[0, rows, :], d)
        a_row = a_ref[0, c, pl.ds(d * M_HEADS + hi, 1), :]
        mask = (s_idx <= t_idx) if d == 0 else (s_idx >= t_idx)
        w_intra = jnp.exp2(jnp.where(mask, a_row + _twice(neg_g_b), NEG_BIG))
        s = _dot_nt(q, k) * w_intra
        num = (_twice(w_inter_b) * jnp.dot(q, ct_ref[d].astype(BF16), preferred_element_type=F32)
               + jnp.dot(s.astype(BF16), v, preferred_element_type=F32))
        den_b = (w_inter_b * jnp.sum(q.astype(F32) * n_ref[d], axis=-1, keepdims=True)
                 + jnp.sum(s, axis=-1, keepdims=True))
        h = num * _twice(1.0 / jnp.maximum(jnp.abs(den_b), jnp.exp2(neg_mt_b)))
        update(k, v, e_b, decay_of(ncc + c, d), d)
        return rows, h

    def finalize(rows, h):
        hn = h * lax.rsqrt(jnp.mean(h * h, axis=-1, keepdims=True) + EPS) * hnw_ref[...]
        out_ref[0, rows, :] = (hn * oz_ref[0, rows, :].astype(F32)).astype(out_ref.dtype)

    def first(i, carry):
        for d in range(N_DIR):
            rows, h = run_chunk(i if d == 0 else nc - 1 - i, d)
            hs_ref[rows, :] = h
        return carry

    def second(i, carry):
        for d in range(N_DIR):
            rows, h = run_chunk(i if d == 0 else nc - 1 - i, d)
            finalize(rows, hs_ref[rows, :] + h)
        return carry

    lax.fori_loop(0, nc // 2, first, 0, unroll=SCAN_UNROLL)
    lax.fori_loop(nc // 2, nc, second, 0, unroll=SCAN_UNROLL)


def _term_selectors():
    sel = np.zeros((M_HEADS, N_DIR, LANES, len(TERM_PIECES) * LANES), np.float32)
    first = np.concatenate([[0], np.cumsum(TERM_PIECES)])
    for h in range(M_HEADS):
        for d in range(N_DIR):
            for q in range(len(TERM_PIECES)):
                for j in range(first[q], first[q + 1]):
                    sel[h, d, j * N_DH + d * M_HEADS + h, q * LANES:(q + 1) * LANES] = 1.0
    return jnp.asarray(sel, BF16)


def _mlstm(y, yc, p, a, pc, decay, head_norm_w, chunk, col):
    b, t, _ = y.shape
    tc = yc.shape[1]
    dh = MXU_DIM
    sel = _term_selectors()
    assert t % (2 * chunk) == 0 and tc % chunk == 0
    blk = lambda name: pl.BlockSpec((1, t, dh), lambda i, h, c=col[name] // dh: (i, 0, c + h))
    blkc = lambda col0: pl.BlockSpec((1, tc, dh), lambda i, h, c=col0: (i, 0, c + h))
    return pl.pallas_call(
        functools.partial(_mlstm_kernel, chunk=chunk),
        grid=(b, M_HEADS),
        in_specs=[pl.BlockSpec(memory_space=pltpu.SMEM),
                  blk("q"), blk("k"), blk("v"), blk("oz"),
                  pl.BlockSpec((1, t, LANES), lambda i, h: (i, 0, 0)),
                  pl.BlockSpec((1, t // chunk, N_DH, chunk), lambda i, h: (i, 0, 0, 0)),
                  blkc(0), blkc(M_HEADS),
                  pl.BlockSpec((1, tc, LANES), lambda i, h: (i, 0, 0)),
                  pl.BlockSpec((1, N_DIR) + sel.shape[2:], lambda i, h: (h, 0, 0, 0)),
                  pl.BlockSpec((1, dh), lambda i, h: (0, h))],
        out_specs=pl.BlockSpec((1, t, dh), lambda i, h: (i, 0, h)),
        out_shape=jax.ShapeDtypeStruct((b, t, M_HEADS * dh), BF16),
        scratch_shapes=[pltpu.VMEM((N_DIR, dh, dh), F32),
                        pltpu.VMEM((N_DIR, 1, dh), F32),
                        pltpu.VMEM((t, dh), F32)],
        compiler_params=_cparams(("arbitrary", "arbitrary"), 48),
        name="mlstm",
    )(decay, y, y, y, y, p, a, yc, yc, pc, sel, head_norm_w)


def _window_bounds(pos, w, length):
    return np.clip(pos - w // 2, 0, length), np.clip(pos + w - w // 2, 0, length)


def _col_pool_matrices(tile):
    pos = np.arange(GRID_W)
    mats = []
    for w in POOL_WINDOWS:
        lo, hi = _window_bounds(pos, w, GRID_W)
        band = ((pos[None, :] >= lo[:, None]) & (pos[None, :] < hi[:, None])).astype(np.float32)
        mats.append(np.kron(np.eye(tile // GRID_W, dtype=np.float32), band))
    return jnp.asarray(np.stack(mats), dtype=BF16)


def _pool_kernel(p_ref, z_ref, mc_ref, pw_ref, ps_ref, out_ref, pad_ref, scale_ref, *, tile, rows):
    t = p_ref.shape[1]
    gd = LANES
    pad = (max(POOL_WINDOWS) // 2) * GRID_W
    n_tiles = t // tile

    @pl.when(pl.program_id(0) == 0)
    def _():
        pad_ref[...] = jnp.zeros(pad_ref.shape, F32)
        tok = lax.broadcasted_iota(jnp.int32, (t, gd), 0)
        r = tok // GRID_W
        c = tok % GRID_W
        for g, w in enumerate(POOL_WINDOWS):
            cnt_r = jnp.minimum(r + (w - w // 2), rows) - jnp.maximum(r - w // 2, 0)
            cnt_c = jnp.minimum(c + (w - w // 2), GRID_W) - jnp.maximum(c - w // 2, 0)
            scale_ref[g] = 1.0 / (cnt_r.astype(F32) * cnt_c.astype(F32))

    for g, w in enumerate(POOL_WINDOWS):
        cols = pl.ds(g * gd, gd)

        def col_sum(i, carry):
            start = pl.multiple_of(i * tile, tile)
            pad_ref[pl.ds(pad + start, tile), :] = jnp.dot(
                mc_ref[g], p_ref[0, pl.ds(start, tile), cols], preferred_element_type=F32)
            return carry

        lax.fori_loop(0, n_tiles, col_sum, 0, unroll=POOL_UNROLL)

        def row_sum(i, carry):
            start = pl.multiple_of(i * tile, tile)
            acc = pad_ref[pl.ds(pad + start - (w // 2) * GRID_W, tile), :]
            for j in range(1 - w // 2, w - w // 2):
                acc = acc + pad_ref[pl.ds(pad + start + j * GRID_W, tile), :]
            mixed = acc * scale_ref[g, pl.ds(start, tile), :] - p_ref[0, pl.ds(start, tile), cols].astype(F32)
            mm = jnp.dot(mixed.astype(BF16), pw_ref[g], preferred_element_type=F32)
            zz = z_ref[0, pl.ds(start, tile), cols].astype(F32)
            out_ref[0, pl.ds(start, tile), cols] = (mm * ps_ref[:, cols] * zz).astype(out_ref.dtype)
            return carry

        lax.fori_loop(0, n_tiles, row_sum, 0, unroll=POOL_UNROLL)


def _pool(y, pool_w, pool_scale, p_col0):
    b, t, _ = y.shape
    pw = P_GROUPS * LANES
    tile = MXU_DIM
    pad = (max(POOL_WINDOWS) // 2) * GRID_W
    return pl.pallas_call(
        functools.partial(_pool_kernel, tile=tile, rows=t // GRID_W),
        grid=(b,),
        in_specs=[pl.BlockSpec((1, t, pw), lambda i: (i, 0, p_col0 // pw)),
                  pl.BlockSpec((1, t, pw), lambda i: (i, 0, p_col0 // pw + 1)),
                  pl.BlockSpec((P_GROUPS, tile, tile), lambda i: (0, 0, 0)),
                  pl.BlockSpec((P_GROUPS, LANES, LANES), lambda i: (0, 0, 0)),
                  pl.BlockSpec((1, pw), lambda i: (0, 0))],
        out_specs=pl.BlockSpec((1, t, pw), lambda i: (i, 0, 0)),
        out_shape=jax.ShapeDtypeStruct((b, t, pw), BF16),
        scratch_shapes=[pltpu.VMEM((t + 2 * pad, LANES), F32),
                        pltpu.VMEM((P_GROUPS, t, LANES), F32)],
        compiler_params=_cparams(("arbitrary",), 48),
        name="pool",
    )(y, y, _col_pool_matrices(tile), pool_w, pool_scale)


def _merge_kernel(x_ref, ym_ref, yp_ref, gm_ref, gp_ref, gate_ref, wm_ref, wp_ref, wo_ref, fw_ref, o_ref):
    y_m = jnp.dot(ym_ref[...], wm_ref[...], preferred_element_type=F32)
    y_p = jnp.dot(yp_ref[...], wp_ref[...], preferred_element_type=F32)
    merged = _sigmoid_t(gm_ref[...].astype(F32)) * y_m + _sigmoid_t(gp_ref[...].astype(F32)) * y_p
    upd = jnp.dot(merged.astype(BF16), wo_ref[...], preferred_element_type=F32)
    xo = x_ref[...] + gate_ref[0] * upd
    o_ref[...] = xo * lax.rsqrt(jnp.mean(xo * xo, axis=-1, keepdims=True) + EPS) * fw_ref[...]


def _merge(x2, ym, yp, y, gate, wm, wp, wo, final_w, rows_per_mod, g_col0):
    n, d = x2.shape
    tm = 1024
    per = rows_per_mod // tm
    pw = yp.shape[1]
    full = lambda shape: pl.BlockSpec(shape, lambda i: (0,) * len(shape))
    return pl.pallas_call(
        _merge_kernel,
        grid=(n // tm,),
        in_specs=[pl.BlockSpec((tm, d), lambda i: (i, 0)),
                  pl.BlockSpec((tm, d), lambda i: (i, 0)),
                  pl.BlockSpec((tm, pw), lambda i: (i, 0)),
                  pl.BlockSpec((tm, d), lambda i: (i, g_col0 // d)),
                  pl.BlockSpec((tm, d), lambda i: (i, g_col0 // d + 1)),
                  pl.BlockSpec((1, 1, d), lambda i: (i // per, 0, 0)),
                  full((d, d)), full((pw, d)), full((d, d)), full((1, d))],
        out_specs=pl.BlockSpec((tm, d), lambda i: (i, 0)),
        out_shape=jax.ShapeDtypeStruct((n, d), F32),
        compiler_params=_cparams(("arbitrary",), 48),
        name="merge",
    )(x2, ym, yp, y, y, gate, wm, wp, wo, final_w)


def kernel(x, c, ctx, c_ctx, norm_w, ada_w, ada_b, in_w, gate_b, head_norm_w, pool_w, pool_scale,
           branch_m_w, branch_p_w, out_w, final_norm_w):
    b, t, d = x.shape
    tc = ctx.shape[1]
    depth = norm_w.shape[0]
    assert depth == 1, "context stream update between layers is not implemented"
    mw = M_HEADS * MXU_DIM
    assert d == mw
    g0 = 5 * mw

    cc = jnp.zeros((2 * 8, d), F32).at[:b].set(c).at[b].set(c_ctx)
    mod = _adaln(cc, ada_w[0], ada_b[0])
    shift, scale, gate = mod[:, :d], mod[:, d:2 * d], mod[:, 2 * d:]

    wt_f32 = jnp.transpose(in_w[0])
    nw = norm_w[0].reshape(1, d)

    r_q, r_k, r_v, r_o, r_z = (i * mw for i in range(5))
    r_p = g0 + N_GATE_COLS
    r_gm, r_gp = r_p + mw, r_p + 2 * mw
    pairs = ((r_q, r_k, "id"), (r_v, r_p, "half_silu"), (r_o, r_z, "gate"), (r_gm, r_gp, "id"))
    col = {name: i * mw for i, name in enumerate(("q", "k", "v", "p", "o", "oz", "g_m", "g_p"))}

    wt, yc, gates_tok_c = _ctx_proj(ctx.reshape(b * tc, d), scale[b:b + 1, None, :], shift[b:b + 1, None, :],
                                    nw, wt_f32, g0, r_k, r_v)
    y, gates_tok = _in_proj(x.reshape(b * t, d), scale[:b, None, :], shift[:b, None, :], nw,
                            wt, g0, t, "in_proj", pairs)
    y = y.reshape(b, t, -1)
    yc = yc.reshape(b, tc, -1)

    pc, p, a, decay = _gates(gates_tok.reshape(b, t, LANES), gates_tok_c.reshape(b, tc, LANES), gate_b[0], CHUNK)
    ym = _mlstm(y, yc, p, a, pc, decay, head_norm_w[0].reshape(1, mw), CHUNK, col)
    yp = _pool(y, pool_w[0].astype(BF16), pool_scale[0].reshape(1, -1), col["p"])

    out = _merge(x.reshape(b * t, d), ym.reshape(b * t, mw), yp.reshape(b * t, -1), y.reshape(b * t, -1),
                 gate[:b, None, :], branch_m_w[0].astype(BF16), branch_p_w[0].astype(BF16),
                 out_w[0].astype(BF16), final_norm_w.reshape(1, d), t, col["g_m"])
    return out.reshape(b, t, d)
```

```python
import functools

import numpy as np
import jax
import jax.numpy as jnp
from jax import lax
from jax.experimental import pallas as pl
from jax.experimental.pallas import tpu as pltpu

F32 = jnp.float32
BF16 = jnp.bfloat16

EPS = 1e-6
LOG2E = 1.4426950408889634
M_INIT = -1e30
NEG_BIG = -1e30
M_HEADS = 4
N_DIR = 2
N_GATE_COLS = N_DIR * 2 * M_HEADS
P_GROUPS = 4
POOL_WINDOWS = (2, 4, 8, 16)
GRID_W = 64
LANES = 128
MXU_DIM = 256
CHUNK = 256
POOL_UNROLL = 4
SCAN_UNROLL = 4
MIB = 1024 * 1024


def _cparams(sem, vmem_mib):
    return pltpu.CompilerParams(dimension_semantics=sem, vmem_limit_bytes=vmem_mib * MIB)


def _sigmoid(x):
    return 1.0 / (1.0 + jnp.exp(-x))


def _silu(x):
    return x * _sigmoid(x)


def _log_sigmoid(x):
    return jnp.minimum(x, 0.0) - jnp.log(1.0 + jnp.exp(-jnp.abs(x)))


def _dot_nt(a, b):
    return lax.dot_general(a, b, (((1,), (1,)), ((), ())), preferred_element_type=F32)


def _adaln_kernel(c_ref, w_ref, b_ref, o_ref):
    o_ref[...] = jnp.dot(_silu(c_ref[...]), w_ref[...], preferred_element_type=F32) + b_ref[...]


def _adaln(cc, ada_w, ada_b):
    rows, d = cc.shape
    n = ada_w.shape[1]
    tn = d
    return pl.pallas_call(
        _adaln_kernel,
        grid=(n // tn,),
        in_specs=[pl.BlockSpec((rows, d), lambda j: (0, 0)),
                  pl.BlockSpec((d, tn), lambda j: (0, j)),
                  pl.BlockSpec((1, tn), lambda j: (0, j))],
        out_specs=pl.BlockSpec((rows, tn), lambda j: (0, j)),
        out_shape=jax.ShapeDtypeStruct((rows, n), F32),
        compiler_params=_cparams(("arbitrary",), 32),
        name="adaln",
    )(cc, ada_w, ada_b.reshape(1, n))


def _sigmoid_t(x):
    return 0.5 * jnp.tanh(0.5 * x) + 0.5


IN_TILE = 1024
GATE_SUB = 256


def _inproj_kernel(x_ref, sc_ref, sh_ref, nw_ref, wa_ref, wb_ref, wg_ref, y_ref, g_ref, xn_ref, *, acts):
    j = pl.program_id(1)

    @pl.when(j == 0)
    def _():
        x = x_ref[...]
        ms = jnp.mean(x * x, axis=-1, keepdims=True)
        xn = x * lax.rsqrt(ms + EPS) * (nw_ref[...] * (1.0 + sc_ref[0])) + sh_ref[0]
        xn_ref[...] = xn.astype(BF16)
        wg = wg_ref[...]
        wg = jnp.concatenate([wg, jnp.zeros((LANES - wg.shape[0], wg.shape[1]), wg.dtype)], axis=0)
        g_ref[...] = _dot_nt(xn_ref[...], wg)

    def pair(act):
        tn = IN_TILE
        xn = xn_ref[...]
        if act == "gate":
            for c in range(0, tn, GATE_SUB):
                a = _dot_nt(xn, wa_ref[c:c + GATE_SUB, :])
                b = _dot_nt(xn, wb_ref[c:c + GATE_SUB, :])
                y_ref[:, c:c + GATE_SUB] = a.astype(BF16)
                y_ref[:, tn + c:tn + c + GATE_SUB] = (_sigmoid_t(a) * (b * _sigmoid_t(b))).astype(BF16)
            return
        y_ref[:, :tn] = _dot_nt(xn, wa_ref[...]).astype(BF16)
        b = _dot_nt(xn, wb_ref[...])
        if act == "half_silu":
            half = tn // 2
            b = jnp.concatenate([b[:, :half], b[:, half:] * _sigmoid_t(b[:, half:])], axis=1)
        y_ref[:, tn:] = b.astype(BF16)

    for act in sorted(set(acts)):
        hit = functools.reduce(jnp.logical_or, [j == s for s, a in enumerate(acts) if a == act])
        pl.when(hit)(functools.partial(pair, act))


def _in_proj(x2, scale, shift, norm_w, w, gate_row0, rows_per_mod, name, pairs):
    n, d = x2.shape
    tm = min(2048, rows_per_mod)
    tn = IN_TILE
    per = rows_per_mod // tm
    acts = tuple(p[2] for p in pairs)

    def w_rows(which):
        def index_map(i, j):
            r = sum(jnp.where(j == s, p[which], 0) for s, p in enumerate(pairs))
            return (pl.multiple_of(r, N_GATE_COLS), 0)
        return index_map

    return pl.pallas_call(
        functools.partial(_inproj_kernel, acts=acts),
        grid=(n // tm, len(pairs)),
        in_specs=[pl.BlockSpec((tm, d), lambda i, j: (i, 0)),
                  pl.BlockSpec((1, 1, d), lambda i, j: (i // per, 0, 0)),
                  pl.BlockSpec((1, 1, d), lambda i, j: (i // per, 0, 0)),
                  pl.BlockSpec((1, d), lambda i, j: (0, 0)),
                  pl.BlockSpec((pl.Element(tn), pl.Element(d)), w_rows(0)),
                  pl.BlockSpec((pl.Element(tn), pl.Element(d)), w_rows(1)),
                  pl.BlockSpec((pl.Element(N_GATE_COLS), pl.Element(d)), lambda i, j: (gate_row0, 0))],
        out_specs=[pl.BlockSpec((tm, 2 * tn), lambda i, j: (i, j)),
                   pl.BlockSpec((tm, LANES), lambda i, j: (i, 0))],
        out_shape=[jax.ShapeDtypeStruct((n, 2 * tn * len(pairs)), BF16),
                   jax.ShapeDtypeStruct((n, LANES), F32)],
        scratch_shapes=[pltpu.VMEM((tm, d), BF16)],
        compiler_params=_cparams(("arbitrary", "arbitrary"), 56),
        name=name,
    )(x2, scale, shift, norm_w, w, w, w)


def _ctx_proj_kernel(x_ref, sc_ref, sh_ref, nw_ref, wsrc_ref, wk_ref, wv_ref, wg_ref,
                     wdst_ref, y_ref, g_ref, xn_ref, *, n_cast, steps, k_rows):
    i, j = pl.program_id(0), pl.program_id(1)
    slab = i * steps + j
    kscale = MXU_DIM ** -0.5

    @pl.when(slab < n_cast)
    def _():
        rows = wsrc_ref.shape[0]
        r = slab * rows + lax.broadcasted_iota(jnp.int32, (rows, 1), 0)
        scale = jnp.where((r >= k_rows[0]) & (r < k_rows[1]), kscale, 1.0)
        wdst_ref[...] = (wsrc_ref[...] * scale).astype(BF16)

    @pl.when(j == 0)
    def _():
        x = x_ref[...]
        ms = jnp.mean(x * x, axis=-1, keepdims=True)
        xn = x * lax.rsqrt(ms + EPS) * (nw_ref[...] * (1.0 + sc_ref[0])) + sh_ref[0]
        xn_ref[...] = xn.astype(BF16)
        wg = wg_ref[...].astype(BF16)
        wg = jnp.concatenate([wg, jnp.zeros((LANES - wg.shape[0], wg.shape[1]), wg.dtype)], axis=0)
        g_ref[...] = _dot_nt(xn_ref[...], wg)

    @pl.when(j == 1)
    def _():
        y_ref[:, :IN_TILE] = _dot_nt(xn_ref[...], (wk_ref[...] * kscale).astype(BF16)).astype(BF16)

    @pl.when(j == 2)
    def _():
        y_ref[:, IN_TILE:] = _dot_nt(xn_ref[...], wv_ref[...].astype(BF16)).astype(BF16)


def _ctx_proj(xc, scale, shift, norm_w, wt, gate_row0, r_k, r_v):
    n, d = xc.shape
    tm, tn = 1024, IN_TILE
    n_cast = 9
    slab = wt.shape[0] // n_cast
    assert slab * n_cast == wt.shape[0] and slab % 16 == 0 and n % tm == 0
    steps = -(-n_cast // (n // tm))
    assert steps >= 3
    cast_idx = lambda i, j: (jnp.minimum(i * steps + j, n_cast - 1), 0)
    elem = lambda rows, r0: pl.BlockSpec((pl.Element(rows), pl.Element(d)), lambda i, j: (r0, 0))
    return pl.pallas_call(
        functools.partial(_ctx_proj_kernel, n_cast=n_cast, steps=steps, k_rows=(r_k, r_k + tn)),
        grid=(n // tm, steps),
        in_specs=[pl.BlockSpec((tm, d), lambda i, j: (i, 0)),
                  pl.BlockSpec((1, 1, d), lambda i, j: (0, 0, 0)),
                  pl.BlockSpec((1, 1, d), lambda i, j: (0, 0, 0)),
                  pl.BlockSpec((1, d), lambda i, j: (0, 0)),
                  pl.BlockSpec((slab, d), cast_idx),
                  elem(tn, r_k), elem(tn, r_v), elem(N_GATE_COLS, gate_row0)],
        out_specs=[pl.BlockSpec((slab, d), cast_idx),
                   pl.BlockSpec((tm, 2 * tn), lambda i, j: (i, 0)),
                   pl.BlockSpec((tm, LANES), lambda i, j: (i, 0))],
        out_shape=[jax.ShapeDtypeStruct(wt.shape, BF16),
                   jax.ShapeDtypeStruct((n, 2 * tn), BF16),
                   jax.ShapeDtypeStruct((n, LANES), F32)],
        scratch_shapes=[pltpu.VMEM((tm, d), BF16)],
        compiler_params=_cparams(("arbitrary", "arbitrary"), 56),
        name="ctx_proj",
    )(xc, scale, shift, norm_w, wt, wt, wt, wt)


N_DH = N_DIR * M_HEADS
TERM_PIECES = (3, 2, 2, 3)


def _split(x, pieces):
    out = []
    for _ in range(pieces):
        p = x.astype(BF16)
        out.append(p)
        x = x - p.astype(F32)
    return out


def _chunk_scan(x, pos, chunk, op, reverse):
    t = x.shape[1]
    k = 1
    while k < chunk:
        if reverse:
            x = jnp.where(pos < chunk - k, op(x, pltpu.roll(x, t - k, 1)), x)
        else:
            x = jnp.where(pos >= k, op(x, pltpu.roll(x, k, 1)), x)
        k *= 2
    return x


def _gate_terms(li, pre_f, m0, chunk):
    rows, t = li.shape
    nc = t // chunk
    pos = lax.broadcasted_iota(jnp.int32, (rows, t), 1) % chunk
    rev = lax.broadcasted_iota(jnp.int32, (rows, t), 0) >= M_HEADS
    rev1 = lax.broadcasted_iota(jnp.int32, (rows, 1), 0) >= M_HEADS
    lf = _log_sigmoid(pre_f)
    b = jnp.where(rev, _chunk_scan(lf, pos, chunk, jnp.add, True), _chunk_scan(lf, pos, chunk, jnp.add, False))
    a = li - b
    cmax = jnp.where(rev, _chunk_scan(a, pos, chunk, jnp.maximum, True),
                     _chunk_scan(a, pos, chunk, jnp.maximum, False))

    def at_scan_end(x, c):
        lo = c * chunk
        return jnp.where(rev1, x[:, lo:lo + 1], x[:, lo + chunk - 1:lo + chunk])

    b_end = [at_scan_end(b, c) for c in range(nc)]
    a_max = [at_scan_end(cmax, c) for c in range(nc)]
    m_f, m_b = m0, m0
    m_in_f, m_in_b = [None] * nc, [None] * nc
    for i in range(nc):
        j = nc - 1 - i
        m_in_f[i] = m_f
        m_f = b_end[i] + jnp.maximum(m_f, a_max[i])
        m_in_b[j] = m_b
        m_b = b_end[j] + jnp.maximum(m_b, a_max[j])
    m_in = [jnp.where(rev1, m_in_b[c], m_in_f[c]) for c in range(nc)]
    per_chunk = lambda vals: jnp.concatenate([jnp.broadcast_to(v, (rows, chunk)) for v in vals], axis=1)
    m_in_t = per_chunk(m_in)
    g = jnp.maximum(m_in_t, cmax)
    g_end = [jnp.maximum(m_in[c], a_max[c]) for c in range(nc)]
    pieces = []
    terms = (-g * LOG2E, jnp.exp(m_in_t - g), jnp.exp(a - per_chunk(g_end)), -(b + g) * LOG2E)
    for term, n in zip(terms, TERM_PIECES):
        pieces += _split(term, n)
    pieces.append(jnp.zeros((LANES - rows * sum(TERM_PIECES), t), BF16))
    decay = jnp.concatenate([jnp.exp(m_in[c] - g_end[c]) for c in range(nc)], axis=1)
    return jnp.concatenate(pieces, axis=0), a * LOG2E, decay, jnp.where(rev1, m_b, m_f)


def _gates_kernel(ic_ref, fc_ref, i_ref, f_ref, bi_ref, bf_ref, rc_ref, dc_ref, r_ref, a_ref, d_ref, *, chunk):
    m0 = jnp.full((N_DH, 1), M_INIT, F32)
    rc_ref[0], _, dc_ref[0], m_ctx = _gate_terms(ic_ref[0] + bi_ref[...], fc_ref[0] + bf_ref[...], m0, chunk)
    r_ref[0], a_ref[0], d_ref[0], _ = _gate_terms(i_ref[0] + bi_ref[...], f_ref[0] + bf_ref[...], m_ctx, chunk)


def _gates(gates_tok, gates_tok_c, gate_b, chunk):
    b, t, _ = gates_tok.shape
    tc = gates_tok_c.shape[1]
    nc, ncc = t // chunk, tc // chunk

    def rows(g):
        g = jnp.transpose(g[:, :, :N_GATE_COLS], (0, 2, 1)).reshape(b, N_DIR, 2, M_HEADS, -1)
        return g[:, :, 0].reshape(b, N_DH, -1), g[:, :, 1].reshape(b, N_DH, -1)

    gb = gate_b.reshape(N_DIR, 2, M_HEADS)
    spec = lambda r, n: pl.BlockSpec((1, r, n), lambda i: (i, 0, 0))
    col = pl.BlockSpec((N_DH, 1), lambda i: (0, 0))
    rc, dc, r, a, dec = pl.pallas_call(
        functools.partial(_gates_kernel, chunk=chunk),
        grid=(b,),
        in_specs=[spec(N_DH, tc), spec(N_DH, tc), spec(N_DH, t), spec(N_DH, t), col, col],
        out_specs=[spec(LANES, tc), spec(N_DH, ncc), spec(LANES, t), spec(N_DH, t), spec(N_DH, nc)],
        out_shape=[jax.ShapeDtypeStruct((b, LANES, tc), BF16), jax.ShapeDtypeStruct((b, N_DH, ncc), F32),
                   jax.ShapeDtypeStruct((b, LANES, t), BF16), jax.ShapeDtypeStruct((b, N_DH, t), F32),
                   jax.ShapeDtypeStruct((b, N_DH, nc), F32)],
        compiler_params=_cparams(("arbitrary",), 32),
        name="gates",
    )(*rows(gates_tok_c), *rows(gates_tok), gb[:, 0].reshape(N_DH, 1), gb[:, 1].reshape(N_DH, 1))
    a = a.reshape(b, N_DH, nc, chunk).transpose(0, 2, 1, 3)
    decay = jnp.concatenate([dc, dec], axis=2).reshape(b * N_DH, ncc + nc)
    return jnp.transpose(rc, (0, 2, 1)), jnp.transpose(r, (0, 2, 1)), a, decay


def _twice(x):
    return jnp.concatenate([x, x], axis=1)


def _mlstm_kernel(dec_ref, q_ref, k_ref, v_ref, oz_ref, p_ref, a_ref, kc_ref, vc_ref, pc_ref,
                  sel_ref, hnw_ref, out_ref, ct_ref, n_ref, hs_ref, *, chunk):
    t = q_ref.shape[1]
    nc = t // chunk
    ncc = kc_ref.shape[1] // chunk
    bi, hi = pl.program_id(0), pl.program_id(1)
    t_idx = lax.broadcasted_iota(jnp.int32, (chunk, chunk), 0)
    s_idx = lax.broadcasted_iota(jnp.int32, (chunk, chunk), 1)

    ct_ref[...] = jnp.zeros(ct_ref.shape, F32)
    n_ref[...] = jnp.zeros(n_ref.shape, F32)

    def row_terms(p, d):
        w = jnp.dot(p, sel_ref[0, d], preferred_element_type=F32)
        return [w[:, i * LANES:(i + 1) * LANES] for i in range(len(TERM_PIECES))]

    def update(k, v, e_b, decay, d):
        ke = k.astype(F32) * _twice(e_b)
        upd = lax.dot_general(ke.astype(BF16), v, (((0,), (0,)), ((), ())), preferred_element_type=F32)
        ct_ref[d] = decay * ct_ref[d] + upd
        n_ref[d] = decay * n_ref[d] + jnp.sum(ke, axis=0, keepdims=True)

    def decay_of(c, d):
        return dec_ref[bi * N_DH + d * M_HEADS + hi, c]

    for d in range(N_DIR):
        for i in range(ncc):
            c = i if d == 0 else ncc - 1 - i
            rows = pl.ds(c * chunk, chunk)
            e_b = row_terms(pc_ref[0, rows, :], d)[2]
            update(kc_ref[0, rows, :], vc_ref[0, rows, :], e_b, decay_of(c, d), d)

    def run_chunk(c, d):
        rows = pl.ds(pl.multiple_of(c * chunk, chunk), chunk)
        q, k, v = q_ref[0, rows, :], k_ref[0, rows, :], v_ref[0, rows, :]
        neg_g_b, w_inter_b, e_b, neg_mt_b = row_terms(p_ref[0, rows, :], d)
        a_row = a_ref[0, c, pl.ds(d * M_HEADS + hi, 1), :]
        mask = (s_idx <= t_idx) if d == 0 else (s_idx >= t_idx)
        w_intra = jnp.exp2(jnp.where(mask, a_row + _twice(neg_g_b), NEG_BIG))
        s = _dot_nt(q, k) * w_intra
        num = (_twice(w_inter_b) * jnp.dot(q, ct_ref[d].astype(BF16), preferred_element_type=F32)
               + jnp.dot(s.astype(BF16), v, preferred_element_type=F32))
        den_b = (w_inter_b * jnp.sum(q.astype(F32) * n_ref[d], axis=-1, keepdims=True)
                 + jnp.sum(s, axis=-1, keepdims=True))
        h = num * _twice(1.0 / jnp.maximum(jnp.abs(den_b), jnp.exp2(neg_mt_b)))
        update(k, v, e_b, decay_of(ncc + c, d), d)
        return rows, h

    def finalize(rows, h):
        hn = h * lax.rsqrt(jnp.mean(h * h, axis=-1, keepdims=True) + EPS) * hnw_ref[...]
        out_ref[0, rows, :] = (hn * oz_ref[0, rows, :].astype(F32)).astype(out_ref.dtype)

    def first(i, carry):
        for d in range(N_DIR):
            rows, h = run_chunk(i if d == 0 else nc - 1 - i, d)
            hs_ref[rows, :] = h
        return carry

    def second(i, carry):
        for d in range(N_DIR):
            rows, h = run_chunk(i if d == 0 else nc - 1 - i, d)
            finalize(rows, hs_ref[rows, :] + h)
        return carry

    lax.fori_loop(0, nc // 2, first, 0, unroll=SCAN_UNROLL)
    lax.fori_loop(nc // 2, nc, second, 0, unroll=SCAN_UNROLL)


def _term_selectors():
    sel = np.zeros((M_HEADS, N_DIR, LANES, len(TERM_PIECES) * LANES), np.float32)
    first = np.concatenate([[0], np.cumsum(TERM_PIECES)])
    for h in range(M_HEADS):
        for d in range(N_DIR):
            for q in range(len(TERM_PIECES)):
                for j in range(first[q], first[q + 1]):
                    sel[h, d, j * N_DH + d * M_HEADS + h, q * LANES:(q + 1) * LANES] = 1.0
    return jnp.asarray(sel, BF16)


def _mlstm(y, yc, p, a, pc, decay, head_norm_w, chunk, col):
    b, t, _ = y.shape
    tc = yc.shape[1]
    dh = MXU_DIM
    sel = _term_selectors()
    assert t % (2 * chunk) == 0 and tc % chunk == 0
    blk = lambda name: pl.BlockSpec((1, t, dh), lambda i, h, c=col[name] // dh: (i, 0, c + h))
    blkc = lambda col0: pl.BlockSpec((1, tc, dh), lambda i, h, c=col0: (i, 0, c + h))
    return pl.pallas_call(
        functools.partial(_mlstm_kernel, chunk=chunk),
        grid=(b, M_HEADS),
        in_specs=[pl.BlockSpec(memory_space=pltpu.SMEM),
                  blk("q"), blk("k"), blk("v"), blk("oz"),
                  pl.BlockSpec((1, t, LANES), lambda i, h: (i, 0, 0)),
                  pl.BlockSpec((1, t // chunk, N_DH, chunk), lambda i, h: (i, 0, 0, 0)),
                  blkc(0), blkc(M_HEADS),
                  pl.BlockSpec((1, tc, LANES), lambda i, h: (i, 0, 0)),
                  pl.BlockSpec((1, N_DIR) + sel.shape[2:], lambda i, h: (h, 0, 0, 0)),
                  pl.BlockSpec((1, dh), lambda i, h: (0, h))],
        out_specs=pl.BlockSpec((1, t, dh), lambda i, h: (i, 0, h)),
        out_shape=jax.ShapeDtypeStruct((b, t, M_HEADS * dh), BF16),
        scratch_shapes=[pltpu.VMEM((N_DIR, dh, dh), F32),
                        pltpu.VMEM((N_DIR, 1, dh), F32),
                        pltpu.VMEM((t, dh), F32)],
        compiler_params=_cparams(("arbitrary", "arbitrary"), 48),
        name="mlstm",
    )(decay, y, y, y, y, p, a, yc, yc, pc, sel, head_norm_w)


def _window_bounds(pos, w, length):
    return np.clip(pos - w // 2, 0, length), np.clip(pos + w - w // 2, 0, length)


def _col_pool_matrices(tile):
    pos = np.arange(GRID_W)
    mats = []
    for w in POOL_WINDOWS:
        lo, hi = _window_bounds(pos, w, GRID_W)
        band = ((pos[None, :] >= lo[:, None]) & (pos[None, :] < hi[:, None])).astype(np.float32)
        mats.append(np.kron(np.eye(tile // GRID_W, dtype=np.float32), band))
    return jnp.asarray(np.stack(mats), dtype=BF16)


def _pool_kernel(p_ref, z_ref, mc_ref, pw_ref, ps_ref, out_ref, pad_ref, scale_ref, *, tile, rows):
    t = p_ref.shape[1]
    gd = LANES
    pad = (max(POOL_WINDOWS) // 2) * GRID_W
    n_tiles = t // tile

    @pl.when(pl.program_id(0) == 0)
    def _():
        pad_ref[...] = jnp.zeros(pad_ref.shape, F32)
        tok = lax.broadcasted_iota(jnp.int32, (t, gd), 0)
        r = tok // GRID_W
        c = tok % GRID_W
        for g, w in enumerate(POOL_WINDOWS):
            cnt_r = jnp.minimum(r + (w - w // 2), rows) - jnp.maximum(r - w // 2, 0)
            cnt_c = jnp.minimum(c + (w - w // 2), GRID_W) - jnp.maximum(c - w // 2, 0)
            scale_ref[g] = 1.0 / (cnt_r.astype(F32) * cnt_c.astype(F32))

    def col_sum(g, start):
        cols = pl.ds(g * gd, gd)
        pad_ref[g % 2, pl.ds(pad + start, tile), :] = jnp.dot(
            mc_ref[g], p_ref[0, pl.ds(start, tile), cols], preferred_element_type=F32)

    def row_sum(g, start):
        w = POOL_WINDOWS[g]
        cols = pl.ds(g * gd, gd)
        buf = pad_ref.at[g % 2]
        acc = buf[pl.ds(pad + start - (w // 2) * GRID_W, tile), :]
        for j in range(1 - w // 2, w - w // 2):
            acc = acc + buf[pl.ds(pad + start + j * GRID_W, tile), :]
        mixed = acc * scale_ref[g, pl.ds(start, tile), :] - p_ref[0, pl.ds(start, tile), cols].astype(F32)
        mm = jnp.dot(mixed.astype(BF16), pw_ref[g], preferred_element_type=F32)
        zz = z_ref[0, pl.ds(start, tile), cols].astype(F32)
        out_ref[0, pl.ds(start, tile), cols] = (mm * ps_ref[:, cols] * zz).astype(out_ref.dtype)

    for g in range(-1, P_GROUPS):
        def body(i, carry, g=g):
            start = pl.multiple_of(i * tile, tile)
            if g + 1 < P_GROUPS:
                col_sum(g + 1, start)
            if g >= 0:
                row_sum(g, start)
            return carry

        lax.fori_loop(0, n_tiles, body, 0, unroll=POOL_UNROLL)


def _pool(y, pool_w, pool_scale, p_col0):
    b, t, _ = y.shape
    pw = P_GROUPS * LANES
    tile = MXU_DIM
    pad = (max(POOL_WINDOWS) // 2) * GRID_W
    return pl.pallas_call(
        functools.partial(_pool_kernel, tile=tile, rows=t // GRID_W),
        grid=(b,),
        in_specs=[pl.BlockSpec((1, t, pw), lambda i: (i, 0, p_col0 // pw)),
                  pl.BlockSpec((1, t, pw), lambda i: (i, 0, p_col0 // pw + 1)),
                  pl.BlockSpec((P_GROUPS, tile, tile), lambda i: (0, 0, 0)),
                  pl.BlockSpec((P_GROUPS, LANES, LANES), lambda i: (0, 0, 0)),
                  pl.BlockSpec((1, pw), lambda i: (0, 0))],
        out_specs=pl.BlockSpec((1, t, pw), lambda i: (i, 0, 0)),
        out_shape=jax.ShapeDtypeStruct((b, t, pw), BF16),
        scratch_shapes=[pltpu.VMEM((2, t + 2 * pad, LANES), F32),
                        pltpu.VMEM((P_GROUPS, t, LANES), F32)],
        compiler_params=_cparams(("arbitrary",), 48),
        name="pool",
    )(y, y, _col_pool_matrices(tile), pool_w, pool_scale)


def _merge_kernel(x_ref, ym_ref, yp_ref, gm_ref, gp_ref, gate_ref, wm_ref, wp_ref, wo_ref, fw_ref, o_ref):
    y_m = jnp.dot(ym_ref[...], wm_ref[...], preferred_element_type=F32)
    y_p = jnp.dot(yp_ref[...], wp_ref[...], preferred_element_type=F32)
    merged = _sigmoid_t(gm_ref[...].astype(F32)) * y_m + _sigmoid_t(gp_ref[...].astype(F32)) * y_p
    upd = jnp.dot(merged.astype(BF16), wo_ref[...], preferred_element_type=F32)
    xo = x_ref[...] + gate_ref[0] * upd
    o_ref[...] = xo * lax.rsqrt(jnp.mean(xo * xo, axis=-1, keepdims=True) + EPS) * fw_ref[...]


def _merge(x2, ym, yp, y, gate, wm, wp, wo, final_w, rows_per_mod, g_col0):
    n, d = x2.shape
    tm = 1024
    per = rows_per_mod // tm
    pw = yp.shape[1]
    full = lambda shape: pl.BlockSpec(shape, lambda i: (0,) * len(shape))
    return pl.pallas_call(
        _merge_kernel,
        grid=(n // tm,),
        in_specs=[pl.BlockSpec((tm, d), lambda i: (i, 0)),
                  pl.BlockSpec((tm, d), lambda i: (i, 0)),
                  pl.BlockSpec((tm, pw), lambda i: (i, 0)),
                  pl.BlockSpec((tm, d), lambda i: (i, g_col0 // d)),
                  pl.BlockSpec((tm, d), lambda i: (i, g_col0 // d + 1)),
                  pl.BlockSpec((1, 1, d), lambda i: (i // per, 0, 0)),
                  full((d, d)), full((pw, d)), full((d, d)), full((1, d))],
        out_specs=pl.BlockSpec((tm, d), lambda i: (i, 0)),
        out_shape=jax.ShapeDtypeStruct((n, d), F32),
        compiler_params=_cparams(("arbitrary",), 48),
        name="merge",
    )(x2, ym, yp, y, y, gate, wm, wp, wo, final_w)


def kernel(x, c, ctx, c_ctx, norm_w, ada_w, ada_b, in_w, gate_b, head_norm_w, pool_w, pool_scale,
           branch_m_w, branch_p_w, out_w, final_norm_w):
    b, t, d = x.shape
    tc = ctx.shape[1]
    depth = norm_w.shape[0]
    assert depth == 1, "context stream update between layers is not implemented"
    mw = M_HEADS * MXU_DIM
    assert d == mw
    g0 = 5 * mw

    cc = jnp.zeros((2 * 8, d), F32).at[:b].set(c).at[b].set(c_ctx)
    mod = _adaln(cc, ada_w[0], ada_b[0])
    shift, scale, gate = mod[:, :d], mod[:, d:2 * d], mod[:, 2 * d:]

    wt_f32 = jnp.transpose(in_w[0])
    nw = norm_w[0].reshape(1, d)

    r_q, r_k, r_v, r_o, r_z = (i * mw for i in range(5))
    r_p = g0 + N_GATE_COLS
    r_gm, r_gp = r_p + mw, r_p + 2 * mw
    pairs = ((r_q, r_k, "id"), (r_v, r_p, "half_silu"), (r_o, r_z, "gate"), (r_gm, r_gp, "id"))
    col = {name: i * mw for i, name in enumerate(("q", "k", "v", "p", "o", "oz", "g_m", "g_p"))}

    wt, yc, gates_tok_c = _ctx_proj(ctx.reshape(b * tc, d), scale[b:b + 1, None, :], shift[b:b + 1, None, :],
                                    nw, wt_f32, g0, r_k, r_v)
    y, gates_tok = _in_proj(x.reshape(b * t, d), scale[:b, None, :], shift[:b, None, :], nw,
                            wt, g0, t, "in_proj", pairs)
    y = y.reshape(b, t, -1)
    yc = yc.reshape(b, tc, -1)

    pc, p, a, decay = _gates(gates_tok.reshape(b, t, LANES), gates_tok_c.reshape(b, tc, LANES), gate_b[0], CHUNK)
    ym = _mlstm(y, yc, p, a, pc, decay, head_norm_w[0].reshape(1, mw), CHUNK, col)
    yp = _pool(y, pool_w[0].astype(BF16), pool_scale[0].reshape(1, -1), col["p"])

    out = _merge(x.reshape(b * t, d), ym.reshape(b * t, mw), yp.reshape(b * t, -1), y.reshape(b * t, -1),
                 gate[:b, None, :], branch_m_w[0].astype(BF16), branch_p_w[0].astype(BF16),
                 out_w[0].astype(BF16), final_norm_w.reshape(1, d), t, col["g_m"])
    return out.reshape(b, t, d)
```

```python
import functools

import numpy as np
import jax
import jax.numpy as jnp
from jax import lax
from jax.experimental import pallas as pl
from jax.experimental.pallas import tpu as pltpu

F32 = jnp.float32
BF16 = jnp.bfloat16

EPS = 1e-6
LOG2E = 1.4426950408889634
M_INIT = -1e30
NEG_BIG = -1e30
M_HEADS = 4
N_DIR = 2
N_GATE_COLS = N_DIR * 2 * M_HEADS
P_GROUPS = 4
POOL_WINDOWS = (2, 4, 8, 16)
GRID_W = 64
LANES = 128
MXU_DIM = 256
CHUNK = 256
POOL_UNROLL = 4
SCAN_UNROLL = 2
MERGE_SLABS = 2
HEADS_PER_STEP = 2
MIB = 1024 * 1024


def _cparams(sem, vmem_mib):
    return pltpu.CompilerParams(dimension_semantics=sem, vmem_limit_bytes=vmem_mib * MIB)


def _sigmoid(x):
    return 1.0 / (1.0 + jnp.exp(-x))


def _silu(x):
    return x * _sigmoid(x)


def _log_sigmoid(x):
    return jnp.minimum(x, 0.0) - jnp.log(1.0 + jnp.exp(-jnp.abs(x)))


def _dot_nt(a, b):
    return lax.dot_general(a, b, (((1,), (1,)), ((), ())), preferred_element_type=F32)


def _adaln_kernel(c_ref, w_ref, b_ref, o_ref):
    o_ref[...] = jnp.dot(_silu(c_ref[...]), w_ref[...], preferred_element_type=F32) + b_ref[...]


def _adaln(cc, ada_w, ada_b):
    rows, d = cc.shape
    n = ada_w.shape[1]
    tn = d
    return pl.pallas_call(
        _adaln_kernel,
        grid=(n // tn,),
        in_specs=[pl.BlockSpec((rows, d), lambda j: (0, 0)),
                  pl.BlockSpec((d, tn), lambda j: (0, j)),
                  pl.BlockSpec((1, tn), lambda j: (0, j))],
        out_specs=pl.BlockSpec((rows, tn), lambda j: (0, j)),
        out_shape=jax.ShapeDtypeStruct((rows, n), F32),
        compiler_params=_cparams(("arbitrary",), 32),
        name="adaln",
    )(cc, ada_w, ada_b.reshape(1, n))


def _sigmoid_t(x):
    return 0.5 * jnp.tanh(0.5 * x) + 0.5


IN_TILE = 1024
GATE_SUB = 256


def _inproj_kernel(x_ref, sc_ref, sh_ref, nw_ref, wa_ref, wb_ref, wg_ref, y_ref, g_ref, xn_ref, *, acts):
    j = pl.program_id(1)

    @pl.when(j == 0)
    def _():
        x = x_ref[...]
        ms = jnp.mean(x * x, axis=-1, keepdims=True)
        xn = x * lax.rsqrt(ms + EPS) * (nw_ref[...] * (1.0 + sc_ref[0])) + sh_ref[0]
        xn_ref[...] = xn.astype(BF16)
        wg = wg_ref[...]
        wg = jnp.concatenate([wg, jnp.zeros((LANES - wg.shape[0], wg.shape[1]), wg.dtype)], axis=0)
        g_ref[...] = _dot_nt(xn_ref[...], wg)

    def pair(act):
        tn = IN_TILE
        xn = xn_ref[...]
        if act == "gate":
            for c in range(0, tn, GATE_SUB):
                a = _dot_nt(xn, wa_ref[c:c + GATE_SUB, :])
                b = _dot_nt(xn, wb_ref[c:c + GATE_SUB, :])
                y_ref[:, c:c + GATE_SUB] = a.astype(BF16)
                y_ref[:, tn + c:tn + c + GATE_SUB] = (_sigmoid_t(a) * (b * _sigmoid_t(b))).astype(BF16)
            return
        y_ref[:, :tn] = _dot_nt(xn, wa_ref[...]).astype(BF16)
        b = _dot_nt(xn, wb_ref[...])
        if act == "half_silu":
            half = tn // 2
            b = jnp.concatenate([b[:, :half], b[:, half:] * _sigmoid_t(b[:, half:])], axis=1)
        y_ref[:, tn:] = b.astype(BF16)

    for act in sorted(set(acts)):
        hit = functools.reduce(jnp.logical_or, [j == s for s, a in enumerate(acts) if a == act])
        pl.when(hit)(functools.partial(pair, act))


def _in_proj(x2, scale, shift, norm_w, w, gate_row0, rows_per_mod, name, pairs):
    n, d = x2.shape
    tm = min(2048, rows_per_mod)
    tn = IN_TILE
    per = rows_per_mod // tm
    acts = tuple(p[2] for p in pairs)

    def w_rows(which):
        def index_map(i, j):
            r = sum(jnp.where(j == s, p[which], 0) for s, p in enumerate(pairs))
            return (pl.multiple_of(r, N_GATE_COLS), 0)
        return index_map

    return pl.pallas_call(
        functools.partial(_inproj_kernel, acts=acts),
        grid=(n // tm, len(pairs)),
        in_specs=[pl.BlockSpec((tm, d), lambda i, j: (i, 0)),
                  pl.BlockSpec((1, 1, d), lambda i, j: (i // per, 0, 0)),
                  pl.BlockSpec((1, 1, d), lambda i, j: (i // per, 0, 0)),
                  pl.BlockSpec((1, d), lambda i, j: (0, 0)),
                  pl.BlockSpec((pl.Element(tn), pl.Element(d)), w_rows(0)),
                  pl.BlockSpec((pl.Element(tn), pl.Element(d)), w_rows(1)),
                  pl.BlockSpec((pl.Element(N_GATE_COLS), pl.Element(d)), lambda i, j: (gate_row0, 0))],
        out_specs=[pl.BlockSpec((tm, 2 * tn), lambda i, j: (i, j)),
                   pl.BlockSpec((tm, LANES), lambda i, j: (i, 0))],
        out_shape=[jax.ShapeDtypeStruct((n, 2 * tn * len(pairs)), BF16),
                   jax.ShapeDtypeStruct((n, LANES), F32)],
        scratch_shapes=[pltpu.VMEM((tm, d), BF16)],
        compiler_params=_cparams(("arbitrary", "arbitrary"), 56),
        name=name,
    )(x2, scale, shift, norm_w, w, w, w)


def _ctx_proj_kernel(x_ref, sc_ref, sh_ref, nw_ref, wsrc_ref, wk_ref, wv_ref, wg_ref,
                     wdst_ref, y_ref, g_ref, xn_ref, *, n_cast, steps, k_rows):
    i, j = pl.program_id(0), pl.program_id(1)
    slab = i * steps + j
    kscale = MXU_DIM ** -0.5

    @pl.when(slab < n_cast)
    def _():
        rows = wsrc_ref.shape[0]
        r = slab * rows + lax.broadcasted_iota(jnp.int32, (rows, 1), 0)
        scale = jnp.where((r >= k_rows[0]) & (r < k_rows[1]), kscale, 1.0)
        wdst_ref[...] = (wsrc_ref[...] * scale).astype(BF16)

    @pl.when(j == 0)
    def _():
        x = x_ref[...]
        ms = jnp.mean(x * x, axis=-1, keepdims=True)
        xn = x * lax.rsqrt(ms + EPS) * (nw_ref[...] * (1.0 + sc_ref[0])) + sh_ref[0]
        xn_ref[...] = xn.astype(BF16)
        wg = wg_ref[...].astype(BF16)
        wg = jnp.concatenate([wg, jnp.zeros((LANES - wg.shape[0], wg.shape[1]), wg.dtype)], axis=0)
        g_ref[...] = _dot_nt(xn_ref[...], wg)

    @pl.when(j == 1)
    def _():
        y_ref[:, :IN_TILE] = _dot_nt(xn_ref[...], (wk_ref[...] * kscale).astype(BF16)).astype(BF16)

    @pl.when(j == 2)
    def _():
        y_ref[:, IN_TILE:] = _dot_nt(xn_ref[...], wv_ref[...].astype(BF16)).astype(BF16)


def _ctx_proj(xc, scale, shift, norm_w, wt, gate_row0, r_k, r_v):
    n, d = xc.shape
    tm, tn = 1024, IN_TILE
    n_cast = 9
    slab = wt.shape[0] // n_cast
    assert slab * n_cast == wt.shape[0] and slab % 16 == 0 and n % tm == 0
    steps = -(-n_cast // (n // tm))
    assert steps >= 3
    cast_idx = lambda i, j: (jnp.minimum(i * steps + j, n_cast - 1), 0)
    elem = lambda rows, r0: pl.BlockSpec((pl.Element(rows), pl.Element(d)), lambda i, j: (r0, 0))
    return pl.pallas_call(
        functools.partial(_ctx_proj_kernel, n_cast=n_cast, steps=steps, k_rows=(r_k, r_k + tn)),
        grid=(n // tm, steps),
        in_specs=[pl.BlockSpec((tm, d), lambda i, j: (i, 0)),
                  pl.BlockSpec((1, 1, d), lambda i, j: (0, 0, 0)),
                  pl.BlockSpec((1, 1, d), lambda i, j: (0, 0, 0)),
                  pl.BlockSpec((1, d), lambda i, j: (0, 0)),
                  pl.BlockSpec((slab, d), cast_idx),
                  elem(tn, r_k), elem(tn, r_v), elem(N_GATE_COLS, gate_row0)],
        out_specs=[pl.BlockSpec((slab, d), cast_idx),
                   pl.BlockSpec((tm, 2 * tn), lambda i, j: (i, 0)),
                   pl.BlockSpec((tm, LANES), lambda i, j: (i, 0))],
        out_shape=[jax.ShapeDtypeStruct(wt.shape, BF16),
                   jax.ShapeDtypeStruct((n, 2 * tn), BF16),
                   jax.ShapeDtypeStruct((n, LANES), F32)],
        scratch_shapes=[pltpu.VMEM((tm, d), BF16)],
        compiler_params=_cparams(("arbitrary", "arbitrary"), 56),
        name="ctx_proj",
    )(xc, scale, shift, norm_w, wt, wt, wt, wt)


N_DH = N_DIR * M_HEADS
TERM_PIECES = (3, 2, 2, 3)


def _split(x, pieces):
    out = []
    for _ in range(pieces):
        p = x.astype(BF16)
        out.append(p)
        x = x - p.astype(F32)
    return out


def _chunk_scan(x, pos, chunk, op, reverse):
    t = x.shape[1]
    k = 1
    while k < chunk:
        if reverse:
            x = jnp.where(pos < chunk - k, op(x, pltpu.roll(x, t - k, 1)), x)
        else:
            x = jnp.where(pos >= k, op(x, pltpu.roll(x, k, 1)), x)
        k *= 2
    return x


def _gate_terms(li, pre_f, m0, chunk):
    rows, t = li.shape
    nc = t // chunk
    pos = lax.broadcasted_iota(jnp.int32, (rows, t), 1) % chunk
    rev = lax.broadcasted_iota(jnp.int32, (rows, t), 0) >= M_HEADS
    rev1 = lax.broadcasted_iota(jnp.int32, (rows, 1), 0) >= M_HEADS
    lf = _log_sigmoid(pre_f)
    b = jnp.where(rev, _chunk_scan(lf, pos, chunk, jnp.add, True), _chunk_scan(lf, pos, chunk, jnp.add, False))
    a = li - b
    cmax = jnp.where(rev, _chunk_scan(a, pos, chunk, jnp.maximum, True),
                     _chunk_scan(a, pos, chunk, jnp.maximum, False))

    def at_scan_end(x, c):
        lo = c * chunk
        return jnp.where(rev1, x[:, lo:lo + 1], x[:, lo + chunk - 1:lo + chunk])

    b_end = [at_scan_end(b, c) for c in range(nc)]
    a_max = [at_scan_end(cmax, c) for c in range(nc)]
    m_f, m_b = m0, m0
    m_in_f, m_in_b = [None] * nc, [None] * nc
    for i in range(nc):
        j = nc - 1 - i
        m_in_f[i] = m_f
        m_f = b_end[i] + jnp.maximum(m_f, a_max[i])
        m_in_b[j] = m_b
        m_b = b_end[j] + jnp.maximum(m_b, a_max[j])
    m_in = [jnp.where(rev1, m_in_b[c], m_in_f[c]) for c in range(nc)]
    per_chunk = lambda vals: jnp.concatenate([jnp.broadcast_to(v, (rows, chunk)) for v in vals], axis=1)
    m_in_t = per_chunk(m_in)
    g = jnp.maximum(m_in_t, cmax)
    g_end = [jnp.maximum(m_in[c], a_max[c]) for c in range(nc)]
    pieces = []
    terms = (-g * LOG2E, jnp.exp(m_in_t - g), jnp.exp(a - per_chunk(g_end)), -(b + g) * LOG2E)
    for term, n in zip(terms, TERM_PIECES):
        pieces += _split(term, n)
    pieces.append(jnp.zeros((LANES - rows * sum(TERM_PIECES), t), BF16))
    decay = jnp.concatenate([jnp.exp(m_in[c] - g_end[c]) for c in range(nc)], axis=1)
    return jnp.concatenate(pieces, axis=0), a * LOG2E, decay, jnp.where(rev1, m_b, m_f)


def _gates_kernel(ic_ref, fc_ref, i_ref, f_ref, bi_ref, bf_ref, rc_ref, dc_ref, r_ref, a_ref, d_ref, *, chunk):
    m0 = jnp.full((N_DH, 1), M_INIT, F32)
    rc_ref[0], _, dc_ref[0], m_ctx = _gate_terms(ic_ref[0] + bi_ref[...], fc_ref[0] + bf_ref[...], m0, chunk)
    r_ref[0], a_ref[0], d_ref[0], _ = _gate_terms(i_ref[0] + bi_ref[...], f_ref[0] + bf_ref[...], m_ctx, chunk)


def _gates(gates_tok, gates_tok_c, gate_b, chunk):
    b, t, _ = gates_tok.shape
    tc = gates_tok_c.shape[1]
    nc, ncc = t // chunk, tc // chunk

    def rows(g):
        g = jnp.transpose(g[:, :, :N_GATE_COLS], (0, 2, 1)).reshape(b, N_DIR, 2, M_HEADS, -1)
        return g[:, :, 0].reshape(b, N_DH, -1), g[:, :, 1].reshape(b, N_DH, -1)

    gb = gate_b.reshape(N_DIR, 2, M_HEADS)
    spec = lambda r, n: pl.BlockSpec((1, r, n), lambda i: (i, 0, 0))
    col = pl.BlockSpec((N_DH, 1), lambda i: (0, 0))
    rc, dc, r, a, dec = pl.pallas_call(
        functools.partial(_gates_kernel, chunk=chunk),
        grid=(b,),
        in_specs=[spec(N_DH, tc), spec(N_DH, tc), spec(N_DH, t), spec(N_DH, t), col, col],
        out_specs=[spec(LANES, tc), spec(N_DH, ncc), spec(LANES, t), spec(N_DH, t), spec(N_DH, nc)],
        out_shape=[jax.ShapeDtypeStruct((b, LANES, tc), BF16), jax.ShapeDtypeStruct((b, N_DH, ncc), F32),
                   jax.ShapeDtypeStruct((b, LANES, t), BF16), jax.ShapeDtypeStruct((b, N_DH, t), F32),
                   jax.ShapeDtypeStruct((b, N_DH, nc), F32)],
        compiler_params=_cparams(("arbitrary",), 32),
        name="gates",
    )(*rows(gates_tok_c), *rows(gates_tok), gb[:, 0].reshape(N_DH, 1), gb[:, 1].reshape(N_DH, 1))
    a = a.reshape(b, N_DH, nc, chunk).transpose(0, 2, 1, 3)
    decay = jnp.concatenate([dc, dec], axis=2).reshape(b * N_DH, ncc + nc)
    return jnp.transpose(rc, (0, 2, 1)), jnp.transpose(r, (0, 2, 1)), a, decay


def _twice(x):
    return jnp.concatenate([x, x], axis=1)


def _mlstm_kernel(dec_ref, q_ref, k_ref, v_ref, oz_ref, p_ref, a_ref, kc_ref, vc_ref, pc_ref,
                  sel_ref, hnw_ref, out_ref, ct_ref, n_ref, hs_ref, *, chunk, dh):
    t = q_ref.shape[1]
    nc = t // chunk
    ncc = kc_ref.shape[1] // chunk
    hps = q_ref.shape[2] // dh
    bi, h0 = pl.program_id(0), pl.program_id(1) * hps
    t_idx = lax.broadcasted_iota(jnp.int32, (chunk, chunk), 0)
    s_idx = lax.broadcasted_iota(jnp.int32, (chunk, chunk), 1)

    ct_ref[...] = jnp.zeros(ct_ref.shape, F32)
    n_ref[...] = jnp.zeros(n_ref.shape, F32)

    def head_cols(hh):
        return slice(hh * dh, (hh + 1) * dh)

    def row_terms(p, hh, d):
        w = jnp.dot(p, sel_ref[hh, d], preferred_element_type=F32)
        return [w[:, i * LANES:(i + 1) * LANES] for i in range(len(TERM_PIECES))]

    def update(k, v, e_b, decay, u):
        ke = k.astype(F32) * _twice(e_b)
        upd = lax.dot_general(ke.astype(BF16), v, (((0,), (0,)), ((), ())), preferred_element_type=F32)
        ct_ref[u] = decay * ct_ref[u] + upd
        n_ref[u] = decay * n_ref[u] + jnp.sum(ke, axis=0, keepdims=True)

    def decay_of(c, hh, d):
        return dec_ref[bi * N_DH + d * M_HEADS + h0 + hh, c]

    for hh in range(hps):
        for d in range(N_DIR):
            for i in range(ncc):
                c = i if d == 0 else ncc - 1 - i
                rows = pl.ds(c * chunk, chunk)
                e_b = row_terms(pc_ref[0, rows, :], hh, d)[2]
                update(kc_ref[0, rows, head_cols(hh)], vc_ref[0, rows, head_cols(hh)], e_b,
                       decay_of(c, hh, d), hh * N_DIR + d)

    def run_chunk(c, hh, d):
        u = hh * N_DIR + d
        rows = pl.ds(pl.multiple_of(c * chunk, chunk), chunk)
        cols = head_cols(hh)
        q, k, v = q_ref[0, rows, cols], k_ref[0, rows, cols], v_ref[0, rows, cols]
        neg_g_b, w_inter_b, e_b, neg_mt_b = row_terms(p_ref[0, rows, :], hh, d)
        a_row = a_ref[0, c, pl.ds(d * M_HEADS + h0 + hh, 1), :]
        mask = (s_idx <= t_idx) if d == 0 else (s_idx >= t_idx)
        w_intra = jnp.exp2(jnp.where(mask, a_row + _twice(neg_g_b), NEG_BIG))
        s = _dot_nt(q, k) * w_intra
        num = (_twice(w_inter_b) * jnp.dot(q, ct_ref[u].astype(BF16), preferred_element_type=F32)
               + jnp.dot(s.astype(BF16), v, preferred_element_type=F32))
        den_b = (w_inter_b * jnp.sum(q.astype(F32) * n_ref[u], axis=-1, keepdims=True)
                 + jnp.sum(s, axis=-1, keepdims=True))
        h = num * _twice(1.0 / jnp.maximum(jnp.abs(den_b), jnp.exp2(neg_mt_b)))
        update(k, v, e_b, decay_of(ncc + c, hh, d), u)
        return rows, cols, h

    def finalize(rows, cols, h):
        hn = h * lax.rsqrt(jnp.mean(h * h, axis=-1, keepdims=True) + EPS) * hnw_ref[:, cols]
        out_ref[0, rows, cols] = (hn * oz_ref[0, rows, cols].astype(F32)).astype(out_ref.dtype)

    def first(i, carry):
        for hh in range(hps):
            for d in range(N_DIR):
                rows, cols, h = run_chunk(i if d == 0 else nc - 1 - i, hh, d)
                hs_ref[rows, cols] = h
        return carry

    def second(i, carry):
        for hh in range(hps):
            for d in range(N_DIR):
                rows, cols, h = run_chunk(i if d == 0 else nc - 1 - i, hh, d)
                finalize(rows, cols, hs_ref[rows, cols] + h)
        return carry

    lax.fori_loop(0, nc // 2, first, 0, unroll=SCAN_UNROLL)
    lax.fori_loop(nc // 2, nc, second, 0, unroll=SCAN_UNROLL)


def _term_selectors():
    sel = np.zeros((M_HEADS, N_DIR, LANES, len(TERM_PIECES) * LANES), np.float32)
    first = np.concatenate([[0], np.cumsum(TERM_PIECES)])
    for h in range(M_HEADS):
        for d in range(N_DIR):
            for q in range(len(TERM_PIECES)):
                for j in range(first[q], first[q + 1]):
                    sel[h, d, j * N_DH + d * M_HEADS + h, q * LANES:(q + 1) * LANES] = 1.0
    return jnp.asarray(sel, BF16)


def _mlstm(y, yc, p, a, pc, decay, head_norm_w, chunk, col):
    b, t, _ = y.shape
    tc = yc.shape[1]
    dh = MXU_DIM
    hps = HEADS_PER_STEP
    w = hps * dh
    sel = _term_selectors()
    assert t % (2 * chunk) == 0 and tc % chunk == 0 and M_HEADS % hps == 0
    blk = lambda name: pl.BlockSpec((1, t, w), lambda i, h, c=col[name] // w: (i, 0, c + h))
    blkc = lambda col0: pl.BlockSpec((1, tc, w), lambda i, h, c=col0 // w: (i, 0, c + h))
    return pl.pallas_call(
        functools.partial(_mlstm_kernel, chunk=chunk, dh=dh),
        grid=(b, M_HEADS // hps),
        in_specs=[pl.BlockSpec(memory_space=pltpu.SMEM),
                  blk("q"), blk("k"), blk("v"), blk("oz"),
                  pl.BlockSpec((1, t, LANES), lambda i, h: (i, 0, 0)),
                  pl.BlockSpec((1, t // chunk, N_DH, chunk), lambda i, h: (i, 0, 0, 0)),
                  blkc(0), blkc(M_HEADS * dh),
                  pl.BlockSpec((1, tc, LANES), lambda i, h: (i, 0, 0)),
                  pl.BlockSpec((hps, N_DIR) + sel.shape[2:], lambda i, h: (h, 0, 0, 0)),
                  pl.BlockSpec((1, w), lambda i, h: (0, h))],
        out_specs=pl.BlockSpec((1, t, w), lambda i, h: (i, 0, h)),
        out_shape=jax.ShapeDtypeStruct((b, t, M_HEADS * dh), BF16),
        scratch_shapes=[pltpu.VMEM((hps * N_DIR, dh, dh), F32),
                        pltpu.VMEM((hps * N_DIR, 1, dh), F32),
                        pltpu.VMEM((t, w), F32)],
        compiler_params=_cparams(("arbitrary", "arbitrary"), 48),
        name="mlstm",
    )(decay, y, y, y, y, p, a, yc, yc, pc, sel, head_norm_w)


def _window_bounds(pos, w, length):
    return np.clip(pos - w // 2, 0, length), np.clip(pos + w - w // 2, 0, length)


def _col_pool_matrices(tile):
    pos = np.arange(GRID_W)
    mats = []
    for w in POOL_WINDOWS:
        lo, hi = _window_bounds(pos, w, GRID_W)
        band = ((pos[None, :] >= lo[:, None]) & (pos[None, :] < hi[:, None])).astype(np.float32)
        mats.append(np.kron(np.eye(tile // GRID_W, dtype=np.float32), band))
    return jnp.asarray(np.stack(mats), dtype=BF16)


def _pool_kernel(p_ref, z_ref, mc_ref, pw_ref, ps_ref, out_ref, pad_ref, scale_ref, *, tile, rows):
    t = p_ref.shape[1]
    gd = LANES
    pad = (max(POOL_WINDOWS) // 2) * GRID_W
    n_tiles = t // tile

    @pl.when(pl.program_id(0) == 0)
    def _():
        pad_ref[...] = jnp.zeros(pad_ref.shape, F32)
        tok = lax.broadcasted_iota(jnp.int32, (t, gd), 0)
        r = tok // GRID_W
        c = tok % GRID_W
        for g, w in enumerate(POOL_WINDOWS):
            cnt_r = jnp.minimum(r + (w - w // 2), rows) - jnp.maximum(r - w // 2, 0)
            cnt_c = jnp.minimum(c + (w - w // 2), GRID_W) - jnp.maximum(c - w // 2, 0)
            scale_ref[g] = 1.0 / (cnt_r.astype(F32) * cnt_c.astype(F32))

    def col_sum(g, start):
        cols = pl.ds(g * gd, gd)
        pad_ref[g % 2, pl.ds(pad + start, tile), :] = jnp.dot(
            mc_ref[g], p_ref[0, pl.ds(start, tile), cols], preferred_element_type=F32)

    def row_sum(g, start):
        w = POOL_WINDOWS[g]
        cols = pl.ds(g * gd, gd)
        buf = pad_ref.at[g % 2]
        acc = buf[pl.ds(pad + start - (w // 2) * GRID_W, tile), :]
        for j in range(1 - w // 2, w - w // 2):
            acc = acc + buf[pl.ds(pad + start + j * GRID_W, tile), :]
        mixed = acc * scale_ref[g, pl.ds(start, tile), :] - p_ref[0, pl.ds(start, tile), cols].astype(F32)
        mm = jnp.dot(mixed.astype(BF16), pw_ref[g], preferred_element_type=F32)
        zz = z_ref[0, pl.ds(start, tile), cols].astype(F32)
        out_ref[0, pl.ds(start, tile), cols] = (mm * ps_ref[:, cols] * zz).astype(out_ref.dtype)

    for g in range(-1, P_GROUPS):
        def body(i, carry, g=g):
            start = pl.multiple_of(i * tile, tile)
            if g + 1 < P_GROUPS:
                col_sum(g + 1, start)
            if g >= 0:
                row_sum(g, start)
            return carry

        lax.fori_loop(0, n_tiles, body, 0, unroll=POOL_UNROLL)


def _pool(y, pool_w, pool_scale, p_col0):
    b, t, _ = y.shape
    pw = P_GROUPS * LANES
    tile = MXU_DIM
    pad = (max(POOL_WINDOWS) // 2) * GRID_W
    return pl.pallas_call(
        functools.partial(_pool_kernel, tile=tile, rows=t // GRID_W),
        grid=(b,),
        in_specs=[pl.BlockSpec((1, t, pw), lambda i: (i, 0, p_col0 // pw)),
                  pl.BlockSpec((1, t, pw), lambda i: (i, 0, p_col0 // pw + 1)),
                  pl.BlockSpec((P_GROUPS, tile, tile), lambda i: (0, 0, 0)),
                  pl.BlockSpec((P_GROUPS, LANES, LANES), lambda i: (0, 0, 0)),
                  pl.BlockSpec((1, pw), lambda i: (0, 0))],
        out_specs=pl.BlockSpec((1, t, pw), lambda i: (i, 0, 0)),
        out_shape=jax.ShapeDtypeStruct((b, t, pw), BF16),
        scratch_shapes=[pltpu.VMEM((2, t + 2 * pad, LANES), F32),
                        pltpu.VMEM((P_GROUPS, t, LANES), F32)],
        compiler_params=_cparams(("arbitrary",), 48),
        name="pool",
    )(y, y, _col_pool_matrices(tile), pool_w, pool_scale)


def _merge_kernel(x_ref, ym_ref, yp_ref, gm_ref, gp_ref, gate_ref, wm_ref, wp_ref, wo_ref, fw_ref, o_ref):
    slab = x_ref.shape[0] // MERGE_SLABS
    for r in range(MERGE_SLABS):
        rows = slice(r * slab, (r + 1) * slab)
        y_m = jnp.dot(ym_ref[rows, :], wm_ref[...], preferred_element_type=F32)
        y_p = jnp.dot(yp_ref[rows, :], wp_ref[...], preferred_element_type=F32)
        merged = (_sigmoid_t(gm_ref[rows, :].astype(F32)) * y_m
                  + _sigmoid_t(gp_ref[rows, :].astype(F32)) * y_p)
        upd = jnp.dot(merged.astype(BF16), wo_ref[...], preferred_element_type=F32)
        xo = x_ref[rows, :] + gate_ref[0] * upd
        o_ref[rows, :] = xo * lax.rsqrt(jnp.mean(xo * xo, axis=-1, keepdims=True) + EPS) * fw_ref[...]


def _merge(x2, ym, yp, y, gate, wm, wp, wo, final_w, rows_per_mod, g_col0):
    n, d = x2.shape
    tm = 1024
    per = rows_per_mod // tm
    pw = yp.shape[1]
    full = lambda shape: pl.BlockSpec(shape, lambda i: (0,) * len(shape))
    return pl.pallas_call(
        _merge_kernel,
        grid=(n // tm,),
        in_specs=[pl.BlockSpec((tm, d), lambda i: (i, 0)),
                  pl.BlockSpec((tm, d), lambda i: (i, 0)),
                  pl.BlockSpec((tm, pw), lambda i: (i, 0)),
                  pl.BlockSpec((tm, d), lambda i: (i, g_col0 // d)),
                  pl.BlockSpec((tm, d), lambda i: (i, g_col0 // d + 1)),
                  pl.BlockSpec((1, 1, d), lambda i: (i // per, 0, 0)),
                  full((d, d)), full((pw, d)), full((d, d)), full((1, d))],
        out_specs=pl.BlockSpec((tm, d), lambda i: (i, 0)),
        out_shape=jax.ShapeDtypeStruct((n, d), F32),
        compiler_params=_cparams(("arbitrary",), 48),
        name="merge",
    )(x2, ym, yp, y, y, gate, wm, wp, wo, final_w)


def kernel(x, c, ctx, c_ctx, norm_w, ada_w, ada_b, in_w, gate_b, head_norm_w, pool_w, pool_scale,
           branch_m_w, branch_p_w, out_w, final_norm_w):
    b, t, d = x.shape
    tc = ctx.shape[1]
    depth = norm_w.shape[0]
    assert depth == 1, "context stream update between layers is not implemented"
    mw = M_HEADS * MXU_DIM
    assert d == mw
    g0 = 5 * mw

    cc = jnp.zeros((2 * 8, d), F32).at[:b].set(c).at[b].set(c_ctx)
    mod = _adaln(cc, ada_w[0], ada_b[0])
    shift, scale, gate = mod[:, :d], mod[:, d:2 * d], mod[:, 2 * d:]

    wt_f32 = jnp.transpose(in_w[0])
    nw = norm_w[0].reshape(1, d)

    r_q, r_k, r_v, r_o, r_z = (i * mw for i in range(5))
    r_p = g0 + N_GATE_COLS
    r_gm, r_gp = r_p + mw, r_p + 2 * mw
    pairs = ((r_q, r_k, "id"), (r_v, r_p, "half_silu"), (r_o, r_z, "gate"), (r_gm, r_gp, "id"))
    col = {name: i * mw for i, name in enumerate(("q", "k", "v", "p", "o", "oz", "g_m", "g_p"))}

    wt, yc, gates_tok_c = _ctx_proj(ctx.reshape(b * tc, d), scale[b:b + 1, None, :], shift[b:b + 1, None, :],
                                    nw, wt_f32, g0, r_k, r_v)
    y, gates_tok = _in_proj(x.reshape(b * t, d), scale[:b, None, :], shift[:b, None, :], nw,
                            wt, g0, t, "in_proj", pairs)
    y = y.reshape(b, t, -1)
    yc = yc.reshape(b, tc, -1)

    pc, p, a, decay = _gates(gates_tok.reshape(b, t, LANES), gates_tok_c.reshape(b, tc, LANES), gate_b[0], CHUNK)
    ym = _mlstm(y, yc, p, a, pc, decay, head_norm_w[0].reshape(1, mw), CHUNK, col)
    yp = _pool(y, pool_w[0].astype(BF16), pool_scale[0].reshape(1, -1), col["p"])

    out = _merge(x.reshape(b * t, d), ym.reshape(b * t, mw), yp.reshape(b * t, -1), y.reshape(b * t, -1),
                 gate[:b, None, :], branch_m_w[0].astype(BF16), branch_p_w[0].astype(BF16),
                 out_w[0].astype(BF16), final_norm_w.reshape(1, d), t, col["g_m"])
    return out.reshape(b, t, d)
```

```python
import functools

import numpy as np
import jax
import jax.numpy as jnp
from jax import lax
from jax.experimental import pallas as pl
from jax.experimental.pallas import tpu as pltpu

F32 = jnp.float32
BF16 = jnp.bfloat16

EPS = 1e-6
LOG2E = 1.4426950408889634
M_INIT = -1e30
NEG_BIG = -1e30
M_HEADS = 4
N_DIR = 2
N_GATE_COLS = N_DIR * 2 * M_HEADS
P_GROUPS = 4
POOL_WINDOWS = (2, 4, 8, 16)
GRID_W = 64
LANES = 128
MXU_DIM = 256
CHUNK = 256
POOL_UNROLL = 4
SCAN_UNROLL = 4
MERGE_SLABS = 2
HEADS_PER_STEP = 1
MIB = 1024 * 1024


def _cparams(sem, vmem_mib):
    return pltpu.CompilerParams(dimension_semantics=sem, vmem_limit_bytes=vmem_mib * MIB)


def _sigmoid(x):
    return 1.0 / (1.0 + jnp.exp(-x))


def _silu(x):
    return x * _sigmoid(x)


def _log_sigmoid(x):
    return jnp.minimum(x, 0.0) - jnp.log(1.0 + jnp.exp(-jnp.abs(x)))


def _dot_nt(a, b):
    return lax.dot_general(a, b, (((1,), (1,)), ((), ())), preferred_element_type=F32)


def _adaln_kernel(c_ref, w_ref, b_ref, o_ref):
    o_ref[...] = jnp.dot(_silu(c_ref[...]), w_ref[...], preferred_element_type=F32) + b_ref[...]


def _adaln(cc, ada_w, ada_b):
    rows, d = cc.shape
    n = ada_w.shape[1]
    tn = d
    return pl.pallas_call(
        _adaln_kernel,
        grid=(n // tn,),
        in_specs=[pl.BlockSpec((rows, d), lambda j: (0, 0)),
                  pl.BlockSpec((d, tn), lambda j: (0, j)),
                  pl.BlockSpec((1, tn), lambda j: (0, j))],
        out_specs=pl.BlockSpec((rows, tn), lambda j: (0, j)),
        out_shape=jax.ShapeDtypeStruct((rows, n), F32),
        compiler_params=_cparams(("arbitrary",), 32),
        name="adaln",
    )(cc, ada_w, ada_b.reshape(1, n))


def _sigmoid_t(x):
    return 0.5 * jnp.tanh(0.5 * x) + 0.5


IN_TILE = 1024
GATE_SUB = 256


def _inproj_kernel(x_ref, sc_ref, sh_ref, nw_ref, wa_ref, wb_ref, wg_ref, y_ref, g_ref, xn_ref, *, acts):
    j = pl.program_id(1)

    @pl.when(j == 0)
    def _():
        x = x_ref[...]
        ms = jnp.mean(x * x, axis=-1, keepdims=True)
        xn = x * lax.rsqrt(ms + EPS) * (nw_ref[...] * (1.0 + sc_ref[0])) + sh_ref[0]
        xn_ref[...] = xn.astype(BF16)
        wg = wg_ref[...]
        wg = jnp.concatenate([wg, jnp.zeros((LANES - wg.shape[0], wg.shape[1]), wg.dtype)], axis=0)
        g_ref[...] = _dot_nt(xn_ref[...], wg)

    def pair(act):
        tn = IN_TILE
        xn = xn_ref[...]
        if act == "gate":
            for c in range(0, tn, GATE_SUB):
                a = _dot_nt(xn, wa_ref[c:c + GATE_SUB, :])
                b = _dot_nt(xn, wb_ref[c:c + GATE_SUB, :])
                y_ref[:, c:c + GATE_SUB] = a.astype(BF16)
                y_ref[:, tn + c:tn + c + GATE_SUB] = (_sigmoid_t(a) * (b * _sigmoid_t(b))).astype(BF16)
            return
        y_ref[:, :tn] = _dot_nt(xn, wa_ref[...]).astype(BF16)
        b = _dot_nt(xn, wb_ref[...])
        if act == "half_silu":
            half = tn // 2
            b = jnp.concatenate([b[:, :half], b[:, half:] * _sigmoid_t(b[:, half:])], axis=1)
        y_ref[:, tn:] = b.astype(BF16)

    for act in sorted(set(acts)):
        hit = functools.reduce(jnp.logical_or, [j == s for s, a in enumerate(acts) if a == act])
        pl.when(hit)(functools.partial(pair, act))


def _in_proj(x2, scale, shift, norm_w, w, gate_row0, rows_per_mod, name, pairs):
    n, d = x2.shape
    tm = min(2048, rows_per_mod)
    tn = IN_TILE
    per = rows_per_mod // tm
    acts = tuple(p[2] for p in pairs)

    def w_rows(which):
        def index_map(i, j):
            r = sum(jnp.where(j == s, p[which], 0) for s, p in enumerate(pairs))
            return (pl.multiple_of(r, N_GATE_COLS), 0)
        return index_map

    return pl.pallas_call(
        functools.partial(_inproj_kernel, acts=acts),
        grid=(n // tm, len(pairs)),
        in_specs=[pl.BlockSpec((tm, d), lambda i, j: (i, 0)),
                  pl.BlockSpec((1, 1, d), lambda i, j: (i // per, 0, 0)),
                  pl.BlockSpec((1, 1, d), lambda i, j: (i // per, 0, 0)),
                  pl.BlockSpec((1, d), lambda i, j: (0, 0)),
                  pl.BlockSpec((pl.Element(tn), pl.Element(d)), w_rows(0)),
                  pl.BlockSpec((pl.Element(tn), pl.Element(d)), w_rows(1)),
                  pl.BlockSpec((pl.Element(N_GATE_COLS), pl.Element(d)), lambda i, j: (gate_row0, 0))],
        out_specs=[pl.BlockSpec((tm, 2 * tn), lambda i, j: (i, j)),
                   pl.BlockSpec((tm, LANES), lambda i, j: (i, 0))],
        out_shape=[jax.ShapeDtypeStruct((n, 2 * tn * len(pairs)), BF16),
                   jax.ShapeDtypeStruct((n, LANES), F32)],
        scratch_shapes=[pltpu.VMEM((tm, d), BF16)],
        compiler_params=_cparams(("arbitrary", "arbitrary"), 56),
        name=name,
    )(x2, scale, shift, norm_w, w, w, w)


def _ctx_proj_kernel(x_ref, sc_ref, sh_ref, nw_ref, wsrc_ref, wk_ref, wv_ref, wg_ref,
                     wdst_ref, y_ref, g_ref, xn_ref, *, n_cast, steps, k_rows):
    i, j = pl.program_id(0), pl.program_id(1)
    slab = i * steps + j
    kscale = MXU_DIM ** -0.5

    @pl.when(slab < n_cast)
    def _():
        rows = wsrc_ref.shape[0]
        r = slab * rows + lax.broadcasted_iota(jnp.int32, (rows, 1), 0)
        scale = jnp.where((r >= k_rows[0]) & (r < k_rows[1]), kscale, 1.0)
        wdst_ref[...] = (wsrc_ref[...] * scale).astype(BF16)

    @pl.when(j == 0)
    def _():
        x = x_ref[...]
        ms = jnp.mean(x * x, axis=-1, keepdims=True)
        xn = x * lax.rsqrt(ms + EPS) * (nw_ref[...] * (1.0 + sc_ref[0])) + sh_ref[0]
        xn_ref[...] = xn.astype(BF16)
        wg = wg_ref[...].astype(BF16)
        wg = jnp.concatenate([wg, jnp.zeros((LANES - wg.shape[0], wg.shape[1]), wg.dtype)], axis=0)
        g_ref[...] = _dot_nt(xn_ref[...], wg)

    @pl.when(j == 1)
    def _():
        y_ref[:, :IN_TILE] = _dot_nt(xn_ref[...], (wk_ref[...] * kscale).astype(BF16)).astype(BF16)

    @pl.when(j == 2)
    def _():
        y_ref[:, IN_TILE:] = _dot_nt(xn_ref[...], wv_ref[...].astype(BF16)).astype(BF16)


def _ctx_proj(xc, scale, shift, norm_w, wt, gate_row0, r_k, r_v):
    n, d = xc.shape
    tm, tn = 1024, IN_TILE
    n_cast = 9
    slab = wt.shape[0] // n_cast
    assert slab * n_cast == wt.shape[0] and slab % 16 == 0 and n % tm == 0
    steps = -(-n_cast // (n // tm))
    assert steps >= 3
    cast_idx = lambda i, j: (jnp.minimum(i * steps + j, n_cast - 1), 0)
    elem = lambda rows, r0: pl.BlockSpec((pl.Element(rows), pl.Element(d)), lambda i, j: (r0, 0))
    return pl.pallas_call(
        functools.partial(_ctx_proj_kernel, n_cast=n_cast, steps=steps, k_rows=(r_k, r_k + tn)),
        grid=(n // tm, steps),
        in_specs=[pl.BlockSpec((tm, d), lambda i, j: (i, 0)),
                  pl.BlockSpec((1, 1, d), lambda i, j: (0, 0, 0)),
                  pl.BlockSpec((1, 1, d), lambda i, j: (0, 0, 0)),
                  pl.BlockSpec((1, d), lambda i, j: (0, 0)),
                  pl.BlockSpec((slab, d), cast_idx),
                  elem(tn, r_k), elem(tn, r_v), elem(N_GATE_COLS, gate_row0)],
        out_specs=[pl.BlockSpec((slab, d), cast_idx),
                   pl.BlockSpec((tm, 2 * tn), lambda i, j: (i, 0)),
                   pl.BlockSpec((tm, LANES), lambda i, j: (i, 0))],
        out_shape=[jax.ShapeDtypeStruct(wt.shape, BF16),
                   jax.ShapeDtypeStruct((n, 2 * tn), BF16),
                   jax.ShapeDtypeStruct((n, LANES), F32)],
        scratch_shapes=[pltpu.VMEM((tm, d), BF16)],
        compiler_params=_cparams(("arbitrary", "arbitrary"), 56),
        name="ctx_proj",
    )(xc, scale, shift, norm_w, wt, wt, wt, wt)


N_DH = N_DIR * M_HEADS
TERM_PIECES = (3, 2, 2, 3)


def _split(x, pieces):
    out = []
    for _ in range(pieces):
        p = x.astype(BF16)
        out.append(p)
        x = x - p.astype(F32)
    return out


def _chunk_scan(x, pos, chunk, op, reverse):
    t = x.shape[1]
    k = 1
    while k < chunk:
        if reverse:
            x = jnp.where(pos < chunk - k, op(x, pltpu.roll(x, t - k, 1)), x)
        else:
            x = jnp.where(pos >= k, op(x, pltpu.roll(x, k, 1)), x)
        k *= 2
    return x


def _gate_terms(li, pre_f, m0, chunk):
    rows, t = li.shape
    nc = t // chunk
    pos = lax.broadcasted_iota(jnp.int32, (rows, t), 1) % chunk
    rev = lax.broadcasted_iota(jnp.int32, (rows, t), 0) >= M_HEADS
    rev1 = lax.broadcasted_iota(jnp.int32, (rows, 1), 0) >= M_HEADS
    lf = _log_sigmoid(pre_f)
    b = jnp.where(rev, _chunk_scan(lf, pos, chunk, jnp.add, True), _chunk_scan(lf, pos, chunk, jnp.add, False))
    a = li - b
    cmax = jnp.where(rev, _chunk_scan(a, pos, chunk, jnp.maximum, True),
                     _chunk_scan(a, pos, chunk, jnp.maximum, False))

    def at_scan_end(x, c):
        lo = c * chunk
        return jnp.where(rev1, x[:, lo:lo + 1], x[:, lo + chunk - 1:lo + chunk])

    b_end = [at_scan_end(b, c) for c in range(nc)]
    a_max = [at_scan_end(cmax, c) for c in range(nc)]
    m_f, m_b = m0, m0
    m_in_f, m_in_b = [None] * nc, [None] * nc
    for i in range(nc):
        j = nc - 1 - i
        m_in_f[i] = m_f
        m_f = b_end[i] + jnp.maximum(m_f, a_max[i])
        m_in_b[j] = m_b
        m_b = b_end[j] + jnp.maximum(m_b, a_max[j])
    m_in = [jnp.where(rev1, m_in_b[c], m_in_f[c]) for c in range(nc)]
    per_chunk = lambda vals: jnp.concatenate([jnp.broadcast_to(v, (rows, chunk)) for v in vals], axis=1)
    m_in_t = per_chunk(m_in)
    g = jnp.maximum(m_in_t, cmax)
    g_end = [jnp.maximum(m_in[c], a_max[c]) for c in range(nc)]
    pieces = []
    terms = (-g * LOG2E, jnp.exp(m_in_t - g), jnp.exp(a - per_chunk(g_end)), -(b + g) * LOG2E)
    for term, n in zip(terms, TERM_PIECES):
        pieces += _split(term, n)
    pieces.append(jnp.zeros((LANES - rows * sum(TERM_PIECES), t), BF16))
    decay = jnp.concatenate([jnp.exp(m_in[c] - g_end[c]) for c in range(nc)], axis=1)
    return jnp.concatenate(pieces, axis=0), a * LOG2E, decay, jnp.where(rev1, m_b, m_f)


def _gates_kernel(ic_ref, fc_ref, i_ref, f_ref, bi_ref, bf_ref, rc_ref, dc_ref, r_ref, a_ref, d_ref, *, chunk):
    m0 = jnp.full((N_DH, 1), M_INIT, F32)
    rc_ref[0], _, dc_ref[0], m_ctx = _gate_terms(ic_ref[0] + bi_ref[...], fc_ref[0] + bf_ref[...], m0, chunk)
    r_ref[0], a_ref[0], d_ref[0], _ = _gate_terms(i_ref[0] + bi_ref[...], f_ref[0] + bf_ref[...], m_ctx, chunk)


def _gates(gates_tok, gates_tok_c, gate_b, chunk):
    b, t, _ = gates_tok.shape
    tc = gates_tok_c.shape[1]
    nc, ncc = t // chunk, tc // chunk

    def rows(g):
        g = jnp.transpose(g[:, :, :N_GATE_COLS], (0, 2, 1)).reshape(b, N_DIR, 2, M_HEADS, -1)
        return g[:, :, 0].reshape(b, N_DH, -1), g[:, :, 1].reshape(b, N_DH, -1)

    gb = gate_b.reshape(N_DIR, 2, M_HEADS)
    spec = lambda r, n: pl.BlockSpec((1, r, n), lambda i: (i, 0, 0))
    col = pl.BlockSpec((N_DH, 1), lambda i: (0, 0))
    rc, dc, r, a, dec = pl.pallas_call(
        functools.partial(_gates_kernel, chunk=chunk),
        grid=(b,),
        in_specs=[spec(N_DH, tc), spec(N_DH, tc), spec(N_DH, t), spec(N_DH, t), col, col],
        out_specs=[spec(LANES, tc), spec(N_DH, ncc), spec(LANES, t), spec(N_DH, t), spec(N_DH, nc)],
        out_shape=[jax.ShapeDtypeStruct((b, LANES, tc), BF16), jax.ShapeDtypeStruct((b, N_DH, ncc), F32),
                   jax.ShapeDtypeStruct((b, LANES, t), BF16), jax.ShapeDtypeStruct((b, N_DH, t), F32),
                   jax.ShapeDtypeStruct((b, N_DH, nc), F32)],
        compiler_params=_cparams(("arbitrary",), 32),
        name="gates",
    )(*rows(gates_tok_c), *rows(gates_tok), gb[:, 0].reshape(N_DH, 1), gb[:, 1].reshape(N_DH, 1))
    a = a.reshape(b, N_DH, nc, chunk).transpose(0, 2, 1, 3)
    decay = jnp.concatenate([dc, dec], axis=2).reshape(b * N_DH, ncc + nc)
    return jnp.transpose(rc, (0, 2, 1)), jnp.transpose(r, (0, 2, 1)), a, decay


def _twice(x):
    return jnp.concatenate([x, x], axis=1)


def _mlstm_kernel(dec_ref, q_ref, k_ref, v_ref, oz_ref, p_ref, a_ref, kc_ref, vc_ref, pc_ref,
                  sel_ref, hnw_ref, out_ref, ct_ref, n_ref, hs_ref, *, chunk, dh):
    t = q_ref.shape[1]
    nc = t // chunk
    ncc = kc_ref.shape[1] // chunk
    hps = q_ref.shape[2] // dh
    bi, h0 = pl.program_id(0), pl.program_id(1) * hps
    t_idx = lax.broadcasted_iota(jnp.int32, (chunk, chunk), 0)
    s_idx = lax.broadcasted_iota(jnp.int32, (chunk, chunk), 1)

    def head_cols(hh):
        return slice(hh * dh, (hh + 1) * dh)

    def row_terms(p, hh, d):
        w = jnp.dot(p, sel_ref[hh, d], preferred_element_type=F32)
        return [w[:, i * LANES:(i + 1) * LANES] for i in range(len(TERM_PIECES))]

    def update(k, v, e_b, decay, u):
        ke = k.astype(F32) * _twice(e_b)
        upd = lax.dot_general(ke.astype(BF16), v, (((0,), (0,)), ((), ())), preferred_element_type=F32)
        n_new = jnp.sum(ke, axis=0, keepdims=True)
        ct_ref[u] = upd if decay is None else decay * ct_ref[u] + upd
        n_ref[u] = n_new if decay is None else decay * n_ref[u] + n_new

    def decay_of(c, hh, d):
        return dec_ref[bi * N_DH + d * M_HEADS + h0 + hh, c]

    for hh in range(hps):
        for d in range(N_DIR):
            for i in range(ncc):
                c = i if d == 0 else ncc - 1 - i
                rows = pl.ds(c * chunk, chunk)
                e_b = jnp.dot(pc_ref[0, rows, :], sel_ref[hh, d, :, 2 * LANES:3 * LANES],
                              preferred_element_type=F32)
                update(kc_ref[0, rows, head_cols(hh)], vc_ref[0, rows, head_cols(hh)], e_b,
                       None if i == 0 else decay_of(c, hh, d), hh * N_DIR + d)

    def run_chunk(c, hh, d):
        u = hh * N_DIR + d
        rows = pl.ds(pl.multiple_of(c * chunk, chunk), chunk)
        cols = head_cols(hh)
        q, k, v = q_ref[0, rows, cols], k_ref[0, rows, cols], v_ref[0, rows, cols]
        neg_g_b, w_inter_b, e_b, neg_mt_b = row_terms(p_ref[0, rows, :], hh, d)
        a_row = a_ref[0, c, pl.ds(d * M_HEADS + h0 + hh, 1), :]
        mask = (s_idx <= t_idx) if d == 0 else (s_idx >= t_idx)
        w_intra = jnp.exp2(jnp.where(mask, a_row + _twice(neg_g_b), NEG_BIG))
        s = _dot_nt(q, k) * w_intra
        num = (_twice(w_inter_b) * jnp.dot(q, ct_ref[u].astype(BF16), preferred_element_type=F32)
               + jnp.dot(s.astype(BF16), v, preferred_element_type=F32))
        den_b = (w_inter_b * jnp.sum(q.astype(F32) * n_ref[u], axis=-1, keepdims=True)
                 + jnp.sum(s, axis=-1, keepdims=True))
        h = num * _twice(1.0 / jnp.maximum(jnp.abs(den_b), jnp.exp2(neg_mt_b)))
        update(k, v, e_b, decay_of(ncc + c, hh, d), u)
        return rows, cols, h

    def finalize(rows, cols, h):
        hn = h * lax.rsqrt(jnp.mean(h * h, axis=-1, keepdims=True) + EPS) * hnw_ref[:, cols]
        out_ref[0, rows, cols] = (hn * oz_ref[0, rows, cols].astype(F32)).astype(out_ref.dtype)

    def first(i, carry):
        for hh in range(hps):
            for d in range(N_DIR):
                rows, cols, h = run_chunk(i if d == 0 else nc - 1 - i, hh, d)
                hs_ref[rows, cols] = h
        return carry

    def second(i, carry):
        for hh in range(hps):
            for d in range(N_DIR):
                rows, cols, h = run_chunk(i if d == 0 else nc - 1 - i, hh, d)
                finalize(rows, cols, hs_ref[rows, cols] + h)
        return carry

    lax.fori_loop(0, nc // 2, first, 0, unroll=SCAN_UNROLL)
    lax.fori_loop(nc // 2, nc, second, 0, unroll=SCAN_UNROLL)


def _term_selectors():
    sel = np.zeros((M_HEADS, N_DIR, LANES, len(TERM_PIECES) * LANES), np.float32)
    first = np.concatenate([[0], np.cumsum(TERM_PIECES)])
    for h in range(M_HEADS):
        for d in range(N_DIR):
            for q in range(len(TERM_PIECES)):
                for j in range(first[q], first[q + 1]):
                    sel[h, d, j * N_DH + d * M_HEADS + h, q * LANES:(q + 1) * LANES] = 1.0
    return jnp.asarray(sel, BF16)


def _mlstm(y, yc, p, a, pc, decay, head_norm_w, chunk, col):
    b, t, _ = y.shape
    tc = yc.shape[1]
    dh = MXU_DIM
    hps = HEADS_PER_STEP
    w = hps * dh
    sel = _term_selectors()
    assert t % (2 * chunk) == 0 and tc % chunk == 0 and M_HEADS % hps == 0
    blk = lambda name: pl.BlockSpec((1, t, w), lambda i, h, c=col[name] // w: (i, 0, c + h))
    blkc = lambda col0: pl.BlockSpec((1, tc, w), lambda i, h, c=col0 // w: (i, 0, c + h))
    return pl.pallas_call(
        functools.partial(_mlstm_kernel, chunk=chunk, dh=dh),
        grid=(b, M_HEADS // hps),
        in_specs=[pl.BlockSpec(memory_space=pltpu.SMEM),
                  blk("q"), blk("k"), blk("v"), blk("oz"),
                  pl.BlockSpec((1, t, LANES), lambda i, h: (i, 0, 0)),
                  pl.BlockSpec((1, t // chunk, N_DH, chunk), lambda i, h: (i, 0, 0, 0)),
                  blkc(0), blkc(M_HEADS * dh),
                  pl.BlockSpec((1, tc, LANES), lambda i, h: (i, 0, 0)),
                  pl.BlockSpec((hps, N_DIR) + sel.shape[2:], lambda i, h: (h, 0, 0, 0)),
                  pl.BlockSpec((1, w), lambda i, h: (0, h))],
        out_specs=pl.BlockSpec((1, t, w), lambda i, h: (i, 0, h)),
        out_shape=jax.ShapeDtypeStruct((b, t, M_HEADS * dh), BF16),
        scratch_shapes=[pltpu.VMEM((hps * N_DIR, dh, dh), F32),
                        pltpu.VMEM((hps * N_DIR, 1, dh), F32),
                        pltpu.VMEM((t, w), F32)],
        compiler_params=_cparams(("arbitrary", "arbitrary"), 48),
        name="mlstm",
    )(decay, y, y, y, y, p, a, yc, yc, pc, sel, head_norm_w)


def _window_bounds(pos, w, length):
    return np.clip(pos - w // 2, 0, length), np.clip(pos + w - w // 2, 0, length)


def _col_pool_matrices(tile):
    pos = np.arange(GRID_W)
    mats = []
    for w in POOL_WINDOWS:
        lo, hi = _window_bounds(pos, w, GRID_W)
        band = ((pos[None, :] >= lo[:, None]) & (pos[None, :] < hi[:, None])).astype(np.float32)
        mats.append(np.kron(np.eye(tile // GRID_W, dtype=np.float32), band))
    return jnp.asarray(np.stack(mats), dtype=BF16)


def _pool_kernel(p_ref, z_ref, mc_ref, pw_ref, ps_ref, out_ref, pad_ref, scale_ref, *, tile, rows):
    t = p_ref.shape[1]
    gd = LANES
    pad = (max(POOL_WINDOWS) // 2) * GRID_W
    n_tiles = t // tile

    @pl.when(pl.program_id(0) == 0)
    def _():
        pad_ref[...] = jnp.zeros(pad_ref.shape, F32)
        tok = lax.broadcasted_iota(jnp.int32, (t, gd), 0)
        r = tok // GRID_W
        c = tok % GRID_W
        for g, w in enumerate(POOL_WINDOWS):
            cnt_r = jnp.minimum(r + (w - w // 2), rows) - jnp.maximum(r - w // 2, 0)
            cnt_c = jnp.minimum(c + (w - w // 2), GRID_W) - jnp.maximum(c - w // 2, 0)
            scale_ref[g] = 1.0 / (cnt_r.astype(F32) * cnt_c.astype(F32))

    def col_sum(g, start):
        cols = pl.ds(g * gd, gd)
        pad_ref[g % 2, pl.ds(pad + start, tile), :] = jnp.dot(
            mc_ref[g], p_ref[0, pl.ds(start, tile), cols], preferred_element_type=F32)

    def row_sum(g, start):
        w = POOL_WINDOWS[g]
        cols = pl.ds(g * gd, gd)
        buf = pad_ref.at[g % 2]
        acc = buf[pl.ds(pad + start - (w // 2) * GRID_W, tile), :]
        for j in range(1 - w // 2, w - w // 2):
            acc = acc + buf[pl.ds(pad + start + j * GRID_W, tile), :]
        mixed = acc * scale_ref[g, pl.ds(start, tile), :] - p_ref[0, pl.ds(start, tile), cols].astype(F32)
        mm = jnp.dot(mixed.astype(BF16), pw_ref[g], preferred_element_type=F32)
        zz = z_ref[0, pl.ds(start, tile), cols].astype(F32)
        out_ref[0, pl.ds(start, tile), cols] = (mm * ps_ref[:, cols] * zz).astype(out_ref.dtype)

    for g in range(-1, P_GROUPS):
        def body(i, carry, g=g):
            start = pl.multiple_of(i * tile, tile)
            if g + 1 < P_GROUPS:
                col_sum(g + 1, start)
            if g >= 0:
                row_sum(g, start)
            return carry

        lax.fori_loop(0, n_tiles, body, 0, unroll=POOL_UNROLL)


def _pool(y, pool_w, pool_scale, p_col0):
    b, t, _ = y.shape
    pw = P_GROUPS * LANES
    tile = MXU_DIM
    pad = (max(POOL_WINDOWS) // 2) * GRID_W
    return pl.pallas_call(
        functools.partial(_pool_kernel, tile=tile, rows=t // GRID_W),
        grid=(b,),
        in_specs=[pl.BlockSpec((1, t, pw), lambda i: (i, 0, p_col0 // pw)),
                  pl.BlockSpec((1, t, pw), lambda i: (i, 0, p_col0 // pw + 1)),
                  pl.BlockSpec((P_GROUPS, tile, tile), lambda i: (0, 0, 0)),
                  pl.BlockSpec((P_GROUPS, LANES, LANES), lambda i: (0, 0, 0)),
                  pl.BlockSpec((1, pw), lambda i: (0, 0))],
        out_specs=pl.BlockSpec((1, t, pw), lambda i: (i, 0, 0)),
        out_shape=jax.ShapeDtypeStruct((b, t, pw), BF16),
        scratch_shapes=[pltpu.VMEM((2, t + 2 * pad, LANES), F32),
                        pltpu.VMEM((P_GROUPS, t, LANES), F32)],
        compiler_params=_cparams(("arbitrary",), 48),
        name="pool",
    )(y, y, _col_pool_matrices(tile), pool_w, pool_scale)


def _merge_kernel(x_ref, ym_ref, yp_ref, gm_ref, gp_ref, gate_ref, wm_ref, wp_ref, wo_ref, fw_ref, o_ref):
    slab = x_ref.shape[0] // MERGE_SLABS
    for r in range(MERGE_SLABS):
        rows = slice(r * slab, (r + 1) * slab)
        y_m = jnp.dot(ym_ref[rows, :], wm_ref[...], preferred_element_type=F32)
        y_p = jnp.dot(yp_ref[rows, :], wp_ref[...], preferred_element_type=F32)
        merged = (_sigmoid_t(gm_ref[rows, :].astype(F32)) * y_m
                  + _sigmoid_t(gp_ref[rows, :].astype(F32)) * y_p)
        upd = jnp.dot(merged.astype(BF16), wo_ref[...], preferred_element_type=F32)
        xo = x_ref[rows, :] + gate_ref[0] * upd
        o_ref[rows, :] = xo * lax.rsqrt(jnp.mean(xo * xo, axis=-1, keepdims=True) + EPS) * fw_ref[...]


def _merge(x2, ym, yp, y, gate, wm, wp, wo, final_w, rows_per_mod, g_col0):
    n, d = x2.shape
    tm = 1024
    per = rows_per_mod // tm
    pw = yp.shape[1]
    full = lambda shape: pl.BlockSpec(shape, lambda i: (0,) * len(shape))
    return pl.pallas_call(
        _merge_kernel,
        grid=(n // tm,),
        in_specs=[pl.BlockSpec((tm, d), lambda i: (i, 0)),
                  pl.BlockSpec((tm, d), lambda i: (i, 0)),
                  pl.BlockSpec((tm, pw), lambda i: (i, 0)),
                  pl.BlockSpec((tm, d), lambda i: (i, g_col0 // d)),
                  pl.BlockSpec((tm, d), lambda i: (i, g_col0 // d + 1)),
                  pl.BlockSpec((1, 1, d), lambda i: (i // per, 0, 0)),
                  full((d, d)), full((pw, d)), full((d, d)), full((1, d))],
        out_specs=pl.BlockSpec((tm, d), lambda i: (i, 0)),
        out_shape=jax.ShapeDtypeStruct((n, d), F32),
        compiler_params=_cparams(("arbitrary",), 48),
        name="merge",
    )(x2, ym, yp, y, y, gate, wm, wp, wo, final_w)


def kernel(x, c, ctx, c_ctx, norm_w, ada_w, ada_b, in_w, gate_b, head_norm_w, pool_w, pool_scale,
           branch_m_w, branch_p_w, out_w, final_norm_w):
    b, t, d = x.shape
    tc = ctx.shape[1]
    depth = norm_w.shape[0]
    assert depth == 1, "context stream update between layers is not implemented"
    mw = M_HEADS * MXU_DIM
    assert d == mw
    g0 = 5 * mw

    cc = jnp.zeros((2 * 8, d), F32).at[:b].set(c).at[b].set(c_ctx)
    mod = _adaln(cc, ada_w[0], ada_b[0])
    shift, scale, gate = mod[:, :d], mod[:, d:2 * d], mod[:, 2 * d:]

    wt_f32 = jnp.transpose(in_w[0])
    nw = norm_w[0].reshape(1, d)

    r_q, r_k, r_v, r_o, r_z = (i * mw for i in range(5))
    r_p = g0 + N_GATE_COLS
    r_gm, r_gp = r_p + mw, r_p + 2 * mw
    pairs = ((r_q, r_k, "id"), (r_v, r_p, "half_silu"), (r_o, r_z, "gate"), (r_gm, r_gp, "id"))
    col = {name: i * mw for i, name in enumerate(("q", "k", "v", "p", "o", "oz", "g_m", "g_p"))}

    wt, yc, gates_tok_c = _ctx_proj(ctx.reshape(b * tc, d), scale[b:b + 1, None, :], shift[b:b + 1, None, :],
                                    nw, wt_f32, g0, r_k, r_v)
    y, gates_tok = _in_proj(x.reshape(b * t, d), scale[:b, None, :], shift[:b, None, :], nw,
                            wt, g0, t, "in_proj", pairs)
    y = y.reshape(b, t, -1)
    yc = yc.reshape(b, tc, -1)

    pc, p, a, decay = _gates(gates_tok.reshape(b, t, LANES), gates_tok_c.reshape(b, tc, LANES), gate_b[0], CHUNK)
    ym = _mlstm(y, yc, p, a, pc, decay, head_norm_w[0].reshape(1, mw), CHUNK, col)
    yp = _pool(y, pool_w[0].astype(BF16), pool_scale[0].reshape(1, -1), col["p"])

    out = _merge(x.reshape(b * t, d), ym.reshape(b * t, mw), yp.reshape(b * t, -1), y.reshape(b * t, -1),
                 gate[:b, None, :], branch_m_w[0].astype(BF16), branch_p_w[0].astype(BF16),
                 out_w[0].astype(BF16), final_norm_w.reshape(1, d), t, col["g_m"])
    return out.reshape(b, t, d)
```

```python
import functools

import numpy as np
import jax
import jax.numpy as jnp
from jax import lax
from jax.experimental import pallas as pl
from jax.experimental.pallas import tpu as pltpu

F32 = jnp.float32
BF16 = jnp.bfloat16

EPS = 1e-6
LOG2E = 1.4426950408889634
M_INIT = -1e30
NEG_BIG = -1e30
M_HEADS = 4
N_DIR = 2
N_GATE_COLS = N_DIR * 2 * M_HEADS
P_GROUPS = 4
POOL_WINDOWS = (2, 4, 8, 16)
GRID_W = 64
LANES = 128
MXU_DIM = 256
CHUNK = 256
POOL_UNROLL = 4
SCAN_UNROLL = 4
MERGE_SLABS = 2
HEADS_PER_STEP = 1
MIB = 1024 * 1024


def _cparams(sem, vmem_mib):
    return pltpu.CompilerParams(dimension_semantics=sem, vmem_limit_bytes=vmem_mib * MIB)


def _sigmoid(x):
    return 1.0 / (1.0 + jnp.exp(-x))


def _silu(x):
    return x * _sigmoid(x)


def _log_sigmoid(x):
    return jnp.minimum(x, 0.0) - jnp.log(1.0 + jnp.exp(-jnp.abs(x)))


def _dot_nt(a, b):
    return lax.dot_general(a, b, (((1,), (1,)), ((), ())), preferred_element_type=F32)


def _adaln_kernel(c_ref, cc_ref, w_ref, b_ref, o_ref):
    b = c_ref.shape[0]
    rows = jnp.concatenate([c_ref[...], jnp.broadcast_to(cc_ref[...], (o_ref.shape[0] - b, c_ref.shape[1]))], axis=0)
    o_ref[:, 0, :] = jnp.dot(_silu(rows), w_ref[...], preferred_element_type=F32) + b_ref[...]


def _adaln(c, c_ctx, ada_w, ada_b):
    b, d = c.shape
    rows = 8 * (b // 8 + 1)
    n = ada_w.shape[1]
    tn = d
    return pl.pallas_call(
        _adaln_kernel,
        grid=(n // tn,),
        in_specs=[pl.BlockSpec((b, d), lambda j: (0, 0)),
                  pl.BlockSpec((1, d), lambda j: (0, 0)),
                  pl.BlockSpec((d, tn), lambda j: (0, j)),
                  pl.BlockSpec((1, tn), lambda j: (0, j))],
        out_specs=pl.BlockSpec((rows, 1, tn), lambda j: (0, 0, j)),
        out_shape=jax.ShapeDtypeStruct((rows, 1, n), F32),
        compiler_params=_cparams(("arbitrary",), 32),
        name="adaln",
    )(c, c_ctx.reshape(1, d), ada_w, ada_b.reshape(1, n))


def _sigmoid_t(x):
    return 0.5 * jnp.tanh(0.5 * x) + 0.5


IN_TILE = 1024
GATE_SUB = 256


def _inproj_kernel(x_ref, sc_ref, sh_ref, nw_ref, wa_ref, wb_ref, wg_ref, y_ref, g_ref, xn_ref, *, acts):
    j = pl.program_id(1)

    @pl.when(j == 0)
    def _():
        x = x_ref[...]
        ms = jnp.mean(x * x, axis=-1, keepdims=True)
        xn = x * lax.rsqrt(ms + EPS) * (nw_ref[...] * (1.0 + sc_ref[0])) + sh_ref[0]
        xn_ref[...] = xn.astype(BF16)
        wg = wg_ref[...]
        wg = jnp.concatenate([wg, jnp.zeros((LANES - wg.shape[0], wg.shape[1]), wg.dtype)], axis=0)
        g_ref[...] = _dot_nt(xn_ref[...], wg).T

    def pair(act):
        tn = IN_TILE
        xn = xn_ref[...]
        if act == "gate":
            for c in range(0, tn, GATE_SUB):
                a = _dot_nt(xn, wa_ref[c:c + GATE_SUB, :])
                b = _dot_nt(xn, wb_ref[c:c + GATE_SUB, :])
                y_ref[:, c:c + GATE_SUB] = a.astype(BF16)
                y_ref[:, tn + c:tn + c + GATE_SUB] = (_sigmoid_t(a) * (b * _sigmoid_t(b))).astype(BF16)
            return
        y_ref[:, :tn] = _dot_nt(xn, wa_ref[...]).astype(BF16)
        b = _dot_nt(xn, wb_ref[...])
        if act == "half_silu":
            half = tn // 2
            b = jnp.concatenate([b[:, :half], b[:, half:] * _sigmoid_t(b[:, half:])], axis=1)
        y_ref[:, tn:] = b.astype(BF16)

    for act in sorted(set(acts)):
        hit = functools.reduce(jnp.logical_or, [j == s for s, a in enumerate(acts) if a == act])
        pl.when(hit)(functools.partial(pair, act))


def _in_proj(x2, mod, norm_w, w, gate_row0, rows_per_mod, name, pairs):
    n, d = x2.shape
    tm = min(2048, rows_per_mod)
    tn = IN_TILE
    per = rows_per_mod // tm
    acts = tuple(p[2] for p in pairs)

    def w_rows(which):
        def index_map(i, j):
            r = sum(jnp.where(j == s, p[which], 0) for s, p in enumerate(pairs))
            return (pl.multiple_of(r, N_GATE_COLS), 0)
        return index_map

    return pl.pallas_call(
        functools.partial(_inproj_kernel, acts=acts),
        grid=(n // tm, len(pairs)),
        in_specs=[pl.BlockSpec((tm, d), lambda i, j: (i, 0)),
                  pl.BlockSpec((1, 1, d), lambda i, j: (i // per, 0, 1)),
                  pl.BlockSpec((1, 1, d), lambda i, j: (i // per, 0, 0)),
                  pl.BlockSpec((1, d), lambda i, j: (0, 0)),
                  pl.BlockSpec((pl.Element(tn), pl.Element(d)), w_rows(0)),
                  pl.BlockSpec((pl.Element(tn), pl.Element(d)), w_rows(1)),
                  pl.BlockSpec((pl.Element(N_GATE_COLS), pl.Element(d)), lambda i, j: (gate_row0, 0))],
        out_specs=[pl.BlockSpec((tm, 2 * tn), lambda i, j: (i, j)),
                   pl.BlockSpec((LANES, tm), lambda i, j: (0, i))],
        out_shape=[jax.ShapeDtypeStruct((n, 2 * tn * len(pairs)), BF16),
                   jax.ShapeDtypeStruct((LANES, n), F32)],
        scratch_shapes=[pltpu.VMEM((tm, d), BF16)],
        compiler_params=_cparams(("arbitrary", "arbitrary"), 56),
        name=name,
    )(x2, mod, mod, norm_w, w, w, w)


def _ctx_proj_kernel(x_ref, sc_ref, sh_ref, nw_ref, wsrc_ref, wk_ref, wv_ref, wg_ref,
                     wdst_ref, y_ref, g_ref, xn_ref, *, n_cast, steps, k_rows):
    i, j = pl.program_id(0), pl.program_id(1)
    slab = i * steps + j
    kscale = MXU_DIM ** -0.5

    @pl.when(slab < n_cast)
    def _():
        rows = wsrc_ref.shape[0]
        r = slab * rows + lax.broadcasted_iota(jnp.int32, (rows, 1), 0)
        scale = jnp.where((r >= k_rows[0]) & (r < k_rows[1]), kscale, 1.0)
        wdst_ref[...] = (wsrc_ref[...] * scale).astype(BF16)

    @pl.when(j == 0)
    def _():
        x = x_ref[...]
        ms = jnp.mean(x * x, axis=-1, keepdims=True)
        xn = x * lax.rsqrt(ms + EPS) * (nw_ref[...] * (1.0 + sc_ref[0])) + sh_ref[0]
        xn_ref[...] = xn.astype(BF16)
        wg = wg_ref[...].astype(BF16)
        wg = jnp.concatenate([wg, jnp.zeros((LANES - wg.shape[0], wg.shape[1]), wg.dtype)], axis=0)
        g_ref[...] = _dot_nt(xn_ref[...], wg).T

    @pl.when(j == 1)
    def _():
        y_ref[:, :IN_TILE] = _dot_nt(xn_ref[...], (wk_ref[...] * kscale).astype(BF16)).astype(BF16)

    @pl.when(j == 2)
    def _():
        y_ref[:, IN_TILE:] = _dot_nt(xn_ref[...], wv_ref[...].astype(BF16)).astype(BF16)


def _ctx_proj(xc, mod, mod_row, norm_w, wt, gate_row0, r_k, r_v):
    n, d = xc.shape
    tm, tn = 1024, IN_TILE
    n_cast = 9
    slab = wt.shape[0] // n_cast
    assert slab * n_cast == wt.shape[0] and slab % 16 == 0 and n % tm == 0
    steps = -(-n_cast // (n // tm))
    assert steps >= 3
    cast_idx = lambda i, j: (jnp.minimum(i * steps + j, n_cast - 1), 0)
    elem = lambda rows, r0: pl.BlockSpec((pl.Element(rows), pl.Element(d)), lambda i, j: (r0, 0))
    return pl.pallas_call(
        functools.partial(_ctx_proj_kernel, n_cast=n_cast, steps=steps, k_rows=(r_k, r_k + tn)),
        grid=(n // tm, steps),
        in_specs=[pl.BlockSpec((tm, d), lambda i, j: (i, 0)),
                  pl.BlockSpec((1, 1, d), lambda i, j: (mod_row, 0, 1)),
                  pl.BlockSpec((1, 1, d), lambda i, j: (mod_row, 0, 0)),
                  pl.BlockSpec((1, d), lambda i, j: (0, 0)),
                  pl.BlockSpec((slab, d), cast_idx),
                  elem(tn, r_k), elem(tn, r_v), elem(N_GATE_COLS, gate_row0)],
        out_specs=[pl.BlockSpec((slab, d), cast_idx),
                   pl.BlockSpec((tm, 2 * tn), lambda i, j: (i, 0)),
                   pl.BlockSpec((LANES, tm), lambda i, j: (0, i))],
        out_shape=[jax.ShapeDtypeStruct(wt.shape, BF16),
                   jax.ShapeDtypeStruct((n, 2 * tn), BF16),
                   jax.ShapeDtypeStruct((LANES, n), F32)],
        scratch_shapes=[pltpu.VMEM((tm, d), BF16)],
        compiler_params=_cparams(("arbitrary", "arbitrary"), 56),
        name="ctx_proj",
    )(xc, mod, mod, norm_w, wt, wt, wt, wt)


N_DH = N_DIR * M_HEADS
TERM_PIECES = (3, 2, 2, 3)


def _split(x, pieces):
    out = []
    for _ in range(pieces):
        p = x.astype(BF16)
        out.append(p)
        x = x - p.astype(F32)
    return out


def _chunk_scan(x, pos, chunk, op, reverse):
    t = x.shape[1]
    k = 1
    while k < chunk:
        if reverse:
            x = jnp.where(pos < chunk - k, op(x, pltpu.roll(x, t - k, 1)), x)
        else:
            x = jnp.where(pos >= k, op(x, pltpu.roll(x, k, 1)), x)
        k *= 2
    return x


def _gate_terms(li, pre_f, m0, chunk):
    rows, t = li.shape
    nc = t // chunk
    pos = lax.broadcasted_iota(jnp.int32, (rows, t), 1) % chunk
    rev = lax.broadcasted_iota(jnp.int32, (rows, t), 0) >= M_HEADS
    rev1 = lax.broadcasted_iota(jnp.int32, (rows, 1), 0) >= M_HEADS
    lf = _log_sigmoid(pre_f)
    b = jnp.where(rev, _chunk_scan(lf, pos, chunk, jnp.add, True), _chunk_scan(lf, pos, chunk, jnp.add, False))
    a = li - b
    cmax = jnp.where(rev, _chunk_scan(a, pos, chunk, jnp.maximum, True),
                     _chunk_scan(a, pos, chunk, jnp.maximum, False))

    def at_scan_end(x, c):
        lo = c * chunk
        return jnp.where(rev1, x[:, lo:lo + 1], x[:, lo + chunk - 1:lo + chunk])

    b_end = [at_scan_end(b, c) for c in range(nc)]
    a_max = [at_scan_end(cmax, c) for c in range(nc)]
    m_f, m_b = m0, m0
    m_in_f, m_in_b = [None] * nc, [None] * nc
    for i in range(nc):
        j = nc - 1 - i
        m_in_f[i] = m_f
        m_f = b_end[i] + jnp.maximum(m_f, a_max[i])
        m_in_b[j] = m_b
        m_b = b_end[j] + jnp.maximum(m_b, a_max[j])
    m_in = [jnp.where(rev1, m_in_b[c], m_in_f[c]) for c in range(nc)]
    per_chunk = lambda vals: jnp.concatenate([jnp.broadcast_to(v, (rows, chunk)) for v in vals], axis=1)
    m_in_t = per_chunk(m_in)
    g = jnp.maximum(m_in_t, cmax)
    g_end = [jnp.maximum(m_in[c], a_max[c]) for c in range(nc)]
    pieces = []
    terms = (-g * LOG2E, jnp.exp(m_in_t - g), jnp.exp(a - per_chunk(g_end)), -(b + g) * LOG2E)
    for term, n in zip(terms, TERM_PIECES):
        pieces += _split(term, n)
    pieces.append(jnp.zeros((LANES - rows * sum(TERM_PIECES), t), BF16))
    decay = jnp.concatenate([jnp.exp(m_in[c] - g_end[c]) for c in range(nc)], axis=1)
    return jnp.concatenate(pieces, axis=0), a * LOG2E, decay, jnp.where(rev1, m_b, m_f)


def _gates_kernel(gc_ref, g_ref, b_ref, rc_ref, dc_ref, r_ref, a_ref, d_ref, *, chunk):
    def split(ref):
        pre = ref[...] + b_ref[...]
        pick = lambda gate: jnp.concatenate(
            [pre[(2 * d + gate) * M_HEADS:(2 * d + gate + 1) * M_HEADS] for d in range(N_DIR)], axis=0)
        return pick(0), pick(1)

    m0 = jnp.full((N_DH, 1), M_INIT, F32)
    rc_ref[0], _, dc_ref[0], m_ctx = _gate_terms(*split(gc_ref), m0, chunk)
    r_ref[0], a, d_ref[0], _ = _gate_terms(*split(g_ref), m_ctx, chunk)
    for c in range(a_ref.shape[1]):
        a_ref[0, c] = a[:, c * chunk:(c + 1) * chunk]


def _gates(gates_t, gates_tc, gate_b, b, chunk):
    t = gates_t.shape[1] // b
    tc = gates_tc.shape[1] // b
    nc, ncc = t // chunk, tc // chunk
    spec = lambda r, n: pl.BlockSpec((1, r, n), lambda i: (i, 0, 0))
    seq = lambda n: pl.BlockSpec((N_GATE_COLS, n), lambda i: (0, i))
    rc, dc, r, a, dec = pl.pallas_call(
        functools.partial(_gates_kernel, chunk=chunk),
        grid=(b,),
        in_specs=[seq(tc), seq(t), pl.BlockSpec((N_GATE_COLS, 1), lambda i: (0, 0))],
        out_specs=[spec(LANES, tc), spec(N_DH, ncc), spec(LANES, t),
                   pl.BlockSpec((1, nc, N_DH, chunk), lambda i: (i, 0, 0, 0)), spec(N_DH, nc)],
        out_shape=[jax.ShapeDtypeStruct((b, LANES, tc), BF16), jax.ShapeDtypeStruct((b, N_DH, ncc), F32),
                   jax.ShapeDtypeStruct((b, LANES, t), BF16), jax.ShapeDtypeStruct((b, nc, N_DH, chunk), F32),
                   jax.ShapeDtypeStruct((b, N_DH, nc), F32)],
        compiler_params=_cparams(("arbitrary",), 32),
        name="gates",
    )(gates_tc, gates_t, gate_b.reshape(N_GATE_COLS, 1))
    decay = jnp.concatenate([dc, dec], axis=2).reshape(b * N_DH, ncc + nc)
    return jnp.transpose(rc, (0, 2, 1)), jnp.transpose(r, (0, 2, 1)), a, decay


def _twice(x):
    return jnp.concatenate([x, x], axis=1)


def _mlstm_kernel(dec_ref, q_ref, k_ref, v_ref, oz_ref, p_ref, a_ref, kc_ref, vc_ref, pc_ref,
                  sel_ref, hnw_ref, out_ref, ct_ref, n_ref, hs_ref, *, chunk, dh):
    t = q_ref.shape[1]
    nc = t // chunk
    ncc = kc_ref.shape[1] // chunk
    hps = q_ref.shape[2] // dh
    bi, h0 = pl.program_id(0), pl.program_id(1) * hps
    t_idx = lax.broadcasted_iota(jnp.int32, (chunk, chunk), 0)
    s_idx = lax.broadcasted_iota(jnp.int32, (chunk, chunk), 1)

    def head_cols(hh):
        return slice(hh * dh, (hh + 1) * dh)

    def row_terms(p, hh, d):
        w = jnp.dot(p, sel_ref[hh, d], preferred_element_type=F32)
        return [w[:, i * LANES:(i + 1) * LANES] for i in range(len(TERM_PIECES))]

    def update(k, v, e_b, decay, u):
        ke = k.astype(F32) * _twice(e_b)
        upd = lax.dot_general(ke.astype(BF16), v, (((0,), (0,)), ((), ())), preferred_element_type=F32)
        n_new = jnp.sum(ke, axis=0, keepdims=True)
        ct_ref[u] = upd if decay is None else decay * ct_ref[u] + upd
        n_ref[u] = n_new if decay is None else decay * n_ref[u] + n_new

    def decay_of(c, hh, d):
        return dec_ref[bi * N_DH + d * M_HEADS + h0 + hh, c]

    for hh in range(hps):
        for d in range(N_DIR):
            for i in range(ncc):
                c = i if d == 0 else ncc - 1 - i
                rows = pl.ds(c * chunk, chunk)
                e_b = jnp.dot(pc_ref[0, rows, :], sel_ref[hh, d, :, 2 * LANES:3 * LANES],
                              preferred_element_type=F32)
                update(kc_ref[0, rows, head_cols(hh)], vc_ref[0, rows, head_cols(hh)], e_b,
                       None if i == 0 else decay_of(c, hh, d), hh * N_DIR + d)

    def run_chunk(c, hh, d):
        u = hh * N_DIR + d
        rows = pl.ds(pl.multiple_of(c * chunk, chunk), chunk)
        cols = head_cols(hh)
        q, k, v = q_ref[0, rows, cols], k_ref[0, rows, cols], v_ref[0, rows, cols]
        neg_g_b, w_inter_b, e_b, neg_mt_b = row_terms(p_ref[0, rows, :], hh, d)
        a_row = a_ref[0, c, pl.ds(d * M_HEADS + h0 + hh, 1), :]
        mask = (s_idx <= t_idx) if d == 0 else (s_idx >= t_idx)
        w_intra = jnp.exp2(jnp.where(mask, a_row + _twice(neg_g_b), NEG_BIG))
        s = _dot_nt(q, k) * w_intra
        num = (_twice(w_inter_b) * jnp.dot(q, ct_ref[u].astype(BF16), preferred_element_type=F32)
               + jnp.dot(s.astype(BF16), v, preferred_element_type=F32))
        den_b = (w_inter_b * jnp.sum(q.astype(F32) * n_ref[u], axis=-1, keepdims=True)
                 + jnp.sum(s, axis=-1, keepdims=True))
        h = num * _twice(1.0 / jnp.maximum(jnp.abs(den_b), jnp.exp2(neg_mt_b)))
        update(k, v, e_b, decay_of(ncc + c, hh, d), u)
        return rows, cols, h

    def finalize(rows, cols, h):
        hn = h * lax.rsqrt(jnp.mean(h * h, axis=-1, keepdims=True) + EPS) * hnw_ref[:, cols]
        out_ref[0, rows, cols] = (hn * oz_ref[0, rows, cols].astype(F32)).astype(out_ref.dtype)

    def first(i, carry):
        for hh in range(hps):
            for d in range(N_DIR):
                rows, cols, h = run_chunk(i if d == 0 else nc - 1 - i, hh, d)
                hs_ref[rows, cols] = h
        return carry

    def second(i, carry):
        for hh in range(hps):
            for d in range(N_DIR):
                rows, cols, h = run_chunk(i if d == 0 else nc - 1 - i, hh, d)
                finalize(rows, cols, hs_ref[rows, cols] + h)
        return carry

    lax.fori_loop(0, nc // 2, first, 0, unroll=SCAN_UNROLL)
    lax.fori_loop(nc // 2, nc, second, 0, unroll=SCAN_UNROLL)


def _term_selectors():
    sel = np.zeros((M_HEADS, N_DIR, LANES, len(TERM_PIECES) * LANES), np.float32)
    first = np.concatenate([[0], np.cumsum(TERM_PIECES)])
    for h in range(M_HEADS):
        for d in range(N_DIR):
            for q in range(len(TERM_PIECES)):
                for j in range(first[q], first[q + 1]):
                    sel[h, d, j * N_DH + d * M_HEADS + h, q * LANES:(q + 1) * LANES] = 1.0
    return jnp.asarray(sel, BF16)


def _mlstm(y, yc, p, a, pc, decay, head_norm_w, chunk, col):
    b, t, _ = y.shape
    tc = yc.shape[1]
    dh = MXU_DIM
    hps = HEADS_PER_STEP
    w = hps * dh
    sel = _term_selectors()
    assert t % (2 * chunk) == 0 and tc % chunk == 0 and M_HEADS % hps == 0
    blk = lambda name: pl.BlockSpec((1, t, w), lambda i, h, c=col[name] // w: (i, 0, c + h))
    blkc = lambda col0: pl.BlockSpec((1, tc, w), lambda i, h, c=col0 // w: (i, 0, c + h))
    return pl.pallas_call(
        functools.partial(_mlstm_kernel, chunk=chunk, dh=dh),
        grid=(b, M_HEADS // hps),
        in_specs=[pl.BlockSpec(memory_space=pltpu.SMEM),
                  blk("q"), blk("k"), blk("v"), blk("oz"),
                  pl.BlockSpec((1, t, LANES), lambda i, h: (i, 0, 0)),
                  pl.BlockSpec((1, t // chunk, N_DH, chunk), lambda i, h: (i, 0, 0, 0)),
                  blkc(0), blkc(M_HEADS * dh),
                  pl.BlockSpec((1, tc, LANES), lambda i, h: (i, 0, 0)),
                  pl.BlockSpec((hps, N_DIR) + sel.shape[2:], lambda i, h: (h, 0, 0, 0)),
                  pl.BlockSpec((1, w), lambda i, h: (0, h))],
        out_specs=pl.BlockSpec((1, t, w), lambda i, h: (i, 0, h)),
        out_shape=jax.ShapeDtypeStruct((b, t, M_HEADS * dh), BF16),
        scratch_shapes=[pltpu.VMEM((hps * N_DIR, dh, dh), F32),
                        pltpu.VMEM((hps * N_DIR, 1, dh), F32),
                        pltpu.VMEM((t, w), F32)],
        compiler_params=_cparams(("arbitrary", "arbitrary"), 48),
        name="mlstm",
    )(decay, y, y, y, y, p, a, yc, yc, pc, sel, head_norm_w)


def _window_bounds(pos, w, length):
    return np.clip(pos - w // 2, 0, length), np.clip(pos + w - w // 2, 0, length)


def _col_pool_matrices(tile):
    pos = np.arange(GRID_W)
    mats = []
    for w in POOL_WINDOWS:
        lo, hi = _window_bounds(pos, w, GRID_W)
        band = ((pos[None, :] >= lo[:, None]) & (pos[None, :] < hi[:, None])).astype(np.float32)
        mats.append(np.kron(np.eye(tile // GRID_W, dtype=np.float32), band))
    return jnp.asarray(np.stack(mats), dtype=BF16)


def _pool_kernel(p_ref, z_ref, mc_ref, pw_ref, ps_ref, out_ref, pad_ref, scale_ref, *, tile, rows):
    t = p_ref.shape[1]
    gd = LANES
    pad = (max(POOL_WINDOWS) // 2) * GRID_W
    n_tiles = t // tile

    @pl.when(pl.program_id(0) == 0)
    def _():
        pad_ref[...] = jnp.zeros(pad_ref.shape, F32)
        tok = lax.broadcasted_iota(jnp.int32, (t, gd), 0)
        r = tok // GRID_W
        c = tok % GRID_W
        for g, w in enumerate(POOL_WINDOWS):
            cnt_r = jnp.minimum(r + (w - w // 2), rows) - jnp.maximum(r - w // 2, 0)
            cnt_c = jnp.minimum(c + (w - w // 2), GRID_W) - jnp.maximum(c - w // 2, 0)
            scale_ref[g] = 1.0 / (cnt_r.astype(F32) * cnt_c.astype(F32))

    def col_sum(g, start):
        cols = pl.ds(g * gd, gd)
        pad_ref[g % 2, pl.ds(pad + start, tile), :] = jnp.dot(
            mc_ref[g], p_ref[0, pl.ds(start, tile), cols], preferred_element_type=F32)

    def row_sum(g, start):
        w = POOL_WINDOWS[g]
        cols = pl.ds(g * gd, gd)
        buf = pad_ref.at[g % 2]
        acc = buf[pl.ds(pad + start - (w // 2) * GRID_W, tile), :]
        for j in range(1 - w // 2, w - w // 2):
            acc = acc + buf[pl.ds(pad + start + j * GRID_W, tile), :]
        mixed = acc * scale_ref[g, pl.ds(start, tile), :] - p_ref[0, pl.ds(start, tile), cols].astype(F32)
        mm = jnp.dot(mixed.astype(BF16), pw_ref[g], preferred_element_type=F32)
        zz = z_ref[0, pl.ds(start, tile), cols].astype(F32)
        out_ref[0, pl.ds(start, tile), cols] = (mm * ps_ref[:, cols] * zz).astype(out_ref.dtype)

    for g in range(-1, P_GROUPS):
        def body(i, carry, g=g):
            start = pl.multiple_of(i * tile, tile)
            if g + 1 < P_GROUPS:
                col_sum(g + 1, start)
            if g >= 0:
                row_sum(g, start)
            return carry

        lax.fori_loop(0, n_tiles, body, 0, unroll=POOL_UNROLL)


def _pool(y, pool_w, pool_scale, p_col0):
    b, t, _ = y.shape
    pw = P_GROUPS * LANES
    tile = MXU_DIM
    pad = (max(POOL_WINDOWS) // 2) * GRID_W
    return pl.pallas_call(
        functools.partial(_pool_kernel, tile=tile, rows=t // GRID_W),
        grid=(b,),
        in_specs=[pl.BlockSpec((1, t, pw), lambda i: (i, 0, p_col0 // pw)),
                  pl.BlockSpec((1, t, pw), lambda i: (i, 0, p_col0 // pw + 1)),
                  pl.BlockSpec((P_GROUPS, tile, tile), lambda i: (0, 0, 0)),
                  pl.BlockSpec((P_GROUPS, LANES, LANES), lambda i: (0, 0, 0)),
                  pl.BlockSpec((1, pw), lambda i: (0, 0))],
        out_specs=pl.BlockSpec((1, t, pw), lambda i: (i, 0, 0)),
        out_shape=jax.ShapeDtypeStruct((b, t, pw), BF16),
        scratch_shapes=[pltpu.VMEM((2, t + 2 * pad, LANES), F32),
                        pltpu.VMEM((P_GROUPS, t, LANES), F32)],
        compiler_params=_cparams(("arbitrary",), 48),
        name="pool",
    )(y, y, _col_pool_matrices(tile), pool_w, pool_scale)


def _merge_kernel(x_ref, ym_ref, yp_ref, gm_ref, gp_ref, gate_ref, wm_ref, wp_ref, wo_ref, fw_ref, o_ref):
    slab = x_ref.shape[0] // MERGE_SLABS
    for r in range(MERGE_SLABS):
        rows = slice(r * slab, (r + 1) * slab)
        y_m = jnp.dot(ym_ref[rows, :], wm_ref[...], preferred_element_type=F32)
        y_p = jnp.dot(yp_ref[rows, :], wp_ref[...], preferred_element_type=F32)
        merged = (_sigmoid_t(gm_ref[rows, :].astype(F32)) * y_m
                  + _sigmoid_t(gp_ref[rows, :].astype(F32)) * y_p)
        upd = jnp.dot(merged.astype(BF16), wo_ref[...], preferred_element_type=F32)
        xo = x_ref[rows, :] + gate_ref[0] * upd
        o_ref[rows, :] = xo * lax.rsqrt(jnp.mean(xo * xo, axis=-1, keepdims=True) + EPS) * fw_ref[...]


def _merge(x2, ym, yp, y, mod, wm, wp, wo, final_w, rows_per_mod, g_col0):
    n, d = x2.shape
    tm = 1024
    per = rows_per_mod // tm
    pw = yp.shape[1]
    full = lambda shape: pl.BlockSpec(shape, lambda i: (0,) * len(shape))
    return pl.pallas_call(
        _merge_kernel,
        grid=(n // tm,),
        in_specs=[pl.BlockSpec((tm, d), lambda i: (i, 0)),
                  pl.BlockSpec((tm, d), lambda i: (i, 0)),
                  pl.BlockSpec((tm, pw), lambda i: (i, 0)),
                  pl.BlockSpec((tm, d), lambda i: (i, g_col0 // d)),
                  pl.BlockSpec((tm, d), lambda i: (i, g_col0 // d + 1)),
                  pl.BlockSpec((1, 1, d), lambda i: (i // per, 0, 2)),
                  full((d, d)), full((pw, d)), full((d, d)), full((1, d))],
        out_specs=pl.BlockSpec((tm, d), lambda i: (i, 0)),
        out_shape=jax.ShapeDtypeStruct((n, d), F32),
        compiler_params=_cparams(("arbitrary",), 48),
        name="merge",
    )(x2, ym, yp, y, y, mod, wm, wp, wo, final_w)


def kernel(x, c, ctx, c_ctx, norm_w, ada_w, ada_b, in_w, gate_b, head_norm_w, pool_w, pool_scale,
           branch_m_w, branch_p_w, out_w, final_norm_w):
    b, t, d = x.shape
    tc = ctx.shape[1]
    depth = norm_w.shape[0]
    assert depth == 1, "context stream update between layers is not implemented"
    mw = M_HEADS * MXU_DIM
    assert d == mw
    g0 = 5 * mw

    mod = _adaln(c, c_ctx, ada_w[0], ada_b[0])

    wt_f32 = jnp.transpose(in_w[0])
    nw = norm_w[0].reshape(1, d)

    r_q, r_k, r_v, r_o, r_z = (i * mw for i in range(5))
    r_p = g0 + N_GATE_COLS
    r_gm, r_gp = r_p + mw, r_p + 2 * mw
    pairs = ((r_q, r_k, "id"), (r_v, r_p, "half_silu"), (r_o, r_z, "gate"), (r_gm, r_gp, "id"))
    col = {name: i * mw for i, name in enumerate(("q", "k", "v", "p", "o", "oz", "g_m", "g_p"))}

    wt, yc, gates_tc = _ctx_proj(ctx.reshape(b * tc, d), mod, b, nw, wt_f32, g0, r_k, r_v)
    y, gates_t = _in_proj(x.reshape(b * t, d), mod, nw, wt, g0, t, "in_proj", pairs)
    y = y.reshape(b, t, -1)
    yc = yc.reshape(b, tc, -1)

    pc, p, a, decay = _gates(gates_t, gates_tc, gate_b[0], b, CHUNK)
    ym = _mlstm(y, yc, p, a, pc, decay, head_norm_w[0].reshape(1, mw), CHUNK, col)
    yp = _pool(y, pool_w[0].astype(BF16), pool_scale[0].reshape(1, -1), col["p"])

    out = _merge(x.reshape(b * t, d), ym.reshape(b * t, mw), yp.reshape(b * t, -1), y.reshape(b * t, -1),
                 mod, branch_m_w[0].astype(BF16), branch_p_w[0].astype(BF16),
                 out_w[0].astype(BF16), final_norm_w.reshape(1, d), t, col["g_m"])
    return out.reshape(b, t, d)
```

```python
import functools

import numpy as np
import jax
import jax.numpy as jnp
from jax import lax
from jax.experimental import pallas as pl
from jax.experimental.pallas import tpu as pltpu

F32 = jnp.float32
BF16 = jnp.bfloat16

EPS = 1e-6
LOG2E = 1.4426950408889634
M_INIT = -1e30
NEG_BIG = -1e30
M_HEADS = 4
N_DIR = 2
N_GATE_COLS = N_DIR * 2 * M_HEADS
P_GROUPS = 4
POOL_WINDOWS = (2, 4, 8, 16)
GRID_W = 64
LANES = 128
MXU_DIM = 256
CHUNK = 256
POOL_UNROLL = 4
MERGE_SLABS = 2
HEADS_PER_STEP = 1
MIB = 1024 * 1024


def _cparams(sem, vmem_mib):
    return pltpu.CompilerParams(dimension_semantics=sem, vmem_limit_bytes=vmem_mib * MIB)


def _sigmoid(x):
    return 1.0 / (1.0 + jnp.exp(-x))


def _silu(x):
    return x * _sigmoid(x)


def _log_sigmoid(x):
    return jnp.minimum(x, 0.0) - jnp.log(1.0 + jnp.exp(-jnp.abs(x)))


def _dot_nt(a, b):
    return lax.dot_general(a, b, (((1,), (1,)), ((), ())), preferred_element_type=F32)


def _adaln_kernel(c_ref, cc_ref, w_ref, b_ref, o_ref):
    b = c_ref.shape[0]
    rows = jnp.concatenate([c_ref[...], jnp.broadcast_to(cc_ref[...], (o_ref.shape[0] - b, c_ref.shape[1]))], axis=0)
    o_ref[:, 0, :] = jnp.dot(_silu(rows), w_ref[...], preferred_element_type=F32) + b_ref[...]


def _adaln(c, c_ctx, ada_w, ada_b):
    b, d = c.shape
    rows = 8 * (b // 8 + 1)
    n = ada_w.shape[1]
    tn = d
    return pl.pallas_call(
        _adaln_kernel,
        grid=(n // tn,),
        in_specs=[pl.BlockSpec((b, d), lambda j: (0, 0)),
                  pl.BlockSpec((1, d), lambda j: (0, 0)),
                  pl.BlockSpec((d, tn), lambda j: (0, j)),
                  pl.BlockSpec((1, tn), lambda j: (0, j))],
        out_specs=pl.BlockSpec((rows, 1, tn), lambda j: (0, 0, j)),
        out_shape=jax.ShapeDtypeStruct((rows, 1, n), F32),
        compiler_params=_cparams(("arbitrary",), 32),
        name="adaln",
    )(c, c_ctx.reshape(1, d), ada_w, ada_b.reshape(1, n))


def _sigmoid_t(x):
    return 0.5 * jnp.tanh(0.5 * x) + 0.5


IN_TILE = 1024
GATE_SUB = 256


def _inproj_kernel(x_ref, sc_ref, sh_ref, nw_ref, wa_ref, wb_ref, wg_ref, y_ref, g_ref, xn_ref, *, acts):
    j = pl.program_id(1)

    @pl.when(j == 0)
    def _():
        x = x_ref[...]
        ms = jnp.mean(x * x, axis=-1, keepdims=True)
        xn = x * lax.rsqrt(ms + EPS) * (nw_ref[...] * (1.0 + sc_ref[0])) + sh_ref[0]
        xn_ref[...] = xn.astype(BF16)
        wg = wg_ref[...]
        wg = jnp.concatenate([wg, jnp.zeros((LANES - wg.shape[0], wg.shape[1]), wg.dtype)], axis=0)
        g_ref[...] = _dot_nt(xn_ref[...], wg).T[:N_GATE_COLS]

    def pair(act):
        tn = IN_TILE
        xn = xn_ref[...]
        if act == "gate":
            for c in range(0, tn, GATE_SUB):
                a = _dot_nt(xn, wa_ref[c:c + GATE_SUB, :])
                b = _dot_nt(xn, wb_ref[c:c + GATE_SUB, :])
                y_ref[:, c:c + GATE_SUB] = a.astype(BF16)
                y_ref[:, tn + c:tn + c + GATE_SUB] = (_sigmoid_t(a) * (b * _sigmoid_t(b))).astype(BF16)
            return
        y_ref[:, :tn] = _dot_nt(xn, wa_ref[...]).astype(BF16)
        b = _dot_nt(xn, wb_ref[...])
        if act == "half_silu":
            half = tn // 2
            b = jnp.concatenate([b[:, :half], b[:, half:] * _sigmoid_t(b[:, half:])], axis=1)
        y_ref[:, tn:] = b.astype(BF16)

    for act in sorted(set(acts)):
        hit = functools.reduce(jnp.logical_or, [j == s for s, a in enumerate(acts) if a == act])
        pl.when(hit)(functools.partial(pair, act))


def _in_proj(x2, mod, norm_w, w, gate_row0, rows_per_mod, name, pairs):
    n, d = x2.shape
    tm = min(2048, rows_per_mod)
    tn = IN_TILE
    per = rows_per_mod // tm
    acts = tuple(p[2] for p in pairs)

    def w_rows(which):
        def index_map(i, j):
            r = sum(jnp.where(j == s, p[which], 0) for s, p in enumerate(pairs))
            return (pl.multiple_of(r, N_GATE_COLS), 0)
        return index_map

    return pl.pallas_call(
        functools.partial(_inproj_kernel, acts=acts),
        grid=(n // tm, len(pairs)),
        in_specs=[pl.BlockSpec((tm, d), lambda i, j: (i, 0)),
                  pl.BlockSpec((1, 1, d), lambda i, j: (i // per, 0, 1)),
                  pl.BlockSpec((1, 1, d), lambda i, j: (i // per, 0, 0)),
                  pl.BlockSpec((1, d), lambda i, j: (0, 0)),
                  pl.BlockSpec((pl.Element(tn), pl.Element(d)), w_rows(0)),
                  pl.BlockSpec((pl.Element(tn), pl.Element(d)), w_rows(1)),
                  pl.BlockSpec((pl.Element(N_GATE_COLS), pl.Element(d)), lambda i, j: (gate_row0, 0))],
        out_specs=[pl.BlockSpec((tm, 2 * tn), lambda i, j: (i, j)),
                   pl.BlockSpec((N_GATE_COLS, tm), lambda i, j: (0, i))],
        out_shape=[jax.ShapeDtypeStruct((n, 2 * tn * len(pairs)), BF16),
                   jax.ShapeDtypeStruct((N_GATE_COLS, n), F32)],
        scratch_shapes=[pltpu.VMEM((tm, d), BF16)],
        compiler_params=_cparams(("arbitrary", "arbitrary"), 56),
        name=name,
    )(x2, mod, mod, norm_w, w, w, w)


def _ctx_proj_kernel(x_ref, sc_ref, sh_ref, nw_ref, wsrc_ref, wk_ref, wv_ref, wg_ref,
                     wdst_ref, y_ref, g_ref, xn_ref, *, n_cast, steps, k_rows):
    i, j = pl.program_id(0), pl.program_id(1)
    slab = i * steps + j
    kscale = MXU_DIM ** -0.5

    @pl.when(slab < n_cast)
    def _():
        rows = wsrc_ref.shape[0]
        r = slab * rows + lax.broadcasted_iota(jnp.int32, (rows, 1), 0)
        scale = jnp.where((r >= k_rows[0]) & (r < k_rows[1]), kscale, 1.0)
        wdst_ref[...] = (wsrc_ref[...] * scale).astype(BF16)

    @pl.when(j == 0)
    def _():
        x = x_ref[...]
        ms = jnp.mean(x * x, axis=-1, keepdims=True)
        xn = x * lax.rsqrt(ms + EPS) * (nw_ref[...] * (1.0 + sc_ref[0])) + sh_ref[0]
        xn_ref[...] = xn.astype(BF16)
        wg = wg_ref[...].astype(BF16)
        wg = jnp.concatenate([wg, jnp.zeros((LANES - wg.shape[0], wg.shape[1]), wg.dtype)], axis=0)
        g_ref[...] = _dot_nt(xn_ref[...], wg).T[:N_GATE_COLS]

    @pl.when(j == 1)
    def _():
        y_ref[:, :IN_TILE] = _dot_nt(xn_ref[...], (wk_ref[...] * kscale).astype(BF16)).astype(BF16)

    @pl.when(j == 2)
    def _():
        y_ref[:, IN_TILE:] = _dot_nt(xn_ref[...], wv_ref[...].astype(BF16)).astype(BF16)


def _ctx_proj(xc, mod, mod_row, norm_w, wt, gate_row0, r_k, r_v):
    n, d = xc.shape
    tm, tn = 1024, IN_TILE
    n_cast = 9
    slab = wt.shape[0] // n_cast
    assert slab * n_cast == wt.shape[0] and slab % 16 == 0 and n % tm == 0
    steps = -(-n_cast // (n // tm))
    assert steps >= 3
    cast_idx = lambda i, j: (jnp.minimum(i * steps + j, n_cast - 1), 0)
    elem = lambda rows, r0: pl.BlockSpec((pl.Element(rows), pl.Element(d)), lambda i, j: (r0, 0))
    return pl.pallas_call(
        functools.partial(_ctx_proj_kernel, n_cast=n_cast, steps=steps, k_rows=(r_k, r_k + tn)),
        grid=(n // tm, steps),
        in_specs=[pl.BlockSpec((tm, d), lambda i, j: (i, 0)),
                  pl.BlockSpec((1, 1, d), lambda i, j: (mod_row, 0, 1)),
                  pl.BlockSpec((1, 1, d), lambda i, j: (mod_row, 0, 0)),
                  pl.BlockSpec((1, d), lambda i, j: (0, 0)),
                  pl.BlockSpec((slab, d), cast_idx),
                  elem(tn, r_k), elem(tn, r_v), elem(N_GATE_COLS, gate_row0)],
        out_specs=[pl.BlockSpec((slab, d), cast_idx),
                   pl.BlockSpec((tm, 2 * tn), lambda i, j: (i, 0)),
                   pl.BlockSpec((N_GATE_COLS, tm), lambda i, j: (0, i))],
        out_shape=[jax.ShapeDtypeStruct(wt.shape, BF16),
                   jax.ShapeDtypeStruct((n, 2 * tn), BF16),
                   jax.ShapeDtypeStruct((N_GATE_COLS, n), F32)],
        scratch_shapes=[pltpu.VMEM((tm, d), BF16)],
        compiler_params=_cparams(("arbitrary", "arbitrary"), 56),
        name="ctx_proj",
    )(xc, mod, mod, norm_w, wt, wt, wt, wt)


N_DH = N_DIR * M_HEADS
TERM_PIECES = (3, 2, 2, 3)


def _split(x, pieces):
    out = []
    for _ in range(pieces):
        p = x.astype(BF16)
        out.append(p)
        x = x - p.astype(F32)
    return out


def _chunk_scan(x, pos, chunk, op, reverse):
    t = x.shape[1]
    k = 1
    while k < chunk:
        if reverse:
            x = jnp.where(pos < chunk - k, op(x, pltpu.roll(x, t - k, 1)), x)
        else:
            x = jnp.where(pos >= k, op(x, pltpu.roll(x, k, 1)), x)
        k *= 2
    return x


def _gate_terms(li, pre_f, m0, chunk):
    rows, t = li.shape
    nc = t // chunk
    pos = lax.broadcasted_iota(jnp.int32, (rows, t), 1) % chunk
    rev = lax.broadcasted_iota(jnp.int32, (rows, t), 0) >= M_HEADS
    rev1 = lax.broadcasted_iota(jnp.int32, (rows, 1), 0) >= M_HEADS
    lf = _log_sigmoid(pre_f)
    b = jnp.where(rev, _chunk_scan(lf, pos, chunk, jnp.add, True), _chunk_scan(lf, pos, chunk, jnp.add, False))
    a = li - b
    cmax = jnp.where(rev, _chunk_scan(a, pos, chunk, jnp.maximum, True),
                     _chunk_scan(a, pos, chunk, jnp.maximum, False))

    def at_scan_end(x, c):
        lo = c * chunk
        return jnp.where(rev1, x[:, lo:lo + 1], x[:, lo + chunk - 1:lo + chunk])

    b_end = [at_scan_end(b, c) for c in range(nc)]
    a_max = [at_scan_end(cmax, c) for c in range(nc)]
    m_f, m_b = m0, m0
    m_in_f, m_in_b = [None] * nc, [None] * nc
    for i in range(nc):
        j = nc - 1 - i
        m_in_f[i] = m_f
        m_f = b_end[i] + jnp.maximum(m_f, a_max[i])
        m_in_b[j] = m_b
        m_b = b_end[j] + jnp.maximum(m_b, a_max[j])
    m_in = [jnp.where(rev1, m_in_b[c], m_in_f[c]) for c in range(nc)]
    per_chunk = lambda vals: jnp.concatenate([jnp.broadcast_to(v, (rows, chunk)) for v in vals], axis=1)
    m_in_t = per_chunk(m_in)
    g = jnp.maximum(m_in_t, cmax)
    g_end = [jnp.maximum(m_in[c], a_max[c]) for c in range(nc)]
    pieces = []
    terms = (-g * LOG2E, jnp.exp(m_in_t - g), jnp.exp(a - per_chunk(g_end)), -(b + g) * LOG2E)
    for term, n in zip(terms, TERM_PIECES):
        pieces += _split(term, n)
    pieces.append(jnp.zeros((LANES - rows * sum(TERM_PIECES), t), BF16))
    decay = jnp.concatenate([jnp.exp(m_in[c] - g_end[c]) for c in range(nc)], axis=1)
    return jnp.concatenate(pieces, axis=0), a * LOG2E, decay, jnp.where(rev1, m_b, m_f)


def _gates_kernel(gc_ref, g_ref, b_ref, rc_ref, dc_ref, r_ref, a_ref, d_ref, *, chunk):
    def split(ref):
        pre = ref[...] + b_ref[...]
        pick = lambda gate: jnp.concatenate(
            [pre[(2 * d + gate) * M_HEADS:(2 * d + gate + 1) * M_HEADS] for d in range(N_DIR)], axis=0)
        return pick(0), pick(1)

    m0 = jnp.full((N_DH, 1), M_INIT, F32)
    rc_ref[0], _, dc_ref[0], m_ctx = _gate_terms(*split(gc_ref), m0, chunk)
    r_ref[0], a, d_ref[0], _ = _gate_terms(*split(g_ref), m_ctx, chunk)
    for c in range(a_ref.shape[1]):
        a_ref[0, c] = a[:, c * chunk:(c + 1) * chunk]


def _gates(gates_t, gates_tc, gate_b, b, chunk):
    t = gates_t.shape[1] // b
    tc = gates_tc.shape[1] // b
    nc, ncc = t // chunk, tc // chunk
    spec = lambda r, n: pl.BlockSpec((1, r, n), lambda i: (i, 0, 0))
    seq = lambda n: pl.BlockSpec((N_GATE_COLS, n), lambda i: (0, i))
    rc, dc, r, a, dec = pl.pallas_call(
        functools.partial(_gates_kernel, chunk=chunk),
        grid=(b,),
        in_specs=[seq(tc), seq(t), pl.BlockSpec((N_GATE_COLS, 1), lambda i: (0, 0))],
        out_specs=[spec(LANES, tc), spec(N_DH, ncc), spec(LANES, t),
                   pl.BlockSpec((1, nc, N_DH, chunk), lambda i: (i, 0, 0, 0)), spec(N_DH, nc)],
        out_shape=[jax.ShapeDtypeStruct((b, LANES, tc), BF16), jax.ShapeDtypeStruct((b, N_DH, ncc), F32),
                   jax.ShapeDtypeStruct((b, LANES, t), BF16), jax.ShapeDtypeStruct((b, nc, N_DH, chunk), F32),
                   jax.ShapeDtypeStruct((b, N_DH, nc), F32)],
        compiler_params=_cparams(("arbitrary",), 32),
        name="gates",
    )(gates_tc, gates_t, gate_b.reshape(N_GATE_COLS, 1))
    decay = jnp.concatenate([dc, dec], axis=2).reshape(b * N_DH, ncc + nc)
    return rc, r, a, decay


def _twice(x):
    return jnp.concatenate([x, x], axis=1)


def _mlstm_kernel(dec_ref, q_ref, k_ref, v_ref, oz_ref, p_ref, a_ref, kc_ref, vc_ref, pc_ref,
                  sel_ref, hnw_ref, out_ref, ct_ref, n_ref, hs_ref, *, chunk, dh):
    t = q_ref.shape[1]
    nc = t // chunk
    ncc = kc_ref.shape[1] // chunk
    hps = q_ref.shape[2] // dh
    bi, h0 = pl.program_id(0), pl.program_id(1) * hps
    t_idx = lax.broadcasted_iota(jnp.int32, (chunk, chunk), 0)
    s_idx = lax.broadcasted_iota(jnp.int32, (chunk, chunk), 1)

    def head_cols(hh):
        return slice(hh * dh, (hh + 1) * dh)

    def bcast_terms(pieces, sel):
        return lax.dot_general(pieces, sel, (((0,), (0,)), ((), ())), preferred_element_type=F32)

    def row_terms(c, hh, d):
        w = bcast_terms(p_ref[0, :, c * chunk:(c + 1) * chunk], sel_ref[hh, d])
        return [w[:, i * LANES:(i + 1) * LANES] for i in range(len(TERM_PIECES))]

    def update(k, v, e_b, decay, u):
        ke = k.astype(F32) * _twice(e_b)
        upd = lax.dot_general(ke.astype(BF16), v, (((0,), (0,)), ((), ())), preferred_element_type=F32)
        n_new = jnp.sum(ke, axis=0, keepdims=True)
        ct_ref[u] = upd if decay is None else decay * ct_ref[u] + upd
        n_ref[u] = n_new if decay is None else decay * n_ref[u] + n_new

    def decay_of(c, hh, d):
        return dec_ref[bi * N_DH + d * M_HEADS + h0 + hh, c]

    for hh in range(hps):
        for d in range(N_DIR):
            for i in range(ncc):
                c = i if d == 0 else ncc - 1 - i
                rows = slice(c * chunk, (c + 1) * chunk)
                e_b = bcast_terms(pc_ref[0, :, rows], sel_ref[hh, d, :, 2 * LANES:3 * LANES])
                update(kc_ref[0, rows, head_cols(hh)], vc_ref[0, rows, head_cols(hh)], e_b,
                       None if i == 0 else decay_of(c, hh, d), hh * N_DIR + d)

    def run_chunk(c, hh, d):
        u = hh * N_DIR + d
        rows = slice(c * chunk, (c + 1) * chunk)
        cols = head_cols(hh)
        q, k, v = q_ref[0, rows, cols], k_ref[0, rows, cols], v_ref[0, rows, cols]
        neg_g_b, w_inter_b, e_b, neg_mt_b = row_terms(c, hh, d)
        a_row = a_ref[0, c, pl.ds(d * M_HEADS + h0 + hh, 1), :]
        mask = (s_idx <= t_idx) if d == 0 else (s_idx >= t_idx)
        w_intra = jnp.exp2(jnp.where(mask, a_row + _twice(neg_g_b), NEG_BIG))
        s = _dot_nt(q, k) * w_intra
        num = (_twice(w_inter_b) * jnp.dot(q, ct_ref[u].astype(BF16), preferred_element_type=F32)
               + jnp.dot(s.astype(BF16), v, preferred_element_type=F32))
        den_b = (w_inter_b * jnp.sum(q.astype(F32) * n_ref[u], axis=-1, keepdims=True)
                 + jnp.sum(s, axis=-1, keepdims=True))
        h = num * _twice(1.0 / jnp.maximum(jnp.abs(den_b), jnp.exp2(neg_mt_b)))
        update(k, v, e_b, decay_of(ncc + c, hh, d), u)
        return rows, cols, h

    def finalize(rows, cols, h):
        hn = h * lax.rsqrt(jnp.mean(h * h, axis=-1, keepdims=True) + EPS) * hnw_ref[:, cols]
        out_ref[0, rows, cols] = (hn * oz_ref[0, rows, cols].astype(F32)).astype(out_ref.dtype)

    for i in range(nc):
        for hh in range(hps):
            for d in range(N_DIR):
                rows, cols, h = run_chunk(i if d == 0 else nc - 1 - i, hh, d)
                if i < nc // 2:
                    hs_ref[rows, cols] = h
                else:
                    finalize(rows, cols, hs_ref[rows, cols] + h)


def _term_selectors():
    sel = np.zeros((M_HEADS, N_DIR, LANES, len(TERM_PIECES) * LANES), np.float32)
    first = np.concatenate([[0], np.cumsum(TERM_PIECES)])
    for h in range(M_HEADS):
        for d in range(N_DIR):
            for q in range(len(TERM_PIECES)):
                for j in range(first[q], first[q + 1]):
                    sel[h, d, j * N_DH + d * M_HEADS + h, q * LANES:(q + 1) * LANES] = 1.0
    return jnp.asarray(sel, BF16)


def _mlstm(y, yc, p, a, pc, decay, head_norm_w, chunk, col):
    b, t, _ = y.shape
    tc = yc.shape[1]
    dh = MXU_DIM
    hps = HEADS_PER_STEP
    w = hps * dh
    sel = _term_selectors()
    assert t % (2 * chunk) == 0 and tc % chunk == 0 and M_HEADS % hps == 0
    blk = lambda name: pl.BlockSpec((1, t, w), lambda i, h, c=col[name] // w: (i, 0, c + h))
    blkc = lambda col0: pl.BlockSpec((1, tc, w), lambda i, h, c=col0 // w: (i, 0, c + h))
    return pl.pallas_call(
        functools.partial(_mlstm_kernel, chunk=chunk, dh=dh),
        grid=(b, M_HEADS // hps),
        in_specs=[pl.BlockSpec(memory_space=pltpu.SMEM),
                  blk("q"), blk("k"), blk("v"), blk("oz"),
                  pl.BlockSpec((1, LANES, t), lambda i, h: (i, 0, 0)),
                  pl.BlockSpec((1, t // chunk, N_DH, chunk), lambda i, h: (i, 0, 0, 0)),
                  blkc(0), blkc(M_HEADS * dh),
                  pl.BlockSpec((1, LANES, tc), lambda i, h: (i, 0, 0)),
                  pl.BlockSpec((hps, N_DIR) + sel.shape[2:], lambda i, h: (h, 0, 0, 0)),
                  pl.BlockSpec((1, w), lambda i, h: (0, h))],
        out_specs=pl.BlockSpec((1, t, w), lambda i, h: (i, 0, h)),
        out_shape=jax.ShapeDtypeStruct((b, t, M_HEADS * dh), BF16),
        scratch_shapes=[pltpu.VMEM((hps * N_DIR, dh, dh), F32),
                        pltpu.VMEM((hps * N_DIR, 1, dh), F32),
                        pltpu.VMEM((t, w), F32)],
        compiler_params=_cparams(("arbitrary", "arbitrary"), 48),
        name="mlstm",
    )(decay, y, y, y, y, p, a, yc, yc, pc, sel, head_norm_w)


def _window_bounds(pos, w, length):
    return np.clip(pos - w // 2, 0, length), np.clip(pos + w - w // 2, 0, length)


def _col_pool_matrices(tile):
    pos = np.arange(GRID_W)
    mats = []
    for w in POOL_WINDOWS:
        lo, hi = _window_bounds(pos, w, GRID_W)
        band = ((pos[None, :] >= lo[:, None]) & (pos[None, :] < hi[:, None])).astype(np.float32)
        mats.append(np.kron(np.eye(tile // GRID_W, dtype=np.float32), band))
    return jnp.asarray(np.stack(mats), dtype=BF16)


def _pool_kernel(p_ref, z_ref, mc_ref, pw_ref, ps_ref, out_ref, pad_ref, scale_ref, *, tile, rows):
    t = p_ref.shape[1]
    gd = LANES
    pad = (max(POOL_WINDOWS) // 2) * GRID_W
    n_tiles = t // tile

    @pl.when(pl.program_id(0) == 0)
    def _():
        pad_ref[...] = jnp.zeros(pad_ref.shape, F32)
        tok = lax.broadcasted_iota(jnp.int32, (t, gd), 0)
        r = tok // GRID_W
        c = tok % GRID_W
        for g, w in enumerate(POOL_WINDOWS):
            cnt_r = jnp.minimum(r + (w - w // 2), rows) - jnp.maximum(r - w // 2, 0)
            cnt_c = jnp.minimum(c + (w - w // 2), GRID_W) - jnp.maximum(c - w // 2, 0)
            scale_ref[g] = 1.0 / (cnt_r.astype(F32) * cnt_c.astype(F32))

    def col_sum(g, start):
        cols = pl.ds(g * gd, gd)
        pad_ref[g % 2, pl.ds(pad + start, tile), :] = jnp.dot(
            mc_ref[g], p_ref[0, pl.ds(start, tile), cols], preferred_element_type=F32)

    def row_sum(g, start):
        w = POOL_WINDOWS[g]
        cols = pl.ds(g * gd, gd)
        buf = pad_ref.at[g % 2]
        acc = buf[pl.ds(pad + start - (w // 2) * GRID_W, tile), :]
        for j in range(1 - w // 2, w - w // 2):
            acc = acc + buf[pl.ds(pad + start + j * GRID_W, tile), :]
        mixed = acc * scale_ref[g, pl.ds(start, tile), :] - p_ref[0, pl.ds(start, tile), cols].astype(F32)
        mm = jnp.dot(mixed.astype(BF16), pw_ref[g], preferred_element_type=F32)
        zz = z_ref[0, pl.ds(start, tile), cols].astype(F32)
        out_ref[0, pl.ds(start, tile), cols] = (mm * ps_ref[:, cols] * zz).astype(out_ref.dtype)

    for g in range(-1, P_GROUPS):
        def body(i, carry, g=g):
            start = pl.multiple_of(i * tile, tile)
            if g + 1 < P_GROUPS:
                col_sum(g + 1, start)
            if g >= 0:
                row_sum(g, start)
            return carry

        lax.fori_loop(0, n_tiles, body, 0, unroll=POOL_UNROLL)


def _pool(y, pool_w, pool_scale, p_col0):
    b, t, _ = y.shape
    pw = P_GROUPS * LANES
    tile = MXU_DIM
    pad = (max(POOL_WINDOWS) // 2) * GRID_W
    return pl.pallas_call(
        functools.partial(_pool_kernel, tile=tile, rows=t // GRID_W),
        grid=(b,),
        in_specs=[pl.BlockSpec((1, t, pw), lambda i: (i, 0, p_col0 // pw)),
                  pl.BlockSpec((1, t, pw), lambda i: (i, 0, p_col0 // pw + 1)),
                  pl.BlockSpec((P_GROUPS, tile, tile), lambda i: (0, 0, 0)),
                  pl.BlockSpec((P_GROUPS, LANES, LANES), lambda i: (0, 0, 0)),
                  pl.BlockSpec((1, pw), lambda i: (0, 0))],
        out_specs=pl.BlockSpec((1, t, pw), lambda i: (i, 0, 0)),
        out_shape=jax.ShapeDtypeStruct((b, t, pw), BF16),
        scratch_shapes=[pltpu.VMEM((2, t + 2 * pad, LANES), F32),
                        pltpu.VMEM((P_GROUPS, t, LANES), F32)],
        compiler_params=_cparams(("arbitrary",), 48),
        name="pool",
    )(y, y, _col_pool_matrices(tile), pool_w, pool_scale)


def _merge_kernel(x_ref, ym_ref, yp_ref, gm_ref, gp_ref, gate_ref, wm_ref, wp_ref, wo_ref, fw_ref, o_ref):
    slab = x_ref.shape[0] // MERGE_SLABS
    for r in range(MERGE_SLABS):
        rows = slice(r * slab, (r + 1) * slab)
        y_m = jnp.dot(ym_ref[rows, :], wm_ref[...], preferred_element_type=F32)
        y_p = jnp.dot(yp_ref[rows, :], wp_ref[...], preferred_element_type=F32)
        merged = (_sigmoid_t(gm_ref[rows, :].astype(F32)) * y_m
                  + _sigmoid_t(gp_ref[rows, :].astype(F32)) * y_p)
        upd = jnp.dot(merged.astype(BF16), wo_ref[...], preferred_element_type=F32)
        xo = x_ref[rows, :] + gate_ref[0] * upd
        o_ref[rows, :] = xo * lax.rsqrt(jnp.mean(xo * xo, axis=-1, keepdims=True) + EPS) * fw_ref[...]


def _merge(x2, ym, yp, y, mod, wm, wp, wo, final_w, rows_per_mod, g_col0):
    n, d = x2.shape
    tm = 1024
    per = rows_per_mod // tm
    pw = yp.shape[1]
    full = lambda shape: pl.BlockSpec(shape, lambda i: (0,) * len(shape))
    return pl.pallas_call(
        _merge_kernel,
        grid=(n // tm,),
        in_specs=[pl.BlockSpec((tm, d), lambda i: (i, 0)),
                  pl.BlockSpec((tm, d), lambda i: (i, 0)),
                  pl.BlockSpec((tm, pw), lambda i: (i, 0)),
                  pl.BlockSpec((tm, d), lambda i: (i, g_col0 // d)),
                  pl.BlockSpec((tm, d), lambda i: (i, g_col0 // d + 1)),
                  pl.BlockSpec((1, 1, d), lambda i: (i // per, 0, 2)),
                  full((d, d)), full((pw, d)), full((d, d)), full((1, d))],
        out_specs=pl.BlockSpec((tm, d), lambda i: (i, 0)),
        out_shape=jax.ShapeDtypeStruct((n, d), F32),
        compiler_params=_cparams(("arbitrary",), 48),
        name="merge",
    )(x2, ym, yp, y, y, mod, wm, wp, wo, final_w)


def kernel(x, c, ctx, c_ctx, norm_w, ada_w, ada_b, in_w, gate_b, head_norm_w, pool_w, pool_scale,
           branch_m_w, branch_p_w, out_w, final_norm_w):
    b, t, d = x.shape
    tc = ctx.shape[1]
    depth = norm_w.shape[0]
    assert depth == 1, "context stream update between layers is not implemented"
    mw = M_HEADS * MXU_DIM
    assert d == mw
    g0 = 5 * mw

    mod = _adaln(c, c_ctx, ada_w[0], ada_b[0])

    wt_f32 = jnp.transpose(in_w[0])
    nw = norm_w[0].reshape(1, d)

    r_q, r_k, r_v, r_o, r_z = (i * mw for i in range(5))
    r_p = g0 + N_GATE_COLS
    r_gm, r_gp = r_p + mw, r_p + 2 * mw
    pairs = ((r_q, r_k, "id"), (r_v, r_p, "half_silu"), (r_o, r_z, "gate"), (r_gm, r_gp, "id"))
    col = {name: i * mw for i, name in enumerate(("q", "k", "v", "p", "o", "oz", "g_m", "g_p"))}

    wt, yc, gates_tc = _ctx_proj(ctx.reshape(b * tc, d), mod, b, nw, wt_f32, g0, r_k, r_v)
    y, gates_t = _in_proj(x.reshape(b * t, d), mod, nw, wt, g0, t, "in_proj", pairs)
    y = y.reshape(b, t, -1)
    yc = yc.reshape(b, tc, -1)

    pc, p, a, decay = _gates(gates_t, gates_tc, gate_b[0], b, CHUNK)
    ym = _mlstm(y, yc, p, a, pc, decay, head_norm_w[0].reshape(1, mw), CHUNK, col)
    yp = _pool(y, pool_w[0].astype(BF16), pool_scale[0].reshape(1, -1), col["p"])

    out = _merge(x.reshape(b * t, d), ym.reshape(b * t, mw), yp.reshape(b * t, -1), y.reshape(b * t, -1),
                 mod, branch_m_w[0].astype(BF16), branch_p_w[0].astype(BF16),
                 out_w[0].astype(BF16), final_norm_w.reshape(1, d), t, col["g_m"])
    return out.reshape(b, t, d)
```

```python
import functools

import numpy as np
import jax
import jax.numpy as jnp
from jax import lax
from jax.experimental import pallas as pl
from jax.experimental.pallas import tpu as pltpu

F32 = jnp.float32
BF16 = jnp.bfloat16

EPS = 1e-6
LOG2E = 1.4426950408889634
M_INIT = -1e30
NEG_BIG = -1e30
M_HEADS = 4
N_DIR = 2
N_GATE_COLS = N_DIR * 2 * M_HEADS
P_GROUPS = 4
POOL_WINDOWS = (2, 4, 8, 16)
GRID_W = 64
LANES = 128
MXU_DIM = 256
CHUNK = 256
POOL_UNROLL = 4
MERGE_SLABS = 2
HEADS_PER_STEP = 1
MIB = 1024 * 1024


def _cparams(sem, vmem_mib):
    return pltpu.CompilerParams(dimension_semantics=sem, vmem_limit_bytes=vmem_mib * MIB)


def _sigmoid(x):
    return 1.0 / (1.0 + jnp.exp(-x))


def _silu(x):
    return x * _sigmoid(x)


def _log_sigmoid(x):
    return jnp.minimum(x, 0.0) - jnp.log(1.0 + jnp.exp(-jnp.abs(x)))


def _dot_nt(a, b):
    return lax.dot_general(a, b, (((1,), (1,)), ((), ())), preferred_element_type=F32)


def _adaln_kernel(c_ref, cc_ref, w_ref, b_ref, o_ref):
    b = c_ref.shape[0]
    rows = jnp.concatenate([c_ref[...], jnp.broadcast_to(cc_ref[...], (o_ref.shape[0] - b, c_ref.shape[1]))], axis=0)
    o_ref[:, 0, :] = jnp.dot(_silu(rows), w_ref[...], preferred_element_type=F32) + b_ref[...]


def _adaln(c, c_ctx, ada_w, ada_b):
    b, d = c.shape
    rows = 8 * (b // 8 + 1)
    n = ada_w.shape[1]
    tn = d
    return pl.pallas_call(
        _adaln_kernel,
        grid=(n // tn,),
        in_specs=[pl.BlockSpec((b, d), lambda j: (0, 0)),
                  pl.BlockSpec((1, d), lambda j: (0, 0)),
                  pl.BlockSpec((d, tn), lambda j: (0, j)),
                  pl.BlockSpec((1, tn), lambda j: (0, j))],
        out_specs=pl.BlockSpec((rows, 1, tn), lambda j: (0, 0, j)),
        out_shape=jax.ShapeDtypeStruct((rows, 1, n), F32),
        compiler_params=_cparams(("arbitrary",), 32),
        name="adaln",
    )(c, c_ctx.reshape(1, d), ada_w, ada_b.reshape(1, n))


def _sigmoid_t(x):
    return 0.5 * jnp.tanh(0.5 * x) + 0.5


IN_TILE = 1024
GATE_SUB = 256


def _inproj_kernel(x_ref, sc_ref, sh_ref, nw_ref, wa_ref, wb_ref, wg_ref, y_ref, g_ref, xn_ref, *, acts):
    j = pl.program_id(1)

    @pl.when(j == 0)
    def _():
        x = x_ref[...]
        ms = jnp.mean(x * x, axis=-1, keepdims=True)
        xn = x * lax.rsqrt(ms + EPS) * (nw_ref[...] * (1.0 + sc_ref[0])) + sh_ref[0]
        xn_ref[...] = xn.astype(BF16)
        wg = wg_ref[...]
        wg = jnp.concatenate([wg, jnp.zeros((LANES - wg.shape[0], wg.shape[1]), wg.dtype)], axis=0)
        g_ref[...] = _dot_nt(xn_ref[...], wg).T[:N_GATE_COLS]

    def pair(act):
        tn = IN_TILE
        xn = xn_ref[...]
        if act == "gate":
            for c in range(0, tn, GATE_SUB):
                a = _dot_nt(xn, wa_ref[c:c + GATE_SUB, :])
                b = _dot_nt(xn, wb_ref[c:c + GATE_SUB, :])
                y_ref[:, c:c + GATE_SUB] = a.astype(BF16)
                y_ref[:, tn + c:tn + c + GATE_SUB] = (_sigmoid_t(a) * (b * _sigmoid_t(b))).astype(BF16)
            return
        y_ref[:, :tn] = _dot_nt(xn, wa_ref[...]).astype(BF16)
        b = _dot_nt(xn, wb_ref[...])
        if act == "half_silu":
            half = tn // 2
            b = jnp.concatenate([b[:, :half], b[:, half:] * _sigmoid_t(b[:, half:])], axis=1)
        y_ref[:, tn:] = b.astype(BF16)

    for act in sorted(set(acts)):
        hit = functools.reduce(jnp.logical_or, [j == s for s, a in enumerate(acts) if a == act])
        pl.when(hit)(functools.partial(pair, act))


def _in_proj(x2, mod, norm_w, w, wg, rows_per_mod, name, pairs):
    n, d = x2.shape
    tm = min(2048, rows_per_mod)
    tn = IN_TILE
    per = rows_per_mod // tm
    acts = tuple(p[2] for p in pairs)

    def w_group(which):
        return lambda i, j: (sum(jnp.where(j == s, p[which], 0) for s, p in enumerate(pairs)), 0)

    return pl.pallas_call(
        functools.partial(_inproj_kernel, acts=acts),
        grid=(n // tm, len(pairs)),
        in_specs=[pl.BlockSpec((tm, d), lambda i, j: (i, 0)),
                  pl.BlockSpec((1, 1, d), lambda i, j: (i // per, 0, 1)),
                  pl.BlockSpec((1, 1, d), lambda i, j: (i // per, 0, 0)),
                  pl.BlockSpec((1, d), lambda i, j: (0, 0)),
                  pl.BlockSpec((tn, d), w_group(0)),
                  pl.BlockSpec((tn, d), w_group(1)),
                  pl.BlockSpec((N_GATE_COLS, d), lambda i, j: (0, 0))],
        out_specs=[pl.BlockSpec((tm, 2 * tn), lambda i, j: (i, j)),
                   pl.BlockSpec((N_GATE_COLS, tm), lambda i, j: (0, i))],
        out_shape=[jax.ShapeDtypeStruct((n, 2 * tn * len(pairs)), BF16),
                   jax.ShapeDtypeStruct((N_GATE_COLS, n), F32)],
        scratch_shapes=[pltpu.VMEM((tm, d), BF16)],
        compiler_params=_cparams(("arbitrary", "arbitrary"), 56),
        name=name,
    )(x2, mod, mod, norm_w, w, w, wg)


def _ctx_proj_kernel(x_ref, sc_ref, sh_ref, nw_ref, wsrc_ref, wgsrc_ref, wdst_ref, wgdst_ref, y_ref, g_ref,
                     xn_ref, kv_ref, *, steps, k_block, v_block):
    i, j = pl.program_id(0), pl.program_id(1)
    group = i * steps + j
    w = wsrc_ref[...]
    w = jnp.where(group == k_block, w * MXU_DIM ** -0.5, w).astype(BF16)
    wdst_ref[...] = w

    @pl.when(group == k_block)
    def _():
        kv_ref[0] = w

    @pl.when(group == v_block)
    def _():
        kv_ref[1] = w

    @pl.when(j == 0)
    def _():
        x = x_ref[...]
        ms = jnp.mean(x * x, axis=-1, keepdims=True)
        xn = x * lax.rsqrt(ms + EPS) * (nw_ref[...] * (1.0 + sc_ref[0])) + sh_ref[0]
        xn_ref[...] = xn.astype(BF16)
        wg = wgsrc_ref[...].astype(BF16)
        wgdst_ref[...] = wg
        wg = jnp.concatenate([wg, jnp.zeros((LANES - wg.shape[0], wg.shape[1]), wg.dtype)], axis=0)
        g_ref[...] = _dot_nt(xn_ref[...], wg).T[:N_GATE_COLS]

    @pl.when(j == k_block)
    def _():
        y_ref[:, :IN_TILE] = _dot_nt(xn_ref[...], kv_ref[0]).astype(BF16)

    @pl.when(j == v_block)
    def _():
        y_ref[:, IN_TILE:] = _dot_nt(xn_ref[...], kv_ref[1]).astype(BF16)


def _ctx_proj(xc, mod, mod_row, norm_w, wt, gate_row0, k_block, v_block):
    n, d = xc.shape
    tn = IN_TILE
    n_groups = (wt.shape[0] - N_GATE_COLS) // tn
    row_tiles = 2
    tm, steps = n // row_tiles, n_groups // row_tiles
    assert gate_row0 % tn == 0 and n_groups % row_tiles == 0 and 0 < k_block < v_block < steps

    def src_rows(i, j):
        r = (i * steps + j) * tn
        return (pl.multiple_of(r + jnp.where(r >= gate_row0, N_GATE_COLS, 0), N_GATE_COLS), 0)

    return pl.pallas_call(
        functools.partial(_ctx_proj_kernel, steps=steps, k_block=k_block, v_block=v_block),
        grid=(row_tiles, steps),
        in_specs=[pl.BlockSpec((tm, d), lambda i, j: (i, 0)),
                  pl.BlockSpec((1, 1, d), lambda i, j: (mod_row, 0, 1)),
                  pl.BlockSpec((1, 1, d), lambda i, j: (mod_row, 0, 0)),
                  pl.BlockSpec((1, d), lambda i, j: (0, 0)),
                  pl.BlockSpec((pl.Element(tn), pl.Element(d)), src_rows),
                  pl.BlockSpec((pl.Element(N_GATE_COLS), pl.Element(d)), lambda i, j: (gate_row0, 0))],
        out_specs=[pl.BlockSpec((tn, d), lambda i, j: (i * steps + j, 0)),
                   pl.BlockSpec((N_GATE_COLS, d), lambda i, j: (0, 0)),
                   pl.BlockSpec((tm, 2 * tn), lambda i, j: (i, 0)),
                   pl.BlockSpec((N_GATE_COLS, tm), lambda i, j: (0, i))],
        out_shape=[jax.ShapeDtypeStruct((n_groups * tn, d), BF16),
                   jax.ShapeDtypeStruct((N_GATE_COLS, d), BF16),
                   jax.ShapeDtypeStruct((n, 2 * tn), BF16),
                   jax.ShapeDtypeStruct((N_GATE_COLS, n), F32)],
        scratch_shapes=[pltpu.VMEM((tm, d), BF16), pltpu.VMEM((2, tn, d), BF16)],
        compiler_params=_cparams(("arbitrary", "arbitrary"), 56),
        name="ctx_proj",
    )(xc, mod, mod, norm_w, wt, wt)


N_DH = N_DIR * M_HEADS
TERM_PIECES = (3, 2, 2, 3)


def _split(x, pieces):
    out = []
    for _ in range(pieces):
        p = x.astype(BF16).astype(F32)
        out.append(p)
        x = x - p
    return out


def _chunk_scan(x, pos, chunk, op, reverse):
    t = x.shape[1]
    k = 1
    while k < chunk:
        if reverse:
            x = jnp.where(pos < chunk - k, op(x, pltpu.roll(x, t - k, 1)), x)
        else:
            x = jnp.where(pos >= k, op(x, pltpu.roll(x, k, 1)), x)
        k *= 2
    return x


def _gate_terms(li, pre_f, m0, chunk):
    rows, t = li.shape
    nc = t // chunk
    pos = lax.broadcasted_iota(jnp.int32, (rows, t), 1) % chunk
    rev = lax.broadcasted_iota(jnp.int32, (rows, t), 0) % N_DH >= M_HEADS
    rev1 = lax.broadcasted_iota(jnp.int32, (rows, 1), 0) % N_DH >= M_HEADS
    lf = _log_sigmoid(pre_f)
    b = jnp.where(rev, _chunk_scan(lf, pos, chunk, jnp.add, True), _chunk_scan(lf, pos, chunk, jnp.add, False))
    a = li - b
    cmax = jnp.where(rev, _chunk_scan(a, pos, chunk, jnp.maximum, True),
                     _chunk_scan(a, pos, chunk, jnp.maximum, False))

    def at_scan_end(x, c):
        lo = c * chunk
        return jnp.where(rev1, x[:, lo:lo + 1], x[:, lo + chunk - 1:lo + chunk])

    b_end = [at_scan_end(b, c) for c in range(nc)]
    a_max = [at_scan_end(cmax, c) for c in range(nc)]
    m_f, m_b = m0, m0
    m_in_f, m_in_b = [None] * nc, [None] * nc
    for i in range(nc):
        j = nc - 1 - i
        m_in_f[i] = m_f
        m_f = b_end[i] + jnp.maximum(m_f, a_max[i])
        m_in_b[j] = m_b
        m_b = b_end[j] + jnp.maximum(m_b, a_max[j])
    m_in = [jnp.where(rev1, m_in_b[c], m_in_f[c]) for c in range(nc)]
    per_chunk = lambda vals: jnp.concatenate([jnp.broadcast_to(v, (rows, chunk)) for v in vals], axis=1)
    m_in_t = per_chunk(m_in)
    g = jnp.maximum(m_in_t, cmax)
    g_end = [jnp.maximum(m_in[c], a_max[c]) for c in range(nc)]
    pieces = []
    terms = (-g * LOG2E, jnp.exp(m_in_t - g), jnp.exp(a - per_chunk(g_end)), -(b + g) * LOG2E)
    for term, n in zip(terms, TERM_PIECES):
        pieces += _split(term, n)
    decay = jnp.concatenate([jnp.exp(m_in[c] - g_end[c]) for c in range(nc)], axis=1)
    return pieces, a * LOG2E, decay, jnp.where(rev1, m_b, m_f)


def _gates_kernel(gc_ref, g_ref, b_ref, rc_ref, dc_ref, r_ref, a_ref, d_ref, *, chunk):
    nb = r_ref.shape[0]

    def split(ref):
        t = ref.shape[1] // nb
        li, pre_f = [], []
        for e in range(nb):
            pre = ref[:, e * t:(e + 1) * t] + b_ref[...]
            for d in range(N_DIR):
                li.append(pre[2 * d * M_HEADS:(2 * d + 1) * M_HEADS])
                pre_f.append(pre[(2 * d + 1) * M_HEADS:(2 * d + 2) * M_HEADS])
        return jnp.concatenate(li, axis=0), jnp.concatenate(pre_f, axis=0)

    def store_pieces(ref, pieces):
        pad = jnp.zeros((LANES - N_DH * len(pieces), pieces[0].shape[1]), F32)
        for e in range(nb):
            own = [p[e * N_DH:(e + 1) * N_DH] for p in pieces]
            ref[e] = jnp.concatenate(own + [pad], axis=0).astype(BF16)

    m0 = jnp.full((nb * N_DH, 1), M_INIT, F32)
    pieces_c, _, decay_c, m_ctx = _gate_terms(*split(gc_ref), m0, chunk)
    store_pieces(rc_ref, pieces_c)
    pieces, a, decay, _ = _gate_terms(*split(g_ref), m_ctx, chunk)
    store_pieces(r_ref, pieces)
    for e in range(nb):
        own = slice(e * N_DH, (e + 1) * N_DH)
        dc_ref[e] = decay_c[own]
        d_ref[e] = decay[own]
        for c in range(a_ref.shape[1]):
            a_ref[e, c] = a[own, c * chunk:(c + 1) * chunk]


def _gates(gates_t, gates_tc, gate_b, b, chunk):
    t = gates_t.shape[1] // b
    tc = gates_tc.shape[1] // b
    nc, ncc = t // chunk, tc // chunk
    nb = b
    spec = lambda r, n: pl.BlockSpec((nb, r, n), lambda i: (i, 0, 0))
    seq = lambda n: pl.BlockSpec((N_GATE_COLS, nb * n), lambda i: (0, i))
    rc, dc, r, a, dec = pl.pallas_call(
        functools.partial(_gates_kernel, chunk=chunk),
        grid=(b // nb,),
        in_specs=[seq(tc), seq(t), pl.BlockSpec((N_GATE_COLS, 1), lambda i: (0, 0))],
        out_specs=[spec(LANES, tc), spec(N_DH, ncc), spec(LANES, t),
                   pl.BlockSpec((nb, nc, N_DH, chunk), lambda i: (i, 0, 0, 0)), spec(N_DH, nc)],
        out_shape=[jax.ShapeDtypeStruct((b, LANES, tc), BF16), jax.ShapeDtypeStruct((b, N_DH, ncc), F32),
                   jax.ShapeDtypeStruct((b, LANES, t), BF16), jax.ShapeDtypeStruct((b, nc, N_DH, chunk), F32),
                   jax.ShapeDtypeStruct((b, N_DH, nc), F32)],
        compiler_params=_cparams(("arbitrary",), 32),
        name="gates",
    )(gates_tc, gates_t, gate_b.reshape(N_GATE_COLS, 1))
    decay = jnp.concatenate([dc, dec], axis=2).reshape(b * N_DH, ncc + nc)
    return rc, r, a, decay


def _twice(x):
    return jnp.concatenate([x, x], axis=1)


def _mlstm_kernel(dec_ref, q_ref, k_ref, v_ref, oz_ref, p_ref, a_ref, kc_ref, vc_ref, pc_ref,
                  sel_ref, hnw_ref, out_ref, ct_ref, n_ref, hs_ref, *, chunk, dh):
    t = q_ref.shape[1]
    nc = t // chunk
    ncc = kc_ref.shape[1] // chunk
    hps = q_ref.shape[2] // dh
    bi, h0 = pl.program_id(0), pl.program_id(1) * hps
    t_idx = lax.broadcasted_iota(jnp.int32, (chunk, chunk), 0)
    s_idx = lax.broadcasted_iota(jnp.int32, (chunk, chunk), 1)

    def head_cols(hh):
        return slice(hh * dh, (hh + 1) * dh)

    def bcast_terms(pieces, sel):
        return lax.dot_general(pieces, sel, (((0,), (0,)), ((), ())), preferred_element_type=F32)

    def row_terms(c, hh, d):
        w = bcast_terms(p_ref[0, :, c * chunk:(c + 1) * chunk], sel_ref[hh, d])
        return [w[:, i * LANES:(i + 1) * LANES] for i in range(len(TERM_PIECES))]

    def update(k, v, e_b, decay, u):
        ke = k.astype(F32) * _twice(e_b)
        upd = lax.dot_general(ke.astype(BF16), v, (((0,), (0,)), ((), ())), preferred_element_type=F32)
        n_new = jnp.sum(ke, axis=0, keepdims=True)
        ct_ref[u] = upd if decay is None else decay * ct_ref[u] + upd
        n_ref[u] = n_new if decay is None else decay * n_ref[u] + n_new

    def decay_of(c, hh, d):
        return dec_ref[bi * N_DH + d * M_HEADS + h0 + hh, c]

    for hh in range(hps):
        for d in range(N_DIR):
            for i in range(ncc):
                c = i if d == 0 else ncc - 1 - i
                rows = slice(c * chunk, (c + 1) * chunk)
                e_b = bcast_terms(pc_ref[0, :, rows], sel_ref[hh, d, :, 2 * LANES:3 * LANES])
                update(kc_ref[0, rows, head_cols(hh)], vc_ref[0, rows, head_cols(hh)], e_b,
                       None if i == 0 else decay_of(c, hh, d), hh * N_DIR + d)

    def run_chunk(c, hh, d):
        u = hh * N_DIR + d
        rows = slice(c * chunk, (c + 1) * chunk)
        cols = head_cols(hh)
        q, k, v = q_ref[0, rows, cols], k_ref[0, rows, cols], v_ref[0, rows, cols]
        neg_g_b, w_inter_b, e_b, neg_mt_b = row_terms(c, hh, d)
        a_row = a_ref[0, c, pl.ds(d * M_HEADS + h0 + hh, 1), :]
        mask = (s_idx <= t_idx) if d == 0 else (s_idx >= t_idx)
        w_intra = jnp.exp2(jnp.where(mask, a_row + _twice(neg_g_b), NEG_BIG))
        s = _dot_nt(q, k) * w_intra
        num = (_twice(w_inter_b) * jnp.dot(q, ct_ref[u].astype(BF16), preferred_element_type=F32)
               + jnp.dot(s.astype(BF16), v, preferred_element_type=F32))
        den_b = (w_inter_b * jnp.sum(q.astype(F32) * n_ref[u], axis=-1, keepdims=True)
                 + jnp.sum(s, axis=-1, keepdims=True))
        h = num * _twice(1.0 / jnp.maximum(jnp.abs(den_b), jnp.exp2(neg_mt_b)))
        update(k, v, e_b, decay_of(ncc + c, hh, d), u)
        return rows, cols, h

    def finalize(rows, cols, h):
        hn = h * lax.rsqrt(jnp.mean(h * h, axis=-1, keepdims=True) + EPS) * hnw_ref[:, cols]
        out_ref[0, rows, cols] = (hn * oz_ref[0, rows, cols].astype(F32)).astype(out_ref.dtype)

    for i in range(nc):
        for hh in range(hps):
            for d in range(N_DIR):
                rows, cols, h = run_chunk(i if d == 0 else nc - 1 - i, hh, d)
                if i < nc // 2:
                    hs_ref[rows, cols] = h
                else:
                    finalize(rows, cols, hs_ref[rows, cols] + h)


def _term_selectors():
    sel = np.zeros((M_HEADS, N_DIR, LANES, len(TERM_PIECES) * LANES), np.float32)
    first = np.concatenate([[0], np.cumsum(TERM_PIECES)])
    for h in range(M_HEADS):
        for d in range(N_DIR):
            for q in range(len(TERM_PIECES)):
                for j in range(first[q], first[q + 1]):
                    sel[h, d, j * N_DH + d * M_HEADS + h, q * LANES:(q + 1) * LANES] = 1.0
    return jnp.asarray(sel, BF16)


def _mlstm(y, yc, p, a, pc, decay, head_norm_w, chunk, col):
    b, t, _ = y.shape
    tc = yc.shape[1]
    dh = MXU_DIM
    hps = HEADS_PER_STEP
    w = hps * dh
    sel = _term_selectors()
    assert t % (2 * chunk) == 0 and tc % chunk == 0 and M_HEADS % hps == 0
    blk = lambda name: pl.BlockSpec((1, t, w), lambda i, h, c=col[name] // w: (i, 0, c + h))
    blkc = lambda col0: pl.BlockSpec((1, tc, w), lambda i, h, c=col0 // w: (i, 0, c + h))
    return pl.pallas_call(
        functools.partial(_mlstm_kernel, chunk=chunk, dh=dh),
        grid=(b, M_HEADS // hps),
        in_specs=[pl.BlockSpec(memory_space=pltpu.SMEM),
                  blk("q"), blk("k"), blk("v"), blk("oz"),
                  pl.BlockSpec((1, LANES, t), lambda i, h: (i, 0, 0)),
                  pl.BlockSpec((1, t // chunk, N_DH, chunk), lambda i, h: (i, 0, 0, 0)),
                  blkc(0), blkc(M_HEADS * dh),
                  pl.BlockSpec((1, LANES, tc), lambda i, h: (i, 0, 0)),
                  pl.BlockSpec((hps, N_DIR) + sel.shape[2:], lambda i, h: (h, 0, 0, 0)),
                  pl.BlockSpec((1, w), lambda i, h: (0, h))],
        out_specs=pl.BlockSpec((1, t, w), lambda i, h: (i, 0, h)),
        out_shape=jax.ShapeDtypeStruct((b, t, M_HEADS * dh), BF16),
        scratch_shapes=[pltpu.VMEM((hps * N_DIR, dh, dh), F32),
                        pltpu.VMEM((hps * N_DIR, 1, dh), F32),
                        pltpu.VMEM((t, w), F32)],
        compiler_params=_cparams(("arbitrary", "arbitrary"), 48),
        name="mlstm",
    )(decay, y, y, y, y, p, a, yc, yc, pc, sel, head_norm_w)


def _window_bounds(pos, w, length):
    return np.clip(pos - w // 2, 0, length), np.clip(pos + w - w // 2, 0, length)


def _col_pool_matrices(tile):
    pos = np.arange(GRID_W)
    mats = []
    for w in POOL_WINDOWS:
        lo, hi = _window_bounds(pos, w, GRID_W)
        band = ((pos[None, :] >= lo[:, None]) & (pos[None, :] < hi[:, None])).astype(np.float32)
        mats.append(np.kron(np.eye(tile // GRID_W, dtype=np.float32), band))
    return jnp.asarray(np.stack(mats), dtype=BF16)


def _pool_kernel(p_ref, z_ref, mc_ref, pw_ref, ps_ref, out_ref, pad_ref, scale_ref, *, tile, rows):
    t = p_ref.shape[1]
    gd = LANES
    pad = (max(POOL_WINDOWS) // 2) * GRID_W
    n_tiles = t // tile

    @pl.when(pl.program_id(0) == 0)
    def _():
        pad_ref[...] = jnp.zeros(pad_ref.shape, F32)
        tok = lax.broadcasted_iota(jnp.int32, (t, gd), 0)
        r = tok // GRID_W
        c = tok % GRID_W
        for g, w in enumerate(POOL_WINDOWS):
            cnt_r = jnp.minimum(r + (w - w // 2), rows) - jnp.maximum(r - w // 2, 0)
            cnt_c = jnp.minimum(c + (w - w // 2), GRID_W) - jnp.maximum(c - w // 2, 0)
            scale_ref[g] = 1.0 / (cnt_r.astype(F32) * cnt_c.astype(F32))

    def col_sum(g, start):
        cols = pl.ds(g * gd, gd)
        pad_ref[g % 2, pl.ds(pad + start, tile), :] = jnp.dot(
            mc_ref[g], p_ref[0, pl.ds(start, tile), cols], preferred_element_type=F32)

    def row_sum(g, start):
        w = POOL_WINDOWS[g]
        cols = pl.ds(g * gd, gd)
        buf = pad_ref.at[g % 2]
        acc = buf[pl.ds(pad + start - (w // 2) * GRID_W, tile), :]
        for j in range(1 - w // 2, w - w // 2):
            acc = acc + buf[pl.ds(pad + start + j * GRID_W, tile), :]
        mixed = acc * scale_ref[g, pl.ds(start, tile), :] - p_ref[0, pl.ds(start, tile), cols].astype(F32)
        mm = jnp.dot(mixed.astype(BF16), pw_ref[g], preferred_element_type=F32)
        zz = z_ref[0, pl.ds(start, tile), cols].astype(F32)
        out_ref[0, pl.ds(start, tile), cols] = (mm * ps_ref[:, cols] * zz).astype(out_ref.dtype)

    for g in range(-1, P_GROUPS):
        def body(i, carry, g=g):
            start = pl.multiple_of(i * tile, tile)
            if g + 1 < P_GROUPS:
                col_sum(g + 1, start)
            if g >= 0:
                row_sum(g, start)
            return carry

        lax.fori_loop(0, n_tiles, body, 0, unroll=POOL_UNROLL)


def _pool(y, pool_w, pool_scale, p_col0):
    b, t, _ = y.shape
    pw = P_GROUPS * LANES
    tile = MXU_DIM
    pad = (max(POOL_WINDOWS) // 2) * GRID_W
    return pl.pallas_call(
        functools.partial(_pool_kernel, tile=tile, rows=t // GRID_W),
        grid=(b,),
        in_specs=[pl.BlockSpec((1, t, pw), lambda i: (i, 0, p_col0 // pw)),
                  pl.BlockSpec((1, t, pw), lambda i: (i, 0, p_col0 // pw + 1)),
                  pl.BlockSpec((P_GROUPS, tile, tile), lambda i: (0, 0, 0)),
                  pl.BlockSpec((P_GROUPS, LANES, LANES), lambda i: (0, 0, 0)),
                  pl.BlockSpec((1, pw), lambda i: (0, 0))],
        out_specs=pl.BlockSpec((1, t, pw), lambda i: (i, 0, 0)),
        out_shape=jax.ShapeDtypeStruct((b, t, pw), BF16),
        scratch_shapes=[pltpu.VMEM((2, t + 2 * pad, LANES), F32),
                        pltpu.VMEM((P_GROUPS, t, LANES), F32)],
        compiler_params=_cparams(("arbitrary",), 48),
        name="pool",
    )(y, y, _col_pool_matrices(tile), pool_w, pool_scale)


def _merge_kernel(x_ref, ym_ref, yp_ref, gm_ref, gp_ref, gate_ref, wm_ref, wp_ref, wo_ref, fw_ref, o_ref):
    slab = x_ref.shape[0] // MERGE_SLABS
    for r in range(MERGE_SLABS):
        rows = slice(r * slab, (r + 1) * slab)
        y_m = jnp.dot(ym_ref[rows, :], wm_ref[...], preferred_element_type=F32)
        y_p = jnp.dot(yp_ref[rows, :], wp_ref[...], preferred_element_type=F32)
        merged = (_sigmoid_t(gm_ref[rows, :].astype(F32)) * y_m
                  + _sigmoid_t(gp_ref[rows, :].astype(F32)) * y_p)
        upd = jnp.dot(merged.astype(BF16), wo_ref[...], preferred_element_type=F32)
        xo = x_ref[rows, :] + gate_ref[0] * upd
        o_ref[rows, :] = xo * lax.rsqrt(jnp.mean(xo * xo, axis=-1, keepdims=True) + EPS) * fw_ref[...]


def _merge(x2, ym, yp, y, mod, wm, wp, wo, final_w, rows_per_mod, g_col0):
    n, d = x2.shape
    tm = 1024
    per = rows_per_mod // tm
    pw = yp.shape[1]
    full = lambda shape: pl.BlockSpec(shape, lambda i: (0,) * len(shape))
    return pl.pallas_call(
        _merge_kernel,
        grid=(n // tm,),
        in_specs=[pl.BlockSpec((tm, d), lambda i: (i, 0)),
                  pl.BlockSpec((tm, d), lambda i: (i, 0)),
                  pl.BlockSpec((tm, pw), lambda i: (i, 0)),
                  pl.BlockSpec((tm, d), lambda i: (i, g_col0 // d)),
                  pl.BlockSpec((tm, d), lambda i: (i, g_col0 // d + 1)),
                  pl.BlockSpec((1, 1, d), lambda i: (i // per, 0, 2)),
                  full((d, d)), full((pw, d)), full((d, d)), full((1, d))],
        out_specs=pl.BlockSpec((tm, d), lambda i: (i, 0)),
        out_shape=jax.ShapeDtypeStruct((n, d), F32),
        compiler_params=_cparams(("arbitrary",), 48),
        name="merge",
    )(x2, ym, yp, y, y, mod, wm, wp, wo, final_w)


def kernel(x, c, ctx, c_ctx, norm_w, ada_w, ada_b, in_w, gate_b, head_norm_w, pool_w, pool_scale,
           branch_m_w, branch_p_w, out_w, final_norm_w):
    b, t, d = x.shape
    tc = ctx.shape[1]
    depth = norm_w.shape[0]
    assert depth == 1, "context stream update between layers is not implemented"
    mw = M_HEADS * MXU_DIM
    assert d == mw
    g0 = 5 * mw

    mod = _adaln(c, c_ctx, ada_w[0], ada_b[0])

    wt_f32 = jnp.transpose(in_w[0])
    nw = norm_w[0].reshape(1, d)

    g_q, g_k, g_v, g_o, g_z, g_p, g_gm, g_gp = range(8)
    pairs = ((g_q, g_k, "id"), (g_v, g_p, "half_silu"), (g_o, g_z, "gate"), (g_gm, g_gp, "id"))
    col = {name: i * mw for i, name in enumerate(("q", "k", "v", "p", "o", "oz", "g_m", "g_p"))}

    wt, wg, yc, gates_tc = _ctx_proj(ctx.reshape(b * tc, d), mod, b, nw, wt_f32, g0, g_k, g_v)
    y, gates_t = _in_proj(x.reshape(b * t, d), mod, nw, wt, wg, t, "in_proj", pairs)
    y = y.reshape(b, t, -1)
    yc = yc.reshape(b, tc, -1)

    pc, p, a, decay = _gates(gates_t, gates_tc, gate_b[0], b, CHUNK)
    ym = _mlstm(y, yc, p, a, pc, decay, head_norm_w[0].reshape(1, mw), CHUNK, col)
    yp = _pool(y, pool_w[0].astype(BF16), pool_scale[0].reshape(1, -1), col["p"])

    out = _merge(x.reshape(b * t, d), ym.reshape(b * t, mw), yp.reshape(b * t, -1), y.reshape(b * t, -1),
                 mod, branch_m_w[0].astype(BF16), branch_p_w[0].astype(BF16),
                 out_w[0].astype(BF16), final_norm_w.reshape(1, d), t, col["g_m"])
    return out.reshape(b, t, d)
```

```python
import functools

import numpy as np
import jax
import jax.numpy as jnp
from jax import lax
from jax.experimental import pallas as pl
from jax.experimental.pallas import tpu as pltpu

F32 = jnp.float32
BF16 = jnp.bfloat16

EPS = 1e-6
LOG2E = 1.4426950408889634
M_INIT = -1e30
NEG_BIG = -1e30
M_HEADS = 4
N_DIR = 2
N_GATE_COLS = N_DIR * 2 * M_HEADS
P_GROUPS = 4
POOL_WINDOWS = (2, 4, 8, 16)
GRID_W = 64
LANES = 128
MXU_DIM = 256
CHUNK = 256
POOL_UNROLL = 4
MERGE_SLABS = 2
MIB = 1024 * 1024


def _cparams(sem, vmem_mib):
    return pltpu.CompilerParams(dimension_semantics=sem, vmem_limit_bytes=vmem_mib * MIB)


def _sigmoid(x):
    return 1.0 / (1.0 + jnp.exp(-x))


def _silu(x):
    return x * _sigmoid(x)


def _log_sigmoid(x):
    return jnp.minimum(x, 0.0) - jnp.log(1.0 + jnp.exp(-jnp.abs(x)))


def _dot_nt(a, b):
    return lax.dot_general(a, b, (((1,), (1,)), ((), ())), preferred_element_type=F32)


def _adaln_kernel(c_ref, cc_ref, w_ref, b_ref, o_ref):
    b = c_ref.shape[0]
    rows = jnp.concatenate([c_ref[...], jnp.broadcast_to(cc_ref[...], (o_ref.shape[0] - b, c_ref.shape[1]))], axis=0)
    o_ref[:, 0, :] = jnp.dot(_silu(rows), w_ref[...], preferred_element_type=F32) + b_ref[...]


def _adaln(c, c_ctx, ada_w, ada_b):
    b, d = c.shape
    rows = 8 * (b // 8 + 1)
    n = ada_w.shape[1]
    tn = d
    return pl.pallas_call(
        _adaln_kernel,
        grid=(n // tn,),
        in_specs=[pl.BlockSpec((b, d), lambda j: (0, 0)),
                  pl.BlockSpec((1, d), lambda j: (0, 0)),
                  pl.BlockSpec((d, tn), lambda j: (0, j)),
                  pl.BlockSpec((1, tn), lambda j: (0, j))],
        out_specs=pl.BlockSpec((rows, 1, tn), lambda j: (0, 0, j)),
        out_shape=jax.ShapeDtypeStruct((rows, 1, n), F32),
        compiler_params=_cparams(("arbitrary",), 32),
        name="adaln",
    )(c, c_ctx.reshape(1, d), ada_w, ada_b.reshape(1, n))


def _sigmoid_t(x):
    return 0.5 * jnp.tanh(0.5 * x) + 0.5


IN_TILE = 1024
GATE_SUB = 256


def _inproj_kernel(x_ref, sc_ref, sh_ref, nw_ref, wa_ref, wb_ref, wg_ref, y_ref, g_ref, xn_ref, *, acts):
    j = pl.program_id(1)

    @pl.when(j == 0)
    def _():
        x = x_ref[...]
        ms = jnp.mean(x * x, axis=-1, keepdims=True)
        xn = x * lax.rsqrt(ms + EPS) * (nw_ref[...] * (1.0 + sc_ref[0])) + sh_ref[0]
        xn_ref[...] = xn.astype(BF16)
        wg = wg_ref[...]
        wg = jnp.concatenate([wg, jnp.zeros((LANES - wg.shape[0], wg.shape[1]), wg.dtype)], axis=0)
        g_ref[...] = _dot_nt(xn_ref[...], wg).T[:N_GATE_COLS]

    def pair(act):
        tn = IN_TILE
        xn = xn_ref[...]
        if act == "gate":
            for c in range(0, tn, GATE_SUB):
                a = _dot_nt(xn, wa_ref[c:c + GATE_SUB, :])
                b = _dot_nt(xn, wb_ref[c:c + GATE_SUB, :])
                y_ref[:, c:c + GATE_SUB] = a.astype(BF16)
                y_ref[:, tn + c:tn + c + GATE_SUB] = (_sigmoid_t(a) * (b * _sigmoid_t(b))).astype(BF16)
            return
        y_ref[:, :tn] = _dot_nt(xn, wa_ref[...]).astype(BF16)
        b = _dot_nt(xn, wb_ref[...])
        if act == "half_silu":
            half = tn // 2
            b = jnp.concatenate([b[:, :half], b[:, half:] * _sigmoid_t(b[:, half:])], axis=1)
        y_ref[:, tn:] = b.astype(BF16)

    for act in sorted(set(acts)):
        hit = functools.reduce(jnp.logical_or, [j == s for s, a in enumerate(acts) if a == act])
        pl.when(hit)(functools.partial(pair, act))


def _in_proj(x2, mod, norm_w, w, wg, rows_per_mod, name, pairs):
    n, d = x2.shape
    tm = min(2048, rows_per_mod)
    tn = IN_TILE
    per = rows_per_mod // tm
    acts = tuple(p[2] for p in pairs)

    def w_group(which):
        return lambda i, j: (sum(jnp.where(j == s, p[which], 0) for s, p in enumerate(pairs)), 0)

    return pl.pallas_call(
        functools.partial(_inproj_kernel, acts=acts),
        grid=(n // tm, len(pairs)),
        in_specs=[pl.BlockSpec((tm, d), lambda i, j: (i, 0)),
                  pl.BlockSpec((1, 1, d), lambda i, j: (i // per, 0, 1)),
                  pl.BlockSpec((1, 1, d), lambda i, j: (i // per, 0, 0)),
                  pl.BlockSpec((1, d), lambda i, j: (0, 0)),
                  pl.BlockSpec((tn, d), w_group(0)),
                  pl.BlockSpec((tn, d), w_group(1)),
                  pl.BlockSpec((N_GATE_COLS, d), lambda i, j: (0, 0))],
        out_specs=[pl.BlockSpec((tm, 2 * tn), lambda i, j: (i, j)),
                   pl.BlockSpec((N_GATE_COLS, tm), lambda i, j: (0, i))],
        out_shape=[jax.ShapeDtypeStruct((n, 2 * tn * len(pairs)), BF16),
                   jax.ShapeDtypeStruct((N_GATE_COLS, n), F32)],
        scratch_shapes=[pltpu.VMEM((tm, d), BF16)],
        compiler_params=_cparams(("arbitrary", "arbitrary"), 56),
        name=name,
    )(x2, mod, mod, norm_w, w, w, wg)


def _ctx_proj_kernel(x_ref, sc_ref, sh_ref, nw_ref, wsrc_ref, wgsrc_ref, wdst_ref, wgdst_ref, y_ref, g_ref,
                     xn_ref, kv_ref, *, steps, k_block, v_block):
    i, j = pl.program_id(0), pl.program_id(1)
    group = i * steps + j
    w = wsrc_ref[...]
    w = jnp.where(group == k_block, w * MXU_DIM ** -0.5, w).astype(BF16)
    wdst_ref[...] = w

    @pl.when(group == k_block)
    def _():
        kv_ref[0] = w

    @pl.when(group == v_block)
    def _():
        kv_ref[1] = w

    @pl.when(j == 0)
    def _():
        x = x_ref[...]
        ms = jnp.mean(x * x, axis=-1, keepdims=True)
        xn = x * lax.rsqrt(ms + EPS) * (nw_ref[...] * (1.0 + sc_ref[0])) + sh_ref[0]
        xn_ref[...] = xn.astype(BF16)
        wg = wgsrc_ref[...].astype(BF16)
        wgdst_ref[...] = wg
        wg = jnp.concatenate([wg, jnp.zeros((LANES - wg.shape[0], wg.shape[1]), wg.dtype)], axis=0)
        g_ref[...] = _dot_nt(xn_ref[...], wg).T[:N_GATE_COLS]

    @pl.when(j == k_block)
    def _():
        y_ref[:, :IN_TILE] = _dot_nt(xn_ref[...], kv_ref[0]).astype(BF16)

    @pl.when(j == v_block)
    def _():
        y_ref[:, IN_TILE:] = _dot_nt(xn_ref[...], kv_ref[1]).astype(BF16)


def _ctx_proj(xc, mod, mod_row, norm_w, wt, gate_row0, k_block, v_block):
    n, d = xc.shape
    tn = IN_TILE
    n_groups = (wt.shape[0] - N_GATE_COLS) // tn
    row_tiles = 2
    tm, steps = n // row_tiles, n_groups // row_tiles
    assert gate_row0 % tn == 0 and n_groups % row_tiles == 0 and 0 < k_block < v_block < steps

    def src_rows(i, j):
        r = (i * steps + j) * tn
        return (pl.multiple_of(r + jnp.where(r >= gate_row0, N_GATE_COLS, 0), N_GATE_COLS), 0)

    return pl.pallas_call(
        functools.partial(_ctx_proj_kernel, steps=steps, k_block=k_block, v_block=v_block),
        grid=(row_tiles, steps),
        in_specs=[pl.BlockSpec((tm, d), lambda i, j: (i, 0)),
                  pl.BlockSpec((1, 1, d), lambda i, j: (mod_row, 0, 1)),
                  pl.BlockSpec((1, 1, d), lambda i, j: (mod_row, 0, 0)),
                  pl.BlockSpec((1, d), lambda i, j: (0, 0)),
                  pl.BlockSpec((pl.Element(tn), pl.Element(d)), src_rows),
                  pl.BlockSpec((pl.Element(N_GATE_COLS), pl.Element(d)), lambda i, j: (gate_row0, 0))],
        out_specs=[pl.BlockSpec((tn, d), lambda i, j: (i * steps + j, 0)),
                   pl.BlockSpec((N_GATE_COLS, d), lambda i, j: (0, 0)),
                   pl.BlockSpec((tm, 2 * tn), lambda i, j: (i, 0)),
                   pl.BlockSpec((N_GATE_COLS, tm), lambda i, j: (0, i))],
        out_shape=[jax.ShapeDtypeStruct((n_groups * tn, d), BF16),
                   jax.ShapeDtypeStruct((N_GATE_COLS, d), BF16),
                   jax.ShapeDtypeStruct((n, 2 * tn), BF16),
                   jax.ShapeDtypeStruct((N_GATE_COLS, n), F32)],
        scratch_shapes=[pltpu.VMEM((tm, d), BF16), pltpu.VMEM((2, tn, d), BF16)],
        compiler_params=_cparams(("arbitrary", "arbitrary"), 56),
        name="ctx_proj",
    )(xc, mod, mod, norm_w, wt, wt)


N_DH = N_DIR * M_HEADS
TERM_PIECES = (3, 2, 2, 3)


def _split(x, pieces):
    out = []
    for _ in range(pieces):
        p = x.astype(BF16).astype(F32)
        out.append(p)
        x = x - p
    return out


def _chunk_scan(x, pos, chunk, op, reverse):
    t = x.shape[1]
    k = 1
    while k < chunk:
        if reverse:
            x = jnp.where(pos < chunk - k, op(x, pltpu.roll(x, t - k, 1)), x)
        else:
            x = jnp.where(pos >= k, op(x, pltpu.roll(x, k, 1)), x)
        k *= 2
    return x


def _gate_terms(li, pre_f, m0, chunk):
    rows, t = li.shape
    nc = t // chunk
    pos = lax.broadcasted_iota(jnp.int32, (rows, t), 1) % chunk
    rev = lax.broadcasted_iota(jnp.int32, (rows, t), 0) % N_DH >= M_HEADS
    rev1 = lax.broadcasted_iota(jnp.int32, (rows, 1), 0) % N_DH >= M_HEADS
    lf = _log_sigmoid(pre_f)
    b = jnp.where(rev, _chunk_scan(lf, pos, chunk, jnp.add, True), _chunk_scan(lf, pos, chunk, jnp.add, False))
    a = li - b
    cmax = jnp.where(rev, _chunk_scan(a, pos, chunk, jnp.maximum, True),
                     _chunk_scan(a, pos, chunk, jnp.maximum, False))

    def at_scan_end(x, c):
        lo = c * chunk
        return jnp.where(rev1, x[:, lo:lo + 1], x[:, lo + chunk - 1:lo + chunk])

    b_end = [at_scan_end(b, c) for c in range(nc)]
    a_max = [at_scan_end(cmax, c) for c in range(nc)]
    m_f, m_b = m0, m0
    m_in_f, m_in_b = [None] * nc, [None] * nc
    for i in range(nc):
        j = nc - 1 - i
        m_in_f[i] = m_f
        m_f = b_end[i] + jnp.maximum(m_f, a_max[i])
        m_in_b[j] = m_b
        m_b = b_end[j] + jnp.maximum(m_b, a_max[j])
    m_in = [jnp.where(rev1, m_in_b[c], m_in_f[c]) for c in range(nc)]
    per_chunk = lambda vals: jnp.concatenate([jnp.broadcast_to(v, (rows, chunk)) for v in vals], axis=1)
    m_in_t = per_chunk(m_in)
    g = jnp.maximum(m_in_t, cmax)
    g_end = [jnp.maximum(m_in[c], a_max[c]) for c in range(nc)]
    pieces = []
    terms = (-g * LOG2E, jnp.exp(m_in_t - g), jnp.exp(a - per_chunk(g_end)), -(b + g) * LOG2E)
    for term, n in zip(terms, TERM_PIECES):
        pieces += _split(term, n)
    decay = jnp.concatenate([jnp.exp(m_in[c] - g_end[c]) for c in range(nc)], axis=1)
    return pieces, a * LOG2E, decay, jnp.where(rev1, m_b, m_f)


def _gates_kernel(gc_ref, g_ref, b_ref, rc_ref, dc_ref, r_ref, a_ref, d_ref, *, chunk):
    nb = r_ref.shape[0]

    def split(ref):
        t = ref.shape[1] // nb
        li, pre_f = [], []
        for e in range(nb):
            pre = ref[:, e * t:(e + 1) * t] + b_ref[...]
            for d in range(N_DIR):
                li.append(pre[2 * d * M_HEADS:(2 * d + 1) * M_HEADS])
                pre_f.append(pre[(2 * d + 1) * M_HEADS:(2 * d + 2) * M_HEADS])
        return jnp.concatenate(li, axis=0), jnp.concatenate(pre_f, axis=0)

    def store_pieces(ref, pieces):
        pad = jnp.zeros((LANES - N_DH * len(pieces), pieces[0].shape[1]), F32)
        for e in range(nb):
            own = [p[e * N_DH:(e + 1) * N_DH] for p in pieces]
            ref[e] = jnp.concatenate(own + [pad], axis=0).astype(BF16)

    m0 = jnp.full((nb * N_DH, 1), M_INIT, F32)
    pieces_c, _, decay_c, m_ctx = _gate_terms(*split(gc_ref), m0, chunk)
    store_pieces(rc_ref, pieces_c)
    pieces, a, decay, _ = _gate_terms(*split(g_ref), m_ctx, chunk)
    store_pieces(r_ref, pieces)
    for e in range(nb):
        own = slice(e * N_DH, (e + 1) * N_DH)
        dc_ref[e] = decay_c[own]
        d_ref[e] = decay[own]
        for c in range(a_ref.shape[1]):
            a_ref[e, c] = a[own, c * chunk:(c + 1) * chunk]


def _gates(gates_t, gates_tc, gate_b, b, chunk):
    t = gates_t.shape[1] // b
    tc = gates_tc.shape[1] // b
    nc, ncc = t // chunk, tc // chunk
    nb = b
    spec = lambda r, n: pl.BlockSpec((nb, r, n), lambda i: (i, 0, 0))
    seq = lambda n: pl.BlockSpec((N_GATE_COLS, nb * n), lambda i: (0, i))
    rc, dc, r, a, dec = pl.pallas_call(
        functools.partial(_gates_kernel, chunk=chunk),
        grid=(b // nb,),
        in_specs=[seq(tc), seq(t), pl.BlockSpec((N_GATE_COLS, 1), lambda i: (0, 0))],
        out_specs=[spec(LANES, tc), spec(N_DH, ncc), spec(LANES, t),
                   pl.BlockSpec((nb, nc, N_DH, chunk), lambda i: (i, 0, 0, 0)), spec(N_DH, nc)],
        out_shape=[jax.ShapeDtypeStruct((b, LANES, tc), BF16), jax.ShapeDtypeStruct((b, N_DH, ncc), F32),
                   jax.ShapeDtypeStruct((b, LANES, t), BF16), jax.ShapeDtypeStruct((b, nc, N_DH, chunk), F32),
                   jax.ShapeDtypeStruct((b, N_DH, nc), F32)],
        compiler_params=_cparams(("arbitrary",), 32),
        name="gates",
    )(gates_tc, gates_t, gate_b.reshape(N_GATE_COLS, 1))
    decay = jnp.concatenate([dc, dec], axis=2).reshape(b * N_DH, ncc + nc)
    return rc, r, a, decay


def _twice(x):
    return jnp.concatenate([x, x], axis=1)


def _mlstm_kernel(dec_ref, q_ref, k_ref, v_ref, oz_ref, p_ref, a_ref, kc_ref, vc_ref, pc_ref,
                  sel_ref, hnw_ref, out_ref, ct_ref, n_ref, hs_ref, *, chunk):
    t = q_ref.shape[1]
    nc = t // chunk
    ncc = kc_ref.shape[1] // chunk
    bi, hi = pl.program_id(0), pl.program_id(1)
    t_idx = lax.broadcasted_iota(jnp.int32, (chunk, chunk), 0)
    s_idx = lax.broadcasted_iota(jnp.int32, (chunk, chunk), 1)

    def bcast_terms(pieces, sel):
        return lax.dot_general(pieces, sel, (((0,), (0,)), ((), ())), preferred_element_type=F32)

    def update(k, v, e_b, decay, d):
        ke = k.astype(F32) * _twice(e_b)
        upd = lax.dot_general(ke.astype(BF16), v, (((0,), (0,)), ((), ())), preferred_element_type=F32)
        n_new = jnp.sum(ke, axis=0, keepdims=True)
        ct_ref[d] = upd if decay is None else decay * ct_ref[d] + upd
        n_ref[d] = n_new if decay is None else decay * n_ref[d] + n_new

    def decay_of(c, d):
        return dec_ref[bi * N_DH + d * M_HEADS + hi, c]

    for d in range(N_DIR):
        for i in range(ncc):
            c = i if d == 0 else ncc - 1 - i
            rows = slice(c * chunk, (c + 1) * chunk)
            e_b = bcast_terms(pc_ref[0, :, rows], sel_ref[0, d, :, 2 * LANES:3 * LANES])
            update(kc_ref[0, rows, :], vc_ref[0, rows, :], e_b, None if i == 0 else decay_of(c, d), d)

    def run_chunk(c, d):
        rows = slice(c * chunk, (c + 1) * chunk)
        q, k, v = q_ref[0, rows, :], k_ref[0, rows, :], v_ref[0, rows, :]
        w = bcast_terms(p_ref[0, :, rows], sel_ref[0, d])
        neg_g_b, w_inter_b, e_b, neg_mt_b = [w[:, i * LANES:(i + 1) * LANES] for i in range(len(TERM_PIECES))]
        a_row = a_ref[0, c, pl.ds(d * M_HEADS + hi, 1), :]
        mask = (s_idx <= t_idx) if d == 0 else (s_idx >= t_idx)
        w_intra = jnp.exp2(jnp.where(mask, a_row + _twice(neg_g_b), NEG_BIG))
        s = _dot_nt(q, k) * w_intra
        num = (_twice(w_inter_b) * jnp.dot(q, ct_ref[d].astype(BF16), preferred_element_type=F32)
               + jnp.dot(s.astype(BF16), v, preferred_element_type=F32))
        den_b = (w_inter_b * jnp.sum(q.astype(F32) * n_ref[d], axis=-1, keepdims=True)
                 + jnp.sum(s, axis=-1, keepdims=True))
        h = num * _twice(1.0 / jnp.maximum(jnp.abs(den_b), jnp.exp2(neg_mt_b)))
        update(k, v, e_b, decay_of(ncc + c, d), d)
        return rows, h

    def finalize(rows, h):
        hn = h * lax.rsqrt(jnp.mean(h * h, axis=-1, keepdims=True) + EPS) * hnw_ref[...]
        out_ref[0, rows, :] = (hn * oz_ref[0, rows, :].astype(F32)).astype(out_ref.dtype)

    for i in range(nc):
        for d in range(N_DIR):
            rows, h = run_chunk(i if d == 0 else nc - 1 - i, d)
            if i < nc // 2:
                hs_ref[rows, :] = h
            else:
                finalize(rows, hs_ref[rows, :] + h)


def _term_selectors():
    sel = np.zeros((M_HEADS, N_DIR, LANES, len(TERM_PIECES) * LANES), np.float32)
    first = np.concatenate([[0], np.cumsum(TERM_PIECES)])
    for h in range(M_HEADS):
        for d in range(N_DIR):
            for q in range(len(TERM_PIECES)):
                for j in range(first[q], first[q + 1]):
                    sel[h, d, j * N_DH + d * M_HEADS + h, q * LANES:(q + 1) * LANES] = 1.0
    return jnp.asarray(sel, BF16)


def _mlstm(y, yc, p, a, pc, decay, head_norm_w, chunk, col):
    b, t, _ = y.shape
    tc = yc.shape[1]
    dh = MXU_DIM
    sel = _term_selectors()
    assert t % (2 * chunk) == 0 and tc % chunk == 0 and tc >= chunk
    blk = lambda name: pl.BlockSpec((1, t, dh), lambda i, h, c=col[name] // dh: (i, 0, c + h))
    blkc = lambda col0: pl.BlockSpec((1, tc, dh), lambda i, h, c=col0 // dh: (i, 0, c + h))
    return pl.pallas_call(
        functools.partial(_mlstm_kernel, chunk=chunk),
        grid=(b, M_HEADS),
        in_specs=[pl.BlockSpec(memory_space=pltpu.SMEM),
                  blk("q"), blk("k"), blk("v"), blk("oz"),
                  pl.BlockSpec((1, LANES, t), lambda i, h: (i, 0, 0)),
                  pl.BlockSpec((1, t // chunk, N_DH, chunk), lambda i, h: (i, 0, 0, 0)),
                  blkc(0), blkc(M_HEADS * dh),
                  pl.BlockSpec((1, LANES, tc), lambda i, h: (i, 0, 0)),
                  pl.BlockSpec((1, N_DIR) + sel.shape[2:], lambda i, h: (h, 0, 0, 0)),
                  pl.BlockSpec((1, dh), lambda i, h: (0, h))],
        out_specs=pl.BlockSpec((1, t, dh), lambda i, h: (i, 0, h)),
        out_shape=jax.ShapeDtypeStruct((b, t, M_HEADS * dh), BF16),
        scratch_shapes=[pltpu.VMEM((N_DIR, dh, dh), F32),
                        pltpu.VMEM((N_DIR, 1, dh), F32),
                        pltpu.VMEM((t, dh), F32)],
        compiler_params=_cparams(("arbitrary", "arbitrary"), 48),
        name="mlstm",
    )(decay, y, y, y, y, p, a, yc, yc, pc, sel, head_norm_w)


def _window_bounds(pos, w, length):
    return np.clip(pos - w // 2, 0, length), np.clip(pos + w - w // 2, 0, length)


def _col_pool_matrices(tile):
    pos = np.arange(GRID_W)
    mats = []
    for w in POOL_WINDOWS:
        lo, hi = _window_bounds(pos, w, GRID_W)
        band = ((pos[None, :] >= lo[:, None]) & (pos[None, :] < hi[:, None])).astype(np.float32)
        mats.append(np.kron(np.eye(tile // GRID_W, dtype=np.float32), band))
    return jnp.asarray(np.stack(mats), dtype=BF16)


def _pool_kernel(p_ref, z_ref, mc_ref, pw_ref, ps_ref, out_ref, pad_ref, scale_ref, *, tile, rows):
    t = p_ref.shape[1]
    gd = LANES
    pad = (max(POOL_WINDOWS) // 2) * GRID_W
    n_tiles = t // tile

    @pl.when(pl.program_id(0) == 0)
    def _():
        pad_ref[...] = jnp.zeros(pad_ref.shape, F32)
        tok = lax.broadcasted_iota(jnp.int32, (t, gd), 0)
        r = tok // GRID_W
        c = tok % GRID_W
        for g, w in enumerate(POOL_WINDOWS):
            cnt_r = jnp.minimum(r + (w - w // 2), rows) - jnp.maximum(r - w // 2, 0)
            cnt_c = jnp.minimum(c + (w - w // 2), GRID_W) - jnp.maximum(c - w // 2, 0)
            scale_ref[g] = 1.0 / (cnt_r.astype(F32) * cnt_c.astype(F32))

    def col_sum(g, start, slot):
        cols = pl.ds(g * gd, gd)
        pad_ref[slot, pl.ds(pad + start, tile), :] = jnp.dot(
            mc_ref[g], p_ref[0, pl.ds(start, tile), cols], preferred_element_type=F32)

    def row_sum(g, start, slot):
        w = POOL_WINDOWS[g]
        cols = pl.ds(g * gd, gd)
        buf = pad_ref.at[slot]
        acc = buf[pl.ds(pad + start - (w // 2) * GRID_W, tile), :]
        for j in range(1 - w // 2, w - w // 2):
            acc = acc + buf[pl.ds(pad + start + j * GRID_W, tile), :]
        mixed = acc * scale_ref[g, pl.ds(start, tile), :] - p_ref[0, pl.ds(start, tile), cols].astype(F32)
        mm = jnp.dot(mixed.astype(BF16), pw_ref[g], preferred_element_type=F32)
        zz = z_ref[0, pl.ds(start, tile), cols].astype(F32)
        out_ref[0, pl.ds(start, tile), cols] = (mm * ps_ref[:, cols] * zz).astype(out_ref.dtype)

    for n in range(P_GROUPS + 1):
        def body(i, carry, n=n):
            start = pl.multiple_of(i * tile, tile)
            if n < P_GROUPS:
                col_sum(n, start, n % 2)
            if n > 0:
                row_sum(n - 1, start, (n - 1) % 2)
            return carry

        lax.fori_loop(0, n_tiles, body, 0, unroll=POOL_UNROLL)


def _pool(y, pool_w, pool_scale, p_col0):
    b, t, _ = y.shape
    pw = P_GROUPS * LANES
    tile = MXU_DIM
    pad = (max(POOL_WINDOWS) // 2) * GRID_W
    return pl.pallas_call(
        functools.partial(_pool_kernel, tile=tile, rows=t // GRID_W),
        grid=(b,),
        in_specs=[pl.BlockSpec((1, t, pw), lambda i: (i, 0, p_col0 // pw)),
                  pl.BlockSpec((1, t, pw), lambda i: (i, 0, p_col0 // pw + 1)),
                  pl.BlockSpec((P_GROUPS, tile, tile), lambda i: (0, 0, 0)),
                  pl.BlockSpec((P_GROUPS, LANES, LANES), lambda i: (0, 0, 0)),
                  pl.BlockSpec((1, pw), lambda i: (0, 0))],
        out_specs=pl.BlockSpec((1, t, pw), lambda i: (i, 0, 0)),
        out_shape=jax.ShapeDtypeStruct((b, t, pw), BF16),
        scratch_shapes=[pltpu.VMEM((2, t + 2 * pad, LANES), F32),
                        pltpu.VMEM((P_GROUPS, t, LANES), F32)],
        compiler_params=_cparams(("arbitrary",), 48),
        name="pool",
    )(y, y, _col_pool_matrices(tile), pool_w, pool_scale)


def _merge_kernel(x_ref, ym_ref, yp_ref, gm_ref, gp_ref, gate_ref, wm_ref, wp_ref, wo_ref, fw_ref, o_ref):
    slab = x_ref.shape[0] // MERGE_SLABS
    for r in range(MERGE_SLABS):
        rows = slice(r * slab, (r + 1) * slab)
        y_m = jnp.dot(ym_ref[rows, :], wm_ref[...], preferred_element_type=F32)
        y_p = jnp.dot(yp_ref[rows, :], wp_ref[...], preferred_element_type=F32)
        merged = (_sigmoid_t(gm_ref[rows, :].astype(F32)) * y_m
                  + _sigmoid_t(gp_ref[rows, :].astype(F32)) * y_p)
        upd = jnp.dot(merged.astype(BF16), wo_ref[...], preferred_element_type=F32)
        xo = x_ref[rows, :] + gate_ref[0] * upd
        o_ref[rows, :] = xo * lax.rsqrt(jnp.mean(xo * xo, axis=-1, keepdims=True) + EPS) * fw_ref[...]


def _merge(x2, ym, yp, y, mod, wm, wp, wo, final_w, rows_per_mod, g_col0):
    n, d = x2.shape
    tm = 1024
    per = rows_per_mod // tm
    pw = yp.shape[1]
    full = lambda shape: pl.BlockSpec(shape, lambda i: (0,) * len(shape))
    return pl.pallas_call(
        _merge_kernel,
        grid=(n // tm,),
        in_specs=[pl.BlockSpec((tm, d), lambda i: (i, 0)),
                  pl.BlockSpec((tm, d), lambda i: (i, 0)),
                  pl.BlockSpec((tm, pw), lambda i: (i, 0)),
                  pl.BlockSpec((tm, d), lambda i: (i, g_col0 // d)),
                  pl.BlockSpec((tm, d), lambda i: (i, g_col0 // d + 1)),
                  pl.BlockSpec((1, 1, d), lambda i: (i // per, 0, 2)),
                  full((d, d)), full((pw, d)), full((d, d)), full((1, d))],
        out_specs=pl.BlockSpec((tm, d), lambda i: (i, 0)),
        out_shape=jax.ShapeDtypeStruct((n, d), F32),
        compiler_params=_cparams(("arbitrary",), 48),
        name="merge",
    )(x2, ym, yp, y, y, mod, wm, wp, wo, final_w)


def kernel(x, c, ctx, c_ctx, norm_w, ada_w, ada_b, in_w, gate_b, head_norm_w, pool_w, pool_scale,
           branch_m_w, branch_p_w, out_w, final_norm_w):
    b, t, d = x.shape
    tc = ctx.shape[1]
    depth = norm_w.shape[0]
    assert depth == 1, "context stream update between layers is not implemented"
    mw = M_HEADS * MXU_DIM
    assert d == mw
    g0 = 5 * mw

    mod = _adaln(c, c_ctx, ada_w[0], ada_b[0])

    wt_f32 = jnp.transpose(in_w[0])
    nw = norm_w[0].reshape(1, d)

    g_q, g_k, g_v, g_o, g_z, g_p, g_gm, g_gp = range(8)
    pairs = ((g_q, g_k, "id"), (g_v, g_p, "half_silu"), (g_o, g_z, "gate"), (g_gm, g_gp, "id"))
    col = {name: i * mw for i, name in enumerate(("q", "k", "v", "p", "o", "oz", "g_m", "g_p"))}

    wt, wg, yc, gates_tc = _ctx_proj(ctx.reshape(b * tc, d), mod, b, nw, wt_f32, g0, g_k, g_v)
    y, gates_t = _in_proj(x.reshape(b * t, d), mod, nw, wt, wg, t, "in_proj", pairs)
    y = y.reshape(b, t, -1)
    yc = yc.reshape(b, tc, -1)

    pc, p, a, decay = _gates(gates_t, gates_tc, gate_b[0], b, CHUNK)
    ym = _mlstm(y, yc, p, a, pc, decay, head_norm_w[0].reshape(1, mw), CHUNK, col)
    yp = _pool(y, pool_w[0].astype(BF16), pool_scale[0].reshape(1, -1), col["p"])

    out = _merge(x.reshape(b * t, d), ym.reshape(b * t, mw), yp.reshape(b * t, -1), y.reshape(b * t, -1),
                 mod, branch_m_w[0].astype(BF16), branch_p_w[0].astype(BF16),
                 out_w[0].astype(BF16), final_norm_w.reshape(1, d), t, col["g_m"])
    return out.reshape(b, t, d)
```

```python
import functools

import numpy as np
import jax
import jax.numpy as jnp
from jax import lax
from jax.experimental import pallas as pl
from jax.experimental.pallas import tpu as pltpu

F32 = jnp.float32
BF16 = jnp.bfloat16

EPS = 1e-6
LOG2E = 1.4426950408889634
M_INIT = -1e30
NEG_BIG = -1e30
M_HEADS = 4
N_DIR = 2
N_GATE_COLS = N_DIR * 2 * M_HEADS
P_GROUPS = 4
POOL_WINDOWS = (2, 4, 8, 16)
GRID_W = 64
LANES = 128
MXU_DIM = 256
CHUNK = 256
POOL_UNROLL = 4
MERGE_SLABS = 2
MIB = 1024 * 1024


def _cparams(sem, vmem_mib):
    return pltpu.CompilerParams(dimension_semantics=sem, vmem_limit_bytes=vmem_mib * MIB)


def _sigmoid(x):
    return 1.0 / (1.0 + jnp.exp(-x))


def _silu(x):
    return x * _sigmoid(x)


def _log_sigmoid(x):
    return jnp.minimum(x, 0.0) - jnp.log(1.0 + jnp.exp(-jnp.abs(x)))


def _dot_nt(a, b):
    return lax.dot_general(a, b, (((1,), (1,)), ((), ())), preferred_element_type=F32)


def _adaln_kernel(c_ref, cc_ref, w_ref, b_ref, o_ref):
    b = c_ref.shape[0]
    rows = jnp.concatenate([c_ref[...], jnp.broadcast_to(cc_ref[...], (o_ref.shape[0] - b, c_ref.shape[1]))], axis=0)
    o_ref[:, 0, :] = jnp.dot(_silu(rows), w_ref[...], preferred_element_type=F32) + b_ref[...]


def _adaln(c, c_ctx, ada_w, ada_b):
    b, d = c.shape
    rows = 8 * (b // 8 + 1)
    n = ada_w.shape[1]
    tn = d
    return pl.pallas_call(
        _adaln_kernel,
        grid=(n // tn,),
        in_specs=[pl.BlockSpec((b, d), lambda j: (0, 0)),
                  pl.BlockSpec((1, d), lambda j: (0, 0)),
                  pl.BlockSpec((d, tn), lambda j: (0, j)),
                  pl.BlockSpec((1, tn), lambda j: (0, j))],
        out_specs=pl.BlockSpec((rows, 1, tn), lambda j: (0, 0, j)),
        out_shape=jax.ShapeDtypeStruct((rows, 1, n), F32),
        compiler_params=_cparams(("arbitrary",), 32),
        name="adaln",
    )(c, c_ctx.reshape(1, d), ada_w, ada_b.reshape(1, n))


def _sigmoid_t(x):
    return 0.5 * jnp.tanh(0.5 * x) + 0.5


IN_TILE = 1024
GATE_SUB = 256


def _inproj_kernel(x_ref, sc_ref, sh_ref, nw_ref, wa_ref, wb_ref, wg_ref, y_ref, g_ref, xn_ref, *, acts):
    j = pl.program_id(1)

    @pl.when(j == 0)
    def _():
        x = x_ref[...]
        ms = jnp.mean(x * x, axis=-1, keepdims=True)
        xn = x * lax.rsqrt(ms + EPS) * (nw_ref[...] * (1.0 + sc_ref[0])) + sh_ref[0]
        xn_ref[...] = xn.astype(BF16)
        wg = wg_ref[...]
        wg = jnp.concatenate([wg, jnp.zeros((LANES - wg.shape[0], wg.shape[1]), wg.dtype)], axis=0)
        g_ref[...] = _dot_nt(xn_ref[...], wg).T[:N_GATE_COLS]

    def pair(act):
        tn = IN_TILE
        xn = xn_ref[...]
        if act == "gate":
            for c in range(0, tn, GATE_SUB):
                a = _dot_nt(xn, wa_ref[c:c + GATE_SUB, :])
                b = _dot_nt(xn, wb_ref[c:c + GATE_SUB, :])
                y_ref[:, c:c + GATE_SUB] = a.astype(BF16)
                ab, bb = a.astype(BF16), b.astype(BF16)
                y_ref[:, tn + c:tn + c + GATE_SUB] = _sigmoid_t(ab) * (bb * _sigmoid_t(bb))
            return
        y_ref[:, :tn] = _dot_nt(xn, wa_ref[...]).astype(BF16)
        b = _dot_nt(xn, wb_ref[...])
        if act == "half_silu":
            half = tn // 2
            zb = b[:, half:].astype(BF16)
            y_ref[:, tn:tn + half] = b[:, :half].astype(BF16)
            y_ref[:, tn + half:] = zb * _sigmoid_t(zb)
            return
        y_ref[:, tn:] = b.astype(BF16)

    for act in sorted(set(acts)):
        hit = functools.reduce(jnp.logical_or, [j == s for s, a in enumerate(acts) if a == act])
        pl.when(hit)(functools.partial(pair, act))


def _in_proj(x2, mod, norm_w, w, wg, rows_per_mod, name, pairs):
    n, d = x2.shape
    tm = min(2048, rows_per_mod)
    tn = IN_TILE
    per = rows_per_mod // tm
    acts = tuple(p[2] for p in pairs)

    def w_group(which):
        return lambda i, j: (sum(jnp.where(j == s, p[which], 0) for s, p in enumerate(pairs)), 0)

    return pl.pallas_call(
        functools.partial(_inproj_kernel, acts=acts),
        grid=(n // tm, len(pairs)),
        in_specs=[pl.BlockSpec((tm, d), lambda i, j: (i, 0)),
                  pl.BlockSpec((1, 1, d), lambda i, j: (i // per, 0, 1)),
                  pl.BlockSpec((1, 1, d), lambda i, j: (i // per, 0, 0)),
                  pl.BlockSpec((1, d), lambda i, j: (0, 0)),
                  pl.BlockSpec((tn, d), w_group(0)),
                  pl.BlockSpec((tn, d), w_group(1)),
                  pl.BlockSpec((N_GATE_COLS, d), lambda i, j: (0, 0))],
        out_specs=[pl.BlockSpec((tm, 2 * tn), lambda i, j: (i, j)),
                   pl.BlockSpec((N_GATE_COLS, tm), lambda i, j: (0, i))],
        out_shape=[jax.ShapeDtypeStruct((n, 2 * tn * len(pairs)), BF16),
                   jax.ShapeDtypeStruct((N_GATE_COLS, n), F32)],
        scratch_shapes=[pltpu.VMEM((tm, d), BF16)],
        compiler_params=_cparams(("arbitrary", "arbitrary"), 56),
        name=name,
    )(x2, mod, mod, norm_w, w, w, wg)


def _ctx_proj_kernel(x_ref, sc_ref, sh_ref, nw_ref, wsrc_ref, wgsrc_ref, wdst_ref, wgdst_ref, y_ref, g_ref,
                     xn_ref, kv_ref, *, steps, k_block, v_block):
    i, j = pl.program_id(0), pl.program_id(1)
    group = i * steps + j
    w = wsrc_ref[...]
    w = jnp.where(group == k_block, w * MXU_DIM ** -0.5, w).astype(BF16)
    wdst_ref[...] = w

    @pl.when(group == k_block)
    def _():
        kv_ref[0] = w

    @pl.when(group == v_block)
    def _():
        kv_ref[1] = w

    @pl.when(j == 0)
    def _():
        x = x_ref[...]
        ms = jnp.mean(x * x, axis=-1, keepdims=True)
        xn = x * lax.rsqrt(ms + EPS) * (nw_ref[...] * (1.0 + sc_ref[0])) + sh_ref[0]
        xn_ref[...] = xn.astype(BF16)
        wg = wgsrc_ref[...].astype(BF16)
        wgdst_ref[...] = wg
        wg = jnp.concatenate([wg, jnp.zeros((LANES - wg.shape[0], wg.shape[1]), wg.dtype)], axis=0)
        g_ref[...] = _dot_nt(xn_ref[...], wg).T[:N_GATE_COLS]

    @pl.when(j == k_block)
    def _():
        y_ref[:, :IN_TILE] = _dot_nt(xn_ref[...], kv_ref[0]).astype(BF16)

    @pl.when(j == v_block)
    def _():
        y_ref[:, IN_TILE:] = _dot_nt(xn_ref[...], kv_ref[1]).astype(BF16)


def _ctx_proj(xc, mod, mod_row, norm_w, wt, gate_row0, k_block, v_block):
    n, d = xc.shape
    tn = IN_TILE
    n_groups = (wt.shape[0] - N_GATE_COLS) // tn
    row_tiles = 2
    tm, steps = n // row_tiles, n_groups // row_tiles
    assert gate_row0 % tn == 0 and n_groups % row_tiles == 0 and 0 < k_block < v_block < steps

    def src_rows(i, j):
        r = (i * steps + j) * tn
        return (pl.multiple_of(r + jnp.where(r >= gate_row0, N_GATE_COLS, 0), N_GATE_COLS), 0)

    return pl.pallas_call(
        functools.partial(_ctx_proj_kernel, steps=steps, k_block=k_block, v_block=v_block),
        grid=(row_tiles, steps),
        in_specs=[pl.BlockSpec((tm, d), lambda i, j: (i, 0)),
                  pl.BlockSpec((1, 1, d), lambda i, j: (mod_row, 0, 1)),
                  pl.BlockSpec((1, 1, d), lambda i, j: (mod_row, 0, 0)),
                  pl.BlockSpec((1, d), lambda i, j: (0, 0)),
                  pl.BlockSpec((pl.Element(tn), pl.Element(d)), src_rows),
                  pl.BlockSpec((pl.Element(N_GATE_COLS), pl.Element(d)), lambda i, j: (gate_row0, 0))],
        out_specs=[pl.BlockSpec((tn, d), lambda i, j: (i * steps + j, 0)),
                   pl.BlockSpec((N_GATE_COLS, d), lambda i, j: (0, 0)),
                   pl.BlockSpec((tm, 2 * tn), lambda i, j: (i, 0)),
                   pl.BlockSpec((N_GATE_COLS, tm), lambda i, j: (0, i))],
        out_shape=[jax.ShapeDtypeStruct((n_groups * tn, d), BF16),
                   jax.ShapeDtypeStruct((N_GATE_COLS, d), BF16),
                   jax.ShapeDtypeStruct((n, 2 * tn), BF16),
                   jax.ShapeDtypeStruct((N_GATE_COLS, n), F32)],
        scratch_shapes=[pltpu.VMEM((tm, d), BF16), pltpu.VMEM((2, tn, d), BF16)],
        compiler_params=_cparams(("arbitrary", "arbitrary"), 56),
        name="ctx_proj",
    )(xc, mod, mod, norm_w, wt, wt)


N_DH = N_DIR * M_HEADS
TERM_PIECES = (3, 2, 2, 3)


def _split(x, pieces):
    out = []
    for _ in range(pieces):
        p = x.astype(BF16).astype(F32)
        out.append(p)
        x = x - p
    return out


def _chunk_scan(x, pos, chunk, op, reverse):
    t = x.shape[1]
    k = 1
    while k < chunk:
        if reverse:
            x = jnp.where(pos < chunk - k, op(x, pltpu.roll(x, t - k, 1)), x)
        else:
            x = jnp.where(pos >= k, op(x, pltpu.roll(x, k, 1)), x)
        k *= 2
    return x


def _gate_terms(li, pre_f, m0, chunk):
    rows, t = li.shape
    nc = t // chunk
    pos = lax.broadcasted_iota(jnp.int32, (rows, t), 1) % chunk
    rev = lax.broadcasted_iota(jnp.int32, (rows, t), 0) % N_DH >= M_HEADS
    rev1 = lax.broadcasted_iota(jnp.int32, (rows, 1), 0) % N_DH >= M_HEADS
    lf = _log_sigmoid(pre_f)
    b = jnp.where(rev, _chunk_scan(lf, pos, chunk, jnp.add, True), _chunk_scan(lf, pos, chunk, jnp.add, False))
    a = li - b
    cmax = jnp.where(rev, _chunk_scan(a, pos, chunk, jnp.maximum, True),
                     _chunk_scan(a, pos, chunk, jnp.maximum, False))

    def at_scan_end(x, c):
        lo = c * chunk
        return jnp.where(rev1, x[:, lo:lo + 1], x[:, lo + chunk - 1:lo + chunk])

    b_end = [at_scan_end(b, c) for c in range(nc)]
    a_max = [at_scan_end(cmax, c) for c in range(nc)]
    m_f, m_b = m0, m0
    m_in_f, m_in_b = [None] * nc, [None] * nc
    for i in range(nc):
        j = nc - 1 - i
        m_in_f[i] = m_f
        m_f = b_end[i] + jnp.maximum(m_f, a_max[i])
        m_in_b[j] = m_b
        m_b = b_end[j] + jnp.maximum(m_b, a_max[j])
    m_in = [jnp.where(rev1, m_in_b[c], m_in_f[c]) for c in range(nc)]
    per_chunk = lambda vals: jnp.concatenate([jnp.broadcast_to(v, (rows, chunk)) for v in vals], axis=1)
    m_in_t = per_chunk(m_in)
    g = jnp.maximum(m_in_t, cmax)
    g_end = [jnp.maximum(m_in[c], a_max[c]) for c in range(nc)]
    pieces = []
    terms = (-g * LOG2E, jnp.exp(m_in_t - g), jnp.exp(a - per_chunk(g_end)), -(b + g) * LOG2E)
    for term, n in zip(terms, TERM_PIECES):
        pieces += _split(term, n)
    decay = jnp.concatenate([jnp.exp(m_in[c] - g_end[c]) for c in range(nc)], axis=1)
    return pieces, a * LOG2E, decay, jnp.where(rev1, m_b, m_f)


def _gates_kernel(gc_ref, g_ref, b_ref, rc_ref, dc_ref, r_ref, a_ref, d_ref, *, chunk):
    nb = r_ref.shape[0]

    def split(ref):
        t = ref.shape[1] // nb
        li, pre_f = [], []
        for e in range(nb):
            pre = ref[:, e * t:(e + 1) * t] + b_ref[...]
            for d in range(N_DIR):
                li.append(pre[2 * d * M_HEADS:(2 * d + 1) * M_HEADS])
                pre_f.append(pre[(2 * d + 1) * M_HEADS:(2 * d + 2) * M_HEADS])
        return jnp.concatenate(li, axis=0), jnp.concatenate(pre_f, axis=0)

    def store_pieces(ref, pieces):
        pad = jnp.zeros((LANES - N_DH * len(pieces), pieces[0].shape[1]), F32)
        for e in range(nb):
            own = [p[e * N_DH:(e + 1) * N_DH] for p in pieces]
            ref[e] = jnp.concatenate(own + [pad], axis=0).astype(BF16)

    m0 = jnp.full((nb * N_DH, 1), M_INIT, F32)
    pieces_c, _, decay_c, m_ctx = _gate_terms(*split(gc_ref), m0, chunk)
    store_pieces(rc_ref, pieces_c)
    pieces, a, decay, _ = _gate_terms(*split(g_ref), m_ctx, chunk)
    store_pieces(r_ref, pieces)
    for e in range(nb):
        own = slice(e * N_DH, (e + 1) * N_DH)
        dc_ref[e] = decay_c[own]
        d_ref[e] = decay[own]
        for c in range(a_ref.shape[1]):
            a_ref[e, c] = a[own, c * chunk:(c + 1) * chunk]


def _gates(gates_t, gates_tc, gate_b, b, chunk):
    t = gates_t.shape[1] // b
    tc = gates_tc.shape[1] // b
    nc, ncc = t // chunk, tc // chunk
    nb = b
    spec = lambda r, n: pl.BlockSpec((nb, r, n), lambda i: (i, 0, 0))
    seq = lambda n: pl.BlockSpec((N_GATE_COLS, nb * n), lambda i: (0, i))
    rc, dc, r, a, dec = pl.pallas_call(
        functools.partial(_gates_kernel, chunk=chunk),
        grid=(b // nb,),
        in_specs=[seq(tc), seq(t), pl.BlockSpec((N_GATE_COLS, 1), lambda i: (0, 0))],
        out_specs=[spec(LANES, tc), spec(N_DH, ncc), spec(LANES, t),
                   pl.BlockSpec((nb, nc, N_DH, chunk), lambda i: (i, 0, 0, 0)), spec(N_DH, nc)],
        out_shape=[jax.ShapeDtypeStruct((b, LANES, tc), BF16), jax.ShapeDtypeStruct((b, N_DH, ncc), F32),
                   jax.ShapeDtypeStruct((b, LANES, t), BF16), jax.ShapeDtypeStruct((b, nc, N_DH, chunk), F32),
                   jax.ShapeDtypeStruct((b, N_DH, nc), F32)],
        compiler_params=_cparams(("arbitrary",), 32),
        name="gates",
    )(gates_tc, gates_t, gate_b.reshape(N_GATE_COLS, 1))
    decay = jnp.concatenate([dc, dec], axis=2).reshape(b * N_DH, ncc + nc)
    return rc, r, a, decay


def _twice(x):
    return jnp.concatenate([x, x], axis=1)


def _mlstm_kernel(dec_ref, q_ref, k_ref, v_ref, oz_ref, p_ref, a_ref, kc_ref, vc_ref, pc_ref,
                  sel_ref, hnw_ref, out_ref, ct_ref, n_ref, hs_ref, *, chunk):
    t = q_ref.shape[1]
    nc = t // chunk
    ncc = kc_ref.shape[1] // chunk
    bi, hi = pl.program_id(0), pl.program_id(1)
    t_idx = lax.broadcasted_iota(jnp.int32, (chunk, chunk), 0)
    s_idx = lax.broadcasted_iota(jnp.int32, (chunk, chunk), 1)

    def bcast_terms(pieces, sel):
        return lax.dot_general(pieces, sel, (((0,), (0,)), ((), ())), preferred_element_type=F32)

    def update(k, v, e_b, decay, d):
        ke = k.astype(F32) * _twice(e_b)
        upd = lax.dot_general(ke.astype(BF16), v, (((0,), (0,)), ((), ())), preferred_element_type=F32)
        n_new = jnp.sum(ke, axis=0, keepdims=True)
        ct_ref[d] = upd if decay is None else decay * ct_ref[d] + upd
        n_ref[d] = n_new if decay is None else decay * n_ref[d] + n_new

    def decay_of(c, d):
        return dec_ref[bi * N_DH + d * M_HEADS + hi, c]

    for d in range(N_DIR):
        for i in range(ncc):
            c = i if d == 0 else ncc - 1 - i
            rows = slice(c * chunk, (c + 1) * chunk)
            e_b = bcast_terms(pc_ref[0, :, rows], sel_ref[0, d, :, 2 * LANES:3 * LANES])
            update(kc_ref[0, rows, :], vc_ref[0, rows, :], e_b, None if i == 0 else decay_of(c, d), d)

    def run_chunk(c, d):
        rows = slice(c * chunk, (c + 1) * chunk)
        q, k, v = q_ref[0, rows, :], k_ref[0, rows, :], v_ref[0, rows, :]
        w = bcast_terms(p_ref[0, :, rows], sel_ref[0, d])
        neg_g_b, w_inter_b, e_b, neg_mt_b = [w[:, i * LANES:(i + 1) * LANES] for i in range(len(TERM_PIECES))]
        a_row = a_ref[0, c, pl.ds(d * M_HEADS + hi, 1), :]
        mask = (s_idx <= t_idx) if d == 0 else (s_idx >= t_idx)
        w_intra = jnp.exp2(jnp.where(mask, a_row + _twice(neg_g_b), NEG_BIG))
        s = _dot_nt(q, k) * w_intra
        num = (_twice(w_inter_b) * jnp.dot(q, ct_ref[d].astype(BF16), preferred_element_type=F32)
               + jnp.dot(s.astype(BF16), v, preferred_element_type=F32))
        den_b = (w_inter_b * jnp.sum(q.astype(F32) * n_ref[d], axis=-1, keepdims=True)
                 + jnp.sum(s, axis=-1, keepdims=True))
        h = num * _twice(1.0 / jnp.maximum(jnp.abs(den_b), jnp.exp2(neg_mt_b)))
        update(k, v, e_b, decay_of(ncc + c, d), d)
        return rows, h

    def finalize(rows, h):
        hn = h * lax.rsqrt(jnp.mean(h * h, axis=-1, keepdims=True) + EPS) * hnw_ref[...]
        out_ref[0, rows, :] = (hn * oz_ref[0, rows, :].astype(F32)).astype(out_ref.dtype)

    for i in range(nc):
        for d in range(N_DIR):
            rows, h = run_chunk(i if d == 0 else nc - 1 - i, d)
            if i < nc // 2:
                hs_ref[rows, :] = h
            else:
                finalize(rows, hs_ref[rows, :] + h)


def _term_selectors():
    sel = np.zeros((M_HEADS, N_DIR, LANES, len(TERM_PIECES) * LANES), np.float32)
    first = np.concatenate([[0], np.cumsum(TERM_PIECES)])
    for h in range(M_HEADS):
        for d in range(N_DIR):
            for q in range(len(TERM_PIECES)):
                for j in range(first[q], first[q + 1]):
                    sel[h, d, j * N_DH + d * M_HEADS + h, q * LANES:(q + 1) * LANES] = 1.0
    return jnp.asarray(sel, BF16)


def _mlstm(y, yc, p, a, pc, decay, head_norm_w, chunk, col):
    b, t, _ = y.shape
    tc = yc.shape[1]
    dh = MXU_DIM
    sel = _term_selectors()
    assert t % (2 * chunk) == 0 and tc % chunk == 0 and tc >= chunk
    blk = lambda name: pl.BlockSpec((1, t, dh), lambda i, h, c=col[name] // dh: (i, 0, c + h))
    blkc = lambda col0: pl.BlockSpec((1, tc, dh), lambda i, h, c=col0 // dh: (i, 0, c + h))
    return pl.pallas_call(
        functools.partial(_mlstm_kernel, chunk=chunk),
        grid=(b, M_HEADS),
        in_specs=[pl.BlockSpec(memory_space=pltpu.SMEM),
                  blk("q"), blk("k"), blk("v"), blk("oz"),
                  pl.BlockSpec((1, LANES, t), lambda i, h: (i, 0, 0)),
                  pl.BlockSpec((1, t // chunk, N_DH, chunk), lambda i, h: (i, 0, 0, 0)),
                  blkc(0), blkc(M_HEADS * dh),
                  pl.BlockSpec((1, LANES, tc), lambda i, h: (i, 0, 0)),
                  pl.BlockSpec((1, N_DIR) + sel.shape[2:], lambda i, h: (h, 0, 0, 0)),
                  pl.BlockSpec((1, dh), lambda i, h: (0, h))],
        out_specs=pl.BlockSpec((1, t, dh), lambda i, h: (i, 0, h)),
        out_shape=jax.ShapeDtypeStruct((b, t, M_HEADS * dh), BF16),
        scratch_shapes=[pltpu.VMEM((N_DIR, dh, dh), F32),
                        pltpu.VMEM((N_DIR, 1, dh), F32),
                        pltpu.VMEM((t, dh), F32)],
        compiler_params=_cparams(("arbitrary", "arbitrary"), 48),
        name="mlstm",
    )(decay, y, y, y, y, p, a, yc, yc, pc, sel, head_norm_w)


def _window_bounds(pos, w, length):
    return np.clip(pos - w // 2, 0, length), np.clip(pos + w - w // 2, 0, length)


def _col_pool_matrices(tile):
    pos = np.arange(GRID_W)
    mats = []
    for w in POOL_WINDOWS:
        lo, hi = _window_bounds(pos, w, GRID_W)
        band = ((pos[None, :] >= lo[:, None]) & (pos[None, :] < hi[:, None])).astype(np.float32)
        mats.append(np.kron(np.eye(tile // GRID_W, dtype=np.float32), band))
    return jnp.asarray(np.stack(mats), dtype=BF16)


def _pool_kernel(p_ref, z_ref, mc_ref, pw_ref, ps_ref, out_ref, pad_ref, scale_ref, *, tile, rows):
    t = p_ref.shape[1]
    gd = LANES
    pad = (max(POOL_WINDOWS) // 2) * GRID_W
    n_tiles = t // tile

    @pl.when(pl.program_id(0) == 0)
    def _():
        pad_ref[...] = jnp.zeros(pad_ref.shape, F32)
        tok = lax.broadcasted_iota(jnp.int32, (t, gd), 0)
        r = tok // GRID_W
        c = tok % GRID_W
        for g, w in enumerate(POOL_WINDOWS):
            cnt_r = jnp.minimum(r + (w - w // 2), rows) - jnp.maximum(r - w // 2, 0)
            cnt_c = jnp.minimum(c + (w - w // 2), GRID_W) - jnp.maximum(c - w // 2, 0)
            scale_ref[g] = 1.0 / (cnt_r.astype(F32) * cnt_c.astype(F32))

    def col_sum(g, start, slot):
        cols = pl.ds(g * gd, gd)
        pad_ref[slot, pl.ds(pad + start, tile), :] = jnp.dot(
            mc_ref[g], p_ref[0, pl.ds(start, tile), cols], preferred_element_type=F32)

    def row_sum(g, start, slot):
        w = POOL_WINDOWS[g]
        cols = pl.ds(g * gd, gd)
        buf = pad_ref.at[slot]
        acc = buf[pl.ds(pad + start - (w // 2) * GRID_W, tile), :]
        for j in range(1 - w // 2, w - w // 2):
            acc = acc + buf[pl.ds(pad + start + j * GRID_W, tile), :]
        mixed = acc * scale_ref[g, pl.ds(start, tile), :] - p_ref[0, pl.ds(start, tile), cols].astype(F32)
        mm = jnp.dot(mixed.astype(BF16), pw_ref[g], preferred_element_type=F32)
        zz = z_ref[0, pl.ds(start, tile), cols].astype(F32)
        out_ref[0, pl.ds(start, tile), cols] = (mm * ps_ref[:, cols] * zz).astype(out_ref.dtype)

    for n in range(P_GROUPS + 1):
        def body(i, carry, n=n):
            start = pl.multiple_of(i * tile, tile)
            if n < P_GROUPS:
                col_sum(n, start, n % 2)
            if n > 0:
                row_sum(n - 1, start, (n - 1) % 2)
            return carry

        lax.fori_loop(0, n_tiles, body, 0, unroll=POOL_UNROLL)


def _pool(y, pool_w, pool_scale, p_col0):
    b, t, _ = y.shape
    pw = P_GROUPS * LANES
    tile = MXU_DIM
    pad = (max(POOL_WINDOWS) // 2) * GRID_W
    return pl.pallas_call(
        functools.partial(_pool_kernel, tile=tile, rows=t // GRID_W),
        grid=(b,),
        in_specs=[pl.BlockSpec((1, t, pw), lambda i: (i, 0, p_col0 // pw)),
                  pl.BlockSpec((1, t, pw), lambda i: (i, 0, p_col0 // pw + 1)),
                  pl.BlockSpec((P_GROUPS, tile, tile), lambda i: (0, 0, 0)),
                  pl.BlockSpec((P_GROUPS, LANES, LANES), lambda i: (0, 0, 0)),
                  pl.BlockSpec((1, pw), lambda i: (0, 0))],
        out_specs=pl.BlockSpec((1, t, pw), lambda i: (i, 0, 0)),
        out_shape=jax.ShapeDtypeStruct((b, t, pw), BF16),
        scratch_shapes=[pltpu.VMEM((2, t + 2 * pad, LANES), F32),
                        pltpu.VMEM((P_GROUPS, t, LANES), F32)],
        compiler_params=_cparams(("arbitrary",), 48),
        name="pool",
    )(y, y, _col_pool_matrices(tile), pool_w, pool_scale)


def _merge_kernel(x_ref, ym_ref, yp_ref, gm_ref, gp_ref, gate_ref, wm_ref, wp_ref, wo_ref, fw_ref, o_ref):
    slab = x_ref.shape[0] // MERGE_SLABS
    for r in range(MERGE_SLABS):
        rows = slice(r * slab, (r + 1) * slab)
        y_m = jnp.dot(ym_ref[rows, :], wm_ref[...], preferred_element_type=F32)
        y_p = jnp.dot(yp_ref[rows, :], wp_ref[...], preferred_element_type=F32)
        merged = (_sigmoid_t(gm_ref[rows, :].astype(F32)) * y_m
                  + _sigmoid_t(gp_ref[rows, :].astype(F32)) * y_p)
        upd = jnp.dot(merged.astype(BF16), wo_ref[...], preferred_element_type=F32)
        xo = x_ref[rows, :] + gate_ref[0] * upd
        o_ref[rows, :] = xo * lax.rsqrt(jnp.mean(xo * xo, axis=-1, keepdims=True) + EPS) * fw_ref[...]


def _merge(x2, ym, yp, y, mod, wm, wp, wo, final_w, rows_per_mod, g_col0):
    n, d = x2.shape
    tm = 1024
    per = rows_per_mod // tm
    pw = yp.shape[1]
    full = lambda shape: pl.BlockSpec(shape, lambda i: (0,) * len(shape))
    return pl.pallas_call(
        _merge_kernel,
        grid=(n // tm,),
        in_specs=[pl.BlockSpec((tm, d), lambda i: (i, 0)),
                  pl.BlockSpec((tm, d), lambda i: (i, 0)),
                  pl.BlockSpec((tm, pw), lambda i: (i, 0)),
                  pl.BlockSpec((tm, d), lambda i: (i, g_col0 // d)),
                  pl.BlockSpec((tm, d), lambda i: (i, g_col0 // d + 1)),
                  pl.BlockSpec((1, 1, d), lambda i: (i // per, 0, 2)),
                  full((d, d)), full((pw, d)), full((d, d)), full((1, d))],
        out_specs=pl.BlockSpec((tm, d), lambda i: (i, 0)),
        out_shape=jax.ShapeDtypeStruct((n, d), F32),
        compiler_params=_cparams(("arbitrary",), 48),
        name="merge",
    )(x2, ym, yp, y, y, mod, wm, wp, wo, final_w)


def kernel(x, c, ctx, c_ctx, norm_w, ada_w, ada_b, in_w, gate_b, head_norm_w, pool_w, pool_scale,
           branch_m_w, branch_p_w, out_w, final_norm_w):
    b, t, d = x.shape
    tc = ctx.shape[1]
    depth = norm_w.shape[0]
    assert depth == 1, "context stream update between layers is not implemented"
    mw = M_HEADS * MXU_DIM
    assert d == mw
    g0 = 5 * mw

    mod = _adaln(c, c_ctx, ada_w[0], ada_b[0])

    wt_f32 = jnp.transpose(in_w[0])
    nw = norm_w[0].reshape(1, d)

    g_q, g_k, g_v, g_o, g_z, g_p, g_gm, g_gp = range(8)
    pairs = ((g_q, g_k, "id"), (g_v, g_p, "half_silu"), (g_o, g_z, "gate"), (g_gm, g_gp, "id"))
    col = {name: i * mw for i, name in enumerate(("q", "k", "v", "p", "o", "oz", "g_m", "g_p"))}

    wt, wg, yc, gates_tc = _ctx_proj(ctx.reshape(b * tc, d), mod, b, nw, wt_f32, g0, g_k, g_v)
    y, gates_t = _in_proj(x.reshape(b * t, d), mod, nw, wt, wg, t, "in_proj", pairs)
    y = y.reshape(b, t, -1)
    yc = yc.reshape(b, tc, -1)

    pc, p, a, decay = _gates(gates_t, gates_tc, gate_b[0], b, CHUNK)
    ym = _mlstm(y, yc, p, a, pc, decay, head_norm_w[0].reshape(1, mw), CHUNK, col)
    yp = _pool(y, pool_w[0].astype(BF16), pool_scale[0].reshape(1, -1), col["p"])

    out = _merge(x.reshape(b * t, d), ym.reshape(b * t, mw), yp.reshape(b * t, -1), y.reshape(b * t, -1),
                 mod, branch_m_w[0].astype(BF16), branch_p_w[0].astype(BF16),
                 out_w[0].astype(BF16), final_norm_w.reshape(1, d), t, col["g_m"])
    return out.reshape(b, t, d)
```

```python
import functools

import numpy as np
import jax
import jax.numpy as jnp
from jax import lax
from jax.experimental import pallas as pl
from jax.experimental.pallas import tpu as pltpu

F32 = jnp.float32
BF16 = jnp.bfloat16

EPS = 1e-6
LOG2E = 1.4426950408889634
M_INIT = -1e30
NEG_BIG = -1e30
M_HEADS = 4
N_DIR = 2
N_GATE_COLS = N_DIR * 2 * M_HEADS
P_GROUPS = 4
POOL_WINDOWS = (2, 4, 8, 16)
GRID_W = 64
LANES = 128
MXU_DIM = 256
CHUNK = 256
POOL_UNROLL = 4
MERGE_SLABS = 2
MIB = 1024 * 1024


def _cparams(sem, vmem_mib):
    return pltpu.CompilerParams(dimension_semantics=sem, vmem_limit_bytes=vmem_mib * MIB)


def _sigmoid(x):
    return 1.0 / (1.0 + jnp.exp(-x))


def _silu(x):
    return x * _sigmoid(x)


def _log_sigmoid(x):
    return jnp.minimum(x, 0.0) - jnp.log(1.0 + jnp.exp(-jnp.abs(x)))


def _dot_nt(a, b):
    return lax.dot_general(a, b, (((1,), (1,)), ((), ())), preferred_element_type=F32)


def _adaln_kernel(c_ref, cc_ref, w_ref, b_ref, o_ref):
    b = c_ref.shape[0]
    rows = jnp.concatenate([c_ref[...], jnp.broadcast_to(cc_ref[...], (o_ref.shape[0] - b, c_ref.shape[1]))], axis=0)
    o_ref[:, 0, :] = jnp.dot(_silu(rows), w_ref[...], preferred_element_type=F32) + b_ref[...]


def _adaln(c, c_ctx, ada_w, ada_b):
    b, d = c.shape
    rows = 8 * (b // 8 + 1)
    n = ada_w.shape[1]
    tn = d
    return pl.pallas_call(
        _adaln_kernel,
        grid=(n // tn,),
        in_specs=[pl.BlockSpec((b, d), lambda j: (0, 0)),
                  pl.BlockSpec((1, d), lambda j: (0, 0)),
                  pl.BlockSpec((d, tn), lambda j: (0, j)),
                  pl.BlockSpec((1, tn), lambda j: (0, j))],
        out_specs=pl.BlockSpec((rows, 1, tn), lambda j: (0, 0, j)),
        out_shape=jax.ShapeDtypeStruct((rows, 1, n), F32),
        compiler_params=_cparams(("arbitrary",), 32),
        name="adaln",
    )(c, c_ctx.reshape(1, d), ada_w, ada_b.reshape(1, n))


def _sigmoid_t(x):
    return 0.5 * jnp.tanh(0.5 * x) + 0.5


IN_TILE = 1024
GATE_SUB = 256


def _inproj_kernel(x_ref, sc_ref, sh_ref, nw_ref, wa_ref, wb_ref, wg_ref, gw_ref, y_ref, g_ref, xn_ref, *, acts):
    j = pl.program_id(1)

    @pl.when(j == 0)
    def _():
        x = x_ref[...]
        ms = jnp.mean(x * x, axis=-1, keepdims=True)
        xn = x * lax.rsqrt(ms + EPS) * (nw_ref[...] * (1.0 + sc_ref[0])) + sh_ref[0]
        xn_ref[...] = xn.astype(BF16)
        wg = wg_ref[...]
        wg = jnp.concatenate([wg, jnp.zeros((LANES - wg.shape[0], wg.shape[1]), wg.dtype)], axis=0)
        g_ref[...] = _dot_nt(xn_ref[...], wg).T[:N_GATE_COLS]

    def pair(act):
        tn = IN_TILE
        xn = xn_ref[...]
        if act == "gate":
            for c in range(0, tn, GATE_SUB):
                a = _dot_nt(xn, wa_ref[c:c + GATE_SUB, :])
                b = _dot_nt(xn, wb_ref[c:c + GATE_SUB, :])
                y_ref[:, c:c + GATE_SUB] = a.astype(BF16)
                ab, bb = a.astype(BF16), b.astype(BF16)
                gate = _sigmoid_t(ab) * (bb * _sigmoid_t(bb))
                y_ref[:, tn + c:tn + c + GATE_SUB] = gate * gw_ref[:, c:c + GATE_SUB]
            return
        y_ref[:, :tn] = _dot_nt(xn, wa_ref[...]).astype(BF16)
        b = _dot_nt(xn, wb_ref[...])
        if act == "half_silu":
            half = tn // 2
            zb = b[:, half:].astype(BF16)
            y_ref[:, tn:tn + half] = b[:, :half].astype(BF16)
            y_ref[:, tn + half:] = zb * _sigmoid_t(zb)
            return
        y_ref[:, tn:] = b.astype(BF16)

    for act in sorted(set(acts)):
        hit = functools.reduce(jnp.logical_or, [j == s for s, a in enumerate(acts) if a == act])
        pl.when(hit)(functools.partial(pair, act))


def _in_proj(x2, mod, norm_w, w, wg, gate_w, rows_per_mod, name, pairs):
    n, d = x2.shape
    tm = min(2048, rows_per_mod)
    tn = IN_TILE
    per = rows_per_mod // tm
    acts = tuple(p[2] for p in pairs)

    def w_group(which):
        return lambda i, j: (sum(jnp.where(j == s, p[which], 0) for s, p in enumerate(pairs)), 0)

    return pl.pallas_call(
        functools.partial(_inproj_kernel, acts=acts),
        grid=(n // tm, len(pairs)),
        in_specs=[pl.BlockSpec((tm, d), lambda i, j: (i, 0)),
                  pl.BlockSpec((1, 1, d), lambda i, j: (i // per, 0, 1)),
                  pl.BlockSpec((1, 1, d), lambda i, j: (i // per, 0, 0)),
                  pl.BlockSpec((1, d), lambda i, j: (0, 0)),
                  pl.BlockSpec((tn, d), w_group(0)),
                  pl.BlockSpec((tn, d), w_group(1)),
                  pl.BlockSpec((N_GATE_COLS, d), lambda i, j: (0, 0)),
                  pl.BlockSpec((1, tn), lambda i, j: (0, 0))],
        out_specs=[pl.BlockSpec((tm, 2 * tn), lambda i, j: (i, j)),
                   pl.BlockSpec((N_GATE_COLS, tm), lambda i, j: (0, i))],
        out_shape=[jax.ShapeDtypeStruct((n, 2 * tn * len(pairs)), BF16),
                   jax.ShapeDtypeStruct((N_GATE_COLS, n), F32)],
        scratch_shapes=[pltpu.VMEM((tm, d), BF16)],
        compiler_params=_cparams(("arbitrary", "arbitrary"), 56),
        name=name,
    )(x2, mod, mod, norm_w, w, w, wg, gate_w)


def _ctx_proj_kernel(x_ref, sc_ref, sh_ref, nw_ref, wsrc_ref, wgsrc_ref, wdst_ref, wgdst_ref, y_ref, g_ref,
                     xn_ref, kv_ref, *, steps, k_block, v_block):
    i, j = pl.program_id(0), pl.program_id(1)
    group = i * steps + j
    w = wsrc_ref[...]
    w = jnp.where(group == k_block, w * MXU_DIM ** -0.5, w).astype(BF16)
    wdst_ref[...] = w

    @pl.when(group == k_block)
    def _():
        kv_ref[0] = w

    @pl.when(group == v_block)
    def _():
        kv_ref[1] = w

    @pl.when(j == 0)
    def _():
        x = x_ref[...]
        ms = jnp.mean(x * x, axis=-1, keepdims=True)
        xn = x * lax.rsqrt(ms + EPS) * (nw_ref[...] * (1.0 + sc_ref[0])) + sh_ref[0]
        xn_ref[...] = xn.astype(BF16)
        wg = wgsrc_ref[...].astype(BF16)
        wgdst_ref[...] = wg
        wg = jnp.concatenate([wg, jnp.zeros((LANES - wg.shape[0], wg.shape[1]), wg.dtype)], axis=0)
        g_ref[...] = _dot_nt(xn_ref[...], wg).T[:N_GATE_COLS]

    @pl.when(j == k_block)
    def _():
        y_ref[:, :IN_TILE] = _dot_nt(xn_ref[...], kv_ref[0]).astype(BF16)

    @pl.when(j == v_block)
    def _():
        y_ref[:, IN_TILE:] = _dot_nt(xn_ref[...], kv_ref[1]).astype(BF16)


def _ctx_proj(xc, mod, mod_row, norm_w, wt, gate_row0, k_block, v_block):
    n, d = xc.shape
    tn = IN_TILE
    n_groups = (wt.shape[0] - N_GATE_COLS) // tn
    row_tiles = 2
    tm, steps = n // row_tiles, n_groups // row_tiles
    assert gate_row0 % tn == 0 and n_groups % row_tiles == 0 and 0 < k_block < v_block < steps

    def src_rows(i, j):
        r = (i * steps + j) * tn
        return (pl.multiple_of(r + jnp.where(r >= gate_row0, N_GATE_COLS, 0), N_GATE_COLS), 0)

    return pl.pallas_call(
        functools.partial(_ctx_proj_kernel, steps=steps, k_block=k_block, v_block=v_block),
        grid=(row_tiles, steps),
        in_specs=[pl.BlockSpec((tm, d), lambda i, j: (i, 0)),
                  pl.BlockSpec((1, 1, d), lambda i, j: (mod_row, 0, 1)),
                  pl.BlockSpec((1, 1, d), lambda i, j: (mod_row, 0, 0)),
                  pl.BlockSpec((1, d), lambda i, j: (0, 0)),
                  pl.BlockSpec((pl.Element(tn), pl.Element(d)), src_rows),
                  pl.BlockSpec((pl.Element(N_GATE_COLS), pl.Element(d)), lambda i, j: (gate_row0, 0))],
        out_specs=[pl.BlockSpec((tn, d), lambda i, j: (i * steps + j, 0)),
                   pl.BlockSpec((N_GATE_COLS, d), lambda i, j: (0, 0)),
                   pl.BlockSpec((tm, 2 * tn), lambda i, j: (i, 0)),
                   pl.BlockSpec((N_GATE_COLS, tm), lambda i, j: (0, i))],
        out_shape=[jax.ShapeDtypeStruct((n_groups * tn, d), BF16),
                   jax.ShapeDtypeStruct((N_GATE_COLS, d), BF16),
                   jax.ShapeDtypeStruct((n, 2 * tn), BF16),
                   jax.ShapeDtypeStruct((N_GATE_COLS, n), F32)],
        scratch_shapes=[pltpu.VMEM((tm, d), BF16), pltpu.VMEM((2, tn, d), BF16)],
        compiler_params=_cparams(("arbitrary", "arbitrary"), 56),
        name="ctx_proj",
    )(xc, mod, mod, norm_w, wt, wt)


N_DH = N_DIR * M_HEADS
TERM_PIECES = (3, 2, 2, 3)


def _split(x, pieces):
    out = []
    for _ in range(pieces):
        p = x.astype(BF16).astype(F32)
        out.append(p)
        x = x - p
    return out


def _chunk_scan(x, pos, chunk, op, reverse):
    t = x.shape[1]
    k = 1
    while k < chunk:
        if reverse:
            x = jnp.where(pos < chunk - k, op(x, pltpu.roll(x, t - k, 1)), x)
        else:
            x = jnp.where(pos >= k, op(x, pltpu.roll(x, k, 1)), x)
        k *= 2
    return x


def _gate_terms(li, pre_f, m0, chunk):
    rows, t = li.shape
    nc = t // chunk
    pos = lax.broadcasted_iota(jnp.int32, (rows, t), 1) % chunk
    rev = lax.broadcasted_iota(jnp.int32, (rows, t), 0) % N_DH >= M_HEADS
    rev1 = lax.broadcasted_iota(jnp.int32, (rows, 1), 0) % N_DH >= M_HEADS
    lf = _log_sigmoid(pre_f)
    b = jnp.where(rev, _chunk_scan(lf, pos, chunk, jnp.add, True), _chunk_scan(lf, pos, chunk, jnp.add, False))
    a = li - b
    cmax = jnp.where(rev, _chunk_scan(a, pos, chunk, jnp.maximum, True),
                     _chunk_scan(a, pos, chunk, jnp.maximum, False))

    def at_scan_end(x, c):
        lo = c * chunk
        return jnp.where(rev1, x[:, lo:lo + 1], x[:, lo + chunk - 1:lo + chunk])

    b_end = [at_scan_end(b, c) for c in range(nc)]
    a_max = [at_scan_end(cmax, c) for c in range(nc)]
    m_f, m_b = m0, m0
    m_in_f, m_in_b = [None] * nc, [None] * nc
    for i in range(nc):
        j = nc - 1 - i
        m_in_f[i] = m_f
        m_f = b_end[i] + jnp.maximum(m_f, a_max[i])
        m_in_b[j] = m_b
        m_b = b_end[j] + jnp.maximum(m_b, a_max[j])
    m_in = [jnp.where(rev1, m_in_b[c], m_in_f[c]) for c in range(nc)]
    per_chunk = lambda vals: jnp.concatenate([jnp.broadcast_to(v, (rows, chunk)) for v in vals], axis=1)
    m_in_t = per_chunk(m_in)
    g = jnp.maximum(m_in_t, cmax)
    g_end = [jnp.maximum(m_in[c], a_max[c]) for c in range(nc)]
    pieces = []
    terms = (-g * LOG2E, jnp.exp(m_in_t - g), jnp.exp(a - per_chunk(g_end)), -(b + g) * LOG2E)
    for term, n in zip(terms, TERM_PIECES):
        pieces += _split(term, n)
    decay = jnp.concatenate([jnp.exp(m_in[c] - g_end[c]) for c in range(nc)], axis=1)
    return pieces, a * LOG2E, decay, jnp.where(rev1, m_b, m_f)


def _gates_kernel(gc_ref, g_ref, b_ref, rc_ref, dc_ref, r_ref, a_ref, d_ref, *, chunk):
    nb = r_ref.shape[0]

    def split(ref):
        t = ref.shape[1] // nb
        li, pre_f = [], []
        for e in range(nb):
            pre = ref[:, e * t:(e + 1) * t] + b_ref[...]
            for d in range(N_DIR):
                li.append(pre[2 * d * M_HEADS:(2 * d + 1) * M_HEADS])
                pre_f.append(pre[(2 * d + 1) * M_HEADS:(2 * d + 2) * M_HEADS])
        return jnp.concatenate(li, axis=0), jnp.concatenate(pre_f, axis=0)

    def store_pieces(ref, pieces):
        pad = jnp.zeros((LANES - N_DH * len(pieces), pieces[0].shape[1]), F32)
        for e in range(nb):
            own = [p[e * N_DH:(e + 1) * N_DH] for p in pieces]
            ref[e] = jnp.concatenate(own + [pad], axis=0).astype(BF16)

    m0 = jnp.full((nb * N_DH, 1), M_INIT, F32)
    pieces_c, _, decay_c, m_ctx = _gate_terms(*split(gc_ref), m0, chunk)
    store_pieces(rc_ref, pieces_c)
    pieces, a, decay, _ = _gate_terms(*split(g_ref), m_ctx, chunk)
    store_pieces(r_ref, pieces)
    for e in range(nb):
        own = slice(e * N_DH, (e + 1) * N_DH)
        dc_ref[e] = decay_c[own]
        d_ref[e] = decay[own]
        for c in range(a_ref.shape[1]):
            a_ref[e, c] = a[own, c * chunk:(c + 1) * chunk]


def _gates(gates_t, gates_tc, gate_b, b, chunk):
    t = gates_t.shape[1] // b
    tc = gates_tc.shape[1] // b
    nc, ncc = t // chunk, tc // chunk
    nb = b
    spec = lambda r, n: pl.BlockSpec((nb, r, n), lambda i: (i, 0, 0))
    seq = lambda n: pl.BlockSpec((N_GATE_COLS, nb * n), lambda i: (0, i))
    rc, dc, r, a, dec = pl.pallas_call(
        functools.partial(_gates_kernel, chunk=chunk),
        grid=(b // nb,),
        in_specs=[seq(tc), seq(t), pl.BlockSpec((N_GATE_COLS, 1), lambda i: (0, 0))],
        out_specs=[spec(LANES, tc), spec(N_DH, ncc), spec(LANES, t),
                   pl.BlockSpec((nb, nc, N_DH, chunk), lambda i: (i, 0, 0, 0)), spec(N_DH, nc)],
        out_shape=[jax.ShapeDtypeStruct((b, LANES, tc), BF16), jax.ShapeDtypeStruct((b, N_DH, ncc), F32),
                   jax.ShapeDtypeStruct((b, LANES, t), BF16), jax.ShapeDtypeStruct((b, nc, N_DH, chunk), F32),
                   jax.ShapeDtypeStruct((b, N_DH, nc), F32)],
        compiler_params=_cparams(("arbitrary",), 32),
        name="gates",
    )(gates_tc, gates_t, gate_b.reshape(N_GATE_COLS, 1))
    decay = jnp.concatenate([dc, dec], axis=2).reshape(b * N_DH, ncc + nc)
    return rc, r, a, decay


def _twice(x):
    return jnp.concatenate([x, x], axis=1)


def _mlstm_kernel(dec_ref, q_ref, k_ref, v_ref, oz_ref, p_ref, a_ref, kc_ref, vc_ref, pc_ref,
                  sel_ref, out_ref, ct_ref, n_ref, hs_ref, *, chunk):
    t = q_ref.shape[1]
    nc = t // chunk
    ncc = kc_ref.shape[1] // chunk
    bi, hi = pl.program_id(0), pl.program_id(1)
    t_idx = lax.broadcasted_iota(jnp.int32, (chunk, chunk), 0)
    s_idx = lax.broadcasted_iota(jnp.int32, (chunk, chunk), 1)

    def bcast_terms(pieces, sel):
        return lax.dot_general(pieces, sel, (((0,), (0,)), ((), ())), preferred_element_type=F32)

    def update(k, v, e_b, decay, d):
        ke = k.astype(F32) * _twice(e_b)
        upd = lax.dot_general(ke.astype(BF16), v, (((0,), (0,)), ((), ())), preferred_element_type=F32)
        n_new = jnp.sum(ke, axis=0, keepdims=True)
        ct_ref[d] = upd if decay is None else decay * ct_ref[d] + upd
        n_ref[d] = n_new if decay is None else decay * n_ref[d] + n_new

    def decay_of(c, d):
        return dec_ref[bi * N_DH + d * M_HEADS + hi, c]

    for d in range(N_DIR):
        for i in range(ncc):
            c = i if d == 0 else ncc - 1 - i
            rows = slice(c * chunk, (c + 1) * chunk)
            e_b = bcast_terms(pc_ref[0, :, rows], sel_ref[0, d, :, 2 * LANES:3 * LANES])
            update(kc_ref[0, rows, :], vc_ref[0, rows, :], e_b, None if i == 0 else decay_of(c, d), d)

    def run_chunk(c, d):
        rows = slice(c * chunk, (c + 1) * chunk)
        q, k, v = q_ref[0, rows, :], k_ref[0, rows, :], v_ref[0, rows, :]
        w = bcast_terms(p_ref[0, :, rows], sel_ref[0, d])
        neg_g_b, w_inter_b, e_b, neg_mt_b = [w[:, i * LANES:(i + 1) * LANES] for i in range(len(TERM_PIECES))]
        a_row = a_ref[0, c, pl.ds(d * M_HEADS + hi, 1), :]
        mask = (s_idx <= t_idx) if d == 0 else (s_idx >= t_idx)
        w_intra = jnp.exp2(jnp.where(mask, a_row + _twice(neg_g_b), NEG_BIG))
        s = _dot_nt(q, k) * w_intra
        num = (_twice(w_inter_b) * jnp.dot(q, ct_ref[d].astype(BF16), preferred_element_type=F32)
               + jnp.dot(s.astype(BF16), v, preferred_element_type=F32))
        den_b = (w_inter_b * jnp.sum(q.astype(F32) * n_ref[d], axis=-1, keepdims=True)
                 + jnp.sum(s, axis=-1, keepdims=True))
        h = num * _twice(1.0 / jnp.maximum(jnp.abs(den_b), jnp.exp2(neg_mt_b)))
        update(k, v, e_b, decay_of(ncc + c, d), d)
        return rows, h

    def finalize(rows, h):
        hn = h * lax.rsqrt(jnp.mean(h * h, axis=-1, keepdims=True) + EPS)
        out_ref[0, rows, :] = hn.astype(BF16) * oz_ref[0, rows, :]

    for i in range(nc):
        for d in range(N_DIR):
            rows, h = run_chunk(i if d == 0 else nc - 1 - i, d)
            if i < nc // 2:
                hs_ref[rows, :] = h
            else:
                finalize(rows, hs_ref[rows, :] + h)


def _term_selectors():
    sel = np.zeros((M_HEADS, N_DIR, LANES, len(TERM_PIECES) * LANES), np.float32)
    first = np.concatenate([[0], np.cumsum(TERM_PIECES)])
    for h in range(M_HEADS):
        for d in range(N_DIR):
            for q in range(len(TERM_PIECES)):
                for j in range(first[q], first[q + 1]):
                    sel[h, d, j * N_DH + d * M_HEADS + h, q * LANES:(q + 1) * LANES] = 1.0
    return jnp.asarray(sel, BF16)


def _mlstm(y, yc, p, a, pc, decay, chunk, col):
    b, t, _ = y.shape
    tc = yc.shape[1]
    dh = MXU_DIM
    sel = _term_selectors()
    assert t % (2 * chunk) == 0 and tc % chunk == 0 and tc >= chunk
    blk = lambda name: pl.BlockSpec((1, t, dh), lambda i, h, c=col[name] // dh: (i, 0, c + h))
    blkc = lambda col0: pl.BlockSpec((1, tc, dh), lambda i, h, c=col0 // dh: (i, 0, c + h))
    return pl.pallas_call(
        functools.partial(_mlstm_kernel, chunk=chunk),
        grid=(b, M_HEADS),
        in_specs=[pl.BlockSpec(memory_space=pltpu.SMEM),
                  blk("q"), blk("k"), blk("v"), blk("oz"),
                  pl.BlockSpec((1, LANES, t), lambda i, h: (i, 0, 0)),
                  pl.BlockSpec((1, t // chunk, N_DH, chunk), lambda i, h: (i, 0, 0, 0)),
                  blkc(0), blkc(M_HEADS * dh),
                  pl.BlockSpec((1, LANES, tc), lambda i, h: (i, 0, 0)),
                  pl.BlockSpec((1, N_DIR) + sel.shape[2:], lambda i, h: (h, 0, 0, 0))],
        out_specs=pl.BlockSpec((1, t, dh), lambda i, h: (i, 0, h)),
        out_shape=jax.ShapeDtypeStruct((b, t, M_HEADS * dh), BF16),
        scratch_shapes=[pltpu.VMEM((N_DIR, dh, dh), F32),
                        pltpu.VMEM((N_DIR, 1, dh), F32),
                        pltpu.VMEM((t, dh), F32)],
        compiler_params=_cparams(("arbitrary", "arbitrary"), 48),
        name="mlstm",
    )(decay, y, y, y, y, p, a, yc, yc, pc, sel)


def _window_bounds(pos, w, length):
    return np.clip(pos - w // 2, 0, length), np.clip(pos + w - w // 2, 0, length)


def _col_pool_matrices(tile):
    pos = np.arange(GRID_W)
    mats = []
    for w in POOL_WINDOWS:
        lo, hi = _window_bounds(pos, w, GRID_W)
        band = ((pos[None, :] >= lo[:, None]) & (pos[None, :] < hi[:, None])).astype(np.float32)
        mats.append(np.kron(np.eye(tile // GRID_W, dtype=np.float32), band))
    return jnp.asarray(np.stack(mats), dtype=BF16)


def _pool_kernel(p_ref, z_ref, mc_ref, pw_ref, ps_ref, out_ref, pad_ref, scale_ref, *, tile, rows):
    t = p_ref.shape[1]
    gd = LANES
    pad = (max(POOL_WINDOWS) // 2) * GRID_W
    n_tiles = t // tile

    @pl.when(pl.program_id(0) == 0)
    def _():
        pad_ref[...] = jnp.zeros(pad_ref.shape, F32)
        tok = lax.broadcasted_iota(jnp.int32, (t, gd), 0)
        r = tok // GRID_W
        c = tok % GRID_W
        for g, w in enumerate(POOL_WINDOWS):
            cnt_r = jnp.minimum(r + (w - w // 2), rows) - jnp.maximum(r - w // 2, 0)
            cnt_c = jnp.minimum(c + (w - w // 2), GRID_W) - jnp.maximum(c - w // 2, 0)
            scale_ref[g] = 1.0 / (cnt_r.astype(F32) * cnt_c.astype(F32))

    def col_sum(g, start, slot):
        cols = pl.ds(g * gd, gd)
        pad_ref[slot, pl.ds(pad + start, tile), :] = jnp.dot(
            mc_ref[g], p_ref[0, pl.ds(start, tile), cols], preferred_element_type=F32)

    def row_sum(g, start, slot):
        w = POOL_WINDOWS[g]
        cols = pl.ds(g * gd, gd)
        buf = pad_ref.at[slot]
        acc = buf[pl.ds(pad + start - (w // 2) * GRID_W, tile), :]
        for j in range(1 - w // 2, w - w // 2):
            acc = acc + buf[pl.ds(pad + start + j * GRID_W, tile), :]
        mixed = acc * scale_ref[g, pl.ds(start, tile), :] - p_ref[0, pl.ds(start, tile), cols].astype(F32)
        mm = jnp.dot(mixed.astype(BF16), pw_ref[g], preferred_element_type=F32)
        zz = z_ref[0, pl.ds(start, tile), cols].astype(F32)
        out_ref[0, pl.ds(start, tile), cols] = (mm * ps_ref[:, cols] * zz).astype(out_ref.dtype)

    for n in range(P_GROUPS + 1):
        def body(i, carry, n=n):
            start = pl.multiple_of(i * tile, tile)
            if n < P_GROUPS:
                col_sum(n, start, n % 2)
            if n > 0:
                row_sum(n - 1, start, (n - 1) % 2)
            return carry

        lax.fori_loop(0, n_tiles, body, 0, unroll=POOL_UNROLL)


def _pool(y, pool_w, pool_scale, p_col0):
    b, t, _ = y.shape
    pw = P_GROUPS * LANES
    tile = MXU_DIM
    pad = (max(POOL_WINDOWS) // 2) * GRID_W
    return pl.pallas_call(
        functools.partial(_pool_kernel, tile=tile, rows=t // GRID_W),
        grid=(b,),
        in_specs=[pl.BlockSpec((1, t, pw), lambda i: (i, 0, p_col0 // pw)),
                  pl.BlockSpec((1, t, pw), lambda i: (i, 0, p_col0 // pw + 1)),
                  pl.BlockSpec((P_GROUPS, tile, tile), lambda i: (0, 0, 0)),
                  pl.BlockSpec((P_GROUPS, LANES, LANES), lambda i: (0, 0, 0)),
                  pl.BlockSpec((1, pw), lambda i: (0, 0))],
        out_specs=pl.BlockSpec((1, t, pw), lambda i: (i, 0, 0)),
        out_shape=jax.ShapeDtypeStruct((b, t, pw), BF16),
        scratch_shapes=[pltpu.VMEM((2, t + 2 * pad, LANES), F32),
                        pltpu.VMEM((P_GROUPS, t, LANES), F32)],
        compiler_params=_cparams(("arbitrary",), 48),
        name="pool",
    )(y, y, _col_pool_matrices(tile), pool_w, pool_scale)


def _merge_kernel(x_ref, ym_ref, yp_ref, gm_ref, gp_ref, gate_ref, wm_ref, wp_ref, wo_ref, fw_ref, o_ref):
    slab = x_ref.shape[0] // MERGE_SLABS
    for r in range(MERGE_SLABS):
        rows = slice(r * slab, (r + 1) * slab)
        y_m = jnp.dot(ym_ref[rows, :], wm_ref[...], preferred_element_type=F32)
        y_p = jnp.dot(yp_ref[rows, :], wp_ref[...], preferred_element_type=F32)
        merged = (_sigmoid_t(gm_ref[rows, :].astype(F32)) * y_m
                  + _sigmoid_t(gp_ref[rows, :].astype(F32)) * y_p)
        upd = jnp.dot(merged.astype(BF16), wo_ref[...], preferred_element_type=F32)
        xo = x_ref[rows, :] + gate_ref[0] * upd
        o_ref[rows, :] = xo * lax.rsqrt(jnp.mean(xo * xo, axis=-1, keepdims=True) + EPS) * fw_ref[...]


def _merge(x2, ym, yp, y, mod, wm, wp, wo, final_w, rows_per_mod, g_col0):
    n, d = x2.shape
    tm = 1024
    per = rows_per_mod // tm
    pw = yp.shape[1]
    full = lambda shape: pl.BlockSpec(shape, lambda i: (0,) * len(shape))
    return pl.pallas_call(
        _merge_kernel,
        grid=(n // tm,),
        in_specs=[pl.BlockSpec((tm, d), lambda i: (i, 0)),
                  pl.BlockSpec((tm, d), lambda i: (i, 0)),
                  pl.BlockSpec((tm, pw), lambda i: (i, 0)),
                  pl.BlockSpec((tm, d), lambda i: (i, g_col0 // d)),
                  pl.BlockSpec((tm, d), lambda i: (i, g_col0 // d + 1)),
                  pl.BlockSpec((1, 1, d), lambda i: (i // per, 0, 2)),
                  full((d, d)), full((pw, d)), full((d, d)), full((1, d))],
        out_specs=pl.BlockSpec((tm, d), lambda i: (i, 0)),
        out_shape=jax.ShapeDtypeStruct((n, d), F32),
        compiler_params=_cparams(("arbitrary",), 48),
        name="merge",
    )(x2, ym, yp, y, y, mod, wm, wp, wo, final_w)


def kernel(x, c, ctx, c_ctx, norm_w, ada_w, ada_b, in_w, gate_b, head_norm_w, pool_w, pool_scale,
           branch_m_w, branch_p_w, out_w, final_norm_w):
    b, t, d = x.shape
    tc = ctx.shape[1]
    depth = norm_w.shape[0]
    assert depth == 1, "context stream update between layers is not implemented"
    mw = M_HEADS * MXU_DIM
    assert d == mw
    g0 = 5 * mw

    mod = _adaln(c, c_ctx, ada_w[0], ada_b[0])

    wt_f32 = jnp.transpose(in_w[0])
    nw = norm_w[0].reshape(1, d)

    g_q, g_k, g_v, g_o, g_z, g_p, g_gm, g_gp = range(8)
    pairs = ((g_q, g_k, "id"), (g_v, g_p, "half_silu"), (g_o, g_z, "gate"), (g_gm, g_gp, "id"))
    col = {name: i * mw for i, name in enumerate(("q", "k", "v", "p", "o", "oz", "g_m", "g_p"))}

    wt, wg, yc, gates_tc = _ctx_proj(ctx.reshape(b * tc, d), mod, b, nw, wt_f32, g0, g_k, g_v)
    y, gates_t = _in_proj(x.reshape(b * t, d), mod, nw, wt, wg, head_norm_w[0].reshape(1, mw).astype(BF16),
                          t, "in_proj", pairs)
    y = y.reshape(b, t, -1)
    yc = yc.reshape(b, tc, -1)

    pc, p, a, decay = _gates(gates_t, gates_tc, gate_b[0], b, CHUNK)
    ym = _mlstm(y, yc, p, a, pc, decay, CHUNK, col)
    yp = _pool(y, pool_w[0].astype(BF16), pool_scale[0].reshape(1, -1), col["p"])

    out = _merge(x.reshape(b * t, d), ym.reshape(b * t, mw), yp.reshape(b * t, -1), y.reshape(b * t, -1),
                 mod, branch_m_w[0].astype(BF16), branch_p_w[0].astype(BF16),
                 out_w[0].astype(BF16), final_norm_w.reshape(1, d), t, col["g_m"])
    return out.reshape(b, t, d)
```

```python
import functools

import numpy as np
import jax
import jax.numpy as jnp
from jax import lax
from jax.experimental import pallas as pl
from jax.experimental.pallas import tpu as pltpu

F32 = jnp.float32
BF16 = jnp.bfloat16

EPS = 1e-6
LOG2E = 1.4426950408889634
M_INIT = -1e30
NEG_BIG = -1e30
M_HEADS = 4
N_DIR = 2
N_GATE_COLS = N_DIR * 2 * M_HEADS
P_GROUPS = 4
POOL_WINDOWS = (2, 4, 8, 16)
GRID_W = 64
LANES = 128
SUBLANES = 8
MXU_DIM = 256
CHUNK = 256
POOL_UNROLL = 4
MERGE_SLABS = 2
MIB = 1024 * 1024


def _cparams(sem, vmem_mib):
    return pltpu.CompilerParams(dimension_semantics=sem, vmem_limit_bytes=vmem_mib * MIB)


def _sigmoid(x):
    return 1.0 / (1.0 + jnp.exp(-x))


def _silu(x):
    return x * _sigmoid(x)


def _log_sigmoid(x):
    return jnp.minimum(x, 0.0) - jnp.log(1.0 + jnp.exp(-jnp.abs(x)))


def _dot_nt(a, b):
    return lax.dot_general(a, b, (((1,), (1,)), ((), ())), preferred_element_type=F32)


def _adaln_kernel(c_ref, cc_ref, w_ref, b_ref, o_ref):
    b = c_ref.shape[0]
    rows = jnp.concatenate([c_ref[...], jnp.broadcast_to(cc_ref[...], (o_ref.shape[0] - b, c_ref.shape[1]))], axis=0)
    o_ref[:, 0, :] = jnp.dot(_silu(rows), w_ref[...], preferred_element_type=F32) + b_ref[...]


def _adaln(c, c_ctx, ada_w, ada_b):
    b, d = c.shape
    rows = SUBLANES * (b // SUBLANES + 1)
    n = ada_w.shape[1]
    tn = d
    return pl.pallas_call(
        _adaln_kernel,
        grid=(n // tn,),
        in_specs=[pl.BlockSpec((b, d), lambda j: (0, 0)),
                  pl.BlockSpec((1, d), lambda j: (0, 0)),
                  pl.BlockSpec((d, tn), lambda j: (0, j)),
                  pl.BlockSpec((1, tn), lambda j: (0, j))],
        out_specs=pl.BlockSpec((rows, 1, tn), lambda j: (0, 0, j)),
        out_shape=jax.ShapeDtypeStruct((rows, 1, n), F32),
        compiler_params=_cparams(("arbitrary",), 32),
        name="adaln",
    )(c, c_ctx.reshape(1, d), ada_w, ada_b.reshape(1, n))


def _sigmoid_t(x):
    return 0.5 * jnp.tanh(0.5 * x) + 0.5


IN_TILE = 1024
GATE_SUB = 256


def _inproj_kernel(x_ref, sc_ref, sh_ref, nw_ref, wa_ref, wb_ref, wg_ref, gw_ref, y_ref, g_ref, xn_ref, *, acts):
    j = pl.program_id(1)

    @pl.when(j == 0)
    def _():
        x = x_ref[...]
        ms = jnp.mean(x * x, axis=-1, keepdims=True)
        xn = x * lax.rsqrt(ms + EPS) * (nw_ref[...] * (1.0 + sc_ref[0])) + sh_ref[0]
        xn_ref[...] = xn.astype(BF16)
        wg = wg_ref[...]
        wg = jnp.concatenate([wg, jnp.zeros((LANES - wg.shape[0], wg.shape[1]), wg.dtype)], axis=0)
        g_ref[...] = _dot_nt(xn_ref[...], wg).T[:N_GATE_COLS]

    def pair(act):
        tn = IN_TILE
        xn = xn_ref[...]
        if act == "gate":
            for c in range(0, tn, GATE_SUB):
                a = _dot_nt(xn, wa_ref[c:c + GATE_SUB, :])
                b = _dot_nt(xn, wb_ref[c:c + GATE_SUB, :])
                y_ref[:, c:c + GATE_SUB] = a.astype(BF16)
                ab, bb = a.astype(BF16), b.astype(BF16)
                gate = _sigmoid_t(ab) * (bb * _sigmoid_t(bb))
                y_ref[:, tn + c:tn + c + GATE_SUB] = gate * gw_ref[:, c:c + GATE_SUB]
            return
        y_ref[:, :tn] = _dot_nt(xn, wa_ref[...]).astype(BF16)
        b = _dot_nt(xn, wb_ref[...])
        if act == "half_silu":
            half = tn // 2
            zb = b[:, half:].astype(BF16)
            y_ref[:, tn:tn + half] = b[:, :half].astype(BF16)
            y_ref[:, tn + half:] = zb * _sigmoid_t(zb)
            return
        y_ref[:, tn:] = b.astype(BF16)

    for act in sorted(set(acts)):
        hit = functools.reduce(jnp.logical_or, [j == s for s, a in enumerate(acts) if a == act])
        pl.when(hit)(functools.partial(pair, act))


def _in_proj(x2, mod, norm_w, w, wg, gate_w, rows_per_mod, name, pairs):
    n, d = x2.shape
    tm = min(2048, rows_per_mod)
    tn = IN_TILE
    per = rows_per_mod // tm
    acts = tuple(p[2] for p in pairs)

    def w_group(which):
        return lambda i, j: (sum(jnp.where(j == s, p[which], 0) for s, p in enumerate(pairs)), 0)

    return pl.pallas_call(
        functools.partial(_inproj_kernel, acts=acts),
        grid=(n // tm, len(pairs)),
        in_specs=[pl.BlockSpec((tm, d), lambda i, j: (i, 0)),
                  pl.BlockSpec((1, 1, d), lambda i, j: (i // per, 0, 1)),
                  pl.BlockSpec((1, 1, d), lambda i, j: (i // per, 0, 0)),
                  pl.BlockSpec((1, d), lambda i, j: (0, 0)),
                  pl.BlockSpec((tn, d), w_group(0)),
                  pl.BlockSpec((tn, d), w_group(1)),
                  pl.BlockSpec((N_GATE_COLS, d), lambda i, j: (0, 0)),
                  pl.BlockSpec((1, tn), lambda i, j: (0, 0))],
        out_specs=[pl.BlockSpec((tm, 2 * tn), lambda i, j: (i, j)),
                   pl.BlockSpec((N_GATE_COLS, tm), lambda i, j: (0, i))],
        out_shape=[jax.ShapeDtypeStruct((n, 2 * tn * len(pairs)), BF16),
                   jax.ShapeDtypeStruct((N_GATE_COLS, n), F32)],
        scratch_shapes=[pltpu.VMEM((tm, d), BF16)],
        compiler_params=_cparams(("arbitrary", "arbitrary"), 56),
        name=name,
    )(x2, mod, mod, norm_w, w, w, wg, gate_w)


def _ctx_proj_kernel(x_ref, sc_ref, sh_ref, nw_ref, wsrc_ref, wgsrc_ref, wdst_ref, wgdst_ref, y_ref, g_ref,
                     xn_ref, kv_ref, *, steps, k_block, v_block):
    i, j = pl.program_id(0), pl.program_id(1)
    group = i * steps + j
    w = wsrc_ref[...]
    w = jnp.where(group == k_block, w * MXU_DIM ** -0.5, w).astype(BF16)
    wdst_ref[...] = w

    @pl.when(group == k_block)
    def _():
        kv_ref[0] = w

    @pl.when(group == v_block)
    def _():
        kv_ref[1] = w

    @pl.when(j == 0)
    def _():
        x = x_ref[...]
        ms = jnp.mean(x * x, axis=-1, keepdims=True)
        xn = x * lax.rsqrt(ms + EPS) * (nw_ref[...] * (1.0 + sc_ref[0])) + sh_ref[0]
        xn_ref[...] = xn.astype(BF16)
        wg = wgsrc_ref[...].astype(BF16)
        wgdst_ref[...] = wg
        wg = jnp.concatenate([wg, jnp.zeros((LANES - wg.shape[0], wg.shape[1]), wg.dtype)], axis=0)
        g_ref[...] = _dot_nt(xn_ref[...], wg).T[:N_GATE_COLS]

    @pl.when(j == k_block)
    def _():
        y_ref[:, :IN_TILE] = _dot_nt(xn_ref[...], kv_ref[0]).astype(BF16)

    @pl.when(j == v_block)
    def _():
        y_ref[:, IN_TILE:] = _dot_nt(xn_ref[...], kv_ref[1]).astype(BF16)


def _ctx_proj(xc, mod, mod_row, norm_w, wt, gate_row0, k_block, v_block):
    n, d = xc.shape
    tn = IN_TILE
    n_groups = (wt.shape[0] - N_GATE_COLS) // tn
    row_tiles = 2
    tm, steps = n // row_tiles, n_groups // row_tiles
    assert gate_row0 % tn == 0 and n_groups % row_tiles == 0 and 0 < k_block < v_block < steps

    def src_rows(i, j):
        r = (i * steps + j) * tn
        return (pl.multiple_of(r + jnp.where(r >= gate_row0, N_GATE_COLS, 0), N_GATE_COLS), 0)

    return pl.pallas_call(
        functools.partial(_ctx_proj_kernel, steps=steps, k_block=k_block, v_block=v_block),
        grid=(row_tiles, steps),
        in_specs=[pl.BlockSpec((tm, d), lambda i, j: (i, 0)),
                  pl.BlockSpec((1, 1, d), lambda i, j: (mod_row, 0, 1)),
                  pl.BlockSpec((1, 1, d), lambda i, j: (mod_row, 0, 0)),
                  pl.BlockSpec((1, d), lambda i, j: (0, 0)),
                  pl.BlockSpec((pl.Element(tn), pl.Element(d)), src_rows),
                  pl.BlockSpec((pl.Element(N_GATE_COLS), pl.Element(d)), lambda i, j: (gate_row0, 0))],
        out_specs=[pl.BlockSpec((tn, d), lambda i, j: (i * steps + j, 0)),
                   pl.BlockSpec((N_GATE_COLS, d), lambda i, j: (0, 0)),
                   pl.BlockSpec((tm, 2 * tn), lambda i, j: (i, 0)),
                   pl.BlockSpec((N_GATE_COLS, tm), lambda i, j: (0, i))],
        out_shape=[jax.ShapeDtypeStruct((n_groups * tn, d), BF16),
                   jax.ShapeDtypeStruct((N_GATE_COLS, d), BF16),
                   jax.ShapeDtypeStruct((n, 2 * tn), BF16),
                   jax.ShapeDtypeStruct((N_GATE_COLS, n), F32)],
        scratch_shapes=[pltpu.VMEM((tm, d), BF16), pltpu.VMEM((2, tn, d), BF16)],
        compiler_params=_cparams(("arbitrary", "arbitrary"), 56),
        name="ctx_proj",
    )(xc, mod, mod, norm_w, wt, wt)


N_DH = N_DIR * M_HEADS
TERM_PIECES = (3, 2, 2, 3)


def _split(x, pieces):
    out = []
    for _ in range(pieces):
        p = x.astype(BF16).astype(F32)
        out.append(p)
        x = x - p
    return out


def _chunk_scan(x, pos, chunk, op, reverse):
    t = x.shape[1]
    k = 1
    while k < chunk:
        if reverse:
            x = jnp.where(pos < chunk - k, op(x, pltpu.roll(x, t - k, 1)), x)
        else:
            x = jnp.where(pos >= k, op(x, pltpu.roll(x, k, 1)), x)
        k *= 2
    return x


def _gate_terms(li, pre_f, m0, chunk):
    rows, t = li.shape
    nc = t // chunk
    pos = lax.broadcasted_iota(jnp.int32, (rows, t), 1) % chunk
    rev = lax.broadcasted_iota(jnp.int32, (rows, t), 0) % N_DH >= M_HEADS
    rev1 = lax.broadcasted_iota(jnp.int32, (rows, 1), 0) % N_DH >= M_HEADS
    lf = _log_sigmoid(pre_f)
    b = jnp.where(rev, _chunk_scan(lf, pos, chunk, jnp.add, True), _chunk_scan(lf, pos, chunk, jnp.add, False))
    a = li - b
    cmax = jnp.where(rev, _chunk_scan(a, pos, chunk, jnp.maximum, True),
                     _chunk_scan(a, pos, chunk, jnp.maximum, False))

    def at_scan_end(x, c):
        lo = c * chunk
        return jnp.where(rev1, x[:, lo:lo + 1], x[:, lo + chunk - 1:lo + chunk])

    b_end = [at_scan_end(b, c) for c in range(nc)]
    a_max = [at_scan_end(cmax, c) for c in range(nc)]
    m_f, m_b = m0, m0
    m_in_f, m_in_b = [None] * nc, [None] * nc
    for i in range(nc):
        j = nc - 1 - i
        m_in_f[i] = m_f
        m_f = b_end[i] + jnp.maximum(m_f, a_max[i])
        m_in_b[j] = m_b
        m_b = b_end[j] + jnp.maximum(m_b, a_max[j])
    m_in = [jnp.where(rev1, m_in_b[c], m_in_f[c]) for c in range(nc)]
    per_chunk = lambda vals: jnp.concatenate([jnp.broadcast_to(v, (rows, chunk)) for v in vals], axis=1)
    m_in_t = per_chunk(m_in)
    g = jnp.maximum(m_in_t, cmax)
    g_end = [jnp.maximum(m_in[c], a_max[c]) for c in range(nc)]
    pieces = []
    terms = (-g * LOG2E, jnp.exp(m_in_t - g), jnp.exp(a - per_chunk(g_end)), -(b + g) * LOG2E)
    for term, n in zip(terms, TERM_PIECES):
        pieces += _split(term, n)
    decay = jnp.concatenate([jnp.exp(m_in[c] - g_end[c]) for c in range(nc)], axis=1)
    return pieces, a * LOG2E, decay, jnp.where(rev1, m_b, m_f)


def _gates_kernel(gc_ref, g_ref, b_ref, rc_ref, dc_ref, r_ref, a_ref, d_ref, *, chunk):
    nb = r_ref.shape[0]

    def split(ref):
        t = ref.shape[1] // nb
        li, pre_f = [], []
        for e in range(nb):
            pre = ref[:, e * t:(e + 1) * t] + b_ref[...]
            for d in range(N_DIR):
                li.append(pre[2 * d * M_HEADS:(2 * d + 1) * M_HEADS])
                pre_f.append(pre[(2 * d + 1) * M_HEADS:(2 * d + 2) * M_HEADS])
        return jnp.concatenate(li, axis=0), jnp.concatenate(pre_f, axis=0)

    def store_pieces(ref, pieces):
        pad = jnp.zeros((LANES - N_DH * len(pieces), pieces[0].shape[1]), F32)
        for e in range(nb):
            own = [p[e * N_DH:(e + 1) * N_DH] for p in pieces]
            ref[e] = jnp.concatenate(own + [pad], axis=0).astype(BF16)

    m0 = jnp.full((nb * N_DH, 1), M_INIT, F32)
    pieces_c, _, decay_c, m_ctx = _gate_terms(*split(gc_ref), m0, chunk)
    store_pieces(rc_ref, pieces_c)
    pieces, a, decay, _ = _gate_terms(*split(g_ref), m_ctx, chunk)
    store_pieces(r_ref, pieces)
    for e in range(nb):
        own = slice(e * N_DH, (e + 1) * N_DH)
        dc_ref[e] = decay_c[own]
        d_ref[e] = decay[own]
        for c in range(a_ref.shape[1]):
            a_ref[e, c] = a[own, c * chunk:(c + 1) * chunk]


def _gates(gates_t, gates_tc, gate_b, b, chunk):
    t = gates_t.shape[1] // b
    tc = gates_tc.shape[1] // b
    nc, ncc = t // chunk, tc // chunk
    nb = b
    spec = lambda r, n: pl.BlockSpec((nb, r, n), lambda i: (i, 0, 0))
    seq = lambda n: pl.BlockSpec((N_GATE_COLS, nb * n), lambda i: (0, i))
    rc, dc, r, a, dec = pl.pallas_call(
        functools.partial(_gates_kernel, chunk=chunk),
        grid=(b // nb,),
        in_specs=[seq(tc), seq(t), pl.BlockSpec((N_GATE_COLS, 1), lambda i: (0, 0))],
        out_specs=[spec(LANES, tc), spec(N_DH, ncc), spec(LANES, t),
                   pl.BlockSpec((nb, nc, N_DH, chunk), lambda i: (i, 0, 0, 0)), spec(N_DH, nc)],
        out_shape=[jax.ShapeDtypeStruct((b, LANES, tc), BF16), jax.ShapeDtypeStruct((b, N_DH, ncc), F32),
                   jax.ShapeDtypeStruct((b, LANES, t), BF16), jax.ShapeDtypeStruct((b, nc, N_DH, chunk), F32),
                   jax.ShapeDtypeStruct((b, N_DH, nc), F32)],
        compiler_params=_cparams(("arbitrary",), 32),
        name="gates",
    )(gates_tc, gates_t, gate_b.reshape(N_GATE_COLS, 1))
    decay = jnp.concatenate([dc, dec], axis=2).reshape(b * N_DH, ncc + nc)
    return rc, r, a, decay


def _twice(x):
    return jnp.concatenate([x, x], axis=1)


def _mlstm_kernel(dec_ref, q_ref, k_ref, v_ref, oz_ref, p_ref, a_ref, kc_ref, vc_ref, pc_ref,
                  sel_ref, out_ref, ct_ref, n_ref, hs_ref, *, chunk):
    t = q_ref.shape[1]
    nc = t // chunk
    ncc = kc_ref.shape[1] // chunk
    bi, hi = pl.program_id(0), pl.program_id(1)
    t_idx = lax.broadcasted_iota(jnp.int32, (chunk, chunk), 0)
    s_idx = lax.broadcasted_iota(jnp.int32, (chunk, chunk), 1)

    def bcast_terms(pieces, sel):
        return lax.dot_general(pieces, sel, (((0,), (0,)), ((), ())), preferred_element_type=F32)

    def update(k, v, e_b, decay, d):
        ke = k.astype(F32) * _twice(e_b)
        upd = lax.dot_general(ke.astype(BF16), v, (((0,), (0,)), ((), ())), preferred_element_type=F32)
        n_new = jnp.sum(ke, axis=0, keepdims=True)
        ct_ref[d] = upd if decay is None else decay * ct_ref[d] + upd
        n_ref[d] = n_new if decay is None else decay * n_ref[d] + n_new

    def decay_of(c, d):
        return dec_ref[bi * N_DH + d * M_HEADS + hi, c]

    for d in range(N_DIR):
        for i in range(ncc):
            c = i if d == 0 else ncc - 1 - i
            rows = slice(c * chunk, (c + 1) * chunk)
            e_b = bcast_terms(pc_ref[0, :, rows], sel_ref[0, d, :, 2 * LANES:3 * LANES])
            update(kc_ref[0, rows, :], vc_ref[0, rows, :], e_b, None if i == 0 else decay_of(c, d), d)

    def run_chunk(c, d):
        rows = slice(c * chunk, (c + 1) * chunk)
        q, k, v = q_ref[0, rows, :], k_ref[0, rows, :], v_ref[0, rows, :]
        w = bcast_terms(p_ref[0, :, rows], sel_ref[0, d])
        neg_g_b, w_inter_b, e_b, neg_mt_b = [w[:, i * LANES:(i + 1) * LANES] for i in range(len(TERM_PIECES))]
        a_row = a_ref[0, c, pl.ds(d * M_HEADS + hi, 1), :]
        mask = (s_idx <= t_idx) if d == 0 else (s_idx >= t_idx)
        w_intra = jnp.exp2(jnp.where(mask, a_row + _twice(neg_g_b), NEG_BIG))
        s = _dot_nt(q, k) * w_intra
        num = (_twice(w_inter_b) * jnp.dot(q, ct_ref[d].astype(BF16), preferred_element_type=F32)
               + jnp.dot(s.astype(BF16), v, preferred_element_type=F32))
        den_b = (w_inter_b * jnp.sum(q.astype(F32) * n_ref[d], axis=-1, keepdims=True)
                 + jnp.sum(s, axis=-1, keepdims=True))
        h = num * _twice(1.0 / jnp.maximum(jnp.abs(den_b), jnp.exp2(neg_mt_b)))
        update(k, v, e_b, decay_of(ncc + c, d), d)
        return rows, h

    def finalize(rows, h):
        hn = h * lax.rsqrt(jnp.mean(h * h, axis=-1, keepdims=True) + EPS)
        out_ref[0, rows, :] = hn.astype(BF16) * oz_ref[0, rows, :]

    for i in range(nc):
        for d in range(N_DIR):
            rows, h = run_chunk(i if d == 0 else nc - 1 - i, d)
            if i < nc // 2:
                hs_ref[rows, :] = h
            else:
                finalize(rows, hs_ref[rows, :] + h)


def _term_selectors():
    sel = np.zeros((M_HEADS, N_DIR, LANES, len(TERM_PIECES) * LANES), np.float32)
    first = np.concatenate([[0], np.cumsum(TERM_PIECES)])
    for h in range(M_HEADS):
        for d in range(N_DIR):
            for q in range(len(TERM_PIECES)):
                for j in range(first[q], first[q + 1]):
                    sel[h, d, j * N_DH + d * M_HEADS + h, q * LANES:(q + 1) * LANES] = 1.0
    return jnp.asarray(sel, BF16)


def _mlstm(y, yc, p, a, pc, decay, chunk, col):
    b, t, _ = y.shape
    tc = yc.shape[1]
    dh = MXU_DIM
    sel = _term_selectors()
    assert t % (2 * chunk) == 0 and tc % chunk == 0 and tc >= chunk
    blk = lambda name: pl.BlockSpec((1, t, dh), lambda i, h, c=col[name] // dh: (i, 0, c + h))
    blkc = lambda col0: pl.BlockSpec((1, tc, dh), lambda i, h, c=col0 // dh: (i, 0, c + h))
    return pl.pallas_call(
        functools.partial(_mlstm_kernel, chunk=chunk),
        grid=(b, M_HEADS),
        in_specs=[pl.BlockSpec(memory_space=pltpu.SMEM),
                  blk("q"), blk("k"), blk("v"), blk("oz"),
                  pl.BlockSpec((1, LANES, t), lambda i, h: (i, 0, 0)),
                  pl.BlockSpec((1, t // chunk, N_DH, chunk), lambda i, h: (i, 0, 0, 0)),
                  blkc(0), blkc(M_HEADS * dh),
                  pl.BlockSpec((1, LANES, tc), lambda i, h: (i, 0, 0)),
                  pl.BlockSpec((1, N_DIR) + sel.shape[2:], lambda i, h: (h, 0, 0, 0))],
        out_specs=pl.BlockSpec((1, t, dh), lambda i, h: (i, 0, h)),
        out_shape=jax.ShapeDtypeStruct((b, t, M_HEADS * dh), BF16),
        scratch_shapes=[pltpu.VMEM((N_DIR, dh, dh), F32),
                        pltpu.VMEM((N_DIR, 1, dh), F32),
                        pltpu.VMEM((t, dh), F32)],
        compiler_params=_cparams(("arbitrary", "arbitrary"), 48),
        name="mlstm",
    )(decay, y, y, y, y, p, a, yc, yc, pc, sel)


def _window_bounds(pos, w, length):
    return np.clip(pos - w // 2, 0, length), np.clip(pos + w - w // 2, 0, length)


def _col_pool_matrices(tile):
    pos = np.arange(GRID_W)
    mats = []
    for w in POOL_WINDOWS:
        lo, hi = _window_bounds(pos, w, GRID_W)
        band = ((pos[None, :] >= lo[:, None]) & (pos[None, :] < hi[:, None])).astype(np.float32)
        mats.append(np.kron(np.eye(tile // GRID_W, dtype=np.float32), band))
    return jnp.asarray(np.stack(mats), dtype=BF16)


def _pool_kernel(p_ref, z_ref, mc_ref, pw_ref, ps_ref, out_ref, pad_ref, scale_ref, *, tile, rows):
    t = p_ref.shape[1]
    gd = LANES
    pad = (max(POOL_WINDOWS) // 2) * GRID_W
    n_tiles = t // tile

    @pl.when(pl.program_id(0) == 0)
    def _():
        pad_ref[...] = jnp.zeros(pad_ref.shape, F32)
        tok = lax.broadcasted_iota(jnp.int32, (t, gd), 0)
        r = tok // GRID_W
        c = tok % GRID_W
        for g, w in enumerate(POOL_WINDOWS):
            cnt_r = jnp.minimum(r + (w - w // 2), rows) - jnp.maximum(r - w // 2, 0)
            cnt_c = jnp.minimum(c + (w - w // 2), GRID_W) - jnp.maximum(c - w // 2, 0)
            scale_ref[g] = 1.0 / (cnt_r.astype(F32) * cnt_c.astype(F32))

    def col_sum(g, start, slot):
        cols = pl.ds(g * gd, gd)
        pad_ref[slot, pl.ds(pad + start, tile), :] = jnp.dot(
            mc_ref[g], p_ref[0, pl.ds(start, tile), cols], preferred_element_type=F32)

    def row_sum(g, start, slot):
        w = POOL_WINDOWS[g]
        cols = pl.ds(g * gd, gd)
        buf = pad_ref.at[slot]
        acc = buf[pl.ds(pad + start - (w // 2) * GRID_W, tile), :]
        for j in range(1 - w // 2, w - w // 2):
            acc = acc + buf[pl.ds(pad + start + j * GRID_W, tile), :]
        mixed = acc * scale_ref[g, pl.ds(start, tile), :] - p_ref[0, pl.ds(start, tile), cols].astype(F32)
        mm = jnp.dot(mixed.astype(BF16), pw_ref[g], preferred_element_type=F32)
        zz = z_ref[0, pl.ds(start, tile), cols].astype(F32)
        out_ref[0, pl.ds(start, tile), cols] = (mm * ps_ref[:, cols] * zz).astype(out_ref.dtype)

    for n in range(P_GROUPS + 1):
        def body(i, carry, n=n):
            start = pl.multiple_of(i * tile, tile)
            if n < P_GROUPS:
                col_sum(n, start, n % 2)
            if n > 0:
                row_sum(n - 1, start, (n - 1) % 2)
            return carry

        lax.fori_loop(0, n_tiles, body, 0, unroll=POOL_UNROLL)


def _pool(y, pool_w, pool_scale, p_col0):
    b, t, _ = y.shape
    pw = P_GROUPS * LANES
    tile = MXU_DIM
    pad = (max(POOL_WINDOWS) // 2) * GRID_W
    return pl.pallas_call(
        functools.partial(_pool_kernel, tile=tile, rows=t // GRID_W),
        grid=(b,),
        in_specs=[pl.BlockSpec((1, t, pw), lambda i: (i, 0, p_col0 // pw)),
                  pl.BlockSpec((1, t, pw), lambda i: (i, 0, p_col0 // pw + 1)),
                  pl.BlockSpec((P_GROUPS, tile, tile), lambda i: (0, 0, 0)),
                  pl.BlockSpec((P_GROUPS, LANES, LANES), lambda i: (0, 0, 0)),
                  pl.BlockSpec((1, pw), lambda i: (0, 0))],
        out_specs=pl.BlockSpec((1, t, pw), lambda i: (i, 0, 0)),
        out_shape=jax.ShapeDtypeStruct((b, t, pw), BF16),
        scratch_shapes=[pltpu.VMEM((2, t + 2 * pad, LANES), F32),
                        pltpu.VMEM((P_GROUPS, t, LANES), F32)],
        compiler_params=_cparams(("arbitrary",), 48),
        name="pool",
    )(y, y, _col_pool_matrices(tile), pool_w, pool_scale)


def _merge_kernel(x_ref, ym_ref, yp_ref, gm_ref, gp_ref, gate_ref, wm_ref, wp_ref, wo_ref, fw_ref, o_ref):
    slab = x_ref.shape[0] // MERGE_SLABS
    for r in range(MERGE_SLABS):
        rows = slice(r * slab, (r + 1) * slab)
        y_m = jnp.dot(ym_ref[rows, :], wm_ref[...], preferred_element_type=F32)
        y_p = jnp.dot(yp_ref[rows, :], wp_ref[...], preferred_element_type=F32)
        merged = (_sigmoid_t(gm_ref[rows, :].astype(F32)) * y_m
                  + _sigmoid_t(gp_ref[rows, :].astype(F32)) * y_p)
        upd = jnp.dot(merged.astype(BF16), wo_ref[...], preferred_element_type=F32)
        xo = x_ref[rows, :] + gate_ref[0] * upd
        o_ref[rows, :] = xo * lax.rsqrt(jnp.mean(xo * xo, axis=-1, keepdims=True) + EPS) * fw_ref[...]


def _merge(x2, ym, yp, y, mod, wm, wp, wo, final_w, rows_per_mod, g_col0):
    n, d = x2.shape
    tm = 1024
    per = rows_per_mod // tm
    pw = yp.shape[1]
    full = lambda shape: pl.BlockSpec(shape, lambda i: (0,) * len(shape))
    return pl.pallas_call(
        _merge_kernel,
        grid=(n // tm,),
        in_specs=[pl.BlockSpec((tm, d), lambda i: (i, 0)),
                  pl.BlockSpec((tm, d), lambda i: (i, 0)),
                  pl.BlockSpec((tm, pw), lambda i: (i, 0)),
                  pl.BlockSpec((tm, d), lambda i: (i, g_col0 // d)),
                  pl.BlockSpec((tm, d), lambda i: (i, g_col0 // d + 1)),
                  pl.BlockSpec((1, 1, d), lambda i: (i // per, 0, 2)),
                  full((d, d)), full((pw, d)), full((d, d)), full((1, d))],
        out_specs=pl.BlockSpec((tm, d), lambda i: (i, 0)),
        out_shape=jax.ShapeDtypeStruct((n, d), F32),
        compiler_params=_cparams(("arbitrary",), 48),
        name="merge",
    )(x2, ym, yp, y, y, mod, wm, wp, wo, final_w)


def kernel(x, c, ctx, c_ctx, norm_w, ada_w, ada_b, in_w, gate_b, head_norm_w, pool_w, pool_scale,
           branch_m_w, branch_p_w, out_w, final_norm_w):
    b, t, d = x.shape
    tc = ctx.shape[1]
    depth = norm_w.shape[0]
    assert depth == 1, "context stream update between layers is not implemented"
    mw = M_HEADS * MXU_DIM
    assert d == mw
    g0 = 5 * mw

    mod = _adaln(c, c_ctx, ada_w[0], ada_b[0])

    wt_f32 = jnp.transpose(in_w[0])
    nw = norm_w[0].reshape(1, d)

    g_q, g_k, g_v, g_o, g_z, g_p, g_gm, g_gp = range(8)
    pairs = ((g_q, g_k, "id"), (g_v, g_p, "half_silu"), (g_o, g_z, "gate"), (g_gm, g_gp, "id"))
    col = {name: i * mw for i, name in enumerate(("q", "k", "v", "p", "o", "oz", "g_m", "g_p"))}

    wt, wg, yc, gates_tc = _ctx_proj(ctx.reshape(b * tc, d), mod, b, nw, wt_f32, g0, g_k, g_v)
    y, gates_t = _in_proj(x.reshape(b * t, d), mod, nw, wt, wg, head_norm_w[0].reshape(1, mw).astype(BF16),
                          t, "in_proj", pairs)
    y = y.reshape(b, t, -1)
    yc = yc.reshape(b, tc, -1)

    pc, p, a, decay = _gates(gates_t, gates_tc, gate_b[0], b, CHUNK)
    ym = _mlstm(y, yc, p, a, pc, decay, CHUNK, col)
    yp = _pool(y, pool_w[0].astype(BF16), pool_scale[0].reshape(1, -1), col["p"])

    out = _merge(x.reshape(b * t, d), ym.reshape(b * t, mw), yp.reshape(b * t, -1), y.reshape(b * t, -1),
                 mod, branch_m_w[0].astype(BF16), branch_p_w[0].astype(BF16),
                 out_w[0].astype(BF16), final_norm_w.reshape(1, d), t, col["g_m"])
    return out.reshape(b, t, d)
```

```python
import functools

import numpy as np
import jax
import jax.numpy as jnp
from jax import lax
from jax.experimental import pallas as pl
from jax.experimental.pallas import tpu as pltpu

F32 = jnp.float32
BF16 = jnp.bfloat16

EPS = 1e-6
LOG2E = 1.4426950408889634
M_INIT = -1e30
NEG_BIG = -1e30
M_HEADS = 4
N_DIR = 2
N_GATE_COLS = N_DIR * 2 * M_HEADS
P_GROUPS = 4
POOL_WINDOWS = (2, 4, 8, 16)
GRID_W = 64
LANES = 128
SUBLANES = 8
MXU_DIM = 256
CHUNK = 256
POOL_UNROLL = 8
MERGE_SLABS = 2
MIB = 1024 * 1024


def _cparams(sem, vmem_mib):
    return pltpu.CompilerParams(dimension_semantics=sem, vmem_limit_bytes=vmem_mib * MIB)


def _sigmoid(x):
    return 1.0 / (1.0 + jnp.exp(-x))


def _silu(x):
    return x * _sigmoid(x)


def _log_sigmoid(x):
    return jnp.minimum(x, 0.0) - jnp.log(1.0 + jnp.exp(-jnp.abs(x)))


def _dot_nt(a, b):
    return lax.dot_general(a, b, (((1,), (1,)), ((), ())), preferred_element_type=F32)


def _adaln_kernel(c_ref, cc_ref, w_ref, b_ref, o_ref):
    b = c_ref.shape[0]
    rows = jnp.concatenate([c_ref[...], jnp.broadcast_to(cc_ref[...], (o_ref.shape[0] - b, c_ref.shape[1]))], axis=0)
    o_ref[:, 0, :] = jnp.dot(_silu(rows), w_ref[...], preferred_element_type=F32) + b_ref[...]


def _adaln(c, c_ctx, ada_w, ada_b):
    b, d = c.shape
    rows = SUBLANES * (b // SUBLANES + 1)
    n = ada_w.shape[1]
    tn = d
    return pl.pallas_call(
        _adaln_kernel,
        grid=(n // tn,),
        in_specs=[pl.BlockSpec((b, d), lambda j: (0, 0)),
                  pl.BlockSpec((1, d), lambda j: (0, 0)),
                  pl.BlockSpec((d, tn), lambda j: (0, j)),
                  pl.BlockSpec((1, tn), lambda j: (0, j))],
        out_specs=pl.BlockSpec((rows, 1, tn), lambda j: (0, 0, j)),
        out_shape=jax.ShapeDtypeStruct((rows, 1, n), F32),
        compiler_params=_cparams(("arbitrary",), 32),
        name="adaln",
    )(c, c_ctx.reshape(1, d), ada_w, ada_b.reshape(1, n))


def _sigmoid_t(x):
    return 0.5 * jnp.tanh(0.5 * x) + 0.5


IN_TILE = 1024
GATE_SUB = 256


def _inproj_kernel(x_ref, sc_ref, sh_ref, nw_ref, wa_ref, wb_ref, wg_ref, gw_ref, y_ref, g_ref, xn_ref, *, acts):
    j = pl.program_id(1)

    @pl.when(j == 0)
    def _():
        x = x_ref[...]
        ms = jnp.mean(x * x, axis=-1, keepdims=True)
        xn = x * lax.rsqrt(ms + EPS) * (nw_ref[...] * (1.0 + sc_ref[0])) + sh_ref[0]
        xn_ref[...] = xn.astype(BF16)
        wg = wg_ref[...]
        wg = jnp.concatenate([wg, jnp.zeros((LANES - wg.shape[0], wg.shape[1]), wg.dtype)], axis=0)
        g_ref[...] = _dot_nt(xn_ref[...], wg).T[:N_GATE_COLS]

    def pair(act):
        tn = IN_TILE
        xn = xn_ref[...]
        if act == "gate":
            for c in range(0, tn, GATE_SUB):
                a = _dot_nt(xn, wa_ref[c:c + GATE_SUB, :])
                b = _dot_nt(xn, wb_ref[c:c + GATE_SUB, :])
                y_ref[:, c:c + GATE_SUB] = a.astype(BF16)
                ab, bb = a.astype(BF16), b.astype(BF16)
                gate = _sigmoid_t(ab) * (bb * _sigmoid_t(bb))
                y_ref[:, tn + c:tn + c + GATE_SUB] = gate * gw_ref[:, c:c + GATE_SUB]
            return
        y_ref[:, :tn] = _dot_nt(xn, wa_ref[...]).astype(BF16)
        b = _dot_nt(xn, wb_ref[...])
        if act == "half_silu":
            half = tn // 2
            zb = b[:, half:].astype(BF16)
            y_ref[:, tn:tn + half] = b[:, :half].astype(BF16)
            y_ref[:, tn + half:] = zb * _sigmoid_t(zb)
            return
        y_ref[:, tn:] = b.astype(BF16)

    for act in sorted(set(acts)):
        hit = functools.reduce(jnp.logical_or, [j == s for s, a in enumerate(acts) if a == act])
        pl.when(hit)(functools.partial(pair, act))


def _in_proj(x2, mod, norm_w, w, wg, gate_w, rows_per_mod, name, pairs):
    n, d = x2.shape
    tm = min(2048, rows_per_mod)
    tn = IN_TILE
    per = rows_per_mod // tm
    acts = tuple(p[2] for p in pairs)

    def w_group(which):
        return lambda i, j: (sum(jnp.where(j == s, p[which], 0) for s, p in enumerate(pairs)), 0)

    return pl.pallas_call(
        functools.partial(_inproj_kernel, acts=acts),
        grid=(n // tm, len(pairs)),
        in_specs=[pl.BlockSpec((tm, d), lambda i, j: (i, 0)),
                  pl.BlockSpec((1, 1, d), lambda i, j: (i // per, 0, 1)),
                  pl.BlockSpec((1, 1, d), lambda i, j: (i // per, 0, 0)),
                  pl.BlockSpec((1, d), lambda i, j: (0, 0)),
                  pl.BlockSpec((tn, d), w_group(0)),
                  pl.BlockSpec((tn, d), w_group(1)),
                  pl.BlockSpec((N_GATE_COLS, d), lambda i, j: (0, 0)),
                  pl.BlockSpec((1, tn), lambda i, j: (0, 0))],
        out_specs=[pl.BlockSpec((tm, 2 * tn), lambda i, j: (i, j)),
                   pl.BlockSpec((N_GATE_COLS, tm), lambda i, j: (0, i))],
        out_shape=[jax.ShapeDtypeStruct((n, 2 * tn * len(pairs)), BF16),
                   jax.ShapeDtypeStruct((N_GATE_COLS, n), F32)],
        scratch_shapes=[pltpu.VMEM((tm, d), BF16)],
        compiler_params=_cparams(("arbitrary", "arbitrary"), 56),
        name=name,
    )(x2, mod, mod, norm_w, w, w, wg, gate_w)


def _ctx_proj_kernel(x_ref, sc_ref, sh_ref, nw_ref, wsrc_ref, wgsrc_ref, wdst_ref, wgdst_ref, y_ref, g_ref,
                     xn_ref, kv_ref, *, steps, k_block, v_block):
    i, j = pl.program_id(0), pl.program_id(1)
    group = i * steps + j
    w = wsrc_ref[...]
    w = jnp.where(group == k_block, w * MXU_DIM ** -0.5, w).astype(BF16)
    wdst_ref[...] = w

    @pl.when(group == k_block)
    def _():
        kv_ref[0] = w

    @pl.when(group == v_block)
    def _():
        kv_ref[1] = w

    @pl.when(j == 0)
    def _():
        x = x_ref[...]
        ms = jnp.mean(x * x, axis=-1, keepdims=True)
        xn = x * lax.rsqrt(ms + EPS) * (nw_ref[...] * (1.0 + sc_ref[0])) + sh_ref[0]
        xn_ref[...] = xn.astype(BF16)
        wg = wgsrc_ref[...].astype(BF16)
        wgdst_ref[...] = wg
        wg = jnp.concatenate([wg, jnp.zeros((LANES - wg.shape[0], wg.shape[1]), wg.dtype)], axis=0)
        g_ref[...] = _dot_nt(xn_ref[...], wg).T[:N_GATE_COLS]

    @pl.when(j == k_block)
    def _():
        y_ref[:, :IN_TILE] = _dot_nt(xn_ref[...], kv_ref[0]).astype(BF16)

    @pl.when(j == v_block)
    def _():
        y_ref[:, IN_TILE:] = _dot_nt(xn_ref[...], kv_ref[1]).astype(BF16)


def _ctx_proj(xc, mod, mod_row, norm_w, wt, gate_row0, k_block, v_block):
    n, d = xc.shape
    tn = IN_TILE
    n_groups = (wt.shape[0] - N_GATE_COLS) // tn
    row_tiles = 2
    tm, steps = n // row_tiles, n_groups // row_tiles
    assert gate_row0 % tn == 0 and n_groups % row_tiles == 0 and 0 < k_block < v_block < steps

    def src_rows(i, j):
        r = (i * steps + j) * tn
        return (pl.multiple_of(r + jnp.where(r >= gate_row0, N_GATE_COLS, 0), N_GATE_COLS), 0)

    return pl.pallas_call(
        functools.partial(_ctx_proj_kernel, steps=steps, k_block=k_block, v_block=v_block),
        grid=(row_tiles, steps),
        in_specs=[pl.BlockSpec((tm, d), lambda i, j: (i, 0)),
                  pl.BlockSpec((1, 1, d), lambda i, j: (mod_row, 0, 1)),
                  pl.BlockSpec((1, 1, d), lambda i, j: (mod_row, 0, 0)),
                  pl.BlockSpec((1, d), lambda i, j: (0, 0)),
                  pl.BlockSpec((pl.Element(tn), pl.Element(d)), src_rows),
                  pl.BlockSpec((pl.Element(N_GATE_COLS), pl.Element(d)), lambda i, j: (gate_row0, 0))],
        out_specs=[pl.BlockSpec((tn, d), lambda i, j: (i * steps + j, 0)),
                   pl.BlockSpec((N_GATE_COLS, d), lambda i, j: (0, 0)),
                   pl.BlockSpec((tm, 2 * tn), lambda i, j: (i, 0)),
                   pl.BlockSpec((N_GATE_COLS, tm), lambda i, j: (0, i))],
        out_shape=[jax.ShapeDtypeStruct((n_groups * tn, d), BF16),
                   jax.ShapeDtypeStruct((N_GATE_COLS, d), BF16),
                   jax.ShapeDtypeStruct((n, 2 * tn), BF16),
                   jax.ShapeDtypeStruct((N_GATE_COLS, n), F32)],
        scratch_shapes=[pltpu.VMEM((tm, d), BF16), pltpu.VMEM((2, tn, d), BF16)],
        compiler_params=_cparams(("arbitrary", "arbitrary"), 56),
        name="ctx_proj",
    )(xc, mod, mod, norm_w, wt, wt)


N_DH = N_DIR * M_HEADS
TERM_PIECES = (3, 2, 2, 3)


def _split(x, pieces):
    out = []
    for _ in range(pieces):
        p = x.astype(BF16).astype(F32)
        out.append(p)
        x = x - p
    return out


def _chunk_scan(x, pos, chunk, op, reverse):
    t = x.shape[1]
    k = 1
    while k < chunk:
        if reverse:
            x = jnp.where(pos < chunk - k, op(x, pltpu.roll(x, t - k, 1)), x)
        else:
            x = jnp.where(pos >= k, op(x, pltpu.roll(x, k, 1)), x)
        k *= 2
    return x


def _gate_terms(li, pre_f, m0, chunk):
    rows, t = li.shape
    nc = t // chunk
    pos = lax.broadcasted_iota(jnp.int32, (rows, t), 1) % chunk
    rev = lax.broadcasted_iota(jnp.int32, (rows, t), 0) % N_DH >= M_HEADS
    rev1 = lax.broadcasted_iota(jnp.int32, (rows, 1), 0) % N_DH >= M_HEADS
    lf = _log_sigmoid(pre_f)
    b = jnp.where(rev, _chunk_scan(lf, pos, chunk, jnp.add, True), _chunk_scan(lf, pos, chunk, jnp.add, False))
    a = li - b
    cmax = jnp.where(rev, _chunk_scan(a, pos, chunk, jnp.maximum, True),
                     _chunk_scan(a, pos, chunk, jnp.maximum, False))

    def at_scan_end(x, c):
        lo = c * chunk
        return jnp.where(rev1, x[:, lo:lo + 1], x[:, lo + chunk - 1:lo + chunk])

    b_end = [at_scan_end(b, c) for c in range(nc)]
    a_max = [at_scan_end(cmax, c) for c in range(nc)]
    m_f, m_b = m0, m0
    m_in_f, m_in_b = [None] * nc, [None] * nc
    for i in range(nc):
        j = nc - 1 - i
        m_in_f[i] = m_f
        m_f = b_end[i] + jnp.maximum(m_f, a_max[i])
        m_in_b[j] = m_b
        m_b = b_end[j] + jnp.maximum(m_b, a_max[j])
    m_in = [jnp.where(rev1, m_in_b[c], m_in_f[c]) for c in range(nc)]
    per_chunk = lambda vals: jnp.concatenate([jnp.broadcast_to(v, (rows, chunk)) for v in vals], axis=1)
    m_in_t = per_chunk(m_in)
    g = jnp.maximum(m_in_t, cmax)
    g_end = [jnp.maximum(m_in[c], a_max[c]) for c in range(nc)]
    pieces = []
    terms = (-g * LOG2E, jnp.exp(m_in_t - g), jnp.exp(a - per_chunk(g_end)), -(b + g) * LOG2E)
    for term, n in zip(terms, TERM_PIECES):
        pieces += _split(term, n)
    decay = jnp.concatenate([jnp.exp(m_in[c] - g_end[c]) for c in range(nc)], axis=1)
    return pieces, a * LOG2E, decay, jnp.where(rev1, m_b, m_f)


def _gates_kernel(gc_ref, g_ref, b_ref, rc_ref, dc_ref, r_ref, a_ref, d_ref, *, chunk):
    nb = r_ref.shape[0]

    def split(ref):
        t = ref.shape[1] // nb
        li, pre_f = [], []
        for e in range(nb):
            pre = ref[:, e * t:(e + 1) * t] + b_ref[...]
            for d in range(N_DIR):
                li.append(pre[2 * d * M_HEADS:(2 * d + 1) * M_HEADS])
                pre_f.append(pre[(2 * d + 1) * M_HEADS:(2 * d + 2) * M_HEADS])
        return jnp.concatenate(li, axis=0), jnp.concatenate(pre_f, axis=0)

    def store_pieces(ref, pieces):
        pad = jnp.zeros((LANES - N_DH * len(pieces), pieces[0].shape[1]), F32)
        for e in range(nb):
            own = [p[e * N_DH:(e + 1) * N_DH] for p in pieces]
            ref[e] = jnp.concatenate(own + [pad], axis=0).astype(BF16)

    m0 = jnp.full((nb * N_DH, 1), M_INIT, F32)
    pieces_c, _, decay_c, m_ctx = _gate_terms(*split(gc_ref), m0, chunk)
    store_pieces(rc_ref, pieces_c)
    pieces, a, decay, _ = _gate_terms(*split(g_ref), m_ctx, chunk)
    store_pieces(r_ref, pieces)
    for e in range(nb):
        own = slice(e * N_DH, (e + 1) * N_DH)
        dc_ref[e] = decay_c[own]
        d_ref[e] = decay[own]
        for c in range(a_ref.shape[1]):
            a_ref[e, c] = a[own, c * chunk:(c + 1) * chunk]


def _gates(gates_t, gates_tc, gate_b, b, chunk):
    t = gates_t.shape[1] // b
    tc = gates_tc.shape[1] // b
    nc, ncc = t // chunk, tc // chunk
    nb = b
    spec = lambda r, n: pl.BlockSpec((nb, r, n), lambda i: (i, 0, 0))
    seq = lambda n: pl.BlockSpec((N_GATE_COLS, nb * n), lambda i: (0, i))
    rc, dc, r, a, dec = pl.pallas_call(
        functools.partial(_gates_kernel, chunk=chunk),
        grid=(b // nb,),
        in_specs=[seq(tc), seq(t), pl.BlockSpec((N_GATE_COLS, 1), lambda i: (0, 0))],
        out_specs=[spec(LANES, tc), spec(N_DH, ncc), spec(LANES, t),
                   pl.BlockSpec((nb, nc, N_DH, chunk), lambda i: (i, 0, 0, 0)), spec(N_DH, nc)],
        out_shape=[jax.ShapeDtypeStruct((b, LANES, tc), BF16), jax.ShapeDtypeStruct((b, N_DH, ncc), F32),
                   jax.ShapeDtypeStruct((b, LANES, t), BF16), jax.ShapeDtypeStruct((b, nc, N_DH, chunk), F32),
                   jax.ShapeDtypeStruct((b, N_DH, nc), F32)],
        compiler_params=_cparams(("arbitrary",), 32),
        name="gates",
    )(gates_tc, gates_t, gate_b.reshape(N_GATE_COLS, 1))
    decay = jnp.concatenate([dc, dec], axis=2).reshape(b * N_DH, ncc + nc)
    return rc, r, a, decay


def _twice(x):
    return jnp.concatenate([x, x], axis=1)


def _mlstm_kernel(dec_ref, q_ref, k_ref, v_ref, oz_ref, p_ref, a_ref, kc_ref, vc_ref, pc_ref,
                  sel_ref, out_ref, ct_ref, n_ref, hs_ref, *, chunk):
    t = q_ref.shape[1]
    nc = t // chunk
    ncc = kc_ref.shape[1] // chunk
    bi, hi = pl.program_id(0), pl.program_id(1)
    t_idx = lax.broadcasted_iota(jnp.int32, (chunk, chunk), 0)
    s_idx = lax.broadcasted_iota(jnp.int32, (chunk, chunk), 1)

    def bcast_terms(pieces, sel):
        return lax.dot_general(pieces, sel, (((0,), (0,)), ((), ())), preferred_element_type=F32)

    def update(k, v, e_b, decay, d):
        ke = k.astype(F32) * _twice(e_b)
        upd = lax.dot_general(ke.astype(BF16), v, (((0,), (0,)), ((), ())), preferred_element_type=F32)
        n_new = jnp.sum(ke, axis=0, keepdims=True)
        ct_ref[d] = upd if decay is None else decay * ct_ref[d] + upd
        n_ref[d] = n_new if decay is None else decay * n_ref[d] + n_new

    def decay_of(c, d):
        return dec_ref[bi * N_DH + d * M_HEADS + hi, c]

    for d in range(N_DIR):
        for i in range(ncc):
            c = i if d == 0 else ncc - 1 - i
            rows = slice(c * chunk, (c + 1) * chunk)
            e_b = bcast_terms(pc_ref[0, :, rows], sel_ref[0, d, :, 2 * LANES:3 * LANES])
            update(kc_ref[0, rows, :], vc_ref[0, rows, :], e_b, None if i == 0 else decay_of(c, d), d)

    def run_chunk(c, d):
        rows = slice(c * chunk, (c + 1) * chunk)
        q, k, v = q_ref[0, rows, :], k_ref[0, rows, :], v_ref[0, rows, :]
        w = bcast_terms(p_ref[0, :, rows], sel_ref[0, d])
        neg_g_b, w_inter_b, e_b, neg_mt_b = [w[:, i * LANES:(i + 1) * LANES] for i in range(len(TERM_PIECES))]
        a_row = a_ref[0, c, pl.ds(d * M_HEADS + hi, 1), :]
        mask = (s_idx <= t_idx) if d == 0 else (s_idx >= t_idx)
        w_intra = jnp.exp2(jnp.where(mask, a_row + _twice(neg_g_b), NEG_BIG))
        s = _dot_nt(q, k) * w_intra
        num = (_twice(w_inter_b) * jnp.dot(q, ct_ref[d].astype(BF16), preferred_element_type=F32)
               + jnp.dot(s.astype(BF16), v, preferred_element_type=F32))
        den_b = (w_inter_b * jnp.sum(q.astype(F32) * n_ref[d], axis=-1, keepdims=True)
                 + jnp.sum(s, axis=-1, keepdims=True))
        h = num * _twice(1.0 / jnp.maximum(jnp.abs(den_b), jnp.exp2(neg_mt_b)))
        update(k, v, e_b, decay_of(ncc + c, d), d)
        return rows, h

    def finalize(rows, h):
        hn = h * lax.rsqrt(jnp.mean(h * h, axis=-1, keepdims=True) + EPS)
        out_ref[0, rows, :] = hn.astype(BF16) * oz_ref[0, rows, :]

    for i in range(nc):
        for d in range(N_DIR):
            rows, h = run_chunk(i if d == 0 else nc - 1 - i, d)
            if i < nc // 2:
                hs_ref[rows, :] = h
            else:
                finalize(rows, hs_ref[rows, :] + h)


def _term_selectors():
    sel = np.zeros((M_HEADS, N_DIR, LANES, len(TERM_PIECES) * LANES), np.float32)
    first = np.concatenate([[0], np.cumsum(TERM_PIECES)])
    for h in range(M_HEADS):
        for d in range(N_DIR):
            for q in range(len(TERM_PIECES)):
                for j in range(first[q], first[q + 1]):
                    sel[h, d, j * N_DH + d * M_HEADS + h, q * LANES:(q + 1) * LANES] = 1.0
    return jnp.asarray(sel, BF16)


def _mlstm(y, yc, p, a, pc, decay, chunk, col):
    b, t, _ = y.shape
    tc = yc.shape[1]
    dh = MXU_DIM
    sel = _term_selectors()
    assert t % (2 * chunk) == 0 and tc % chunk == 0 and tc >= chunk
    blk = lambda name: pl.BlockSpec((1, t, dh), lambda i, h, c=col[name] // dh: (i, 0, c + h))
    blkc = lambda col0: pl.BlockSpec((1, tc, dh), lambda i, h, c=col0 // dh: (i, 0, c + h))
    return pl.pallas_call(
        functools.partial(_mlstm_kernel, chunk=chunk),
        grid=(b, M_HEADS),
        in_specs=[pl.BlockSpec(memory_space=pltpu.SMEM),
                  blk("q"), blk("k"), blk("v"), blk("oz"),
                  pl.BlockSpec((1, LANES, t), lambda i, h: (i, 0, 0)),
                  pl.BlockSpec((1, t // chunk, N_DH, chunk), lambda i, h: (i, 0, 0, 0)),
                  blkc(0), blkc(M_HEADS * dh),
                  pl.BlockSpec((1, LANES, tc), lambda i, h: (i, 0, 0)),
                  pl.BlockSpec((1, N_DIR) + sel.shape[2:], lambda i, h: (h, 0, 0, 0))],
        out_specs=pl.BlockSpec((1, t, dh), lambda i, h: (i, 0, h)),
        out_shape=jax.ShapeDtypeStruct((b, t, M_HEADS * dh), BF16),
        scratch_shapes=[pltpu.VMEM((N_DIR, dh, dh), F32),
                        pltpu.VMEM((N_DIR, 1, dh), F32),
                        pltpu.VMEM((t, dh), F32)],
        compiler_params=_cparams(("arbitrary", "arbitrary"), 48),
        name="mlstm",
    )(decay, y, y, y, y, p, a, yc, yc, pc, sel)


def _window_bounds(pos, w, length):
    return np.clip(pos - w // 2, 0, length), np.clip(pos + w - w // 2, 0, length)


def _col_pool_matrices(tile):
    pos = np.arange(GRID_W)
    mats = []
    for w in POOL_WINDOWS:
        lo, hi = _window_bounds(pos, w, GRID_W)
        band = ((pos[None, :] >= lo[:, None]) & (pos[None, :] < hi[:, None])).astype(np.float32)
        mats.append(np.kron(np.eye(tile // GRID_W, dtype=np.float32), band))
    return jnp.asarray(np.stack(mats), dtype=BF16)


def _pool_kernel(p_ref, z_ref, mc_ref, pw_ref, ps_ref, out_ref, pad_ref, scale_ref, *, tile, rows):
    t = p_ref.shape[1]
    gd = LANES
    pad = (max(POOL_WINDOWS) // 2) * GRID_W
    n_tiles = t // tile

    @pl.when(pl.program_id(0) == 0)
    def _():
        pad_ref[...] = jnp.zeros(pad_ref.shape, F32)
        tok = lax.broadcasted_iota(jnp.int32, (t, gd), 0)
        r = tok // GRID_W
        c = tok % GRID_W
        for g, w in enumerate(POOL_WINDOWS):
            cnt_r = jnp.minimum(r + (w - w // 2), rows) - jnp.maximum(r - w // 2, 0)
            cnt_c = jnp.minimum(c + (w - w // 2), GRID_W) - jnp.maximum(c - w // 2, 0)
            scale_ref[g] = 1.0 / (cnt_r.astype(F32) * cnt_c.astype(F32))

    def col_sum(g, start, slot):
        cols = pl.ds(g * gd, gd)
        pad_ref[slot, pl.ds(pad + start, tile), :] = jnp.dot(
            mc_ref[g], p_ref[0, pl.ds(start, tile), cols], preferred_element_type=F32)

    def row_sum(g, start, slot):
        w = POOL_WINDOWS[g]
        cols = pl.ds(g * gd, gd)
        buf = pad_ref.at[slot]
        acc = buf[pl.ds(pad + start - (w // 2) * GRID_W, tile), :]
        for j in range(1 - w // 2, w - w // 2):
            acc = acc + buf[pl.ds(pad + start + j * GRID_W, tile), :]
        mixed = acc * scale_ref[g, pl.ds(start, tile), :] - p_ref[0, pl.ds(start, tile), cols].astype(F32)
        mm = jnp.dot(mixed.astype(BF16), pw_ref[g], preferred_element_type=F32)
        zz = z_ref[0, pl.ds(start, tile), cols].astype(F32)
        out_ref[0, pl.ds(start, tile), cols] = (mm * ps_ref[:, cols] * zz).astype(out_ref.dtype)

    for n in range(P_GROUPS + 1):
        def body(i, carry, n=n):
            start = pl.multiple_of(i * tile, tile)
            if n < P_GROUPS:
                col_sum(n, start, n % 2)
            if n > 0:
                row_sum(n - 1, start, (n - 1) % 2)
            return carry

        lax.fori_loop(0, n_tiles, body, 0, unroll=POOL_UNROLL)


def _pool(y, pool_w, pool_scale, p_col0):
    b, t, _ = y.shape
    pw = P_GROUPS * LANES
    tile = MXU_DIM
    pad = (max(POOL_WINDOWS) // 2) * GRID_W
    return pl.pallas_call(
        functools.partial(_pool_kernel, tile=tile, rows=t // GRID_W),
        grid=(b,),
        in_specs=[pl.BlockSpec((1, t, pw), lambda i: (i, 0, p_col0 // pw)),
                  pl.BlockSpec((1, t, pw), lambda i: (i, 0, p_col0 // pw + 1)),
                  pl.BlockSpec((P_GROUPS, tile, tile), lambda i: (0, 0, 0)),
                  pl.BlockSpec((P_GROUPS, LANES, LANES), lambda i: (0, 0, 0)),
                  pl.BlockSpec((1, pw), lambda i: (0, 0))],
        out_specs=pl.BlockSpec((1, t, pw), lambda i: (i, 0, 0)),
        out_shape=jax.ShapeDtypeStruct((b, t, pw), BF16),
        scratch_shapes=[pltpu.VMEM((2, t + 2 * pad, LANES), F32),
                        pltpu.VMEM((P_GROUPS, t, LANES), F32)],
        compiler_params=_cparams(("arbitrary",), 48),
        name="pool",
    )(y, y, _col_pool_matrices(tile), pool_w, pool_scale)


def _merge_kernel(x_ref, ym_ref, yp_ref, gm_ref, gp_ref, gate_ref, wm_ref, wp_ref, wo_ref, fw_ref, o_ref):
    slab = x_ref.shape[0] // MERGE_SLABS
    for r in range(MERGE_SLABS):
        rows = slice(r * slab, (r + 1) * slab)
        y_m = jnp.dot(ym_ref[rows, :], wm_ref[...], preferred_element_type=F32)
        y_p = jnp.dot(yp_ref[rows, :], wp_ref[...], preferred_element_type=F32)
        merged = (_sigmoid_t(gm_ref[rows, :].astype(F32)) * y_m
                  + _sigmoid_t(gp_ref[rows, :].astype(F32)) * y_p)
        upd = jnp.dot(merged.astype(BF16), wo_ref[...], preferred_element_type=F32)
        xo = x_ref[rows, :] + gate_ref[0] * upd
        o_ref[rows, :] = xo * lax.rsqrt(jnp.mean(xo * xo, axis=-1, keepdims=True) + EPS) * fw_ref[...]


def _merge(x2, ym, yp, y, mod, wm, wp, wo, final_w, rows_per_mod, g_col0):
    n, d = x2.shape
    tm = 1024
    per = rows_per_mod // tm
    pw = yp.shape[1]
    full = lambda shape: pl.BlockSpec(shape, lambda i: (0,) * len(shape))
    return pl.pallas_call(
        _merge_kernel,
        grid=(n // tm,),
        in_specs=[pl.BlockSpec((tm, d), lambda i: (i, 0)),
                  pl.BlockSpec((tm, d), lambda i: (i, 0)),
                  pl.BlockSpec((tm, pw), lambda i: (i, 0)),
                  pl.BlockSpec((tm, d), lambda i: (i, g_col0 // d)),
                  pl.BlockSpec((tm, d), lambda i: (i, g_col0 // d + 1)),
                  pl.BlockSpec((1, 1, d), lambda i: (i // per, 0, 2)),
                  full((d, d)), full((pw, d)), full((d, d)), full((1, d))],
        out_specs=pl.BlockSpec((tm, d), lambda i: (i, 0)),
        out_shape=jax.ShapeDtypeStruct((n, d), F32),
        compiler_params=_cparams(("arbitrary",), 48),
        name="merge",
    )(x2, ym, yp, y, y, mod, wm, wp, wo, final_w)


def kernel(x, c, ctx, c_ctx, norm_w, ada_w, ada_b, in_w, gate_b, head_norm_w, pool_w, pool_scale,
           branch_m_w, branch_p_w, out_w, final_norm_w):
    b, t, d = x.shape
    tc = ctx.shape[1]
    depth = norm_w.shape[0]
    assert depth == 1, "context stream update between layers is not implemented"
    mw = M_HEADS * MXU_DIM
    assert d == mw
    g0 = 5 * mw

    mod = _adaln(c, c_ctx, ada_w[0], ada_b[0])

    wt_f32 = jnp.transpose(in_w[0])
    nw = norm_w[0].reshape(1, d)

    g_q, g_k, g_v, g_o, g_z, g_p, g_gm, g_gp = range(8)
    pairs = ((g_q, g_k, "id"), (g_v, g_p, "half_silu"), (g_o, g_z, "gate"), (g_gm, g_gp, "id"))
    col = {name: i * mw for i, name in enumerate(("q", "k", "v", "p", "o", "oz", "g_m", "g_p"))}

    wt, wg, yc, gates_tc = _ctx_proj(ctx.reshape(b * tc, d), mod, b, nw, wt_f32, g0, g_k, g_v)
    y, gates_t = _in_proj(x.reshape(b * t, d), mod, nw, wt, wg, head_norm_w[0].reshape(1, mw).astype(BF16),
                          t, "in_proj", pairs)
    y = y.reshape(b, t, -1)
    yc = yc.reshape(b, tc, -1)

    pc, p, a, decay = _gates(gates_t, gates_tc, gate_b[0], b, CHUNK)
    ym = _mlstm(y, yc, p, a, pc, decay, CHUNK, col)
    yp = _pool(y, pool_w[0].astype(BF16), pool_scale[0].reshape(1, -1), col["p"])

    out = _merge(x.reshape(b * t, d), ym.reshape(b * t, mw), yp.reshape(b * t, -1), y.reshape(b * t, -1),
                 mod, branch_m_w[0].astype(BF16), branch_p_w[0].astype(BF16),
                 out_w[0].astype(BF16), final_norm_w.reshape(1, d), t, col["g_m"])
    return out.reshape(b, t, d)
```

```python
import functools

import numpy as np
import jax
import jax.numpy as jnp
from jax import lax
from jax.experimental import pallas as pl
from jax.experimental.pallas import tpu as pltpu

F32 = jnp.float32
BF16 = jnp.bfloat16

EPS = 1e-6
LOG2E = 1.4426950408889634
M_INIT = -1e30
NEG_BIG = -1e30
M_HEADS = 4
N_DIR = 2
N_GATE_COLS = N_DIR * 2 * M_HEADS
P_GROUPS = 4
POOL_WINDOWS = (2, 4, 8, 16)
GRID_W = 64
LANES = 128
SUBLANES = 8
MXU_DIM = 256
CHUNK = 256
MERGE_SLABS = 2
MIB = 1024 * 1024


def _cparams(sem, vmem_mib):
    return pltpu.CompilerParams(dimension_semantics=sem, vmem_limit_bytes=vmem_mib * MIB)


def _sigmoid(x):
    return 1.0 / (1.0 + jnp.exp(-x))


def _silu(x):
    return x * _sigmoid(x)


def _log_sigmoid(x):
    return jnp.minimum(x, 0.0) - jnp.log(1.0 + jnp.exp(-jnp.abs(x)))


def _dot_nt(a, b):
    return lax.dot_general(a, b, (((1,), (1,)), ((), ())), preferred_element_type=F32)


def _adaln_kernel(c_ref, cc_ref, w_ref, b_ref, o_ref):
    b = c_ref.shape[0]
    rows = jnp.concatenate([c_ref[...], jnp.broadcast_to(cc_ref[...], (o_ref.shape[0] - b, c_ref.shape[1]))], axis=0)
    o_ref[:, 0, :] = jnp.dot(_silu(rows), w_ref[...], preferred_element_type=F32) + b_ref[...]


def _adaln(c, c_ctx, ada_w, ada_b):
    b, d = c.shape
    rows = SUBLANES * (b // SUBLANES + 1)
    n = ada_w.shape[1]
    tn = d
    return pl.pallas_call(
        _adaln_kernel,
        grid=(n // tn,),
        in_specs=[pl.BlockSpec((b, d), lambda j: (0, 0)),
                  pl.BlockSpec((1, d), lambda j: (0, 0)),
                  pl.BlockSpec((d, tn), lambda j: (0, j)),
                  pl.BlockSpec((1, tn), lambda j: (0, j))],
        out_specs=pl.BlockSpec((rows, 1, tn), lambda j: (0, 0, j)),
        out_shape=jax.ShapeDtypeStruct((rows, 1, n), F32),
        compiler_params=_cparams(("arbitrary",), 32),
        name="adaln",
    )(c, c_ctx.reshape(1, d), ada_w, ada_b.reshape(1, n))


def _sigmoid_t(x):
    return 0.5 * jnp.tanh(0.5 * x) + 0.5


IN_TILE = 1024
GATE_SUB = 256


def _inproj_kernel(x_ref, sc_ref, sh_ref, nw_ref, wa_ref, wb_ref, wg_ref, gw_ref, y_ref, g_ref, xn_ref, *, acts):
    j = pl.program_id(1)

    @pl.when(j == 0)
    def _():
        x = x_ref[...]
        ms = jnp.mean(x * x, axis=-1, keepdims=True)
        xn = x * lax.rsqrt(ms + EPS) * (nw_ref[...] * (1.0 + sc_ref[0])) + sh_ref[0]
        xn_ref[...] = xn.astype(BF16)
        wg = wg_ref[...]
        wg = jnp.concatenate([wg, jnp.zeros((LANES - wg.shape[0], wg.shape[1]), wg.dtype)], axis=0)
        g_ref[...] = _dot_nt(xn_ref[...], wg).T[:N_GATE_COLS]

    def pair(act):
        tn = IN_TILE
        xn = xn_ref[...]
        if act == "gate":
            for c in range(0, tn, GATE_SUB):
                a = _dot_nt(xn, wa_ref[c:c + GATE_SUB, :])
                b = _dot_nt(xn, wb_ref[c:c + GATE_SUB, :])
                y_ref[:, c:c + GATE_SUB] = a.astype(BF16)
                ab, bb = a.astype(BF16), b.astype(BF16)
                gate = _sigmoid_t(ab) * (bb * _sigmoid_t(bb))
                y_ref[:, tn + c:tn + c + GATE_SUB] = gate * gw_ref[:, c:c + GATE_SUB]
            return
        y_ref[:, :tn] = _dot_nt(xn, wa_ref[...]).astype(BF16)
        b = _dot_nt(xn, wb_ref[...])
        if act == "half_silu":
            half = tn // 2
            zb = b[:, half:].astype(BF16)
            y_ref[:, tn:tn + half] = b[:, :half].astype(BF16)
            y_ref[:, tn + half:] = zb * _sigmoid_t(zb)
            return
        y_ref[:, tn:] = b.astype(BF16)

    for act in sorted(set(acts)):
        hit = functools.reduce(jnp.logical_or, [j == s for s, a in enumerate(acts) if a == act])
        pl.when(hit)(functools.partial(pair, act))


def _in_proj(x2, mod, norm_w, w, wg, gate_w, rows_per_mod, name, pairs):
    n, d = x2.shape
    tm = min(2048, rows_per_mod)
    tn = IN_TILE
    per = rows_per_mod // tm
    acts = tuple(p[2] for p in pairs)

    def w_group(which):
        return lambda i, j: (sum(jnp.where(j == s, p[which], 0) for s, p in enumerate(pairs)), 0)

    return pl.pallas_call(
        functools.partial(_inproj_kernel, acts=acts),
        grid=(n // tm, len(pairs)),
        in_specs=[pl.BlockSpec((tm, d), lambda i, j: (i, 0)),
                  pl.BlockSpec((1, 1, d), lambda i, j: (i // per, 0, 1)),
                  pl.BlockSpec((1, 1, d), lambda i, j: (i // per, 0, 0)),
                  pl.BlockSpec((1, d), lambda i, j: (0, 0)),
                  pl.BlockSpec((tn, d), w_group(0)),
                  pl.BlockSpec((tn, d), w_group(1)),
                  pl.BlockSpec((N_GATE_COLS, d), lambda i, j: (0, 0)),
                  pl.BlockSpec((1, tn), lambda i, j: (0, 0))],
        out_specs=[pl.BlockSpec((tm, 2 * tn), lambda i, j: (i, j)),
                   pl.BlockSpec((N_GATE_COLS, tm), lambda i, j: (0, i))],
        out_shape=[jax.ShapeDtypeStruct((n, 2 * tn * len(pairs)), BF16),
                   jax.ShapeDtypeStruct((N_GATE_COLS, n), F32)],
        scratch_shapes=[pltpu.VMEM((tm, d), BF16)],
        compiler_params=_cparams(("arbitrary", "arbitrary"), 56),
        name=name,
    )(x2, mod, mod, norm_w, w, w, wg, gate_w)


def _ctx_proj_kernel(x_ref, sc_ref, sh_ref, nw_ref, wsrc_ref, wgsrc_ref, wdst_ref, wgdst_ref, y_ref, g_ref,
                     xn_ref, kv_ref, *, steps, k_block, v_block):
    i, j = pl.program_id(0), pl.program_id(1)
    group = i * steps + j
    w = wsrc_ref[...]
    w = jnp.where(group == k_block, w * MXU_DIM ** -0.5, w).astype(BF16)
    wdst_ref[...] = w

    @pl.when(group == k_block)
    def _():
        kv_ref[0] = w

    @pl.when(group == v_block)
    def _():
        kv_ref[1] = w

    @pl.when(j == 0)
    def _():
        x = x_ref[...]
        ms = jnp.mean(x * x, axis=-1, keepdims=True)
        xn = x * lax.rsqrt(ms + EPS) * (nw_ref[...] * (1.0 + sc_ref[0])) + sh_ref[0]
        xn_ref[...] = xn.astype(BF16)
        wg = wgsrc_ref[...].astype(BF16)
        wgdst_ref[...] = wg
        wg = jnp.concatenate([wg, jnp.zeros((LANES - wg.shape[0], wg.shape[1]), wg.dtype)], axis=0)
        g_ref[...] = _dot_nt(xn_ref[...], wg).T[:N_GATE_COLS]

    @pl.when(j == k_block)
    def _():
        y_ref[:, :IN_TILE] = _dot_nt(xn_ref[...], kv_ref[0]).astype(BF16)

    @pl.when(j == v_block)
    def _():
        y_ref[:, IN_TILE:] = _dot_nt(xn_ref[...], kv_ref[1]).astype(BF16)


def _ctx_proj(xc, mod, mod_row, norm_w, wt, gate_row0, k_block, v_block):
    n, d = xc.shape
    tn = IN_TILE
    n_groups = (wt.shape[0] - N_GATE_COLS) // tn
    row_tiles = 2
    tm, steps = n // row_tiles, n_groups // row_tiles
    assert gate_row0 % tn == 0 and n_groups % row_tiles == 0 and 0 < k_block < v_block < steps

    def src_rows(i, j):
        r = (i * steps + j) * tn
        return (pl.multiple_of(r + jnp.where(r >= gate_row0, N_GATE_COLS, 0), N_GATE_COLS), 0)

    return pl.pallas_call(
        functools.partial(_ctx_proj_kernel, steps=steps, k_block=k_block, v_block=v_block),
        grid=(row_tiles, steps),
        in_specs=[pl.BlockSpec((tm, d), lambda i, j: (i, 0)),
                  pl.BlockSpec((1, 1, d), lambda i, j: (mod_row, 0, 1)),
                  pl.BlockSpec((1, 1, d), lambda i, j: (mod_row, 0, 0)),
                  pl.BlockSpec((1, d), lambda i, j: (0, 0)),
                  pl.BlockSpec((pl.Element(tn), pl.Element(d)), src_rows),
                  pl.BlockSpec((pl.Element(N_GATE_COLS), pl.Element(d)), lambda i, j: (gate_row0, 0))],
        out_specs=[pl.BlockSpec((tn, d), lambda i, j: (i * steps + j, 0)),
                   pl.BlockSpec((N_GATE_COLS, d), lambda i, j: (0, 0)),
                   pl.BlockSpec((tm, 2 * tn), lambda i, j: (i, 0)),
                   pl.BlockSpec((N_GATE_COLS, tm), lambda i, j: (0, i))],
        out_shape=[jax.ShapeDtypeStruct((n_groups * tn, d), BF16),
                   jax.ShapeDtypeStruct((N_GATE_COLS, d), BF16),
                   jax.ShapeDtypeStruct((n, 2 * tn), BF16),
                   jax.ShapeDtypeStruct((N_GATE_COLS, n), F32)],
        scratch_shapes=[pltpu.VMEM((tm, d), BF16), pltpu.VMEM((2, tn, d), BF16)],
        compiler_params=_cparams(("arbitrary", "arbitrary"), 56),
        name="ctx_proj",
    )(xc, mod, mod, norm_w, wt, wt)


N_DH = N_DIR * M_HEADS
TERM_PIECES = (3, 2, 2, 3)


def _split(x, pieces):
    out = []
    for _ in range(pieces):
        p = x.astype(BF16).astype(F32)
        out.append(p)
        x = x - p
    return out


def _chunk_scan(x, pos, chunk, op, reverse):
    t = x.shape[1]
    k = 1
    while k < chunk:
        if reverse:
            x = jnp.where(pos < chunk - k, op(x, pltpu.roll(x, t - k, 1)), x)
        else:
            x = jnp.where(pos >= k, op(x, pltpu.roll(x, k, 1)), x)
        k *= 2
    return x


def _gate_terms(li, pre_f, m0, chunk):
    rows, t = li.shape
    nc = t // chunk
    pos = lax.broadcasted_iota(jnp.int32, (rows, t), 1) % chunk
    rev = lax.broadcasted_iota(jnp.int32, (rows, t), 0) % N_DH >= M_HEADS
    rev1 = lax.broadcasted_iota(jnp.int32, (rows, 1), 0) % N_DH >= M_HEADS
    lf = _log_sigmoid(pre_f)
    b = jnp.where(rev, _chunk_scan(lf, pos, chunk, jnp.add, True), _chunk_scan(lf, pos, chunk, jnp.add, False))
    a = li - b
    cmax = jnp.where(rev, _chunk_scan(a, pos, chunk, jnp.maximum, True),
                     _chunk_scan(a, pos, chunk, jnp.maximum, False))

    def at_scan_end(x, c):
        lo = c * chunk
        return jnp.where(rev1, x[:, lo:lo + 1], x[:, lo + chunk - 1:lo + chunk])

    b_end = [at_scan_end(b, c) for c in range(nc)]
    a_max = [at_scan_end(cmax, c) for c in range(nc)]
    m_f, m_b = m0, m0
    m_in_f, m_in_b = [None] * nc, [None] * nc
    for i in range(nc):
        j = nc - 1 - i
        m_in_f[i] = m_f
        m_f = b_end[i] + jnp.maximum(m_f, a_max[i])
        m_in_b[j] = m_b
        m_b = b_end[j] + jnp.maximum(m_b, a_max[j])
    m_in = [jnp.where(rev1, m_in_b[c], m_in_f[c]) for c in range(nc)]
    per_chunk = lambda vals: jnp.concatenate([jnp.broadcast_to(v, (rows, chunk)) for v in vals], axis=1)
    m_in_t = per_chunk(m_in)
    g = jnp.maximum(m_in_t, cmax)
    g_end = [jnp.maximum(m_in[c], a_max[c]) for c in range(nc)]
    pieces = []
    terms = (-g * LOG2E, jnp.exp(m_in_t - g), jnp.exp(a - per_chunk(g_end)), -(b + g) * LOG2E)
    for term, n in zip(terms, TERM_PIECES):
        pieces += _split(term, n)
    decay = jnp.concatenate([jnp.exp(m_in[c] - g_end[c]) for c in range(nc)], axis=1)
    return pieces, a * LOG2E, decay, jnp.where(rev1, m_b, m_f)


def _gates_kernel(gc_ref, g_ref, b_ref, rc_ref, dc_ref, r_ref, a_ref, d_ref, *, chunk):
    nb = r_ref.shape[0]

    def split(ref):
        t = ref.shape[1] // nb
        li, pre_f = [], []
        for e in range(nb):
            pre = ref[:, e * t:(e + 1) * t] + b_ref[...]
            for d in range(N_DIR):
                li.append(pre[2 * d * M_HEADS:(2 * d + 1) * M_HEADS])
                pre_f.append(pre[(2 * d + 1) * M_HEADS:(2 * d + 2) * M_HEADS])
        return jnp.concatenate(li, axis=0), jnp.concatenate(pre_f, axis=0)

    def store_pieces(ref, pieces):
        pad = jnp.zeros((LANES - N_DH * len(pieces), pieces[0].shape[1]), F32)
        for e in range(nb):
            own = [p[e * N_DH:(e + 1) * N_DH] for p in pieces]
            ref[e] = jnp.concatenate(own + [pad], axis=0).astype(BF16)

    m0 = jnp.full((nb * N_DH, 1), M_INIT, F32)
    pieces_c, _, decay_c, m_ctx = _gate_terms(*split(gc_ref), m0, chunk)
    store_pieces(rc_ref, pieces_c)
    pieces, a, decay, _ = _gate_terms(*split(g_ref), m_ctx, chunk)
    store_pieces(r_ref, pieces)
    for e in range(nb):
        own = slice(e * N_DH, (e + 1) * N_DH)
        dc_ref[e] = decay_c[own]
        d_ref[e] = decay[own]
        for c in range(a_ref.shape[1]):
            a_ref[e, c] = a[own, c * chunk:(c + 1) * chunk]


def _gates(gates_t, gates_tc, gate_b, b, chunk):
    t = gates_t.shape[1] // b
    tc = gates_tc.shape[1] // b
    nc, ncc = t // chunk, tc // chunk
    nb = b
    spec = lambda r, n: pl.BlockSpec((nb, r, n), lambda i: (i, 0, 0))
    seq = lambda n: pl.BlockSpec((N_GATE_COLS, nb * n), lambda i: (0, i))
    rc, dc, r, a, dec = pl.pallas_call(
        functools.partial(_gates_kernel, chunk=chunk),
        grid=(b // nb,),
        in_specs=[seq(tc), seq(t), pl.BlockSpec((N_GATE_COLS, 1), lambda i: (0, 0))],
        out_specs=[spec(LANES, tc), spec(N_DH, ncc), spec(LANES, t),
                   pl.BlockSpec((nb, nc, N_DH, chunk), lambda i: (i, 0, 0, 0)), spec(N_DH, nc)],
        out_shape=[jax.ShapeDtypeStruct((b, LANES, tc), BF16), jax.ShapeDtypeStruct((b, N_DH, ncc), F32),
                   jax.ShapeDtypeStruct((b, LANES, t), BF16), jax.ShapeDtypeStruct((b, nc, N_DH, chunk), F32),
                   jax.ShapeDtypeStruct((b, N_DH, nc), F32)],
        compiler_params=_cparams(("arbitrary",), 32),
        name="gates",
    )(gates_tc, gates_t, gate_b.reshape(N_GATE_COLS, 1))
    decay = jnp.concatenate([dc, dec], axis=2).reshape(b * N_DH, ncc + nc)
    return rc, r, a, decay


def _twice(x):
    return jnp.concatenate([x, x], axis=1)


def _mlstm_kernel(dec_ref, q_ref, k_ref, v_ref, oz_ref, p_ref, a_ref, kc_ref, vc_ref, pc_ref,
                  sel_ref, out_ref, ct_ref, n_ref, hs_ref, *, chunk):
    t = q_ref.shape[1]
    nc = t // chunk
    ncc = kc_ref.shape[1] // chunk
    bi, hi = pl.program_id(0), pl.program_id(1)
    t_idx = lax.broadcasted_iota(jnp.int32, (chunk, chunk), 0)
    s_idx = lax.broadcasted_iota(jnp.int32, (chunk, chunk), 1)

    def bcast_terms(pieces, sel):
        return lax.dot_general(pieces, sel, (((0,), (0,)), ((), ())), preferred_element_type=F32)

    def update(k, v, e_b, decay, d):
        ke = k.astype(F32) * _twice(e_b)
        upd = lax.dot_general(ke.astype(BF16), v, (((0,), (0,)), ((), ())), preferred_element_type=F32)
        n_new = jnp.sum(ke, axis=0, keepdims=True)
        ct_ref[d] = upd if decay is None else decay * ct_ref[d] + upd
        n_ref[d] = n_new if decay is None else decay * n_ref[d] + n_new

    def decay_of(c, d):
        return dec_ref[bi * N_DH + d * M_HEADS + hi, c]

    for d in range(N_DIR):
        for i in range(ncc):
            c = i if d == 0 else ncc - 1 - i
            rows = slice(c * chunk, (c + 1) * chunk)
            e_b = bcast_terms(pc_ref[0, :, rows], sel_ref[0, d, :, 2 * LANES:3 * LANES])
            update(kc_ref[0, rows, :], vc_ref[0, rows, :], e_b, None if i == 0 else decay_of(c, d), d)

    def run_chunk(c, d):
        rows = slice(c * chunk, (c + 1) * chunk)
        q, k, v = q_ref[0, rows, :], k_ref[0, rows, :], v_ref[0, rows, :]
        w = bcast_terms(p_ref[0, :, rows], sel_ref[0, d])
        neg_g_b, w_inter_b, e_b, neg_mt_b = [w[:, i * LANES:(i + 1) * LANES] for i in range(len(TERM_PIECES))]
        a_row = a_ref[0, c, pl.ds(d * M_HEADS + hi, 1), :]
        mask = (s_idx <= t_idx) if d == 0 else (s_idx >= t_idx)
        w_intra = jnp.exp2(jnp.where(mask, a_row + _twice(neg_g_b), NEG_BIG))
        s = _dot_nt(q, k) * w_intra
        num = (_twice(w_inter_b) * jnp.dot(q, ct_ref[d].astype(BF16), preferred_element_type=F32)
               + jnp.dot(s.astype(BF16), v, preferred_element_type=F32))
        den_b = (w_inter_b * jnp.sum(q.astype(F32) * n_ref[d], axis=-1, keepdims=True)
                 + jnp.sum(s, axis=-1, keepdims=True))
        h = num * _twice(1.0 / jnp.maximum(jnp.abs(den_b), jnp.exp2(neg_mt_b)))
        update(k, v, e_b, decay_of(ncc + c, d), d)
        return rows, h

    def finalize(rows, h):
        hn = h * lax.rsqrt(jnp.mean(h * h, axis=-1, keepdims=True) + EPS)
        out_ref[0, rows, :] = hn.astype(BF16) * oz_ref[0, rows, :]

    for i in range(nc):
        for d in range(N_DIR):
            rows, h = run_chunk(i if d == 0 else nc - 1 - i, d)
            if i < nc // 2:
                hs_ref[rows, :] = h
            else:
                finalize(rows, hs_ref[rows, :] + h)


def _term_selectors():
    sel = np.zeros((M_HEADS, N_DIR, LANES, len(TERM_PIECES) * LANES), np.float32)
    first = np.concatenate([[0], np.cumsum(TERM_PIECES)])
    for h in range(M_HEADS):
        for d in range(N_DIR):
            for q in range(len(TERM_PIECES)):
                for j in range(first[q], first[q + 1]):
                    sel[h, d, j * N_DH + d * M_HEADS + h, q * LANES:(q + 1) * LANES] = 1.0
    return jnp.asarray(sel, BF16)


def _mlstm(y, yc, p, a, pc, decay, chunk, col):
    b, t, _ = y.shape
    tc = yc.shape[1]
    dh = MXU_DIM
    sel = _term_selectors()
    assert t % (2 * chunk) == 0 and tc % chunk == 0 and tc >= chunk
    blk = lambda name: pl.BlockSpec((1, t, dh), lambda i, h, c=col[name] // dh: (i, 0, c + h))
    blkc = lambda col0: pl.BlockSpec((1, tc, dh), lambda i, h, c=col0 // dh: (i, 0, c + h))
    return pl.pallas_call(
        functools.partial(_mlstm_kernel, chunk=chunk),
        grid=(b, M_HEADS),
        in_specs=[pl.BlockSpec(memory_space=pltpu.SMEM),
                  blk("q"), blk("k"), blk("v"), blk("oz"),
                  pl.BlockSpec((1, LANES, t), lambda i, h: (i, 0, 0)),
                  pl.BlockSpec((1, t // chunk, N_DH, chunk), lambda i, h: (i, 0, 0, 0)),
                  blkc(0), blkc(M_HEADS * dh),
                  pl.BlockSpec((1, LANES, tc), lambda i, h: (i, 0, 0)),
                  pl.BlockSpec((1, N_DIR) + sel.shape[2:], lambda i, h: (h, 0, 0, 0))],
        out_specs=pl.BlockSpec((1, t, dh), lambda i, h: (i, 0, h)),
        out_shape=jax.ShapeDtypeStruct((b, t, M_HEADS * dh), BF16),
        scratch_shapes=[pltpu.VMEM((N_DIR, dh, dh), F32),
                        pltpu.VMEM((N_DIR, 1, dh), F32),
                        pltpu.VMEM((t, dh), F32)],
        compiler_params=_cparams(("arbitrary", "arbitrary"), 48),
        name="mlstm",
    )(decay, y, y, y, y, p, a, yc, yc, pc, sel)


def _window_bounds(pos, w, length):
    return np.clip(pos - w // 2, 0, length), np.clip(pos + w - w // 2, 0, length)


def _col_pool_matrices(tile):
    pos = np.arange(GRID_W)
    mats = []
    for w in POOL_WINDOWS:
        lo, hi = _window_bounds(pos, w, GRID_W)
        band = ((pos[None, :] >= lo[:, None]) & (pos[None, :] < hi[:, None])).astype(np.float32)
        mats.append(np.kron(np.eye(tile // GRID_W, dtype=np.float32), band))
    return jnp.asarray(np.stack(mats), dtype=BF16)


def _pool_kernel(p_ref, z_ref, mc_ref, pw_ref, ps_ref, out_ref, pad_ref, scale_ref, *, tile, rows):
    t = p_ref.shape[1]
    gd = LANES
    pad = (max(POOL_WINDOWS) // 2) * GRID_W
    n_tiles = t // tile

    @pl.when(pl.program_id(0) == 0)
    def _():
        pad_ref[...] = jnp.zeros(pad_ref.shape, F32)
        tok = lax.broadcasted_iota(jnp.int32, (t, gd), 0)
        r = tok // GRID_W
        c = tok % GRID_W
        for g, w in enumerate(POOL_WINDOWS):
            cnt_r = jnp.minimum(r + (w - w // 2), rows) - jnp.maximum(r - w // 2, 0)
            cnt_c = jnp.minimum(c + (w - w // 2), GRID_W) - jnp.maximum(c - w // 2, 0)
            scale_ref[g] = 1.0 / (cnt_r.astype(F32) * cnt_c.astype(F32))

    def col_sum(g, start, slot):
        cols = pl.ds(g * gd, gd)
        pad_ref[slot, pl.ds(pad + start, tile), :] = jnp.dot(
            mc_ref[g], p_ref[0, pl.ds(start, tile), cols], preferred_element_type=F32)

    def row_sum(g, start, slot):
        w = POOL_WINDOWS[g]
        cols = pl.ds(g * gd, gd)
        buf = pad_ref.at[slot]
        acc = buf[pl.ds(pad + start - (w // 2) * GRID_W, tile), :]
        for j in range(1 - w // 2, w - w // 2):
            acc = acc + buf[pl.ds(pad + start + j * GRID_W, tile), :]
        mixed = acc * scale_ref[g, pl.ds(start, tile), :] - p_ref[0, pl.ds(start, tile), cols].astype(F32)
        mm = jnp.dot(mixed.astype(BF16), pw_ref[g], preferred_element_type=F32)
        zz = z_ref[0, pl.ds(start, tile), cols].astype(F32)
        out_ref[0, pl.ds(start, tile), cols] = (mm * ps_ref[:, cols] * zz).astype(out_ref.dtype)

    for n in range(P_GROUPS + 1):
        for i in range(n_tiles):
            if n < P_GROUPS:
                col_sum(n, i * tile, n % 2)
            if n > 0:
                row_sum(n - 1, i * tile, (n - 1) % 2)


def _pool(y, pool_w, pool_scale, p_col0):
    b, t, _ = y.shape
    pw = P_GROUPS * LANES
    tile = MXU_DIM
    pad = (max(POOL_WINDOWS) // 2) * GRID_W
    return pl.pallas_call(
        functools.partial(_pool_kernel, tile=tile, rows=t // GRID_W),
        grid=(b,),
        in_specs=[pl.BlockSpec((1, t, pw), lambda i: (i, 0, p_col0 // pw)),
                  pl.BlockSpec((1, t, pw), lambda i: (i, 0, p_col0 // pw + 1)),
                  pl.BlockSpec((P_GROUPS, tile, tile), lambda i: (0, 0, 0)),
                  pl.BlockSpec((P_GROUPS, LANES, LANES), lambda i: (0, 0, 0)),
                  pl.BlockSpec((1, pw), lambda i: (0, 0))],
        out_specs=pl.BlockSpec((1, t, pw), lambda i: (i, 0, 0)),
        out_shape=jax.ShapeDtypeStruct((b, t, pw), BF16),
        scratch_shapes=[pltpu.VMEM((2, t + 2 * pad, LANES), F32),
                        pltpu.VMEM((P_GROUPS, t, LANES), F32)],
        compiler_params=_cparams(("arbitrary",), 48),
        name="pool",
    )(y, y, _col_pool_matrices(tile), pool_w, pool_scale)


def _merge_kernel(x_ref, ym_ref, yp_ref, gm_ref, gp_ref, gate_ref, wm_ref, wp_ref, wo_ref, fw_ref, o_ref):
    slab = x_ref.shape[0] // MERGE_SLABS
    for r in range(MERGE_SLABS):
        rows = slice(r * slab, (r + 1) * slab)
        y_m = jnp.dot(ym_ref[rows, :], wm_ref[...], preferred_element_type=F32)
        y_p = jnp.dot(yp_ref[rows, :], wp_ref[...], preferred_element_type=F32)
        merged = (_sigmoid_t(gm_ref[rows, :].astype(F32)) * y_m
                  + _sigmoid_t(gp_ref[rows, :].astype(F32)) * y_p)
        upd = jnp.dot(merged.astype(BF16), wo_ref[...], preferred_element_type=F32)
        xo = x_ref[rows, :] + gate_ref[0] * upd
        o_ref[rows, :] = xo * lax.rsqrt(jnp.mean(xo * xo, axis=-1, keepdims=True) + EPS) * fw_ref[...]


def _merge(x2, ym, yp, y, mod, wm, wp, wo, final_w, rows_per_mod, g_col0):
    n, d = x2.shape
    tm = 1024
    per = rows_per_mod // tm
    pw = yp.shape[1]
    full = lambda shape: pl.BlockSpec(shape, lambda i: (0,) * len(shape))
    return pl.pallas_call(
        _merge_kernel,
        grid=(n // tm,),
        in_specs=[pl.BlockSpec((tm, d), lambda i: (i, 0)),
                  pl.BlockSpec((tm, d), lambda i: (i, 0)),
                  pl.BlockSpec((tm, pw), lambda i: (i, 0)),
                  pl.BlockSpec((tm, d), lambda i: (i, g_col0 // d)),
                  pl.BlockSpec((tm, d), lambda i: (i, g_col0 // d + 1)),
                  pl.BlockSpec((1, 1, d), lambda i: (i // per, 0, 2)),
                  full((d, d)), full((pw, d)), full((d, d)), full((1, d))],
        out_specs=pl.BlockSpec((tm, d), lambda i: (i, 0)),
        out_shape=jax.ShapeDtypeStruct((n, d), F32),
        compiler_params=_cparams(("arbitrary",), 48),
        name="merge",
    )(x2, ym, yp, y, y, mod, wm, wp, wo, final_w)


def kernel(x, c, ctx, c_ctx, norm_w, ada_w, ada_b, in_w, gate_b, head_norm_w, pool_w, pool_scale,
           branch_m_w, branch_p_w, out_w, final_norm_w):
    b, t, d = x.shape
    tc = ctx.shape[1]
    depth = norm_w.shape[0]
    assert depth == 1, "context stream update between layers is not implemented"
    mw = M_HEADS * MXU_DIM
    assert d == mw
    g0 = 5 * mw

    mod = _adaln(c, c_ctx, ada_w[0], ada_b[0])

    wt_f32 = jnp.transpose(in_w[0])
    nw = norm_w[0].reshape(1, d)

    g_q, g_k, g_v, g_o, g_z, g_p, g_gm, g_gp = range(8)
    pairs = ((g_q, g_k, "id"), (g_v, g_p, "half_silu"), (g_o, g_z, "gate"), (g_gm, g_gp, "id"))
    col = {name: i * mw for i, name in enumerate(("q", "k", "v", "p", "o", "oz", "g_m", "g_p"))}

    wt, wg, yc, gates_tc = _ctx_proj(ctx.reshape(b * tc, d), mod, b, nw, wt_f32, g0, g_k, g_v)
    y, gates_t = _in_proj(x.reshape(b * t, d), mod, nw, wt, wg, head_norm_w[0].reshape(1, mw).astype(BF16),
                          t, "in_proj", pairs)
    y = y.reshape(b, t, -1)
    yc = yc.reshape(b, tc, -1)

    pc, p, a, decay = _gates(gates_t, gates_tc, gate_b[0], b, CHUNK)
    ym = _mlstm(y, yc, p, a, pc, decay, CHUNK, col)
    yp = _pool(y, pool_w[0].astype(BF16), pool_scale[0].reshape(1, -1), col["p"])

    out = _merge(x.reshape(b * t, d), ym.reshape(b * t, mw), yp.reshape(b * t, -1), y.reshape(b * t, -1),
                 mod, branch_m_w[0].astype(BF16), branch_p_w[0].astype(BF16),
                 out_w[0].astype(BF16), final_norm_w.reshape(1, d), t, col["g_m"])
    return out.reshape(b, t, d)
```

```python
import functools

import numpy as np
import jax
import jax.numpy as jnp
from jax import lax
from jax.experimental import pallas as pl
from jax.experimental.pallas import tpu as pltpu

F32 = jnp.float32
BF16 = jnp.bfloat16

EPS = 1e-6
LOG2E = 1.4426950408889634
M_INIT = -1e30
NEG_BIG = -1e30
M_HEADS = 4
N_DIR = 2
N_GATE_COLS = N_DIR * 2 * M_HEADS
P_GROUPS = 4
POOL_WINDOWS = (2, 4, 8, 16)
GRID_W = 64
LANES = 128
SUBLANES = 8
MXU_DIM = 256
CHUNK = 256
MERGE_SLABS = 2
MIB = 1024 * 1024


def _cparams(sem, vmem_mib):
    return pltpu.CompilerParams(dimension_semantics=sem, vmem_limit_bytes=vmem_mib * MIB)


def _sigmoid(x):
    return 1.0 / (1.0 + jnp.exp(-x))


def _silu(x):
    return x * _sigmoid(x)


def _log_sigmoid(x):
    return jnp.minimum(x, 0.0) - jnp.log(1.0 + jnp.exp(-jnp.abs(x)))


def _dot_nt(a, b):
    return lax.dot_general(a, b, (((1,), (1,)), ((), ())), preferred_element_type=F32)


def _adaln_kernel(c_ref, cc_ref, w_ref, b_ref, o_ref):
    b = c_ref.shape[0]
    rows = jnp.concatenate([c_ref[...], jnp.broadcast_to(cc_ref[...], (o_ref.shape[0] - b, c_ref.shape[1]))], axis=0)
    o_ref[:, 0, :] = jnp.dot(_silu(rows), w_ref[...], preferred_element_type=F32) + b_ref[...]


def _adaln(c, c_ctx, ada_w, ada_b):
    b, d = c.shape
    rows = SUBLANES * (b // SUBLANES + 1)
    n = ada_w.shape[1]
    tn = d
    return pl.pallas_call(
        _adaln_kernel,
        grid=(n // tn,),
        in_specs=[pl.BlockSpec((b, d), lambda j: (0, 0)),
                  pl.BlockSpec((1, d), lambda j: (0, 0)),
                  pl.BlockSpec((d, tn), lambda j: (0, j)),
                  pl.BlockSpec((1, tn), lambda j: (0, j))],
        out_specs=pl.BlockSpec((rows, 1, tn), lambda j: (0, 0, j)),
        out_shape=jax.ShapeDtypeStruct((rows, 1, n), F32),
        compiler_params=_cparams(("arbitrary",), 32),
        name="adaln",
    )(c, c_ctx.reshape(1, d), ada_w, ada_b.reshape(1, n))


def _sigmoid_t(x):
    return 0.5 * jnp.tanh(0.5 * x) + 0.5


IN_TILE = 1024
GATE_SUB = 256


def _inproj_kernel(x_ref, sc_ref, sh_ref, nw_ref, wa_ref, wb_ref, wg_ref, gw_ref, y_ref, g_ref, xn_ref, *, acts):
    j = pl.program_id(1)

    @pl.when(j == 0)
    def _():
        x = x_ref[...]
        ms = jnp.mean(x * x, axis=-1, keepdims=True)
        xn = x * lax.rsqrt(ms + EPS) * (nw_ref[...] * (1.0 + sc_ref[0])) + sh_ref[0]
        xn_ref[...] = xn.astype(BF16)
        wg = wg_ref[...]
        wg = jnp.concatenate([wg, jnp.zeros((LANES - wg.shape[0], wg.shape[1]), wg.dtype)], axis=0)
        g_ref[...] = _dot_nt(xn_ref[...], wg).T[:N_GATE_COLS]

    def pair(act):
        tn = IN_TILE
        xn = xn_ref[...]
        if act == "gate":
            for c in range(0, tn, GATE_SUB):
                a = _dot_nt(xn, wa_ref[c:c + GATE_SUB, :])
                b = _dot_nt(xn, wb_ref[c:c + GATE_SUB, :])
                y_ref[:, c:c + GATE_SUB] = a.astype(BF16)
                ab, bb = a.astype(BF16), b.astype(BF16)
                gate = _sigmoid_t(ab) * (bb * _sigmoid_t(bb))
                y_ref[:, tn + c:tn + c + GATE_SUB] = gate * gw_ref[:, c:c + GATE_SUB]
            return
        y_ref[:, :tn] = _dot_nt(xn, wa_ref[...]).astype(BF16)
        b = _dot_nt(xn, wb_ref[...])
        if act == "half_silu":
            half = tn // 2
            zb = b[:, half:].astype(BF16)
            y_ref[:, tn:tn + half] = b[:, :half].astype(BF16)
            y_ref[:, tn + half:] = zb * _sigmoid_t(zb)
            return
        y_ref[:, tn:] = b.astype(BF16)

    for act in sorted(set(acts)):
        hit = functools.reduce(jnp.logical_or, [j == s for s, a in enumerate(acts) if a == act])
        pl.when(hit)(functools.partial(pair, act))


def _in_proj(x2, mod, norm_w, w, wg, gate_w, rows_per_mod, name, pairs):
    n, d = x2.shape
    tm = min(2048, rows_per_mod)
    tn = IN_TILE
    per = rows_per_mod // tm
    acts = tuple(p[2] for p in pairs)

    def w_group(which):
        return lambda i, j: (sum(jnp.where(j == s, p[which], 0) for s, p in enumerate(pairs)), 0)

    return pl.pallas_call(
        functools.partial(_inproj_kernel, acts=acts),
        grid=(n // tm, len(pairs)),
        in_specs=[pl.BlockSpec((tm, d), lambda i, j: (i, 0)),
                  pl.BlockSpec((1, 1, d), lambda i, j: (i // per, 0, 1)),
                  pl.BlockSpec((1, 1, d), lambda i, j: (i // per, 0, 0)),
                  pl.BlockSpec((1, d), lambda i, j: (0, 0)),
                  pl.BlockSpec((tn, d), w_group(0)),
                  pl.BlockSpec((tn, d), w_group(1)),
                  pl.BlockSpec((N_GATE_COLS, d), lambda i, j: (0, 0)),
                  pl.BlockSpec((1, tn), lambda i, j: (0, 0))],
        out_specs=[pl.BlockSpec((tm, 2 * tn), lambda i, j: (i, j)),
                   pl.BlockSpec((N_GATE_COLS, tm), lambda i, j: (0, i))],
        out_shape=[jax.ShapeDtypeStruct((n, 2 * tn * len(pairs)), BF16),
                   jax.ShapeDtypeStruct((N_GATE_COLS, n), F32)],
        scratch_shapes=[pltpu.VMEM((tm, d), BF16)],
        compiler_params=_cparams(("arbitrary", "arbitrary"), 56),
        name=name,
    )(x2, mod, mod, norm_w, w, w, wg, gate_w)


def _ctx_proj_kernel(x_ref, sc_ref, sh_ref, nw_ref, wsrc_ref, wgsrc_ref, wdst_ref, wgdst_ref, y_ref, g_ref,
                     xn_ref, kv_ref, *, steps, k_block, v_block):
    i, j = pl.program_id(0), pl.program_id(1)
    group = i * steps + j
    w = wsrc_ref[...]
    w = jnp.where(group == k_block, w * MXU_DIM ** -0.5, w).astype(BF16)
    wdst_ref[...] = w

    @pl.when(group == k_block)
    def _():
        kv_ref[0] = w

    @pl.when(group == v_block)
    def _():
        kv_ref[1] = w

    @pl.when(j == 0)
    def _():
        x = x_ref[...]
        ms = jnp.mean(x * x, axis=-1, keepdims=True)
        xn = x * lax.rsqrt(ms + EPS) * (nw_ref[...] * (1.0 + sc_ref[0])) + sh_ref[0]
        xn_ref[...] = xn.astype(BF16)
        wg = wgsrc_ref[...].astype(BF16)
        wgdst_ref[...] = wg
        wg = jnp.concatenate([wg, jnp.zeros((LANES - wg.shape[0], wg.shape[1]), wg.dtype)], axis=0)
        g_ref[...] = _dot_nt(xn_ref[...], wg).T[:N_GATE_COLS]

    @pl.when(j == k_block)
    def _():
        y_ref[:, :IN_TILE] = _dot_nt(xn_ref[...], kv_ref[0]).astype(BF16)

    @pl.when(j == v_block)
    def _():
        y_ref[:, IN_TILE:] = _dot_nt(xn_ref[...], kv_ref[1]).astype(BF16)


def _ctx_proj(xc, mod, mod_row, norm_w, wt, gate_row0, k_block, v_block):
    n, d = xc.shape
    tn = IN_TILE
    n_groups = (wt.shape[0] - N_GATE_COLS) // tn
    row_tiles = 2
    tm, steps = n // row_tiles, n_groups // row_tiles
    assert gate_row0 % tn == 0 and n_groups % row_tiles == 0 and 0 < k_block < v_block < steps

    def src_rows(i, j):
        r = (i * steps + j) * tn
        return (pl.multiple_of(r + jnp.where(r >= gate_row0, N_GATE_COLS, 0), N_GATE_COLS), 0)

    return pl.pallas_call(
        functools.partial(_ctx_proj_kernel, steps=steps, k_block=k_block, v_block=v_block),
        grid=(row_tiles, steps),
        in_specs=[pl.BlockSpec((tm, d), lambda i, j: (i, 0)),
                  pl.BlockSpec((1, 1, d), lambda i, j: (mod_row, 0, 1)),
                  pl.BlockSpec((1, 1, d), lambda i, j: (mod_row, 0, 0)),
                  pl.BlockSpec((1, d), lambda i, j: (0, 0)),
                  pl.BlockSpec((pl.Element(tn), pl.Element(d)), src_rows),
                  pl.BlockSpec((pl.Element(N_GATE_COLS), pl.Element(d)), lambda i, j: (gate_row0, 0))],
        out_specs=[pl.BlockSpec((tn, d), lambda i, j: (i * steps + j, 0)),
                   pl.BlockSpec((N_GATE_COLS, d), lambda i, j: (0, 0)),
                   pl.BlockSpec((tm, 2 * tn), lambda i, j: (i, 0)),
                   pl.BlockSpec((N_GATE_COLS, tm), lambda i, j: (0, i))],
        out_shape=[jax.ShapeDtypeStruct((n_groups * tn, d), BF16),
                   jax.ShapeDtypeStruct((N_GATE_COLS, d), BF16),
                   jax.ShapeDtypeStruct((n, 2 * tn), BF16),
                   jax.ShapeDtypeStruct((N_GATE_COLS, n), F32)],
        scratch_shapes=[pltpu.VMEM((tm, d), BF16), pltpu.VMEM((2, tn, d), BF16)],
        compiler_params=_cparams(("arbitrary", "arbitrary"), 56),
        name="ctx_proj",
    )(xc, mod, mod, norm_w, wt, wt)


N_DH = N_DIR * M_HEADS
TERM_PIECES = (3, 2, 2, 3)


def _split(x, pieces):
    out = []
    for _ in range(pieces):
        p = x.astype(BF16).astype(F32)
        out.append(p)
        x = x - p
    return out


def _chunk_scan(x, pos, chunk, op, reverse):
    t = x.shape[1]
    k = 1
    while k < chunk:
        if reverse:
            x = jnp.where(pos < chunk - k, op(x, pltpu.roll(x, t - k, 1)), x)
        else:
            x = jnp.where(pos >= k, op(x, pltpu.roll(x, k, 1)), x)
        k *= 2
    return x


def _gate_terms(li, pre_f, m0, chunk):
    rows, t = li.shape
    nc = t // chunk
    pos = lax.broadcasted_iota(jnp.int32, (rows, t), 1) % chunk
    rev = lax.broadcasted_iota(jnp.int32, (rows, t), 0) % N_DH >= M_HEADS
    rev1 = lax.broadcasted_iota(jnp.int32, (rows, 1), 0) % N_DH >= M_HEADS
    lf = _log_sigmoid(pre_f)
    b = jnp.where(rev, _chunk_scan(lf, pos, chunk, jnp.add, True), _chunk_scan(lf, pos, chunk, jnp.add, False))
    a = li - b
    cmax = jnp.where(rev, _chunk_scan(a, pos, chunk, jnp.maximum, True),
                     _chunk_scan(a, pos, chunk, jnp.maximum, False))

    def at_scan_end(x, c):
        lo = c * chunk
        return jnp.where(rev1, x[:, lo:lo + 1], x[:, lo + chunk - 1:lo + chunk])

    b_end = [at_scan_end(b, c) for c in range(nc)]
    a_max = [at_scan_end(cmax, c) for c in range(nc)]
    m_f, m_b = m0, m0
    m_in_f, m_in_b = [None] * nc, [None] * nc
    for i in range(nc):
        j = nc - 1 - i
        m_in_f[i] = m_f
        m_f = b_end[i] + jnp.maximum(m_f, a_max[i])
        m_in_b[j] = m_b
        m_b = b_end[j] + jnp.maximum(m_b, a_max[j])
    m_in = [jnp.where(rev1, m_in_b[c], m_in_f[c]) for c in range(nc)]
    per_chunk = lambda vals: jnp.concatenate([jnp.broadcast_to(v, (rows, chunk)) for v in vals], axis=1)
    m_in_t = per_chunk(m_in)
    g = jnp.maximum(m_in_t, cmax)
    g_end = [jnp.maximum(m_in[c], a_max[c]) for c in range(nc)]
    pieces = []
    terms = (-g * LOG2E, jnp.exp(m_in_t - g), jnp.exp(a - per_chunk(g_end)), -(b + g) * LOG2E)
    for term, n in zip(terms, TERM_PIECES):
        pieces += _split(term, n)
    decay = jnp.concatenate([jnp.exp(m_in[c] - g_end[c]) for c in range(nc)], axis=1)
    return pieces, a * LOG2E, decay, jnp.where(rev1, m_b, m_f)


def _gates_kernel(gc_ref, g_ref, b_ref, rc_ref, dc_ref, r_ref, a_ref, d_ref, *, chunk):
    nb = r_ref.shape[0]

    def split(ref):
        t = ref.shape[1] // nb
        li, pre_f = [], []
        for e in range(nb):
            pre = ref[:, e * t:(e + 1) * t] + b_ref[...]
            for d in range(N_DIR):
                li.append(pre[2 * d * M_HEADS:(2 * d + 1) * M_HEADS])
                pre_f.append(pre[(2 * d + 1) * M_HEADS:(2 * d + 2) * M_HEADS])
        return jnp.concatenate(li, axis=0), jnp.concatenate(pre_f, axis=0)

    def store_pieces(ref, pieces):
        pad = jnp.zeros((LANES - N_DH * len(pieces), pieces[0].shape[1]), F32)
        for e in range(nb):
            own = [p[e * N_DH:(e + 1) * N_DH] for p in pieces]
            ref[e] = jnp.concatenate(own + [pad], axis=0).astype(BF16)

    m0 = jnp.full((nb * N_DH, 1), M_INIT, F32)
    pieces_c, _, decay_c, m_ctx = _gate_terms(*split(gc_ref), m0, chunk)
    store_pieces(rc_ref, pieces_c)
    pieces, a, decay, _ = _gate_terms(*split(g_ref), m_ctx, chunk)
    store_pieces(r_ref, pieces)
    for e in range(nb):
        own = slice(e * N_DH, (e + 1) * N_DH)
        dc_ref[e] = decay_c[own]
        d_ref[e] = decay[own]
        for c in range(a_ref.shape[1]):
            a_ref[e, c] = a[own, c * chunk:(c + 1) * chunk]


def _gates(gates_t, gates_tc, gate_b, b, chunk):
    t = gates_t.shape[1] // b
    tc = gates_tc.shape[1] // b
    nc, ncc = t // chunk, tc // chunk
    nb = b
    spec = lambda r, n: pl.BlockSpec((nb, r, n), lambda i: (i, 0, 0))
    seq = lambda n: pl.BlockSpec((N_GATE_COLS, nb * n), lambda i: (0, i))
    rc, dc, r, a, dec = pl.pallas_call(
        functools.partial(_gates_kernel, chunk=chunk),
        grid=(b // nb,),
        in_specs=[seq(tc), seq(t), pl.BlockSpec((N_GATE_COLS, 1), lambda i: (0, 0))],
        out_specs=[spec(LANES, tc), spec(N_DH, ncc), spec(LANES, t),
                   pl.BlockSpec((nb, nc, N_DH, chunk), lambda i: (i, 0, 0, 0)), spec(N_DH, nc)],
        out_shape=[jax.ShapeDtypeStruct((b, LANES, tc), BF16), jax.ShapeDtypeStruct((b, N_DH, ncc), F32),
                   jax.ShapeDtypeStruct((b, LANES, t), BF16), jax.ShapeDtypeStruct((b, nc, N_DH, chunk), F32),
                   jax.ShapeDtypeStruct((b, N_DH, nc), F32)],
        compiler_params=_cparams(("arbitrary",), 32),
        name="gates",
    )(gates_tc, gates_t, gate_b.reshape(N_GATE_COLS, 1))
    decay = jnp.concatenate([dc, dec], axis=2).reshape(b * N_DH, ncc + nc)
    return rc, r, a, decay


def _twice(x):
    return jnp.concatenate([x, x], axis=1)


def _mlstm_kernel(dec_ref, q_ref, k_ref, v_ref, oz_ref, p_ref, a_ref, kc_ref, vc_ref, pc_ref,
                  sel_ref, out_ref, ct_ref, n_ref, hs_ref, *, chunk):
    t = q_ref.shape[1]
    nc = t // chunk
    ncc = kc_ref.shape[1] // chunk
    bi, hi = pl.program_id(0), pl.program_id(1)
    t_idx = lax.broadcasted_iota(jnp.int32, (chunk, chunk), 0)
    s_idx = lax.broadcasted_iota(jnp.int32, (chunk, chunk), 1)

    def bcast_terms(pieces, sel):
        return lax.dot_general(pieces, sel, (((0,), (0,)), ((), ())), preferred_element_type=F32)

    def update(k, v, e_b, decay, d):
        ke = k.astype(F32) * _twice(e_b)
        upd = lax.dot_general(ke.astype(BF16), v, (((0,), (0,)), ((), ())), preferred_element_type=F32)
        n_new = jnp.sum(ke, axis=0, keepdims=True)
        ct_ref[d] = upd if decay is None else decay * ct_ref[d] + upd
        n_ref[d] = n_new if decay is None else decay * n_ref[d] + n_new

    def decay_of(c, d):
        return dec_ref[bi * N_DH + d * M_HEADS + hi, c]

    for d in range(N_DIR):
        for i in range(ncc):
            c = i if d == 0 else ncc - 1 - i
            rows = slice(c * chunk, (c + 1) * chunk)
            e_b = bcast_terms(pc_ref[0, :, rows], sel_ref[0, d, :, 2 * LANES:3 * LANES])
            update(kc_ref[0, rows, :], vc_ref[0, rows, :], e_b, None if i == 0 else decay_of(c, d), d)

    def run_chunk(c, d):
        rows = slice(c * chunk, (c + 1) * chunk)
        q, k, v = q_ref[0, rows, :], k_ref[0, rows, :], v_ref[0, rows, :]
        w = bcast_terms(p_ref[0, :, rows], sel_ref[0, d])
        neg_g_b, w_inter_b, e_b, neg_mt_b = [w[:, i * LANES:(i + 1) * LANES] for i in range(len(TERM_PIECES))]
        a_row = a_ref[0, c, pl.ds(d * M_HEADS + hi, 1), :]
        mask = (s_idx <= t_idx) if d == 0 else (s_idx >= t_idx)
        w_intra = jnp.exp2(jnp.where(mask, a_row + _twice(neg_g_b), NEG_BIG))
        s = _dot_nt(q, k) * w_intra
        num = (_twice(w_inter_b) * jnp.dot(q, ct_ref[d].astype(BF16), preferred_element_type=F32)
               + jnp.dot(s.astype(BF16), v, preferred_element_type=F32))
        den_b = (w_inter_b * jnp.sum(q.astype(F32) * n_ref[d], axis=-1, keepdims=True)
                 + jnp.sum(s, axis=-1, keepdims=True))
        h = num * _twice(1.0 / jnp.maximum(jnp.abs(den_b), jnp.exp2(neg_mt_b)))
        update(k, v, e_b, decay_of(ncc + c, d), d)
        return rows, h

    def finalize(rows, h):
        hn = h * lax.rsqrt(jnp.mean(h * h, axis=-1, keepdims=True) + EPS)
        out_ref[0, rows, :] = hn.astype(BF16) * oz_ref[0, rows, :]

    for i in range(nc):
        for d in range(N_DIR):
            rows, h = run_chunk(i if d == 0 else nc - 1 - i, d)
            if i < nc // 2:
                hs_ref[rows, :] = h
            else:
                finalize(rows, hs_ref[rows, :] + h)


def _term_selectors():
    sel = np.zeros((M_HEADS, N_DIR, LANES, len(TERM_PIECES) * LANES), np.float32)
    first = np.concatenate([[0], np.cumsum(TERM_PIECES)])
    for h in range(M_HEADS):
        for d in range(N_DIR):
            for q in range(len(TERM_PIECES)):
                for j in range(first[q], first[q + 1]):
                    sel[h, d, j * N_DH + d * M_HEADS + h, q * LANES:(q + 1) * LANES] = 1.0
    return jnp.asarray(sel, BF16)


def _mlstm(y, yc, p, a, pc, decay, chunk, col):
    b, t, _ = y.shape
    tc = yc.shape[1]
    dh = MXU_DIM
    sel = _term_selectors()
    assert t % (2 * chunk) == 0 and tc % chunk == 0 and tc >= chunk
    blk = lambda name: pl.BlockSpec((1, t, dh), lambda i, h, c=col[name] // dh: (i, 0, c + h))
    blkc = lambda col0: pl.BlockSpec((1, tc, dh), lambda i, h, c=col0 // dh: (i, 0, c + h))
    return pl.pallas_call(
        functools.partial(_mlstm_kernel, chunk=chunk),
        grid=(b, M_HEADS),
        in_specs=[pl.BlockSpec(memory_space=pltpu.SMEM),
                  blk("q"), blk("k"), blk("v"), blk("oz"),
                  pl.BlockSpec((1, LANES, t), lambda i, h: (i, 0, 0)),
                  pl.BlockSpec((1, t // chunk, N_DH, chunk), lambda i, h: (i, 0, 0, 0)),
                  blkc(0), blkc(M_HEADS * dh),
                  pl.BlockSpec((1, LANES, tc), lambda i, h: (i, 0, 0)),
                  pl.BlockSpec((1, N_DIR) + sel.shape[2:], lambda i, h: (h, 0, 0, 0))],
        out_specs=pl.BlockSpec((1, t, dh), lambda i, h: (i, 0, h)),
        out_shape=jax.ShapeDtypeStruct((b, t, M_HEADS * dh), BF16),
        scratch_shapes=[pltpu.VMEM((N_DIR, dh, dh), F32),
                        pltpu.VMEM((N_DIR, 1, dh), F32),
                        pltpu.VMEM((t, dh), F32)],
        compiler_params=_cparams(("arbitrary", "arbitrary"), 60),
        name="mlstm",
    )(decay, y, y, y, y, p, a, yc, yc, pc, sel)


def _window_bounds(pos, w, length):
    return np.clip(pos - w // 2, 0, length), np.clip(pos + w - w // 2, 0, length)


def _col_pool_matrices(tile):
    pos = np.arange(GRID_W)
    mats = []
    for w in POOL_WINDOWS:
        lo, hi = _window_bounds(pos, w, GRID_W)
        band = ((pos[None, :] >= lo[:, None]) & (pos[None, :] < hi[:, None])).astype(np.float32)
        mats.append(np.kron(np.eye(tile // GRID_W, dtype=np.float32), band))
    return jnp.asarray(np.stack(mats), dtype=BF16)


def _pool_kernel(p_ref, z_ref, mc_ref, pw_ref, ps_ref, out_ref, pad_ref, scale_ref, *, tile, rows):
    t = p_ref.shape[1]
    gd = LANES
    pad = (max(POOL_WINDOWS) // 2) * GRID_W
    n_tiles = t // tile

    @pl.when(pl.program_id(0) == 0)
    def _():
        pad_ref[...] = jnp.zeros(pad_ref.shape, F32)
        tok = lax.broadcasted_iota(jnp.int32, (t, gd), 0)
        r = tok // GRID_W
        c = tok % GRID_W
        for g, w in enumerate(POOL_WINDOWS):
            cnt_r = jnp.minimum(r + (w - w // 2), rows) - jnp.maximum(r - w // 2, 0)
            cnt_c = jnp.minimum(c + (w - w // 2), GRID_W) - jnp.maximum(c - w // 2, 0)
            scale_ref[g] = 1.0 / (cnt_r.astype(F32) * cnt_c.astype(F32))

    def col_sum(g, start, slot):
        cols = pl.ds(g * gd, gd)
        pad_ref[slot, pl.ds(pad + start, tile), :] = jnp.dot(
            mc_ref[g], p_ref[0, pl.ds(start, tile), cols], preferred_element_type=F32)

    def row_sum(g, start, slot):
        w = POOL_WINDOWS[g]
        cols = pl.ds(g * gd, gd)
        buf = pad_ref.at[slot]
        acc = buf[pl.ds(pad + start - (w // 2) * GRID_W, tile), :]
        for j in range(1 - w // 2, w - w // 2):
            acc = acc + buf[pl.ds(pad + start + j * GRID_W, tile), :]
        mixed = acc * scale_ref[g, pl.ds(start, tile), :] - p_ref[0, pl.ds(start, tile), cols].astype(F32)
        mm = jnp.dot(mixed.astype(BF16), pw_ref[g], preferred_element_type=F32)
        zz = z_ref[0, pl.ds(start, tile), cols].astype(F32)
        out_ref[0, pl.ds(start, tile), cols] = (mm * ps_ref[:, cols] * zz).astype(out_ref.dtype)

    for n in range(P_GROUPS + 1):
        for i in range(n_tiles):
            if n < P_GROUPS:
                col_sum(n, i * tile, n % 2)
            if n > 0:
                row_sum(n - 1, i * tile, (n - 1) % 2)


def _pool(y, pool_w, pool_scale, p_col0):
    b, t, _ = y.shape
    pw = P_GROUPS * LANES
    tile = MXU_DIM
    pad = (max(POOL_WINDOWS) // 2) * GRID_W
    return pl.pallas_call(
        functools.partial(_pool_kernel, tile=tile, rows=t // GRID_W),
        grid=(b,),
        in_specs=[pl.BlockSpec((1, t, pw), lambda i: (i, 0, p_col0 // pw)),
                  pl.BlockSpec((1, t, pw), lambda i: (i, 0, p_col0 // pw + 1)),
                  pl.BlockSpec((P_GROUPS, tile, tile), lambda i: (0, 0, 0)),
                  pl.BlockSpec((P_GROUPS, LANES, LANES), lambda i: (0, 0, 0)),
                  pl.BlockSpec((1, pw), lambda i: (0, 0))],
        out_specs=pl.BlockSpec((1, t, pw), lambda i: (i, 0, 0)),
        out_shape=jax.ShapeDtypeStruct((b, t, pw), BF16),
        scratch_shapes=[pltpu.VMEM((2, t + 2 * pad, LANES), F32),
                        pltpu.VMEM((P_GROUPS, t, LANES), F32)],
        compiler_params=_cparams(("arbitrary",), 60),
        name="pool",
    )(y, y, _col_pool_matrices(tile), pool_w, pool_scale)


def _merge_kernel(x_ref, ym_ref, yp_ref, gm_ref, gp_ref, gate_ref, wm_ref, wp_ref, wo_ref, fw_ref, o_ref):
    slab = x_ref.shape[0] // MERGE_SLABS
    for r in range(MERGE_SLABS):
        rows = slice(r * slab, (r + 1) * slab)
        y_m = jnp.dot(ym_ref[rows, :], wm_ref[...], preferred_element_type=F32)
        y_p = jnp.dot(yp_ref[rows, :], wp_ref[...], preferred_element_type=F32)
        merged = (_sigmoid_t(gm_ref[rows, :].astype(F32)) * y_m
                  + _sigmoid_t(gp_ref[rows, :].astype(F32)) * y_p)
        upd = jnp.dot(merged.astype(BF16), wo_ref[...], preferred_element_type=F32)
        xo = x_ref[rows, :] + gate_ref[0] * upd
        o_ref[rows, :] = xo * lax.rsqrt(jnp.mean(xo * xo, axis=-1, keepdims=True) + EPS) * fw_ref[...]


def _merge(x2, ym, yp, y, mod, wm, wp, wo, final_w, rows_per_mod, g_col0):
    n, d = x2.shape
    tm = 1024
    per = rows_per_mod // tm
    pw = yp.shape[1]
    full = lambda shape: pl.BlockSpec(shape, lambda i: (0,) * len(shape))
    return pl.pallas_call(
        _merge_kernel,
        grid=(n // tm,),
        in_specs=[pl.BlockSpec((tm, d), lambda i: (i, 0)),
                  pl.BlockSpec((tm, d), lambda i: (i, 0)),
                  pl.BlockSpec((tm, pw), lambda i: (i, 0)),
                  pl.BlockSpec((tm, d), lambda i: (i, g_col0 // d)),
                  pl.BlockSpec((tm, d), lambda i: (i, g_col0 // d + 1)),
                  pl.BlockSpec((1, 1, d), lambda i: (i // per, 0, 2)),
                  full((d, d)), full((pw, d)), full((d, d)), full((1, d))],
        out_specs=pl.BlockSpec((tm, d), lambda i: (i, 0)),
        out_shape=jax.ShapeDtypeStruct((n, d), F32),
        compiler_params=_cparams(("arbitrary",), 48),
        name="merge",
    )(x2, ym, yp, y, y, mod, wm, wp, wo, final_w)


def kernel(x, c, ctx, c_ctx, norm_w, ada_w, ada_b, in_w, gate_b, head_norm_w, pool_w, pool_scale,
           branch_m_w, branch_p_w, out_w, final_norm_w):
    b, t, d = x.shape
    tc = ctx.shape[1]
    depth = norm_w.shape[0]
    assert depth == 1, "context stream update between layers is not implemented"
    mw = M_HEADS * MXU_DIM
    assert d == mw
    g0 = 5 * mw

    mod = _adaln(c, c_ctx, ada_w[0], ada_b[0])

    wt_f32 = jnp.transpose(in_w[0])
    nw = norm_w[0].reshape(1, d)

    g_q, g_k, g_v, g_o, g_z, g_p, g_gm, g_gp = range(8)
    pairs = ((g_q, g_k, "id"), (g_v, g_p, "half_silu"), (g_o, g_z, "gate"), (g_gm, g_gp, "id"))
    col = {name: i * mw for i, name in enumerate(("q", "k", "v", "p", "o", "oz", "g_m", "g_p"))}

    wt, wg, yc, gates_tc = _ctx_proj(ctx.reshape(b * tc, d), mod, b, nw, wt_f32, g0, g_k, g_v)
    y, gates_t = _in_proj(x.reshape(b * t, d), mod, nw, wt, wg, head_norm_w[0].reshape(1, mw).astype(BF16),
                          t, "in_proj", pairs)
    y = y.reshape(b, t, -1)
    yc = yc.reshape(b, tc, -1)

    pc, p, a, decay = _gates(gates_t, gates_tc, gate_b[0], b, CHUNK)
    ym = _mlstm(y, yc, p, a, pc, decay, CHUNK, col)
    yp = _pool(y, pool_w[0].astype(BF16), pool_scale[0].reshape(1, -1), col["p"])

    out = _merge(x.reshape(b * t, d), ym.reshape(b * t, mw), yp.reshape(b * t, -1), y.reshape(b * t, -1),
                 mod, branch_m_w[0].astype(BF16), branch_p_w[0].astype(BF16),
                 out_w[0].astype(BF16), final_norm_w.reshape(1, d), t, col["g_m"])
    return out.reshape(b, t, d)
```

```python
import functools

import numpy as np
import jax
import jax.numpy as jnp
from jax import lax
from jax.experimental import pallas as pl
from jax.experimental.pallas import tpu as pltpu

F32 = jnp.float32
BF16 = jnp.bfloat16

EPS = 1e-6
LOG2E = 1.4426950408889634
M_INIT = -1e30
NEG_BIG = -1e30
M_HEADS = 4
N_DIR = 2
N_GATE_COLS = N_DIR * 2 * M_HEADS
P_GROUPS = 4
POOL_WINDOWS = (2, 4, 8, 16)
GRID_W = 64
LANES = 128
SUBLANES = 8
MXU_DIM = 256
CHUNK = 256
MERGE_SLABS = 2
MIB = 1024 * 1024


def _cparams(sem, vmem_mib, fusible_inputs=None):
    return pltpu.CompilerParams(dimension_semantics=sem, vmem_limit_bytes=vmem_mib * MIB,
                                allow_input_fusion=fusible_inputs)


def _sigmoid(x):
    return 1.0 / (1.0 + jnp.exp(-x))


def _silu(x):
    return x * _sigmoid(x)


def _log_sigmoid(x):
    return jnp.minimum(x, 0.0) - jnp.log(1.0 + jnp.exp(-jnp.abs(x)))


def _dot_nt(a, b):
    return lax.dot_general(a, b, (((1,), (1,)), ((), ())), preferred_element_type=F32)


def _adaln_kernel(c_ref, cc_ref, w_ref, b_ref, o_ref):
    b = c_ref.shape[0]
    rows = jnp.concatenate([c_ref[...], jnp.broadcast_to(cc_ref[...], (o_ref.shape[0] - b, c_ref.shape[1]))], axis=0)
    o_ref[:, 0, :] = jnp.dot(_silu(rows), w_ref[...], preferred_element_type=F32) + b_ref[...]


def _adaln(c, c_ctx, ada_w, ada_b):
    b, d = c.shape
    rows = SUBLANES * (b // SUBLANES + 1)
    n = ada_w.shape[1]
    tn = d
    return pl.pallas_call(
        _adaln_kernel,
        grid=(n // tn,),
        in_specs=[pl.BlockSpec((b, d), lambda j: (0, 0)),
                  pl.BlockSpec((1, d), lambda j: (0, 0)),
                  pl.BlockSpec((d, tn), lambda j: (0, j)),
                  pl.BlockSpec((1, tn), lambda j: (0, j))],
        out_specs=pl.BlockSpec((rows, 1, tn), lambda j: (0, 0, j)),
        out_shape=jax.ShapeDtypeStruct((rows, 1, n), F32),
        compiler_params=_cparams(("arbitrary",), 32),
        name="adaln",
    )(c, c_ctx.reshape(1, d), ada_w, ada_b.reshape(1, n))


def _sigmoid_t(x):
    return 0.5 * jnp.tanh(0.5 * x) + 0.5


IN_TILE = 1024
GATE_SUB = 256


def _inproj_kernel(x_ref, sc_ref, sh_ref, nw_ref, wa_ref, wb_ref, wg_ref, gw_ref, y_ref, g_ref, xn_ref, *, acts):
    j = pl.program_id(1)

    @pl.when(j == 0)
    def _():
        x = x_ref[...]
        ms = jnp.mean(x * x, axis=-1, keepdims=True)
        xn = x * lax.rsqrt(ms + EPS) * (nw_ref[...] * (1.0 + sc_ref[0])) + sh_ref[0]
        xn_ref[...] = xn.astype(BF16)
        wg = wg_ref[...]
        wg = jnp.concatenate([wg, jnp.zeros((LANES - wg.shape[0], wg.shape[1]), wg.dtype)], axis=0)
        g_ref[...] = _dot_nt(xn_ref[...], wg).T[:N_GATE_COLS]

    def pair(act):
        tn = IN_TILE
        xn = xn_ref[...]
        if act == "gate":
            for c in range(0, tn, GATE_SUB):
                a = _dot_nt(xn, wa_ref[c:c + GATE_SUB, :])
                b = _dot_nt(xn, wb_ref[c:c + GATE_SUB, :])
                y_ref[:, c:c + GATE_SUB] = a.astype(BF16)
                ab, bb = a.astype(BF16), b.astype(BF16)
                gate = _sigmoid_t(ab) * (bb * _sigmoid_t(bb))
                y_ref[:, tn + c:tn + c + GATE_SUB] = gate * gw_ref[:, c:c + GATE_SUB]
            return
        y_ref[:, :tn] = _dot_nt(xn, wa_ref[...]).astype(BF16)
        b = _dot_nt(xn, wb_ref[...])
        if act == "half_silu":
            half = tn // 2
            zb = b[:, half:].astype(BF16)
            y_ref[:, tn:tn + half] = b[:, :half].astype(BF16)
            y_ref[:, tn + half:] = zb * _sigmoid_t(zb)
            return
        y_ref[:, tn:] = b.astype(BF16)

    for act in sorted(set(acts)):
        hit = functools.reduce(jnp.logical_or, [j == s for s, a in enumerate(acts) if a == act])
        pl.when(hit)(functools.partial(pair, act))


def _in_proj(x2, mod, norm_w, w, wg, gate_w, rows_per_mod, name, pairs):
    n, d = x2.shape
    tm = min(2048, rows_per_mod)
    tn = IN_TILE
    per = rows_per_mod // tm
    acts = tuple(p[2] for p in pairs)

    def w_group(which):
        return lambda i, j: (sum(jnp.where(j == s, p[which], 0) for s, p in enumerate(pairs)), 0)

    return pl.pallas_call(
        functools.partial(_inproj_kernel, acts=acts),
        grid=(n // tm, len(pairs)),
        in_specs=[pl.BlockSpec((tm, d), lambda i, j: (i, 0)),
                  pl.BlockSpec((1, 1, d), lambda i, j: (i // per, 0, 1)),
                  pl.BlockSpec((1, 1, d), lambda i, j: (i // per, 0, 0)),
                  pl.BlockSpec((1, d), lambda i, j: (0, 0)),
                  pl.BlockSpec((tn, d), w_group(0)),
                  pl.BlockSpec((tn, d), w_group(1)),
                  pl.BlockSpec((N_GATE_COLS, d), lambda i, j: (0, 0)),
                  pl.BlockSpec((1, tn), lambda i, j: (0, 0))],
        out_specs=[pl.BlockSpec((tm, 2 * tn), lambda i, j: (i, j)),
                   pl.BlockSpec((N_GATE_COLS, tm), lambda i, j: (0, i))],
        out_shape=[jax.ShapeDtypeStruct((n, 2 * tn * len(pairs)), BF16),
                   jax.ShapeDtypeStruct((N_GATE_COLS, n), F32)],
        scratch_shapes=[pltpu.VMEM((tm, d), BF16)],
        compiler_params=_cparams(("arbitrary", "arbitrary"), 56),
        name=name,
    )(x2, mod, mod, norm_w, w, w, wg, gate_w)


def _ctx_proj_kernel(x_ref, sc_ref, sh_ref, nw_ref, wsrc_ref, wgsrc_ref, wdst_ref, wgdst_ref, y_ref, g_ref,
                     xn_ref, kv_ref, *, steps, k_block, v_block):
    i, j = pl.program_id(0), pl.program_id(1)
    group = i * steps + j
    w = wsrc_ref[...]
    w = jnp.where(group == k_block, w * MXU_DIM ** -0.5, w).astype(BF16)
    wdst_ref[...] = w

    @pl.when(group == k_block)
    def _():
        kv_ref[0] = w

    @pl.when(group == v_block)
    def _():
        kv_ref[1] = w

    @pl.when(j == 0)
    def _():
        x = x_ref[...]
        ms = jnp.mean(x * x, axis=-1, keepdims=True)
        xn = x * lax.rsqrt(ms + EPS) * (nw_ref[...] * (1.0 + sc_ref[0])) + sh_ref[0]
        xn_ref[...] = xn.astype(BF16)
        wg = wgsrc_ref[...].astype(BF16)
        wgdst_ref[...] = wg
        wg = jnp.concatenate([wg, jnp.zeros((LANES - wg.shape[0], wg.shape[1]), wg.dtype)], axis=0)
        g_ref[...] = _dot_nt(xn_ref[...], wg).T[:N_GATE_COLS]

    @pl.when(j == k_block)
    def _():
        y_ref[:, :IN_TILE] = _dot_nt(xn_ref[...], kv_ref[0]).astype(BF16)

    @pl.when(j == v_block)
    def _():
        y_ref[:, IN_TILE:] = _dot_nt(xn_ref[...], kv_ref[1]).astype(BF16)


def _ctx_proj(xc, mod, mod_row, norm_w, wt, gate_row0, k_block, v_block):
    n, d = xc.shape
    tn = IN_TILE
    n_groups = (wt.shape[0] - N_GATE_COLS) // tn
    row_tiles = 2
    tm, steps = n // row_tiles, n_groups // row_tiles
    assert gate_row0 % tn == 0 and n_groups % row_tiles == 0 and 0 < k_block < v_block < steps

    def src_rows(i, j):
        r = (i * steps + j) * tn
        return (pl.multiple_of(r + jnp.where(r >= gate_row0, N_GATE_COLS, 0), N_GATE_COLS), 0)

    return pl.pallas_call(
        functools.partial(_ctx_proj_kernel, steps=steps, k_block=k_block, v_block=v_block),
        grid=(row_tiles, steps),
        in_specs=[pl.BlockSpec((tm, d), lambda i, j: (i, 0)),
                  pl.BlockSpec((1, 1, d), lambda i, j: (mod_row, 0, 1)),
                  pl.BlockSpec((1, 1, d), lambda i, j: (mod_row, 0, 0)),
                  pl.BlockSpec((1, d), lambda i, j: (0, 0)),
                  pl.BlockSpec((pl.Element(tn), pl.Element(d)), src_rows),
                  pl.BlockSpec((pl.Element(N_GATE_COLS), pl.Element(d)), lambda i, j: (gate_row0, 0))],
        out_specs=[pl.BlockSpec((tn, d), lambda i, j: (i * steps + j, 0)),
                   pl.BlockSpec((N_GATE_COLS, d), lambda i, j: (0, 0)),
                   pl.BlockSpec((tm, 2 * tn), lambda i, j: (i, 0)),
                   pl.BlockSpec((N_GATE_COLS, tm), lambda i, j: (0, i))],
        out_shape=[jax.ShapeDtypeStruct((n_groups * tn, d), BF16),
                   jax.ShapeDtypeStruct((N_GATE_COLS, d), BF16),
                   jax.ShapeDtypeStruct((n, 2 * tn), BF16),
                   jax.ShapeDtypeStruct((N_GATE_COLS, n), F32)],
        scratch_shapes=[pltpu.VMEM((tm, d), BF16), pltpu.VMEM((2, tn, d), BF16)],
        compiler_params=_cparams(("arbitrary", "arbitrary"), 56),
        name="ctx_proj",
    )(xc, mod, mod, norm_w, wt, wt)


N_DH = N_DIR * M_HEADS
TERM_PIECES = (3, 2, 2, 3)


def _split(x, pieces):
    out = []
    for _ in range(pieces):
        p = x.astype(BF16).astype(F32)
        out.append(p)
        x = x - p
    return out


def _chunk_scan(x, pos, chunk, op, reverse):
    t = x.shape[1]
    k = 1
    while k < chunk:
        if reverse:
            x = jnp.where(pos < chunk - k, op(x, pltpu.roll(x, t - k, 1)), x)
        else:
            x = jnp.where(pos >= k, op(x, pltpu.roll(x, k, 1)), x)
        k *= 2
    return x


def _gate_terms(li, pre_f, m0, chunk):
    rows, t = li.shape
    nc = t // chunk
    pos = lax.broadcasted_iota(jnp.int32, (rows, t), 1) % chunk
    rev = lax.broadcasted_iota(jnp.int32, (rows, t), 0) % N_DH >= M_HEADS
    rev1 = lax.broadcasted_iota(jnp.int32, (rows, 1), 0) % N_DH >= M_HEADS
    lf = _log_sigmoid(pre_f)
    b = jnp.where(rev, _chunk_scan(lf, pos, chunk, jnp.add, True), _chunk_scan(lf, pos, chunk, jnp.add, False))
    a = li - b
    cmax = jnp.where(rev, _chunk_scan(a, pos, chunk, jnp.maximum, True),
                     _chunk_scan(a, pos, chunk, jnp.maximum, False))

    def at_scan_end(x, c):
        lo = c * chunk
        return jnp.where(rev1, x[:, lo:lo + 1], x[:, lo + chunk - 1:lo + chunk])

    b_end = [at_scan_end(b, c) for c in range(nc)]
    a_max = [at_scan_end(cmax, c) for c in range(nc)]
    m_f, m_b = m0, m0
    m_in_f, m_in_b = [None] * nc, [None] * nc
    for i in range(nc):
        j = nc - 1 - i
        m_in_f[i] = m_f
        m_f = b_end[i] + jnp.maximum(m_f, a_max[i])
        m_in_b[j] = m_b
        m_b = b_end[j] + jnp.maximum(m_b, a_max[j])
    m_in = [jnp.where(rev1, m_in_b[c], m_in_f[c]) for c in range(nc)]
    per_chunk = lambda vals: jnp.concatenate([jnp.broadcast_to(v, (rows, chunk)) for v in vals], axis=1)
    m_in_t = per_chunk(m_in)
    g = jnp.maximum(m_in_t, cmax)
    g_end = [jnp.maximum(m_in[c], a_max[c]) for c in range(nc)]
    pieces = []
    terms = (-g * LOG2E, jnp.exp(m_in_t - g), jnp.exp(a - per_chunk(g_end)), -(b + g) * LOG2E)
    for term, n in zip(terms, TERM_PIECES):
        pieces += _split(term, n)
    decay = jnp.concatenate([jnp.exp(m_in[c] - g_end[c]) for c in range(nc)], axis=1)
    return pieces, a * LOG2E, decay, jnp.where(rev1, m_b, m_f)


def _gates_kernel(gc_ref, g_ref, b_ref, rc_ref, dc_ref, r_ref, a_ref, d_ref, *, chunk):
    nb = r_ref.shape[0]

    def split(ref):
        t = ref.shape[1] // nb
        li, pre_f = [], []
        for e in range(nb):
            pre = ref[:, e * t:(e + 1) * t] + b_ref[...]
            for d in range(N_DIR):
                li.append(pre[2 * d * M_HEADS:(2 * d + 1) * M_HEADS])
                pre_f.append(pre[(2 * d + 1) * M_HEADS:(2 * d + 2) * M_HEADS])
        return jnp.concatenate(li, axis=0), jnp.concatenate(pre_f, axis=0)

    def store_pieces(ref, pieces):
        pad = jnp.zeros((LANES - N_DH * len(pieces), pieces[0].shape[1]), F32)
        for e in range(nb):
            own = [p[e * N_DH:(e + 1) * N_DH] for p in pieces]
            ref[e] = jnp.concatenate(own + [pad], axis=0).astype(BF16)

    m0 = jnp.full((nb * N_DH, 1), M_INIT, F32)
    pieces_c, _, decay_c, m_ctx = _gate_terms(*split(gc_ref), m0, chunk)
    store_pieces(rc_ref, pieces_c)
    pieces, a, decay, _ = _gate_terms(*split(g_ref), m_ctx, chunk)
    store_pieces(r_ref, pieces)
    for e in range(nb):
        own = slice(e * N_DH, (e + 1) * N_DH)
        dc_ref[e] = decay_c[own]
        d_ref[e] = decay[own]
        for c in range(a_ref.shape[1]):
            a_ref[e, c] = a[own, c * chunk:(c + 1) * chunk]


def _gates(gates_t, gates_tc, gate_b, b, chunk):
    t = gates_t.shape[1] // b
    tc = gates_tc.shape[1] // b
    nc, ncc = t // chunk, tc // chunk
    nb = b
    spec = lambda r, n: pl.BlockSpec((nb, r, n), lambda i: (i, 0, 0))
    seq = lambda n: pl.BlockSpec((N_GATE_COLS, nb * n), lambda i: (0, i))
    rc, dc, r, a, dec = pl.pallas_call(
        functools.partial(_gates_kernel, chunk=chunk),
        grid=(b // nb,),
        in_specs=[seq(tc), seq(t), pl.BlockSpec((N_GATE_COLS, 1), lambda i: (0, 0))],
        out_specs=[spec(LANES, tc), spec(N_DH, ncc), spec(LANES, t),
                   pl.BlockSpec((nb, nc, N_DH, chunk), lambda i: (i, 0, 0, 0)), spec(N_DH, nc)],
        out_shape=[jax.ShapeDtypeStruct((b, LANES, tc), BF16), jax.ShapeDtypeStruct((b, N_DH, ncc), F32),
                   jax.ShapeDtypeStruct((b, LANES, t), BF16), jax.ShapeDtypeStruct((b, nc, N_DH, chunk), F32),
                   jax.ShapeDtypeStruct((b, N_DH, nc), F32)],
        compiler_params=_cparams(("arbitrary",), 32),
        name="gates",
    )(gates_tc, gates_t, gate_b.reshape(N_GATE_COLS, 1))
    decay = jnp.concatenate([dc, dec], axis=2).reshape(b * N_DH, ncc + nc)
    return rc, r, a, decay


def _twice(x):
    return jnp.concatenate([x, x], axis=1)


def _mlstm_kernel(dec_ref, q_ref, k_ref, v_ref, oz_ref, p_ref, a_ref, kc_ref, vc_ref, pc_ref,
                  sel_ref, out_ref, ct_ref, n_ref, hs_ref, *, chunk):
    t = q_ref.shape[1]
    nc = t // chunk
    ncc = kc_ref.shape[1] // chunk
    bi, hi = pl.program_id(0), pl.program_id(1)
    t_idx = lax.broadcasted_iota(jnp.int32, (chunk, chunk), 0)
    s_idx = lax.broadcasted_iota(jnp.int32, (chunk, chunk), 1)

    def bcast_terms(pieces, sel):
        return lax.dot_general(pieces, sel, (((0,), (0,)), ((), ())), preferred_element_type=F32)

    def update(k, v, e_b, decay, d):
        ke = k.astype(F32) * _twice(e_b)
        upd = lax.dot_general(ke.astype(BF16), v, (((0,), (0,)), ((), ())), preferred_element_type=F32)
        n_new = jnp.sum(ke, axis=0, keepdims=True)
        ct_ref[d] = upd if decay is None else decay * ct_ref[d] + upd
        n_ref[d] = n_new if decay is None else decay * n_ref[d] + n_new

    def decay_of(c, d):
        return dec_ref[bi * N_DH + d * M_HEADS + hi, c]

    for d in range(N_DIR):
        for i in range(ncc):
            c = i if d == 0 else ncc - 1 - i
            rows = slice(c * chunk, (c + 1) * chunk)
            e_b = bcast_terms(pc_ref[0, :, rows], sel_ref[0, d, :, 2 * LANES:3 * LANES])
            update(kc_ref[0, rows, :], vc_ref[0, rows, :], e_b, None if i == 0 else decay_of(c, d), d)

    def run_chunk(c, d):
        rows = slice(c * chunk, (c + 1) * chunk)
        q, k, v = q_ref[0, rows, :], k_ref[0, rows, :], v_ref[0, rows, :]
        w = bcast_terms(p_ref[0, :, rows], sel_ref[0, d])
        neg_g_b, w_inter_b, e_b, neg_mt_b = [w[:, i * LANES:(i + 1) * LANES] for i in range(len(TERM_PIECES))]
        a_row = a_ref[0, c, pl.ds(d * M_HEADS + hi, 1), :]
        mask = (s_idx <= t_idx) if d == 0 else (s_idx >= t_idx)
        w_intra = jnp.exp2(jnp.where(mask, a_row + _twice(neg_g_b), NEG_BIG))
        s = _dot_nt(q, k) * w_intra
        num = (_twice(w_inter_b) * jnp.dot(q, ct_ref[d].astype(BF16), preferred_element_type=F32)
               + jnp.dot(s.astype(BF16), v, preferred_element_type=F32))
        den_b = (w_inter_b * jnp.sum(q.astype(F32) * n_ref[d], axis=-1, keepdims=True)
                 + jnp.sum(s, axis=-1, keepdims=True))
        h = num * _twice(1.0 / jnp.maximum(jnp.abs(den_b), jnp.exp2(neg_mt_b)))
        update(k, v, e_b, decay_of(ncc + c, d), d)
        return rows, h

    def finalize(rows, h):
        hn = h * lax.rsqrt(jnp.mean(h * h, axis=-1, keepdims=True) + EPS)
        out_ref[0, rows, :] = hn.astype(BF16) * oz_ref[0, rows, :]

    for i in range(nc):
        for d in range(N_DIR):
            rows, h = run_chunk(i if d == 0 else nc - 1 - i, d)
            if i < nc // 2:
                hs_ref[rows, :] = h
            else:
                finalize(rows, hs_ref[rows, :] + h)


def _term_selectors():
    sel = np.zeros((M_HEADS, N_DIR, LANES, len(TERM_PIECES) * LANES), np.float32)
    first = np.concatenate([[0], np.cumsum(TERM_PIECES)])
    for h in range(M_HEADS):
        for d in range(N_DIR):
            for q in range(len(TERM_PIECES)):
                for j in range(first[q], first[q + 1]):
                    sel[h, d, j * N_DH + d * M_HEADS + h, q * LANES:(q + 1) * LANES] = 1.0
    return jnp.asarray(sel, BF16)


def _mlstm(y, yc, p, a, pc, decay, chunk, col):
    b, t, _ = y.shape
    tc = yc.shape[1]
    dh = MXU_DIM
    sel = _term_selectors()
    assert t % (2 * chunk) == 0 and tc % chunk == 0 and tc >= chunk
    blk = lambda name: pl.BlockSpec((1, t, dh), lambda i, h, c=col[name] // dh: (i, 0, c + h))
    blkc = lambda col0: pl.BlockSpec((1, tc, dh), lambda i, h, c=col0 // dh: (i, 0, c + h))
    return pl.pallas_call(
        functools.partial(_mlstm_kernel, chunk=chunk),
        grid=(b, M_HEADS),
        in_specs=[pl.BlockSpec(memory_space=pltpu.SMEM),
                  blk("q"), blk("k"), blk("v"), blk("oz"),
                  pl.BlockSpec((1, LANES, t), lambda i, h: (i, 0, 0)),
                  pl.BlockSpec((1, t // chunk, N_DH, chunk), lambda i, h: (i, 0, 0, 0)),
                  blkc(0), blkc(M_HEADS * dh),
                  pl.BlockSpec((1, LANES, tc), lambda i, h: (i, 0, 0)),
                  pl.BlockSpec((1, N_DIR) + sel.shape[2:], lambda i, h: (h, 0, 0, 0))],
        out_specs=pl.BlockSpec((1, t, dh), lambda i, h: (i, 0, h)),
        out_shape=jax.ShapeDtypeStruct((b, t, M_HEADS * dh), BF16),
        scratch_shapes=[pltpu.VMEM((N_DIR, dh, dh), F32),
                        pltpu.VMEM((N_DIR, 1, dh), F32),
                        pltpu.VMEM((t, dh), F32)],
        compiler_params=_cparams(("arbitrary", "arbitrary"), 48),
        name="mlstm",
    )(decay, y, y, y, y, p, a, yc, yc, pc, sel)


def _window_bounds(pos, w, length):
    return np.clip(pos - w // 2, 0, length), np.clip(pos + w - w // 2, 0, length)


def _col_pool_matrices(tile):
    pos = np.arange(GRID_W)
    mats = []
    for w in POOL_WINDOWS:
        lo, hi = _window_bounds(pos, w, GRID_W)
        band = ((pos[None, :] >= lo[:, None]) & (pos[None, :] < hi[:, None])).astype(np.float32)
        mats.append(np.kron(np.eye(tile // GRID_W, dtype=np.float32), band))
    return jnp.asarray(np.stack(mats), dtype=BF16)


def _pool_kernel(p_ref, z_ref, mc_ref, pw_ref, ps_ref, out_ref, pad_ref, scale_ref, *, tile, rows):
    t = p_ref.shape[1]
    gd = LANES
    pad = (max(POOL_WINDOWS) // 2) * GRID_W
    n_tiles = t // tile

    @pl.when(pl.program_id(0) == 0)
    def _():
        pad_ref[...] = jnp.zeros(pad_ref.shape, F32)
        tok = lax.broadcasted_iota(jnp.int32, (t, gd), 0)
        r = tok // GRID_W
        c = tok % GRID_W
        for g, w in enumerate(POOL_WINDOWS):
            cnt_r = jnp.minimum(r + (w - w // 2), rows) - jnp.maximum(r - w // 2, 0)
            cnt_c = jnp.minimum(c + (w - w // 2), GRID_W) - jnp.maximum(c - w // 2, 0)
            scale_ref[g] = 1.0 / (cnt_r.astype(F32) * cnt_c.astype(F32))

    def col_sum(g, start, slot):
        cols = pl.ds(g * gd, gd)
        pad_ref[slot, pl.ds(pad + start, tile), :] = jnp.dot(
            mc_ref[g], p_ref[0, pl.ds(start, tile), cols], preferred_element_type=F32)

    def row_sum(g, start, slot):
        w = POOL_WINDOWS[g]
        cols = pl.ds(g * gd, gd)
        buf = pad_ref.at[slot]
        acc = buf[pl.ds(pad + start - (w // 2) * GRID_W, tile), :]
        for j in range(1 - w // 2, w - w // 2):
            acc = acc + buf[pl.ds(pad + start + j * GRID_W, tile), :]
        mixed = acc * scale_ref[g, pl.ds(start, tile), :] - p_ref[0, pl.ds(start, tile), cols].astype(F32)
        mm = jnp.dot(mixed.astype(BF16), pw_ref[g], preferred_element_type=F32)
        zz = z_ref[0, pl.ds(start, tile), cols].astype(F32)
        out_ref[0, pl.ds(start, tile), cols] = (mm * ps_ref[:, cols] * zz).astype(out_ref.dtype)

    for n in range(P_GROUPS + 1):
        for i in range(n_tiles):
            if n < P_GROUPS:
                col_sum(n, i * tile, n % 2)
            if n > 0:
                row_sum(n - 1, i * tile, (n - 1) % 2)


def _pool(y, pool_w, pool_scale, p_col0):
    b, t, _ = y.shape
    pw = P_GROUPS * LANES
    tile = MXU_DIM
    pad = (max(POOL_WINDOWS) // 2) * GRID_W
    return pl.pallas_call(
        functools.partial(_pool_kernel, tile=tile, rows=t // GRID_W),
        grid=(b,),
        in_specs=[pl.BlockSpec((1, t, pw), lambda i: (i, 0, p_col0 // pw)),
                  pl.BlockSpec((1, t, pw), lambda i: (i, 0, p_col0 // pw + 1)),
                  pl.BlockSpec((P_GROUPS, tile, tile), lambda i: (0, 0, 0)),
                  pl.BlockSpec((P_GROUPS, LANES, LANES), lambda i: (0, 0, 0)),
                  pl.BlockSpec((1, pw), lambda i: (0, 0))],
        out_specs=pl.BlockSpec((1, t, pw), lambda i: (i, 0, 0)),
        out_shape=jax.ShapeDtypeStruct((b, t, pw), BF16),
        scratch_shapes=[pltpu.VMEM((2, t + 2 * pad, LANES), F32),
                        pltpu.VMEM((P_GROUPS, t, LANES), F32)],
        compiler_params=_cparams(("arbitrary",), 48),
        name="pool",
    )(y, y, _col_pool_matrices(tile), pool_w, pool_scale)


def _merge_kernel(x_ref, ym_ref, yp_ref, gm_ref, gp_ref, gate_ref, wm_ref, wp_ref, wo_ref, fw_ref, o_ref):
    slab = x_ref.shape[0] // MERGE_SLABS
    for r in range(MERGE_SLABS):
        rows = slice(r * slab, (r + 1) * slab)
        y_m = jnp.dot(ym_ref[rows, :], wm_ref[...], preferred_element_type=F32)
        y_p = jnp.dot(yp_ref[rows, :], wp_ref[...], preferred_element_type=F32)
        merged = (_sigmoid_t(gm_ref[rows, :].astype(F32)) * y_m
                  + _sigmoid_t(gp_ref[rows, :].astype(F32)) * y_p)
        upd = jnp.dot(merged.astype(BF16), wo_ref[...], preferred_element_type=F32)
        xo = x_ref[rows, :] + gate_ref[0] * upd
        o_ref[rows, :] = xo * lax.rsqrt(jnp.mean(xo * xo, axis=-1, keepdims=True) + EPS) * fw_ref[...]


def _merge(x2, ym, yp, y, mod, wm, wp, wo, final_w, rows_per_mod, g_col0):
    n, d = x2.shape
    tm = 1024
    per = rows_per_mod // tm
    pw = yp.shape[1]
    full = lambda shape: pl.BlockSpec(shape, lambda i: (0,) * len(shape))
    return pl.pallas_call(
        _merge_kernel,
        grid=(n // tm,),
        in_specs=[pl.BlockSpec((tm, d), lambda i: (i, 0)),
                  pl.BlockSpec((tm, d), lambda i: (i, 0)),
                  pl.BlockSpec((tm, pw), lambda i: (i, 0)),
                  pl.BlockSpec((tm, d), lambda i: (i, g_col0 // d)),
                  pl.BlockSpec((tm, d), lambda i: (i, g_col0 // d + 1)),
                  pl.BlockSpec((1, 1, d), lambda i: (i // per, 0, 2)),
                  full((d, d)), full((pw, d)), full((d, d)), full((1, d))],
        out_specs=pl.BlockSpec((tm, d), lambda i: (i, 0)),
        out_shape=jax.ShapeDtypeStruct((n, d), F32),
        compiler_params=_cparams(("arbitrary",), 48, [False] * 6 + [True] * 3 + [False]),
        name="merge",
    )(x2, ym, yp, y, y, mod, wm, wp, wo, final_w)


def kernel(x, c, ctx, c_ctx, norm_w, ada_w, ada_b, in_w, gate_b, head_norm_w, pool_w, pool_scale,
           branch_m_w, branch_p_w, out_w, final_norm_w):
    b, t, d = x.shape
    tc = ctx.shape[1]
    depth = norm_w.shape[0]
    assert depth == 1, "context stream update between layers is not implemented"
    mw = M_HEADS * MXU_DIM
    assert d == mw
    g0 = 5 * mw

    mod = _adaln(c, c_ctx, ada_w[0], ada_b[0])

    wt_f32 = jnp.transpose(in_w[0])
    nw = norm_w[0].reshape(1, d)

    g_q, g_k, g_v, g_o, g_z, g_p, g_gm, g_gp = range(8)
    pairs = ((g_q, g_k, "id"), (g_v, g_p, "half_silu"), (g_o, g_z, "gate"), (g_gm, g_gp, "id"))
    col = {name: i * mw for i, name in enumerate(("q", "k", "v", "p", "o", "oz", "g_m", "g_p"))}

    wt, wg, yc, gates_tc = _ctx_proj(ctx.reshape(b * tc, d), mod, b, nw, wt_f32, g0, g_k, g_v)
    y, gates_t = _in_proj(x.reshape(b * t, d), mod, nw, wt, wg, head_norm_w[0].reshape(1, mw).astype(BF16),
                          t, "in_proj", pairs)
    y = y.reshape(b, t, -1)
    yc = yc.reshape(b, tc, -1)

    pc, p, a, decay = _gates(gates_t, gates_tc, gate_b[0], b, CHUNK)
    ym = _mlstm(y, yc, p, a, pc, decay, CHUNK, col)
    yp = _pool(y, pool_w[0].astype(BF16), pool_scale[0].reshape(1, -1), col["p"])

    out = _merge(x.reshape(b * t, d), ym.reshape(b * t, mw), yp.reshape(b * t, -1), y.reshape(b * t, -1),
                 mod, branch_m_w[0].astype(BF16), branch_p_w[0].astype(BF16),
                 out_w[0].astype(BF16), final_norm_w.reshape(1, d), t, col["g_m"])
    return out.reshape(b, t, d)
```

```python
import functools

import numpy as np
import jax
import jax.numpy as jnp
from jax import lax
from jax.experimental import pallas as pl
from jax.experimental.pallas import tpu as pltpu

F32 = jnp.float32
BF16 = jnp.bfloat16

EPS = 1e-6
LOG2E = 1.4426950408889634
M_INIT = -1e30
NEG_BIG = -1e30
M_HEADS = 4
N_DIR = 2
N_GATE_COLS = N_DIR * 2 * M_HEADS
P_GROUPS = 4
POOL_WINDOWS = (2, 4, 8, 16)
GRID_W = 64
LANES = 128
SUBLANES = 8
MXU_DIM = 256
CHUNK = 256
MERGE_SLABS = 2
MIB = 1024 * 1024


def _cparams(sem, vmem_mib):
    return pltpu.CompilerParams(dimension_semantics=sem, vmem_limit_bytes=vmem_mib * MIB)


def _sigmoid(x):
    return 1.0 / (1.0 + jnp.exp(-x))


def _silu(x):
    return x * _sigmoid(x)


def _log_sigmoid(x):
    return jnp.minimum(x, 0.0) - jnp.log(1.0 + jnp.exp(-jnp.abs(x)))


def _dot_nt(a, b):
    return lax.dot_general(a, b, (((1,), (1,)), ((), ())), preferred_element_type=F32)


def _adaln_kernel(c_ref, cc_ref, w_ref, b_ref, o_ref):
    b = c_ref.shape[0]
    rows = jnp.concatenate([c_ref[...], jnp.broadcast_to(cc_ref[...], (o_ref.shape[0] - b, c_ref.shape[1]))], axis=0)
    o_ref[:, 0, :] = jnp.dot(_silu(rows), w_ref[...], preferred_element_type=F32) + b_ref[...]


def _adaln(c, c_ctx, ada_w, ada_b):
    b, d = c.shape
    rows = SUBLANES * (b // SUBLANES + 1)
    n = ada_w.shape[1]
    tn = d
    return pl.pallas_call(
        _adaln_kernel,
        grid=(n // tn,),
        in_specs=[pl.BlockSpec((b, d), lambda j: (0, 0)),
                  pl.BlockSpec((1, d), lambda j: (0, 0)),
                  pl.BlockSpec((d, tn), lambda j: (0, j)),
                  pl.BlockSpec((1, tn), lambda j: (0, j))],
        out_specs=pl.BlockSpec((rows, 1, tn), lambda j: (0, 0, j)),
        out_shape=jax.ShapeDtypeStruct((rows, 1, n), F32),
        compiler_params=_cparams(("arbitrary",), 32),
        name="adaln",
    )(c, c_ctx.reshape(1, d), ada_w, ada_b.reshape(1, n))


def _sigmoid_t(x):
    return 0.5 * jnp.tanh(0.5 * x) + 0.5


IN_TILE = 1024
GATE_SUB = 256


def _inproj_kernel(x_ref, sc_ref, sh_ref, nw_ref, wa_ref, wb_ref, wg_ref, gw_ref, y_ref, g_ref, xn_ref, *, acts):
    j = pl.program_id(1)

    @pl.when(j == 0)
    def _():
        x = x_ref[...]
        ms = jnp.mean(x * x, axis=-1, keepdims=True)
        xn = x * lax.rsqrt(ms + EPS) * (nw_ref[...] * (1.0 + sc_ref[0])) + sh_ref[0]
        xn_ref[...] = xn.astype(BF16)
        wg = wg_ref[...]
        wg = jnp.concatenate([wg, jnp.zeros((LANES - wg.shape[0], wg.shape[1]), wg.dtype)], axis=0)
        g_ref[...] = _dot_nt(xn_ref[...], wg).T[:N_GATE_COLS]

    def pair(act):
        tn = IN_TILE
        xn = xn_ref[...]
        if act == "gate":
            for c in range(0, tn, GATE_SUB):
                a = _dot_nt(xn, wa_ref[c:c + GATE_SUB, :])
                b = _dot_nt(xn, wb_ref[c:c + GATE_SUB, :])
                y_ref[:, c:c + GATE_SUB] = a.astype(BF16)
                ab, bb = a.astype(BF16), b.astype(BF16)
                gate = _sigmoid_t(ab) * (bb * _sigmoid_t(bb))
                y_ref[:, tn + c:tn + c + GATE_SUB] = gate * gw_ref[:, c:c + GATE_SUB]
            return
        y_ref[:, :tn] = _dot_nt(xn, wa_ref[...]).astype(BF16)
        b = _dot_nt(xn, wb_ref[...])
        if act == "half_silu":
            half = tn // 2
            zb = b[:, half:].astype(BF16)
            y_ref[:, tn:tn + half] = b[:, :half].astype(BF16)
            y_ref[:, tn + half:] = zb * _sigmoid_t(zb)
            return
        y_ref[:, tn:] = b.astype(BF16)

    for act in sorted(set(acts)):
        hit = functools.reduce(jnp.logical_or, [j == s for s, a in enumerate(acts) if a == act])
        pl.when(hit)(functools.partial(pair, act))


def _in_proj(x2, mod, norm_w, w, wg, gate_w, rows_per_mod, name, pairs):
    n, d = x2.shape
    tm = min(2048, rows_per_mod)
    tn = IN_TILE
    per = rows_per_mod // tm
    acts = tuple(p[2] for p in pairs)

    def w_group(which):
        return lambda i, j: (sum(jnp.where(j == s, p[which], 0) for s, p in enumerate(pairs)), 0)

    return pl.pallas_call(
        functools.partial(_inproj_kernel, acts=acts),
        grid=(n // tm, len(pairs)),
        in_specs=[pl.BlockSpec((tm, d), lambda i, j: (i, 0)),
                  pl.BlockSpec((1, 1, d), lambda i, j: (i // per, 0, 1)),
                  pl.BlockSpec((1, 1, d), lambda i, j: (i // per, 0, 0)),
                  pl.BlockSpec((1, d), lambda i, j: (0, 0)),
                  pl.BlockSpec((tn, d), w_group(0)),
                  pl.BlockSpec((tn, d), w_group(1)),
                  pl.BlockSpec((N_GATE_COLS, d), lambda i, j: (0, 0)),
                  pl.BlockSpec((1, tn), lambda i, j: (0, 0))],
        out_specs=[pl.BlockSpec((tm, 2 * tn), lambda i, j: (i, j)),
                   pl.BlockSpec((N_GATE_COLS, tm), lambda i, j: (0, i))],
        out_shape=[jax.ShapeDtypeStruct((n, 2 * tn * len(pairs)), BF16),
                   jax.ShapeDtypeStruct((N_GATE_COLS, n), F32)],
        scratch_shapes=[pltpu.VMEM((tm, d), BF16)],
        compiler_params=_cparams(("arbitrary", "arbitrary"), 56),
        name=name,
    )(x2, mod, mod, norm_w, w, w, wg, gate_w)


def _ctx_proj_kernel(x_ref, sc_ref, sh_ref, nw_ref, wsrc_ref, wgsrc_ref, wdst_ref, wgdst_ref, y_ref, g_ref,
                     xn_ref, kv_ref, *, steps, k_block, v_block):
    i, j = pl.program_id(0), pl.program_id(1)
    group = i * steps + j
    w = wsrc_ref[...]
    w = jnp.where(group == k_block, w * MXU_DIM ** -0.5, w).astype(BF16)
    wdst_ref[...] = w

    @pl.when(group == k_block)
    def _():
        kv_ref[0] = w

    @pl.when(group == v_block)
    def _():
        kv_ref[1] = w

    @pl.when(j == 0)
    def _():
        x = x_ref[...]
        ms = jnp.mean(x * x, axis=-1, keepdims=True)
        xn = x * lax.rsqrt(ms + EPS) * (nw_ref[...] * (1.0 + sc_ref[0])) + sh_ref[0]
        xn_ref[...] = xn.astype(BF16)
        wg = wgsrc_ref[...].astype(BF16)
        wgdst_ref[...] = wg
        wg = jnp.concatenate([wg, jnp.zeros((LANES - wg.shape[0], wg.shape[1]), wg.dtype)], axis=0)
        g_ref[...] = _dot_nt(xn_ref[...], wg).T[:N_GATE_COLS]

    @pl.when(j == k_block)
    def _():
        y_ref[:, :IN_TILE] = _dot_nt(xn_ref[...], kv_ref[0]).astype(BF16)

    @pl.when(j == v_block)
    def _():
        y_ref[:, IN_TILE:] = _dot_nt(xn_ref[...], kv_ref[1]).astype(BF16)


def _ctx_proj(xc, mod, mod_row, norm_w, wt, gate_row0, k_block, v_block):
    n, d = xc.shape
    tn = IN_TILE
    n_groups = (wt.shape[0] - N_GATE_COLS) // tn
    row_tiles = 2
    tm, steps = n // row_tiles, n_groups // row_tiles
    assert gate_row0 % tn == 0 and n_groups % row_tiles == 0 and 0 < k_block < v_block < steps

    def src_rows(i, j):
        r = (i * steps + j) * tn
        return (pl.multiple_of(r + jnp.where(r >= gate_row0, N_GATE_COLS, 0), N_GATE_COLS), 0)

    return pl.pallas_call(
        functools.partial(_ctx_proj_kernel, steps=steps, k_block=k_block, v_block=v_block),
        grid=(row_tiles, steps),
        in_specs=[pl.BlockSpec((tm, d), lambda i, j: (i, 0)),
                  pl.BlockSpec((1, 1, d), lambda i, j: (mod_row, 0, 1)),
                  pl.BlockSpec((1, 1, d), lambda i, j: (mod_row, 0, 0)),
                  pl.BlockSpec((1, d), lambda i, j: (0, 0)),
                  pl.BlockSpec((pl.Element(tn), pl.Element(d)), src_rows),
                  pl.BlockSpec((pl.Element(N_GATE_COLS), pl.Element(d)), lambda i, j: (gate_row0, 0))],
        out_specs=[pl.BlockSpec((tn, d), lambda i, j: (i * steps + j, 0)),
                   pl.BlockSpec((N_GATE_COLS, d), lambda i, j: (0, 0)),
                   pl.BlockSpec((tm, 2 * tn), lambda i, j: (i, 0)),
                   pl.BlockSpec((N_GATE_COLS, tm), lambda i, j: (0, i))],
        out_shape=[jax.ShapeDtypeStruct((n_groups * tn, d), BF16),
                   jax.ShapeDtypeStruct((N_GATE_COLS, d), BF16),
                   jax.ShapeDtypeStruct((n, 2 * tn), BF16),
                   jax.ShapeDtypeStruct((N_GATE_COLS, n), F32)],
        scratch_shapes=[pltpu.VMEM((tm, d), BF16), pltpu.VMEM((2, tn, d), BF16)],
        compiler_params=_cparams(("arbitrary", "arbitrary"), 56),
        name="ctx_proj",
    )(xc, mod, mod, norm_w, wt, wt)


N_DH = N_DIR * M_HEADS
TERM_PIECES = (3, 2, 2, 3)


def _split(x, pieces):
    out = []
    for _ in range(pieces):
        p = x.astype(BF16).astype(F32)
        out.append(p)
        x = x - p
    return out


def _chunk_scan(x, pos, chunk, op, reverse):
    t = x.shape[1]
    k = 1
    while k < chunk:
        if reverse:
            x = jnp.where(pos < chunk - k, op(x, pltpu.roll(x, t - k, 1)), x)
        else:
            x = jnp.where(pos >= k, op(x, pltpu.roll(x, k, 1)), x)
        k *= 2
    return x


def _gate_terms(li, pre_f, m0, chunk):
    rows, t = li.shape
    nc = t // chunk
    pos = lax.broadcasted_iota(jnp.int32, (rows, t), 1) % chunk
    rev = lax.broadcasted_iota(jnp.int32, (rows, t), 0) % N_DH >= M_HEADS
    rev1 = lax.broadcasted_iota(jnp.int32, (rows, 1), 0) % N_DH >= M_HEADS
    lf = _log_sigmoid(pre_f)
    b = jnp.where(rev, _chunk_scan(lf, pos, chunk, jnp.add, True), _chunk_scan(lf, pos, chunk, jnp.add, False))
    a = li - b
    cmax = jnp.where(rev, _chunk_scan(a, pos, chunk, jnp.maximum, True),
                     _chunk_scan(a, pos, chunk, jnp.maximum, False))

    def at_scan_end(x, c):
        lo = c * chunk
        return jnp.where(rev1, x[:, lo:lo + 1], x[:, lo + chunk - 1:lo + chunk])

    b_end = [at_scan_end(b, c) for c in range(nc)]
    a_max = [at_scan_end(cmax, c) for c in range(nc)]
    m_f, m_b = m0, m0
    m_in_f, m_in_b = [None] * nc, [None] * nc
    for i in range(nc):
        j = nc - 1 - i
        m_in_f[i] = m_f
        m_f = b_end[i] + jnp.maximum(m_f, a_max[i])
        m_in_b[j] = m_b
        m_b = b_end[j] + jnp.maximum(m_b, a_max[j])
    m_in = [jnp.where(rev1, m_in_b[c], m_in_f[c]) for c in range(nc)]
    per_chunk = lambda vals: jnp.concatenate([jnp.broadcast_to(v, (rows, chunk)) for v in vals], axis=1)
    m_in_t = per_chunk(m_in)
    g = jnp.maximum(m_in_t, cmax)
    g_end = [jnp.maximum(m_in[c], a_max[c]) for c in range(nc)]
    pieces = []
    terms = (-g * LOG2E, jnp.exp(m_in_t - g), jnp.exp(a - per_chunk(g_end)), -(b + g) * LOG2E)
    for term, n in zip(terms, TERM_PIECES):
        pieces += _split(term, n)
    decay = jnp.concatenate([jnp.exp(m_in[c] - g_end[c]) for c in range(nc)], axis=1)
    return pieces, a * LOG2E, decay, jnp.where(rev1, m_b, m_f)


def _gates_kernel(gc_ref, g_ref, b_ref, rc_ref, dc_ref, r_ref, a_ref, d_ref, *, chunk):
    nb = r_ref.shape[0]

    def split(ref):
        t = ref.shape[1] // nb
        li, pre_f = [], []
        for e in range(nb):
            pre = ref[:, e * t:(e + 1) * t] + b_ref[...]
            for d in range(N_DIR):
                li.append(pre[2 * d * M_HEADS:(2 * d + 1) * M_HEADS])
                pre_f.append(pre[(2 * d + 1) * M_HEADS:(2 * d + 2) * M_HEADS])
        return jnp.concatenate(li, axis=0), jnp.concatenate(pre_f, axis=0)

    def store_pieces(ref, pieces):
        pad = jnp.zeros((LANES - N_DH * len(pieces), pieces[0].shape[1]), F32)
        for e in range(nb):
            own = [p[e * N_DH:(e + 1) * N_DH] for p in pieces]
            ref[e] = jnp.concatenate(own + [pad], axis=0).astype(BF16)

    m0 = jnp.full((nb * N_DH, 1), M_INIT, F32)
    pieces_c, _, decay_c, m_ctx = _gate_terms(*split(gc_ref), m0, chunk)
    store_pieces(rc_ref, pieces_c)
    pieces, a, decay, _ = _gate_terms(*split(g_ref), m_ctx, chunk)
    store_pieces(r_ref, pieces)
    for e in range(nb):
        own = slice(e * N_DH, (e + 1) * N_DH)
        dc_ref[e] = decay_c[own]
        d_ref[e] = decay[own]
        for c in range(a_ref.shape[1]):
            a_ref[e, c] = a[own, c * chunk:(c + 1) * chunk]


def _gates(gates_t, gates_tc, gate_b, b, chunk):
    t = gates_t.shape[1] // b
    tc = gates_tc.shape[1] // b
    nc, ncc = t // chunk, tc // chunk
    nb = b
    spec = lambda r, n: pl.BlockSpec((nb, r, n), lambda i: (i, 0, 0))
    seq = lambda n: pl.BlockSpec((N_GATE_COLS, nb * n), lambda i: (0, i))
    rc, dc, r, a, dec = pl.pallas_call(
        functools.partial(_gates_kernel, chunk=chunk),
        grid=(b // nb,),
        in_specs=[seq(tc), seq(t), pl.BlockSpec((N_GATE_COLS, 1), lambda i: (0, 0))],
        out_specs=[spec(LANES, tc), spec(N_DH, ncc), spec(LANES, t),
                   pl.BlockSpec((nb, nc, N_DH, chunk), lambda i: (i, 0, 0, 0)), spec(N_DH, nc)],
        out_shape=[jax.ShapeDtypeStruct((b, LANES, tc), BF16), jax.ShapeDtypeStruct((b, N_DH, ncc), F32),
                   jax.ShapeDtypeStruct((b, LANES, t), BF16), jax.ShapeDtypeStruct((b, nc, N_DH, chunk), F32),
                   jax.ShapeDtypeStruct((b, N_DH, nc), F32)],
        compiler_params=_cparams(("arbitrary",), 32),
        name="gates",
    )(gates_tc, gates_t, gate_b.reshape(N_GATE_COLS, 1))
    decay = jnp.concatenate([dc, dec], axis=2).reshape(b * N_DH, ncc + nc)
    return rc, r, a, decay


def _twice(x):
    return jnp.concatenate([x, x], axis=1)


def _mlstm_kernel(dec_ref, q_ref, k_ref, v_ref, oz_ref, p_ref, a_ref, kc_ref, vc_ref, pc_ref,
                  sel_ref, out_ref, ct_ref, n_ref, hs_ref, *, chunk):
    t = q_ref.shape[1]
    nc = t // chunk
    ncc = kc_ref.shape[1] // chunk
    bi, hi = pl.program_id(0), pl.program_id(1)
    t_idx = lax.broadcasted_iota(jnp.int32, (chunk, chunk), 0)
    s_idx = lax.broadcasted_iota(jnp.int32, (chunk, chunk), 1)

    def bcast_terms(pieces, sel):
        return lax.dot_general(pieces, sel, (((0,), (0,)), ((), ())), preferred_element_type=F32)

    def update(k, v, e_b, decay, d):
        ke = k.astype(F32) * _twice(e_b)
        upd = lax.dot_general(ke.astype(BF16), v, (((0,), (0,)), ((), ())), preferred_element_type=F32)
        n_new = jnp.sum(ke, axis=0, keepdims=True)
        ct_ref[d] = upd if decay is None else decay * ct_ref[d] + upd
        n_ref[d] = n_new if decay is None else decay * n_ref[d] + n_new

    def decay_of(c, d):
        return dec_ref[bi * N_DH + d * M_HEADS + hi, c]

    for d in range(N_DIR):
        for i in range(ncc):
            c = i if d == 0 else ncc - 1 - i
            rows = slice(c * chunk, (c + 1) * chunk)
            e_b = bcast_terms(pc_ref[0, :, rows], sel_ref[0, d, :, 2 * LANES:3 * LANES])
            update(kc_ref[0, rows, :], vc_ref[0, rows, :], e_b, None if i == 0 else decay_of(c, d), d)

    def run_chunk(c, d):
        rows = slice(c * chunk, (c + 1) * chunk)
        q, k, v = q_ref[0, rows, :], k_ref[0, rows, :], v_ref[0, rows, :]
        w = bcast_terms(p_ref[0, :, rows], sel_ref[0, d])
        neg_g_b, w_inter_b, e_b, neg_mt_b = [w[:, i * LANES:(i + 1) * LANES] for i in range(len(TERM_PIECES))]
        a_row = a_ref[0, c, pl.ds(d * M_HEADS + hi, 1), :]
        mask = (s_idx <= t_idx) if d == 0 else (s_idx >= t_idx)
        w_intra = jnp.exp2(jnp.where(mask, a_row + _twice(neg_g_b), NEG_BIG))
        s = _dot_nt(q, k) * w_intra
        num = (_twice(w_inter_b) * jnp.dot(q, ct_ref[d].astype(BF16), preferred_element_type=F32)
               + jnp.dot(s.astype(BF16), v, preferred_element_type=F32))
        den_b = (w_inter_b * jnp.sum(q.astype(F32) * n_ref[d], axis=-1, keepdims=True)
                 + jnp.sum(s, axis=-1, keepdims=True))
        h = num * _twice(1.0 / jnp.maximum(jnp.abs(den_b), jnp.exp2(neg_mt_b)))
        update(k, v, e_b, decay_of(ncc + c, d), d)
        return rows, h

    def finalize(rows, h):
        hn = h * lax.rsqrt(jnp.mean(h * h, axis=-1, keepdims=True) + EPS)
        out_ref[0, rows, :] = hn.astype(BF16) * oz_ref[0, rows, :]

    for i in range(nc):
        for d in range(N_DIR):
            rows, h = run_chunk(i if d == 0 else nc - 1 - i, d)
            if i < nc // 2:
                hs_ref[rows, :] = h
            else:
                finalize(rows, hs_ref[rows, :] + h)


def _term_selectors():
    sel = np.zeros((M_HEADS, N_DIR, LANES, len(TERM_PIECES) * LANES), np.float32)
    first = np.concatenate([[0], np.cumsum(TERM_PIECES)])
    for h in range(M_HEADS):
        for d in range(N_DIR):
            for q in range(len(TERM_PIECES)):
                for j in range(first[q], first[q + 1]):
                    sel[h, d, j * N_DH + d * M_HEADS + h, q * LANES:(q + 1) * LANES] = 1.0
    return jnp.asarray(sel, BF16)


def _mlstm(y, yc, p, a, pc, decay, chunk, col):
    b, t, _ = y.shape
    tc = yc.shape[1]
    dh = MXU_DIM
    sel = _term_selectors()
    assert t % (2 * chunk) == 0 and tc % chunk == 0 and tc >= chunk
    blk = lambda name: pl.BlockSpec((1, t, dh), lambda i, h, c=col[name] // dh: (i, 0, c + h))
    blkc = lambda col0: pl.BlockSpec((1, tc, dh), lambda i, h, c=col0 // dh: (i, 0, c + h))
    return pl.pallas_call(
        functools.partial(_mlstm_kernel, chunk=chunk),
        grid=(b, M_HEADS),
        in_specs=[pl.BlockSpec(memory_space=pltpu.SMEM),
                  blk("q"), blk("k"), blk("v"), blk("oz"),
                  pl.BlockSpec((1, LANES, t), lambda i, h: (i, 0, 0)),
                  pl.BlockSpec((1, t // chunk, N_DH, chunk), lambda i, h: (i, 0, 0, 0)),
                  blkc(0), blkc(M_HEADS * dh),
                  pl.BlockSpec((1, LANES, tc), lambda i, h: (i, 0, 0)),
                  pl.BlockSpec((1, N_DIR) + sel.shape[2:], lambda i, h: (h, 0, 0, 0))],
        out_specs=pl.BlockSpec((1, t, dh), lambda i, h: (i, 0, h)),
        out_shape=jax.ShapeDtypeStruct((b, t, M_HEADS * dh), BF16),
        scratch_shapes=[pltpu.VMEM((N_DIR, dh, dh), F32),
                        pltpu.VMEM((N_DIR, 1, dh), F32),
                        pltpu.VMEM((t, dh), F32)],
        compiler_params=_cparams(("arbitrary", "arbitrary"), 48),
        name="mlstm",
    )(decay, y, y, y, y, p, a, yc, yc, pc, sel)


def _window_bounds(pos, w, length):
    return np.clip(pos - w // 2, 0, length), np.clip(pos + w - w // 2, 0, length)


def _col_pool_matrices(tile):
    pos = np.arange(GRID_W)
    mats = []
    for w in POOL_WINDOWS:
        lo, hi = _window_bounds(pos, w, GRID_W)
        band = ((pos[None, :] >= lo[:, None]) & (pos[None, :] < hi[:, None])).astype(np.float32)
        mats.append(np.kron(np.eye(tile // GRID_W, dtype=np.float32), band))
    return jnp.asarray(np.stack(mats), dtype=BF16)


def _pool_kernel(p_ref, z_ref, mc_ref, pw_ref, ps_ref, out_ref, pad_ref, scale_ref, *, tile, rows):
    t = p_ref.shape[1]
    gd = LANES
    pad = (max(POOL_WINDOWS) // 2) * GRID_W
    n_tiles = t // tile

    @pl.when(pl.program_id(0) == 0)
    def _():
        pad_ref[...] = jnp.zeros(pad_ref.shape, F32)
        tok = lax.broadcasted_iota(jnp.int32, (t, gd), 0)
        r = tok // GRID_W
        c = tok % GRID_W
        for g, w in enumerate(POOL_WINDOWS):
            cnt_r = jnp.minimum(r + (w - w // 2), rows) - jnp.maximum(r - w // 2, 0)
            cnt_c = jnp.minimum(c + (w - w // 2), GRID_W) - jnp.maximum(c - w // 2, 0)
            scale_ref[g] = 1.0 / (cnt_r.astype(F32) * cnt_c.astype(F32))

    def col_sum(g, start, slot):
        cols = pl.ds(g * gd, gd)
        pad_ref[slot, pl.ds(pad + start, tile), :] = jnp.dot(
            mc_ref[g], p_ref[0, pl.ds(start, tile), cols], preferred_element_type=F32)

    def row_sum(g, start, slot):
        w = POOL_WINDOWS[g]
        cols = pl.ds(g * gd, gd)
        buf = pad_ref.at[slot]
        acc = buf[pl.ds(pad + start - (w // 2) * GRID_W, tile), :]
        for j in range(1 - w // 2, w - w // 2):
            acc = acc + buf[pl.ds(pad + start + j * GRID_W, tile), :]
        mixed = acc * scale_ref[g, pl.ds(start, tile), :] - p_ref[0, pl.ds(start, tile), cols].astype(F32)
        mm = jnp.dot(mixed.astype(BF16), pw_ref[g], preferred_element_type=F32)
        zz = z_ref[0, pl.ds(start, tile), cols].astype(F32)
        out_ref[0, pl.ds(start, tile), cols] = (mm * ps_ref[:, cols] * zz).astype(out_ref.dtype)

    for n in range(P_GROUPS + 1):
        for i in range(n_tiles):
            if n < P_GROUPS:
                col_sum(n, i * tile, n % 2)
            if n > 0:
                row_sum(n - 1, i * tile, (n - 1) % 2)


def _pool(y, pool_w, pool_scale, p_col0):
    b, t, _ = y.shape
    pw = P_GROUPS * LANES
    tile = MXU_DIM
    pad = (max(POOL_WINDOWS) // 2) * GRID_W
    return pl.pallas_call(
        functools.partial(_pool_kernel, tile=tile, rows=t // GRID_W),
        grid=(b,),
        in_specs=[pl.BlockSpec((1, t, pw), lambda i: (i, 0, p_col0 // pw)),
                  pl.BlockSpec((1, t, pw), lambda i: (i, 0, p_col0 // pw + 1)),
                  pl.BlockSpec((P_GROUPS, tile, tile), lambda i: (0, 0, 0)),
                  pl.BlockSpec((P_GROUPS, LANES, LANES), lambda i: (0, 0, 0)),
                  pl.BlockSpec((1, pw), lambda i: (0, 0))],
        out_specs=pl.BlockSpec((1, t, pw), lambda i: (i, 0, 0)),
        out_shape=jax.ShapeDtypeStruct((b, t, pw), BF16),
        scratch_shapes=[pltpu.VMEM((2, t + 2 * pad, LANES), F32),
                        pltpu.VMEM((P_GROUPS, t, LANES), F32)],
        compiler_params=_cparams(("arbitrary",), 48),
        name="pool",
    )(y, y, _col_pool_matrices(tile), pool_w, pool_scale)


def _merge_kernel(x_ref, ym_ref, yp_ref, gm_ref, gp_ref, gate_ref, wm32_ref, wp32_ref, wo32_ref, fw_ref, o_ref,
                  wm_ref, wp_ref, wo_ref):
    @pl.when(pl.program_id(0) == 0)
    def _():
        wm_ref[...] = wm32_ref[...].astype(BF16)
        wp_ref[...] = wp32_ref[...].astype(BF16)
        wo_ref[...] = wo32_ref[...].astype(BF16)

    slab = x_ref.shape[0] // MERGE_SLABS
    for r in range(MERGE_SLABS):
        rows = slice(r * slab, (r + 1) * slab)
        y_m = jnp.dot(ym_ref[rows, :], wm_ref[...], preferred_element_type=F32)
        y_p = jnp.dot(yp_ref[rows, :], wp_ref[...], preferred_element_type=F32)
        merged = (_sigmoid_t(gm_ref[rows, :].astype(F32)) * y_m
                  + _sigmoid_t(gp_ref[rows, :].astype(F32)) * y_p)
        upd = jnp.dot(merged.astype(BF16), wo_ref[...], preferred_element_type=F32)
        xo = x_ref[rows, :] + gate_ref[0] * upd
        o_ref[rows, :] = xo * lax.rsqrt(jnp.mean(xo * xo, axis=-1, keepdims=True) + EPS) * fw_ref[...]


def _merge(x2, ym, yp, y, mod, wm, wp, wo, final_w, rows_per_mod, g_col0):
    n, d = x2.shape
    tm = 1024
    per = rows_per_mod // tm
    pw = yp.shape[1]
    full = lambda shape: pl.BlockSpec(shape, lambda i: (0,) * len(shape))
    return pl.pallas_call(
        _merge_kernel,
        grid=(n // tm,),
        in_specs=[pl.BlockSpec((tm, d), lambda i: (i, 0)),
                  pl.BlockSpec((tm, d), lambda i: (i, 0)),
                  pl.BlockSpec((tm, pw), lambda i: (i, 0)),
                  pl.BlockSpec((tm, d), lambda i: (i, g_col0 // d)),
                  pl.BlockSpec((tm, d), lambda i: (i, g_col0 // d + 1)),
                  pl.BlockSpec((1, 1, d), lambda i: (i // per, 0, 2)),
                  full((d, d)), full((pw, d)), full((d, d)), full((1, d))],
        out_specs=pl.BlockSpec((tm, d), lambda i: (i, 0)),
        out_shape=jax.ShapeDtypeStruct((n, d), F32),
        scratch_shapes=[pltpu.VMEM((d, d), BF16), pltpu.VMEM((pw, d), BF16), pltpu.VMEM((d, d), BF16)],
        compiler_params=_cparams(("arbitrary",), 56),
        name="merge",
    )(x2, ym, yp, y, y, mod, wm, wp, wo, final_w)


def kernel(x, c, ctx, c_ctx, norm_w, ada_w, ada_b, in_w, gate_b, head_norm_w, pool_w, pool_scale,
           branch_m_w, branch_p_w, out_w, final_norm_w):
    b, t, d = x.shape
    tc = ctx.shape[1]
    depth = norm_w.shape[0]
    assert depth == 1, "context stream update between layers is not implemented"
    mw = M_HEADS * MXU_DIM
    assert d == mw
    g0 = 5 * mw

    mod = _adaln(c, c_ctx, ada_w[0], ada_b[0])

    wt_f32 = jnp.transpose(in_w[0])
    nw = norm_w[0].reshape(1, d)

    g_q, g_k, g_v, g_o, g_z, g_p, g_gm, g_gp = range(8)
    pairs = ((g_q, g_k, "id"), (g_v, g_p, "half_silu"), (g_o, g_z, "gate"), (g_gm, g_gp, "id"))
    col = {name: i * mw for i, name in enumerate(("q", "k", "v", "p", "o", "oz", "g_m", "g_p"))}

    wt, wg, yc, gates_tc = _ctx_proj(ctx.reshape(b * tc, d), mod, b, nw, wt_f32, g0, g_k, g_v)
    y, gates_t = _in_proj(x.reshape(b * t, d), mod, nw, wt, wg, head_norm_w[0].reshape(1, mw).astype(BF16),
                          t, "in_proj", pairs)
    y = y.reshape(b, t, -1)
    yc = yc.reshape(b, tc, -1)

    pc, p, a, decay = _gates(gates_t, gates_tc, gate_b[0], b, CHUNK)
    ym = _mlstm(y, yc, p, a, pc, decay, CHUNK, col)
    yp = _pool(y, pool_w[0].astype(BF16), pool_scale[0].reshape(1, -1), col["p"])

    out = _merge(x.reshape(b * t, d), ym.reshape(b * t, mw), yp.reshape(b * t, -1), y.reshape(b * t, -1),
                 mod, branch_m_w[0], branch_p_w[0], out_w[0], final_norm_w.reshape(1, d), t, col["g_m"])
    return out.reshape(b, t, d)
```

```python
import functools

import numpy as np
import jax
import jax.numpy as jnp
from jax import lax
from jax.experimental import pallas as pl
from jax.experimental.pallas import tpu as pltpu

F32 = jnp.float32
BF16 = jnp.bfloat16

EPS = 1e-6
LOG2E = 1.4426950408889634
M_INIT = -1e30
NEG_BIG = -1e30
M_HEADS = 4
N_DIR = 2
N_GATE_COLS = N_DIR * 2 * M_HEADS
P_GROUPS = 4
POOL_WINDOWS = (2, 4, 8, 16)
GRID_W = 64
LANES = 128
SUBLANES = 8
MXU_DIM = 256
CHUNK = 256
MERGE_SLABS = 2
MIB = 1024 * 1024


def _cparams(sem, vmem_mib):
    return pltpu.CompilerParams(dimension_semantics=sem, vmem_limit_bytes=vmem_mib * MIB)


def _sigmoid(x):
    return 1.0 / (1.0 + jnp.exp(-x))


def _silu(x):
    return x * _sigmoid(x)


def _log_sigmoid(x):
    return jnp.minimum(x, 0.0) - jnp.log(1.0 + jnp.exp(-jnp.abs(x)))


def _dot_nt(a, b):
    return lax.dot_general(a, b, (((1,), (1,)), ((), ())), preferred_element_type=F32)


def _adaln_kernel(c_ref, cc_ref, w_ref, b_ref, o_ref):
    b = c_ref.shape[0]
    rows = jnp.concatenate([c_ref[...], jnp.broadcast_to(cc_ref[...], (o_ref.shape[0] - b, c_ref.shape[1]))], axis=0)
    o_ref[:, 0, :] = jnp.dot(_silu(rows), w_ref[...], preferred_element_type=F32) + b_ref[...]


def _adaln(c, c_ctx, ada_w, ada_b):
    b, d = c.shape
    rows = SUBLANES * (b // SUBLANES + 1)
    n = ada_w.shape[1]
    tn = d
    return pl.pallas_call(
        _adaln_kernel,
        grid=(n // tn,),
        in_specs=[pl.BlockSpec((b, d), lambda j: (0, 0)),
                  pl.BlockSpec((1, d), lambda j: (0, 0)),
                  pl.BlockSpec((d, tn), lambda j: (0, j)),
                  pl.BlockSpec((1, tn), lambda j: (0, j))],
        out_specs=pl.BlockSpec((rows, 1, tn), lambda j: (0, 0, j)),
        out_shape=jax.ShapeDtypeStruct((rows, 1, n), F32),
        compiler_params=_cparams(("arbitrary",), 32),
        name="adaln",
    )(c, c_ctx.reshape(1, d), ada_w, ada_b.reshape(1, n))


def _sigmoid_t(x):
    return 0.5 * jnp.tanh(0.5 * x) + 0.5


IN_TILE = 1024
GATE_SUB = 256


def _inproj_kernel(x_ref, sc_ref, sh_ref, nw_ref, wa_ref, wb_ref, wg_ref, gw_ref, y_ref, g_ref, xn_ref, *, acts):
    j = pl.program_id(1)

    @pl.when(j == 0)
    def _():
        x = x_ref[...]
        ms = jnp.mean(x * x, axis=-1, keepdims=True)
        xn = x * lax.rsqrt(ms + EPS) * (nw_ref[...] * (1.0 + sc_ref[0])) + sh_ref[0]
        xn_ref[...] = xn.astype(BF16)
        wg = wg_ref[...]
        wg = jnp.concatenate([wg, jnp.zeros((LANES - wg.shape[0], wg.shape[1]), wg.dtype)], axis=0)
        g_ref[...] = _dot_nt(xn_ref[...], wg).T[:N_GATE_COLS]

    def pair(act):
        tn = IN_TILE
        xn = xn_ref[...]
        if act == "gate":
            for c in range(0, tn, GATE_SUB):
                a = _dot_nt(xn, wa_ref[c:c + GATE_SUB, :])
                b = _dot_nt(xn, wb_ref[c:c + GATE_SUB, :])
                y_ref[:, c:c + GATE_SUB] = a.astype(BF16)
                ab, bb = a.astype(BF16), b.astype(BF16)
                gate = _sigmoid_t(ab) * (bb * _sigmoid_t(bb))
                y_ref[:, tn + c:tn + c + GATE_SUB] = gate * gw_ref[:, c:c + GATE_SUB]
            return
        y_ref[:, :tn] = _dot_nt(xn, wa_ref[...]).astype(BF16)
        b = _dot_nt(xn, wb_ref[...])
        if act == "half_silu":
            half = tn // 2
            zb = b[:, half:].astype(BF16)
            y_ref[:, tn:tn + half] = b[:, :half].astype(BF16)
            y_ref[:, tn + half:] = zb * _sigmoid_t(zb)
            return
        y_ref[:, tn:] = b.astype(BF16)

    for act in sorted(set(acts)):
        hit = functools.reduce(jnp.logical_or, [j == s for s, a in enumerate(acts) if a == act])
        pl.when(hit)(functools.partial(pair, act))


def _in_proj(x2, mod, norm_w, w, wg, gate_w, rows_per_mod, name, pairs):
    n, d = x2.shape
    tm = min(2048, rows_per_mod)
    tn = IN_TILE
    per = rows_per_mod // tm
    acts = tuple(p[2] for p in pairs)

    def w_group(which):
        return lambda i, j: (sum(jnp.where(j == s, p[which], 0) for s, p in enumerate(pairs)), 0)

    return pl.pallas_call(
        functools.partial(_inproj_kernel, acts=acts),
        grid=(n // tm, len(pairs)),
        in_specs=[pl.BlockSpec((tm, d), lambda i, j: (i, 0)),
                  pl.BlockSpec((1, 1, d), lambda i, j: (i // per, 0, 1)),
                  pl.BlockSpec((1, 1, d), lambda i, j: (i // per, 0, 0)),
                  pl.BlockSpec((1, d), lambda i, j: (0, 0)),
                  pl.BlockSpec((tn, d), w_group(0)),
                  pl.BlockSpec((tn, d), w_group(1)),
                  pl.BlockSpec((N_GATE_COLS, d), lambda i, j: (0, 0)),
                  pl.BlockSpec((1, tn), lambda i, j: (0, 0))],
        out_specs=[pl.BlockSpec((tm, 2 * tn), lambda i, j: (i, j)),
                   pl.BlockSpec((N_GATE_COLS, tm), lambda i, j: (0, i))],
        out_shape=[jax.ShapeDtypeStruct((n, 2 * tn * len(pairs)), BF16),
                   jax.ShapeDtypeStruct((N_GATE_COLS, n), F32)],
        scratch_shapes=[pltpu.VMEM((tm, d), BF16)],
        compiler_params=_cparams(("arbitrary", "arbitrary"), 56),
        name=name,
    )(x2, mod, mod, norm_w, w, w, wg, gate_w)


def _ctx_proj_kernel(x_ref, sc_ref, sh_ref, nw_ref, wsrc_ref, wgsrc_ref, wdst_ref, wgdst_ref, y_ref, g_ref,
                     xn_ref, kv_ref, *, steps, k_block, v_block):
    i, j = pl.program_id(0), pl.program_id(1)
    group = i * steps + j
    w = wsrc_ref[...]
    w = jnp.where(group == k_block, w * MXU_DIM ** -0.5, w).astype(BF16)
    wdst_ref[...] = w

    @pl.when(group == k_block)
    def _():
        kv_ref[0] = w

    @pl.when(group == v_block)
    def _():
        kv_ref[1] = w

    @pl.when(j == 0)
    def _():
        x = x_ref[...]
        ms = jnp.mean(x * x, axis=-1, keepdims=True)
        xn = x * lax.rsqrt(ms + EPS) * (nw_ref[...] * (1.0 + sc_ref[0])) + sh_ref[0]
        xn_ref[...] = xn.astype(BF16)
        wg = wgsrc_ref[...].astype(BF16)
        wgdst_ref[...] = wg
        wg = jnp.concatenate([wg, jnp.zeros((LANES - wg.shape[0], wg.shape[1]), wg.dtype)], axis=0)
        g_ref[...] = _dot_nt(xn_ref[...], wg).T[:N_GATE_COLS]

    @pl.when(j == k_block)
    def _():
        y_ref[:, :IN_TILE] = _dot_nt(xn_ref[...], kv_ref[0]).astype(BF16)

    @pl.when(j == v_block)
    def _():
        y_ref[:, IN_TILE:] = _dot_nt(xn_ref[...], kv_ref[1]).astype(BF16)


def _ctx_proj(xc, mod, mod_row, norm_w, wt, gate_row0, k_block, v_block):
    n, d = xc.shape
    tn = IN_TILE
    n_groups = (wt.shape[0] - N_GATE_COLS) // tn
    row_tiles = 2
    tm, steps = n // row_tiles, n_groups // row_tiles
    assert gate_row0 % tn == 0 and n_groups % row_tiles == 0 and 0 < k_block < v_block < steps

    def src_rows(i, j):
        r = (i * steps + j) * tn
        return (pl.multiple_of(r + jnp.where(r >= gate_row0, N_GATE_COLS, 0), N_GATE_COLS), 0)

    return pl.pallas_call(
        functools.partial(_ctx_proj_kernel, steps=steps, k_block=k_block, v_block=v_block),
        grid=(row_tiles, steps),
        in_specs=[pl.BlockSpec((tm, d), lambda i, j: (i, 0)),
                  pl.BlockSpec((1, 1, d), lambda i, j: (mod_row, 0, 1)),
                  pl.BlockSpec((1, 1, d), lambda i, j: (mod_row, 0, 0)),
                  pl.BlockSpec((1, d), lambda i, j: (0, 0)),
                  pl.BlockSpec((pl.Element(tn), pl.Element(d)), src_rows),
                  pl.BlockSpec((pl.Element(N_GATE_COLS), pl.Element(d)), lambda i, j: (gate_row0, 0))],
        out_specs=[pl.BlockSpec((tn, d), lambda i, j: (i * steps + j, 0)),
                   pl.BlockSpec((N_GATE_COLS, d), lambda i, j: (0, 0)),
                   pl.BlockSpec((tm, 2 * tn), lambda i, j: (i, 0)),
                   pl.BlockSpec((N_GATE_COLS, tm), lambda i, j: (0, i))],
        out_shape=[jax.ShapeDtypeStruct((n_groups * tn, d), BF16),
                   jax.ShapeDtypeStruct((N_GATE_COLS, d), BF16),
                   jax.ShapeDtypeStruct((n, 2 * tn), BF16),
                   jax.ShapeDtypeStruct((N_GATE_COLS, n), F32)],
        scratch_shapes=[pltpu.VMEM((tm, d), BF16), pltpu.VMEM((2, tn, d), BF16)],
        compiler_params=_cparams(("arbitrary", "arbitrary"), 56),
        name="ctx_proj",
    )(xc, mod, mod, norm_w, wt, wt)


N_DH = N_DIR * M_HEADS
TERM_PIECES = (3, 2, 2, 3)


def _split(x, pieces):
    out = []
    for _ in range(pieces):
        p = x.astype(BF16).astype(F32)
        out.append(p)
        x = x - p
    return out


def _chunk_scan(x, pos, chunk, op, reverse):
    t = x.shape[1]
    k = 1
    while k < chunk:
        if reverse:
            x = jnp.where(pos < chunk - k, op(x, pltpu.roll(x, t - k, 1)), x)
        else:
            x = jnp.where(pos >= k, op(x, pltpu.roll(x, k, 1)), x)
        k *= 2
    return x


def _gate_terms(li, pre_f, m0, chunk):
    rows, t = li.shape
    nc = t // chunk
    pos = lax.broadcasted_iota(jnp.int32, (rows, t), 1) % chunk
    rev = lax.broadcasted_iota(jnp.int32, (rows, t), 0) % N_DH >= M_HEADS
    rev1 = lax.broadcasted_iota(jnp.int32, (rows, 1), 0) % N_DH >= M_HEADS
    lf = _log_sigmoid(pre_f)
    b = jnp.where(rev, _chunk_scan(lf, pos, chunk, jnp.add, True), _chunk_scan(lf, pos, chunk, jnp.add, False))
    a = li - b
    cmax = jnp.where(rev, _chunk_scan(a, pos, chunk, jnp.maximum, True),
                     _chunk_scan(a, pos, chunk, jnp.maximum, False))

    def at_scan_end(x, c):
        lo = c * chunk
        return jnp.where(rev1, x[:, lo:lo + 1], x[:, lo + chunk - 1:lo + chunk])

    b_end = [at_scan_end(b, c) for c in range(nc)]
    a_max = [at_scan_end(cmax, c) for c in range(nc)]
    m_f, m_b = m0, m0
    m_in_f, m_in_b = [None] * nc, [None] * nc
    for i in range(nc):
        j = nc - 1 - i
        m_in_f[i] = m_f
        m_f = b_end[i] + jnp.maximum(m_f, a_max[i])
        m_in_b[j] = m_b
        m_b = b_end[j] + jnp.maximum(m_b, a_max[j])
    m_in = [jnp.where(rev1, m_in_b[c], m_in_f[c]) for c in range(nc)]
    per_chunk = lambda vals: jnp.concatenate([jnp.broadcast_to(v, (rows, chunk)) for v in vals], axis=1)
    m_in_t = per_chunk(m_in)
    g = jnp.maximum(m_in_t, cmax)
    g_end = [jnp.maximum(m_in[c], a_max[c]) for c in range(nc)]
    pieces = []
    terms = (-g * LOG2E, jnp.exp(m_in_t - g), jnp.exp(a - per_chunk(g_end)), -(b + g) * LOG2E)
    for term, n in zip(terms, TERM_PIECES):
        pieces += _split(term, n)
    decay = jnp.concatenate([jnp.exp(m_in[c] - g_end[c]) for c in range(nc)], axis=1)
    return pieces, a * LOG2E, decay, jnp.where(rev1, m_b, m_f)


def _gates_kernel(gc_ref, g_ref, b_ref, rc_ref, dc_ref, r_ref, a_ref, d_ref, *, chunk):
    nb = r_ref.shape[0]

    def split(ref):
        t = ref.shape[1] // nb
        li, pre_f = [], []
        for e in range(nb):
            pre = ref[:, e * t:(e + 1) * t] + b_ref[...]
            for d in range(N_DIR):
                li.append(pre[2 * d * M_HEADS:(2 * d + 1) * M_HEADS])
                pre_f.append(pre[(2 * d + 1) * M_HEADS:(2 * d + 2) * M_HEADS])
        return jnp.concatenate(li, axis=0), jnp.concatenate(pre_f, axis=0)

    def store_pieces(ref, pieces):
        pad = jnp.zeros((LANES - N_DH * len(pieces), pieces[0].shape[1]), F32)
        for e in range(nb):
            own = [p[e * N_DH:(e + 1) * N_DH] for p in pieces]
            ref[e] = jnp.concatenate(own + [pad], axis=0).astype(BF16)

    m0 = jnp.full((nb * N_DH, 1), M_INIT, F32)
    pieces_c, _, decay_c, m_ctx = _gate_terms(*split(gc_ref), m0, chunk)
    store_pieces(rc_ref, pieces_c)
    pieces, a, decay, _ = _gate_terms(*split(g_ref), m_ctx, chunk)
    store_pieces(r_ref, pieces)
    for e in range(nb):
        own = slice(e * N_DH, (e + 1) * N_DH)
        dc_ref[e] = decay_c[own]
        d_ref[e] = decay[own]
        for c in range(a_ref.shape[1]):
            a_ref[e, c] = a[own, c * chunk:(c + 1) * chunk]


def _gates(gates_t, gates_tc, gate_b, b, chunk):
    t = gates_t.shape[1] // b
    tc = gates_tc.shape[1] // b
    nc, ncc = t // chunk, tc // chunk
    nb = b
    spec = lambda r, n: pl.BlockSpec((nb, r, n), lambda i: (i, 0, 0))
    seq = lambda n: pl.BlockSpec((N_GATE_COLS, nb * n), lambda i: (0, i))
    rc, dc, r, a, dec = pl.pallas_call(
        functools.partial(_gates_kernel, chunk=chunk),
        grid=(b // nb,),
        in_specs=[seq(tc), seq(t), pl.BlockSpec((N_GATE_COLS, 1), lambda i: (0, 0))],
        out_specs=[spec(LANES, tc), spec(N_DH, ncc), spec(LANES, t),
                   pl.BlockSpec((nb, nc, N_DH, chunk), lambda i: (i, 0, 0, 0)), spec(N_DH, nc)],
        out_shape=[jax.ShapeDtypeStruct((b, LANES, tc), BF16), jax.ShapeDtypeStruct((b, N_DH, ncc), F32),
                   jax.ShapeDtypeStruct((b, LANES, t), BF16), jax.ShapeDtypeStruct((b, nc, N_DH, chunk), F32),
                   jax.ShapeDtypeStruct((b, N_DH, nc), F32)],
        compiler_params=_cparams(("arbitrary",), 32),
        name="gates",
    )(gates_tc, gates_t, gate_b.reshape(N_GATE_COLS, 1))
    decay = jnp.concatenate([dc, dec], axis=2).reshape(b * N_DH, ncc + nc)
    return rc, r, a, decay


def _twice(x):
    return jnp.concatenate([x, x], axis=1)


def _mlstm_kernel(dec_ref, q_ref, k_ref, v_ref, oz_ref, p_ref, a_ref, kc_ref, vc_ref, pc_ref,
                  sel_ref, out_ref, ct_ref, n_ref, hs_ref, *, chunk):
    t = q_ref.shape[1]
    nc = t // chunk
    ncc = kc_ref.shape[1] // chunk
    bi, hi = pl.program_id(0), pl.program_id(1)
    t_idx = lax.broadcasted_iota(jnp.int32, (chunk, chunk), 0)
    s_idx = lax.broadcasted_iota(jnp.int32, (chunk, chunk), 1)

    def bcast_terms(pieces, sel):
        return lax.dot_general(pieces, sel, (((0,), (0,)), ((), ())), preferred_element_type=F32)

    def update(k, v, e_b, decay, d):
        ke = k.astype(F32) * _twice(e_b)
        upd = lax.dot_general(ke.astype(BF16), v, (((0,), (0,)), ((), ())), preferred_element_type=F32)
        n_new = jnp.sum(ke, axis=0, keepdims=True)
        ct_ref[d] = upd if decay is None else decay * ct_ref[d] + upd
        n_ref[d] = n_new if decay is None else decay * n_ref[d] + n_new

    def decay_of(c, d):
        return dec_ref[bi * N_DH + d * M_HEADS + hi, c]

    for d in range(N_DIR):
        for i in range(ncc):
            c = i if d == 0 else ncc - 1 - i
            rows = slice(c * chunk, (c + 1) * chunk)
            e_b = bcast_terms(pc_ref[0, :, rows], sel_ref[0, d, :, 2 * LANES:3 * LANES])
            update(kc_ref[0, rows, :], vc_ref[0, rows, :], e_b, None if i == 0 else decay_of(c, d), d)

    def run_chunk(c, d):
        rows = slice(c * chunk, (c + 1) * chunk)
        q, k, v = q_ref[0, rows, :], k_ref[0, rows, :], v_ref[0, rows, :]
        w = bcast_terms(p_ref[0, :, rows], sel_ref[0, d])
        neg_g_b, w_inter_b, e_b, neg_mt_b = [w[:, i * LANES:(i + 1) * LANES] for i in range(len(TERM_PIECES))]
        a_row = a_ref[0, c, pl.ds(d * M_HEADS + hi, 1), :]
        mask = (s_idx <= t_idx) if d == 0 else (s_idx >= t_idx)
        w_intra = jnp.exp2(jnp.where(mask, a_row + _twice(neg_g_b), NEG_BIG))
        s = _dot_nt(q, k) * w_intra
        num = (_twice(w_inter_b) * jnp.dot(q, ct_ref[d].astype(BF16), preferred_element_type=F32)
               + jnp.dot(s.astype(BF16), v, preferred_element_type=F32))
        den_b = (w_inter_b * jnp.sum(q.astype(F32) * n_ref[d], axis=-1, keepdims=True)
                 + jnp.sum(s, axis=-1, keepdims=True))
        h = num * _twice(1.0 / jnp.maximum(jnp.abs(den_b), jnp.exp2(neg_mt_b)))
        update(k, v, e_b, decay_of(ncc + c, d), d)
        return rows, h

    def finalize(rows, h):
        hn = h * lax.rsqrt(jnp.mean(h * h, axis=-1, keepdims=True) + EPS)
        out_ref[0, rows, :] = hn.astype(BF16) * oz_ref[0, rows, :]

    for i in range(nc):
        for d in range(N_DIR):
            rows, h = run_chunk(i if d == 0 else nc - 1 - i, d)
            if i < nc // 2:
                hs_ref[rows, :] = h
            else:
                finalize(rows, hs_ref[rows, :] + h)


def _term_selectors():
    sel = np.zeros((M_HEADS, N_DIR, LANES, len(TERM_PIECES) * LANES), np.float32)
    first = np.concatenate([[0], np.cumsum(TERM_PIECES)])
    for h in range(M_HEADS):
        for d in range(N_DIR):
            for q in range(len(TERM_PIECES)):
                for j in range(first[q], first[q + 1]):
                    sel[h, d, j * N_DH + d * M_HEADS + h, q * LANES:(q + 1) * LANES] = 1.0
    return jnp.asarray(sel, BF16)


def _mlstm(y, yc, p, a, pc, decay, chunk, col):
    b, t, _ = y.shape
    tc = yc.shape[1]
    dh = MXU_DIM
    sel = _term_selectors()
    assert t % (2 * chunk) == 0 and tc % chunk == 0 and tc >= chunk
    blk = lambda name: pl.BlockSpec((1, t, dh), lambda i, h, c=col[name] // dh: (i, 0, c + h))
    blkc = lambda col0: pl.BlockSpec((1, tc, dh), lambda i, h, c=col0 // dh: (i, 0, c + h))
    return pl.pallas_call(
        functools.partial(_mlstm_kernel, chunk=chunk),
        grid=(b, M_HEADS),
        in_specs=[pl.BlockSpec(memory_space=pltpu.SMEM),
                  blk("q"), blk("k"), blk("v"), blk("oz"),
                  pl.BlockSpec((1, LANES, t), lambda i, h: (i, 0, 0)),
                  pl.BlockSpec((1, t // chunk, N_DH, chunk), lambda i, h: (i, 0, 0, 0)),
                  blkc(0), blkc(M_HEADS * dh),
                  pl.BlockSpec((1, LANES, tc), lambda i, h: (i, 0, 0)),
                  pl.BlockSpec((1, N_DIR) + sel.shape[2:], lambda i, h: (h, 0, 0, 0))],
        out_specs=pl.BlockSpec((1, t, dh), lambda i, h: (i, 0, h)),
        out_shape=jax.ShapeDtypeStruct((b, t, M_HEADS * dh), BF16),
        scratch_shapes=[pltpu.VMEM((N_DIR, dh, dh), F32),
                        pltpu.VMEM((N_DIR, 1, dh), F32),
                        pltpu.VMEM((t, dh), F32)],
        compiler_params=_cparams(("arbitrary", "arbitrary"), 48),
        name="mlstm",
    )(decay, y, y, y, y, p, a, yc, yc, pc, sel)


def _window_bounds(pos, w, length):
    return np.clip(pos - w // 2, 0, length), np.clip(pos + w - w // 2, 0, length)


def _col_pool_matrices(tile):
    pos = np.arange(GRID_W)
    mats = []
    for w in POOL_WINDOWS:
        lo, hi = _window_bounds(pos, w, GRID_W)
        band = ((pos[None, :] >= lo[:, None]) & (pos[None, :] < hi[:, None])).astype(np.float32)
        mats.append(np.kron(np.eye(tile // GRID_W, dtype=np.float32), band))
    return jnp.asarray(np.stack(mats), dtype=BF16)


def _pool_kernel(p_ref, z_ref, mc_ref, pw_ref, ps_ref, out_ref, pad_ref, scale_ref, *, tile, rows):
    t = p_ref.shape[1]
    gd = LANES
    pad = (max(POOL_WINDOWS) // 2) * GRID_W
    n_tiles = t // tile

    @pl.when(pl.program_id(0) == 0)
    def _():
        pad_ref[...] = jnp.zeros(pad_ref.shape, F32)
        tok = lax.broadcasted_iota(jnp.int32, (t, gd), 0)
        r = tok // GRID_W
        c = tok % GRID_W
        for g, w in enumerate(POOL_WINDOWS):
            cnt_r = jnp.minimum(r + (w - w // 2), rows) - jnp.maximum(r - w // 2, 0)
            cnt_c = jnp.minimum(c + (w - w // 2), GRID_W) - jnp.maximum(c - w // 2, 0)
            scale_ref[g] = 1.0 / (cnt_r.astype(F32) * cnt_c.astype(F32))

    def col_sum(g, start, slot):
        cols = pl.ds(g * gd, gd)
        pad_ref[slot, pl.ds(pad + start, tile), :] = jnp.dot(
            mc_ref[g], p_ref[0, pl.ds(start, tile), cols], preferred_element_type=F32)

    def row_sum(g, start, slot):
        w = POOL_WINDOWS[g]
        cols = pl.ds(g * gd, gd)
        buf = pad_ref.at[slot]
        acc = buf[pl.ds(pad + start - (w // 2) * GRID_W, tile), :]
        for j in range(1 - w // 2, w - w // 2):
            acc = acc + buf[pl.ds(pad + start + j * GRID_W, tile), :]
        mixed = acc * scale_ref[g, pl.ds(start, tile), :] - p_ref[0, pl.ds(start, tile), cols].astype(F32)
        mm = jnp.dot(mixed.astype(BF16), pw_ref[g], preferred_element_type=F32)
        zz = z_ref[0, pl.ds(start, tile), cols].astype(F32)
        out_ref[0, pl.ds(start, tile), cols] = (mm * ps_ref[:, cols] * zz).astype(out_ref.dtype)

    for n in range(P_GROUPS + 1):
        for i in range(n_tiles):
            if n < P_GROUPS:
                col_sum(n, i * tile, n % 2)
            if n > 0:
                row_sum(n - 1, i * tile, (n - 1) % 2)


def _pool(y, pool_w, pool_scale, p_col0):
    b, t, _ = y.shape
    pw = P_GROUPS * LANES
    tile = MXU_DIM
    pad = (max(POOL_WINDOWS) // 2) * GRID_W
    return pl.pallas_call(
        functools.partial(_pool_kernel, tile=tile, rows=t // GRID_W),
        grid=(b,),
        in_specs=[pl.BlockSpec((1, t, pw), lambda i: (i, 0, p_col0 // pw)),
                  pl.BlockSpec((1, t, pw), lambda i: (i, 0, p_col0 // pw + 1)),
                  pl.BlockSpec((P_GROUPS, tile, tile), lambda i: (0, 0, 0)),
                  pl.BlockSpec((P_GROUPS, LANES, LANES), lambda i: (0, 0, 0)),
                  pl.BlockSpec((1, pw), lambda i: (0, 0))],
        out_specs=pl.BlockSpec((1, t, pw), lambda i: (i, 0, 0)),
        out_shape=jax.ShapeDtypeStruct((b, t, pw), BF16),
        scratch_shapes=[pltpu.VMEM((2, t + 2 * pad, LANES), F32),
                        pltpu.VMEM((P_GROUPS, t, LANES), F32)],
        compiler_params=_cparams(("arbitrary",), 48),
        name="pool",
    )(y, y, _col_pool_matrices(tile), pool_w, pool_scale)


def _merge_kernel(x_ref, ym_ref, yp_ref, gm_ref, gp_ref, gate_ref, wm32_ref, wp32_ref, wo32_ref, fw_ref, o_ref,
                  wm_ref, wp_ref, wo_ref):
    @pl.when(pl.program_id(0) == 0)
    def _():
        wm_ref[...] = wm32_ref[...].astype(BF16)
        wp_ref[...] = wp32_ref[...].astype(BF16)
        wo_ref[...] = wo32_ref[...].astype(BF16)

    slab = x_ref.shape[0] // MERGE_SLABS
    for r in range(MERGE_SLABS):
        rows = slice(r * slab, (r + 1) * slab)
        y_m = jnp.dot(ym_ref[rows, :], wm_ref[...], preferred_element_type=F32)
        y_p = jnp.dot(yp_ref[rows, :], wp_ref[...], preferred_element_type=F32)
        merged = (_sigmoid_t(gm_ref[rows, :].astype(F32)) * y_m
                  + _sigmoid_t(gp_ref[rows, :].astype(F32)) * y_p)
        upd = jnp.dot(merged.astype(BF16), wo_ref[...], preferred_element_type=F32)
        xo = x_ref[rows, :] + gate_ref[0] * upd
        o_ref[rows, :] = xo * lax.rsqrt(jnp.mean(xo * xo, axis=-1, keepdims=True) + EPS) * fw_ref[...]


def _merge(x2, ym, yp, y, mod, wm, wp, wo, final_w, rows_per_mod, g_col0):
    n, d = x2.shape
    tm = 1024
    per = rows_per_mod // tm
    pw = yp.shape[1]
    full = lambda shape: pl.BlockSpec(shape, lambda i: (0,) * len(shape), pipeline_mode=pl.Buffered(1))
    return pl.pallas_call(
        _merge_kernel,
        grid=(n // tm,),
        in_specs=[pl.BlockSpec((tm, d), lambda i: (i, 0)),
                  pl.BlockSpec((tm, d), lambda i: (i, 0)),
                  pl.BlockSpec((tm, pw), lambda i: (i, 0)),
                  pl.BlockSpec((tm, d), lambda i: (i, g_col0 // d)),
                  pl.BlockSpec((tm, d), lambda i: (i, g_col0 // d + 1)),
                  pl.BlockSpec((1, 1, d), lambda i: (i // per, 0, 2)),
                  full((d, d)), full((pw, d)), full((d, d)), full((1, d))],
        out_specs=pl.BlockSpec((tm, d), lambda i: (i, 0)),
        out_shape=jax.ShapeDtypeStruct((n, d), F32),
        scratch_shapes=[pltpu.VMEM((d, d), BF16), pltpu.VMEM((pw, d), BF16), pltpu.VMEM((d, d), BF16)],
        compiler_params=_cparams(("arbitrary",), 48),
        name="merge",
    )(x2, ym, yp, y, y, mod, wm, wp, wo, final_w)


def kernel(x, c, ctx, c_ctx, norm_w, ada_w, ada_b, in_w, gate_b, head_norm_w, pool_w, pool_scale,
           branch_m_w, branch_p_w, out_w, final_norm_w):
    b, t, d = x.shape
    tc = ctx.shape[1]
    depth = norm_w.shape[0]
    assert depth == 1, "context stream update between layers is not implemented"
    mw = M_HEADS * MXU_DIM
    assert d == mw
    g0 = 5 * mw

    mod = _adaln(c, c_ctx, ada_w[0], ada_b[0])

    wt_f32 = jnp.transpose(in_w[0])
    nw = norm_w[0].reshape(1, d)

    g_q, g_k, g_v, g_o, g_z, g_p, g_gm, g_gp = range(8)
    pairs = ((g_q, g_k, "id"), (g_v, g_p, "half_silu"), (g_o, g_z, "gate"), (g_gm, g_gp, "id"))
    col = {name: i * mw for i, name in enumerate(("q", "k", "v", "p", "o", "oz", "g_m", "g_p"))}

    wt, wg, yc, gates_tc = _ctx_proj(ctx.reshape(b * tc, d), mod, b, nw, wt_f32, g0, g_k, g_v)
    y, gates_t = _in_proj(x.reshape(b * t, d), mod, nw, wt, wg, head_norm_w[0].reshape(1, mw).astype(BF16),
                          t, "in_proj", pairs)
    y = y.reshape(b, t, -1)
    yc = yc.reshape(b, tc, -1)

    pc, p, a, decay = _gates(gates_t, gates_tc, gate_b[0], b, CHUNK)
    ym = _mlstm(y, yc, p, a, pc, decay, CHUNK, col)
    yp = _pool(y, pool_w[0].astype(BF16), pool_scale[0].reshape(1, -1), col["p"])

    out = _merge(x.reshape(b * t, d), ym.reshape(b * t, mw), yp.reshape(b * t, -1), y.reshape(b * t, -1),
                 mod, branch_m_w[0], branch_p_w[0], out_w[0], final_norm_w.reshape(1, d), t, col["g_m"])
    return out.reshape(b, t, d)
```

```python
import functools

import numpy as np
import jax
import jax.numpy as jnp
from jax import lax
from jax.experimental import pallas as pl
from jax.experimental.pallas import tpu as pltpu

F32 = jnp.float32
BF16 = jnp.bfloat16

EPS = 1e-6
LOG2E = 1.4426950408889634
M_INIT = -1e30
NEG_BIG = -1e30
M_HEADS = 4
N_DIR = 2
N_GATE_COLS = N_DIR * 2 * M_HEADS
P_GROUPS = 4
POOL_WINDOWS = (2, 4, 8, 16)
GRID_W = 64
LANES = 128
SUBLANES = 8
MXU_DIM = 256
CHUNK = 256
MERGE_SLABS = 2
MIB = 1024 * 1024


def _cparams(sem, vmem_mib):
    return pltpu.CompilerParams(dimension_semantics=sem, vmem_limit_bytes=vmem_mib * MIB)


def _sigmoid(x):
    return 1.0 / (1.0 + jnp.exp(-x))


def _silu(x):
    return x * _sigmoid(x)


def _log_sigmoid(x):
    return jnp.minimum(x, 0.0) - jnp.log(1.0 + jnp.exp(-jnp.abs(x)))


def _dot_nt(a, b):
    return lax.dot_general(a, b, (((1,), (1,)), ((), ())), preferred_element_type=F32)


def _adaln_kernel(c_ref, cc_ref, w_ref, b_ref, o_ref):
    b = c_ref.shape[0]
    rows = jnp.concatenate([c_ref[...], jnp.broadcast_to(cc_ref[...], (o_ref.shape[0] - b, c_ref.shape[1]))], axis=0)
    o_ref[:, 0, :] = jnp.dot(_silu(rows), w_ref[...], preferred_element_type=F32) + b_ref[...]


def _adaln(c, c_ctx, ada_w, ada_b):
    b, d = c.shape
    rows = SUBLANES * (b // SUBLANES + 1)
    n = ada_w.shape[1]
    tn = d
    return pl.pallas_call(
        _adaln_kernel,
        grid=(n // tn,),
        in_specs=[pl.BlockSpec((b, d), lambda j: (0, 0)),
                  pl.BlockSpec((1, d), lambda j: (0, 0)),
                  pl.BlockSpec((d, tn), lambda j: (0, j)),
                  pl.BlockSpec((1, tn), lambda j: (0, j))],
        out_specs=pl.BlockSpec((rows, 1, tn), lambda j: (0, 0, j)),
        out_shape=jax.ShapeDtypeStruct((rows, 1, n), F32),
        compiler_params=_cparams(("arbitrary",), 32),
        name="adaln",
    )(c, c_ctx.reshape(1, d), ada_w, ada_b.reshape(1, n))


def _sigmoid_t(x):
    return 0.5 * jnp.tanh(0.5 * x) + 0.5


IN_TILE = 1024
GATE_SUB = 256


def _inproj_kernel(x_ref, sc_ref, sh_ref, nw_ref, wa_ref, wb_ref, wg_ref, gw_ref, y_ref, g_ref, xn_ref, *, acts):
    j = pl.program_id(1)

    @pl.when(j == 0)
    def _():
        x = x_ref[...]
        ms = jnp.mean(x * x, axis=-1, keepdims=True)
        xn = x * lax.rsqrt(ms + EPS) * (nw_ref[...] * (1.0 + sc_ref[0])) + sh_ref[0]
        xn_ref[...] = xn.astype(BF16)
        wg = wg_ref[...]
        wg = jnp.concatenate([wg, jnp.zeros((LANES - wg.shape[0], wg.shape[1]), wg.dtype)], axis=0)
        g_ref[...] = _dot_nt(xn_ref[...], wg).T[:N_GATE_COLS]

    def pair(act):
        tn = IN_TILE
        xn = xn_ref[...]
        if act == "gate":
            for c in range(0, tn, GATE_SUB):
                a = _dot_nt(xn, wa_ref[c:c + GATE_SUB, :])
                b = _dot_nt(xn, wb_ref[c:c + GATE_SUB, :])
                y_ref[:, c:c + GATE_SUB] = a.astype(BF16)
                ab, bb = a.astype(BF16), b.astype(BF16)
                gate = _sigmoid_t(ab) * (bb * _sigmoid_t(bb))
                y_ref[:, tn + c:tn + c + GATE_SUB] = gate * gw_ref[:, c:c + GATE_SUB]
            return
        y_ref[:, :tn] = _dot_nt(xn, wa_ref[...]).astype(BF16)
        b = _dot_nt(xn, wb_ref[...])
        if act == "half_silu":
            half = tn // 2
            zb = b[:, half:].astype(BF16)
            y_ref[:, tn:tn + half] = b[:, :half].astype(BF16)
            y_ref[:, tn + half:] = zb * _sigmoid_t(zb)
            return
        y_ref[:, tn:] = b.astype(BF16)

    for act in sorted(set(acts)):
        hit = functools.reduce(jnp.logical_or, [j == s for s, a in enumerate(acts) if a == act])
        pl.when(hit)(functools.partial(pair, act))


def _in_proj(x2, mod, norm_w, w, wg, gate_w, rows_per_mod, name, pairs):
    n, d = x2.shape
    tm = min(2048, rows_per_mod)
    tn = IN_TILE
    per = rows_per_mod // tm
    acts = tuple(p[2] for p in pairs)

    def w_group(which):
        return lambda i, j: (sum(jnp.where(j == s, p[which], 0) for s, p in enumerate(pairs)), 0)

    return pl.pallas_call(
        functools.partial(_inproj_kernel, acts=acts),
        grid=(n // tm, len(pairs)),
        in_specs=[pl.BlockSpec((tm, d), lambda i, j: (i, 0)),
                  pl.BlockSpec((1, 1, d), lambda i, j: (i // per, 0, 1)),
                  pl.BlockSpec((1, 1, d), lambda i, j: (i // per, 0, 0)),
                  pl.BlockSpec((1, d), lambda i, j: (0, 0)),
                  pl.BlockSpec((tn, d), w_group(0)),
                  pl.BlockSpec((tn, d), w_group(1)),
                  pl.BlockSpec((N_GATE_COLS, d), lambda i, j: (0, 0)),
                  pl.BlockSpec((1, tn), lambda i, j: (0, 0))],
        out_specs=[pl.BlockSpec((tm, 2 * tn), lambda i, j: (i, j)),
                   pl.BlockSpec((N_GATE_COLS, tm), lambda i, j: (0, i))],
        out_shape=[jax.ShapeDtypeStruct((n, 2 * tn * len(pairs)), BF16),
                   jax.ShapeDtypeStruct((N_GATE_COLS, n), F32)],
        scratch_shapes=[pltpu.VMEM((tm, d), BF16)],
        compiler_params=_cparams(("arbitrary", "arbitrary"), 56),
        name=name,
    )(x2, mod, mod, norm_w, w, w, wg, gate_w)


def _ctx_proj_kernel(x_ref, sc_ref, sh_ref, nw_ref, wsrc_ref, wgsrc_ref, wdst_ref, wgdst_ref, y_ref, g_ref,
                     xn_ref, kv_ref, *, steps, k_block, v_block):
    i, j = pl.program_id(0), pl.program_id(1)
    group = i * steps + j
    w = wsrc_ref[...]
    w = jnp.where(group == k_block, w * MXU_DIM ** -0.5, w).astype(BF16)
    wdst_ref[...] = w

    @pl.when(group == k_block)
    def _():
        kv_ref[0] = w

    @pl.when(group == v_block)
    def _():
        kv_ref[1] = w

    @pl.when(j == 0)
    def _():
        x = x_ref[...]
        ms = jnp.mean(x * x, axis=-1, keepdims=True)
        xn = x * lax.rsqrt(ms + EPS) * (nw_ref[...] * (1.0 + sc_ref[0])) + sh_ref[0]
        xn_ref[...] = xn.astype(BF16)
        wg = wgsrc_ref[...].astype(BF16)
        wgdst_ref[...] = wg
        wg = jnp.concatenate([wg, jnp.zeros((LANES - wg.shape[0], wg.shape[1]), wg.dtype)], axis=0)
        g_ref[...] = _dot_nt(xn_ref[...], wg).T[:N_GATE_COLS]

    @pl.when(j == k_block)
    def _():
        y_ref[:, :IN_TILE] = _dot_nt(xn_ref[...], kv_ref[0]).astype(BF16)

    @pl.when(j == v_block)
    def _():
        y_ref[:, IN_TILE:] = _dot_nt(xn_ref[...], kv_ref[1]).astype(BF16)


def _ctx_proj(xc, mod, mod_row, norm_w, wt, gate_row0, k_block, v_block):
    n, d = xc.shape
    tn = IN_TILE
    n_groups = (wt.shape[0] - N_GATE_COLS) // tn
    row_tiles = 2
    tm, steps = n // row_tiles, n_groups // row_tiles
    assert gate_row0 % tn == 0 and n_groups % row_tiles == 0 and 0 < k_block < v_block < steps

    def src_rows(i, j):
        r = (i * steps + j) * tn
        return (pl.multiple_of(r + jnp.where(r >= gate_row0, N_GATE_COLS, 0), N_GATE_COLS), 0)

    return pl.pallas_call(
        functools.partial(_ctx_proj_kernel, steps=steps, k_block=k_block, v_block=v_block),
        grid=(row_tiles, steps),
        in_specs=[pl.BlockSpec((tm, d), lambda i, j: (i, 0)),
                  pl.BlockSpec((1, 1, d), lambda i, j: (mod_row, 0, 1)),
                  pl.BlockSpec((1, 1, d), lambda i, j: (mod_row, 0, 0)),
                  pl.BlockSpec((1, d), lambda i, j: (0, 0)),
                  pl.BlockSpec((pl.Element(tn), pl.Element(d)), src_rows),
                  pl.BlockSpec((pl.Element(N_GATE_COLS), pl.Element(d)), lambda i, j: (gate_row0, 0))],
        out_specs=[pl.BlockSpec((tn, d), lambda i, j: (i * steps + j, 0)),
                   pl.BlockSpec((N_GATE_COLS, d), lambda i, j: (0, 0)),
                   pl.BlockSpec((tm, 2 * tn), lambda i, j: (i, 0)),
                   pl.BlockSpec((N_GATE_COLS, tm), lambda i, j: (0, i))],
        out_shape=[jax.ShapeDtypeStruct((n_groups * tn, d), BF16),
                   jax.ShapeDtypeStruct((N_GATE_COLS, d), BF16),
                   jax.ShapeDtypeStruct((n, 2 * tn), BF16),
                   jax.ShapeDtypeStruct((N_GATE_COLS, n), F32)],
        scratch_shapes=[pltpu.VMEM((tm, d), BF16), pltpu.VMEM((2, tn, d), BF16)],
        compiler_params=_cparams(("arbitrary", "arbitrary"), 56),
        name="ctx_proj",
    )(xc, mod, mod, norm_w, wt, wt)


N_DH = N_DIR * M_HEADS
TERM_PIECES = (3, 2, 2, 3)


def _split(x, pieces):
    out = []
    for _ in range(pieces):
        p = x.astype(BF16).astype(F32)
        out.append(p)
        x = x - p
    return out


def _chunk_scan(x, pos, chunk, op, reverse):
    t = x.shape[1]
    k = 1
    while k < chunk:
        if reverse:
            x = jnp.where(pos < chunk - k, op(x, pltpu.roll(x, t - k, 1)), x)
        else:
            x = jnp.where(pos >= k, op(x, pltpu.roll(x, k, 1)), x)
        k *= 2
    return x


def _gate_terms(li, pre_f, m0, chunk):
    rows, t = li.shape
    nc = t // chunk
    pos = lax.broadcasted_iota(jnp.int32, (rows, t), 1) % chunk
    rev = lax.broadcasted_iota(jnp.int32, (rows, t), 0) % N_DH >= M_HEADS
    rev1 = lax.broadcasted_iota(jnp.int32, (rows, 1), 0) % N_DH >= M_HEADS
    lf = _log_sigmoid(pre_f)
    b = jnp.where(rev, _chunk_scan(lf, pos, chunk, jnp.add, True), _chunk_scan(lf, pos, chunk, jnp.add, False))
    a = li - b
    cmax = jnp.where(rev, _chunk_scan(a, pos, chunk, jnp.maximum, True),
                     _chunk_scan(a, pos, chunk, jnp.maximum, False))

    def at_scan_end(x, c):
        lo = c * chunk
        return jnp.where(rev1, x[:, lo:lo + 1], x[:, lo + chunk - 1:lo + chunk])

    b_end = [at_scan_end(b, c) for c in range(nc)]
    a_max = [at_scan_end(cmax, c) for c in range(nc)]
    m_f, m_b = m0, m0
    m_in_f, m_in_b = [None] * nc, [None] * nc
    for i in range(nc):
        j = nc - 1 - i
        m_in_f[i] = m_f
        m_f = b_end[i] + jnp.maximum(m_f, a_max[i])
        m_in_b[j] = m_b
        m_b = b_end[j] + jnp.maximum(m_b, a_max[j])
    m_in = [jnp.where(rev1, m_in_b[c], m_in_f[c]) for c in range(nc)]
    per_chunk = lambda vals: jnp.concatenate([jnp.broadcast_to(v, (rows, chunk)) for v in vals], axis=1)
    m_in_t = per_chunk(m_in)
    g = jnp.maximum(m_in_t, cmax)
    g_end = [jnp.maximum(m_in[c], a_max[c]) for c in range(nc)]
    pieces = []
    terms = (-g * LOG2E, jnp.exp(m_in_t - g), jnp.exp(a - per_chunk(g_end)), -(b + g) * LOG2E)
    for term, n in zip(terms, TERM_PIECES):
        pieces += _split(term, n)
    decay = jnp.concatenate([jnp.exp(m_in[c] - g_end[c]) for c in range(nc)], axis=1)
    return pieces, a * LOG2E, decay, jnp.where(rev1, m_b, m_f)


def _gates_kernel(gc_ref, g_ref, b_ref, rc_ref, dc_ref, r_ref, a_ref, d_ref, *, chunk):
    nb = r_ref.shape[0]

    def split(ref):
        t = ref.shape[1] // nb
        li, pre_f = [], []
        for e in range(nb):
            pre = ref[:, e * t:(e + 1) * t] + b_ref[...]
            for d in range(N_DIR):
                li.append(pre[2 * d * M_HEADS:(2 * d + 1) * M_HEADS])
                pre_f.append(pre[(2 * d + 1) * M_HEADS:(2 * d + 2) * M_HEADS])
        return jnp.concatenate(li, axis=0), jnp.concatenate(pre_f, axis=0)

    def store_pieces(ref, pieces):
        pad = jnp.zeros((LANES - N_DH * len(pieces), pieces[0].shape[1]), F32)
        for e in range(nb):
            own = [p[e * N_DH:(e + 1) * N_DH] for p in pieces]
            ref[e] = jnp.concatenate(own + [pad], axis=0).astype(BF16)

    m0 = jnp.full((nb * N_DH, 1), M_INIT, F32)
    pieces_c, _, decay_c, m_ctx = _gate_terms(*split(gc_ref), m0, chunk)
    store_pieces(rc_ref, pieces_c)
    pieces, a, decay, _ = _gate_terms(*split(g_ref), m_ctx, chunk)
    store_pieces(r_ref, pieces)
    for e in range(nb):
        own = slice(e * N_DH, (e + 1) * N_DH)
        dc_ref[e] = decay_c[own]
        d_ref[e] = decay[own]
        for c in range(a_ref.shape[1]):
            a_ref[e, c] = a[own, c * chunk:(c + 1) * chunk]


def _gates(gates_t, gates_tc, gate_b, b, chunk):
    t = gates_t.shape[1] // b
    tc = gates_tc.shape[1] // b
    nc, ncc = t // chunk, tc // chunk
    nb = b // 2
    spec = lambda r, n: pl.BlockSpec((nb, r, n), lambda i: (i, 0, 0))
    seq = lambda n: pl.BlockSpec((N_GATE_COLS, nb * n), lambda i: (0, i))
    rc, dc, r, a, dec = pl.pallas_call(
        functools.partial(_gates_kernel, chunk=chunk),
        grid=(b // nb,),
        in_specs=[seq(tc), seq(t), pl.BlockSpec((N_GATE_COLS, 1), lambda i: (0, 0))],
        out_specs=[spec(LANES, tc), spec(N_DH, ncc), spec(LANES, t),
                   pl.BlockSpec((nb, nc, N_DH, chunk), lambda i: (i, 0, 0, 0)), spec(N_DH, nc)],
        out_shape=[jax.ShapeDtypeStruct((b, LANES, tc), BF16), jax.ShapeDtypeStruct((b, N_DH, ncc), F32),
                   jax.ShapeDtypeStruct((b, LANES, t), BF16), jax.ShapeDtypeStruct((b, nc, N_DH, chunk), F32),
                   jax.ShapeDtypeStruct((b, N_DH, nc), F32)],
        compiler_params=_cparams(("arbitrary",), 32),
        name="gates",
    )(gates_tc, gates_t, gate_b.reshape(N_GATE_COLS, 1))
    decay = jnp.concatenate([dc, dec], axis=2).reshape(b * N_DH, ncc + nc)
    return rc, r, a, decay


def _twice(x):
    return jnp.concatenate([x, x], axis=1)


def _mlstm_kernel(dec_ref, q_ref, k_ref, v_ref, oz_ref, p_ref, a_ref, kc_ref, vc_ref, pc_ref,
                  sel_ref, out_ref, ct_ref, n_ref, hs_ref, *, chunk):
    t = q_ref.shape[1]
    nc = t // chunk
    ncc = kc_ref.shape[1] // chunk
    bi, hi = pl.program_id(0), pl.program_id(1)
    t_idx = lax.broadcasted_iota(jnp.int32, (chunk, chunk), 0)
    s_idx = lax.broadcasted_iota(jnp.int32, (chunk, chunk), 1)

    def bcast_terms(pieces, sel):
        return lax.dot_general(pieces, sel, (((0,), (0,)), ((), ())), preferred_element_type=F32)

    def update(k, v, e_b, decay, d):
        ke = k.astype(F32) * _twice(e_b)
        upd = lax.dot_general(ke.astype(BF16), v, (((0,), (0,)), ((), ())), preferred_element_type=F32)
        n_new = jnp.sum(ke, axis=0, keepdims=True)
        ct_ref[d] = upd if decay is None else decay * ct_ref[d] + upd
        n_ref[d] = n_new if decay is None else decay * n_ref[d] + n_new

    def decay_of(c, d):
        return dec_ref[bi * N_DH + d * M_HEADS + hi, c]

    for d in range(N_DIR):
        for i in range(ncc):
            c = i if d == 0 else ncc - 1 - i
            rows = slice(c * chunk, (c + 1) * chunk)
            e_b = bcast_terms(pc_ref[0, :, rows], sel_ref[0, d, :, 2 * LANES:3 * LANES])
            update(kc_ref[0, rows, :], vc_ref[0, rows, :], e_b, None if i == 0 else decay_of(c, d), d)

    def run_chunk(c, d):
        rows = slice(c * chunk, (c + 1) * chunk)
        q, k, v = q_ref[0, rows, :], k_ref[0, rows, :], v_ref[0, rows, :]
        w = bcast_terms(p_ref[0, :, rows], sel_ref[0, d])
        neg_g_b, w_inter_b, e_b, neg_mt_b = [w[:, i * LANES:(i + 1) * LANES] for i in range(len(TERM_PIECES))]
        a_row = a_ref[0, c, pl.ds(d * M_HEADS + hi, 1), :]
        mask = (s_idx <= t_idx) if d == 0 else (s_idx >= t_idx)
        w_intra = jnp.exp2(jnp.where(mask, a_row + _twice(neg_g_b), NEG_BIG))
        s = _dot_nt(q, k) * w_intra
        num = (_twice(w_inter_b) * jnp.dot(q, ct_ref[d].astype(BF16), preferred_element_type=F32)
               + jnp.dot(s.astype(BF16), v, preferred_element_type=F32))
        den_b = (w_inter_b * jnp.sum(q.astype(F32) * n_ref[d], axis=-1, keepdims=True)
                 + jnp.sum(s, axis=-1, keepdims=True))
        h = num * _twice(1.0 / jnp.maximum(jnp.abs(den_b), jnp.exp2(neg_mt_b)))
        update(k, v, e_b, decay_of(ncc + c, d), d)
        return rows, h

    def finalize(rows, h):
        hn = h * lax.rsqrt(jnp.mean(h * h, axis=-1, keepdims=True) + EPS)
        out_ref[0, rows, :] = hn.astype(BF16) * oz_ref[0, rows, :]

    for i in range(nc):
        for d in range(N_DIR):
            rows, h = run_chunk(i if d == 0 else nc - 1 - i, d)
            if i < nc // 2:
                hs_ref[rows, :] = h
            else:
                finalize(rows, hs_ref[rows, :] + h)


def _term_selectors():
    sel = np.zeros((M_HEADS, N_DIR, LANES, len(TERM_PIECES) * LANES), np.float32)
    first = np.concatenate([[0], np.cumsum(TERM_PIECES)])
    for h in range(M_HEADS):
        for d in range(N_DIR):
            for q in range(len(TERM_PIECES)):
                for j in range(first[q], first[q + 1]):
                    sel[h, d, j * N_DH + d * M_HEADS + h, q * LANES:(q + 1) * LANES] = 1.0
    return jnp.asarray(sel, BF16)


def _mlstm(y, yc, p, a, pc, decay, chunk, col):
    b, t, _ = y.shape
    tc = yc.shape[1]
    dh = MXU_DIM
    sel = _term_selectors()
    assert t % (2 * chunk) == 0 and tc % chunk == 0 and tc >= chunk
    blk = lambda name: pl.BlockSpec((1, t, dh), lambda i, h, c=col[name] // dh: (i, 0, c + h))
    blkc = lambda col0: pl.BlockSpec((1, tc, dh), lambda i, h, c=col0 // dh: (i, 0, c + h))
    return pl.pallas_call(
        functools.partial(_mlstm_kernel, chunk=chunk),
        grid=(b, M_HEADS),
        in_specs=[pl.BlockSpec(memory_space=pltpu.SMEM),
                  blk("q"), blk("k"), blk("v"), blk("oz"),
                  pl.BlockSpec((1, LANES, t), lambda i, h: (i, 0, 0)),
                  pl.BlockSpec((1, t // chunk, N_DH, chunk), lambda i, h: (i, 0, 0, 0)),
                  blkc(0), blkc(M_HEADS * dh),
                  pl.BlockSpec((1, LANES, tc), lambda i, h: (i, 0, 0)),
                  pl.BlockSpec((1, N_DIR) + sel.shape[2:], lambda i, h: (h, 0, 0, 0))],
        out_specs=pl.BlockSpec((1, t, dh), lambda i, h: (i, 0, h)),
        out_shape=jax.ShapeDtypeStruct((b, t, M_HEADS * dh), BF16),
        scratch_shapes=[pltpu.VMEM((N_DIR, dh, dh), F32),
                        pltpu.VMEM((N_DIR, 1, dh), F32),
                        pltpu.VMEM((t, dh), F32)],
        compiler_params=_cparams(("arbitrary", "arbitrary"), 48),
        name="mlstm",
    )(decay, y, y, y, y, p, a, yc, yc, pc, sel)


def _window_bounds(pos, w, length):
    return np.clip(pos - w // 2, 0, length), np.clip(pos + w - w // 2, 0, length)


def _col_pool_matrices(tile):
    pos = np.arange(GRID_W)
    mats = []
    for w in POOL_WINDOWS:
        lo, hi = _window_bounds(pos, w, GRID_W)
        band = ((pos[None, :] >= lo[:, None]) & (pos[None, :] < hi[:, None])).astype(np.float32)
        mats.append(np.kron(np.eye(tile // GRID_W, dtype=np.float32), band))
    return jnp.asarray(np.stack(mats), dtype=BF16)


def _pool_kernel(p_ref, z_ref, mc_ref, pw_ref, ps_ref, out_ref, pad_ref, scale_ref, *, tile, rows):
    t = p_ref.shape[1]
    gd = LANES
    pad = (max(POOL_WINDOWS) // 2) * GRID_W
    n_tiles = t // tile

    @pl.when(pl.program_id(0) == 0)
    def _():
        pad_ref[...] = jnp.zeros(pad_ref.shape, F32)
        tok = lax.broadcasted_iota(jnp.int32, (t, gd), 0)
        r = tok // GRID_W
        c = tok % GRID_W
        for g, w in enumerate(POOL_WINDOWS):
            cnt_r = jnp.minimum(r + (w - w // 2), rows) - jnp.maximum(r - w // 2, 0)
            cnt_c = jnp.minimum(c + (w - w // 2), GRID_W) - jnp.maximum(c - w // 2, 0)
            scale_ref[g] = 1.0 / (cnt_r.astype(F32) * cnt_c.astype(F32))

    def col_sum(g, start, slot):
        cols = pl.ds(g * gd, gd)
        pad_ref[slot, pl.ds(pad + start, tile), :] = jnp.dot(
            mc_ref[g], p_ref[0, pl.ds(start, tile), cols], preferred_element_type=F32)

    def row_sum(g, start, slot):
        w = POOL_WINDOWS[g]
        cols = pl.ds(g * gd, gd)
        buf = pad_ref.at[slot]
        acc = buf[pl.ds(pad + start - (w // 2) * GRID_W, tile), :]
        for j in range(1 - w // 2, w - w // 2):
            acc = acc + buf[pl.ds(pad + start + j * GRID_W, tile), :]
        mixed = acc * scale_ref[g, pl.ds(start, tile), :] - p_ref[0, pl.ds(start, tile), cols].astype(F32)
        mm = jnp.dot(mixed.astype(BF16), pw_ref[g], preferred_element_type=F32)
        zz = z_ref[0, pl.ds(start, tile), cols].astype(F32)
        out_ref[0, pl.ds(start, tile), cols] = (mm * ps_ref[:, cols] * zz).astype(out_ref.dtype)

    for n in range(P_GROUPS + 1):
        for i in range(n_tiles):
            if n < P_GROUPS:
                col_sum(n, i * tile, n % 2)
            if n > 0:
                row_sum(n - 1, i * tile, (n - 1) % 2)


def _pool(y, pool_w, pool_scale, p_col0):
    b, t, _ = y.shape
    pw = P_GROUPS * LANES
    tile = MXU_DIM
    pad = (max(POOL_WINDOWS) // 2) * GRID_W
    return pl.pallas_call(
        functools.partial(_pool_kernel, tile=tile, rows=t // GRID_W),
        grid=(b,),
        in_specs=[pl.BlockSpec((1, t, pw), lambda i: (i, 0, p_col0 // pw)),
                  pl.BlockSpec((1, t, pw), lambda i: (i, 0, p_col0 // pw + 1)),
                  pl.BlockSpec((P_GROUPS, tile, tile), lambda i: (0, 0, 0)),
                  pl.BlockSpec((P_GROUPS, LANES, LANES), lambda i: (0, 0, 0)),
                  pl.BlockSpec((1, pw), lambda i: (0, 0))],
        out_specs=pl.BlockSpec((1, t, pw), lambda i: (i, 0, 0)),
        out_shape=jax.ShapeDtypeStruct((b, t, pw), BF16),
        scratch_shapes=[pltpu.VMEM((2, t + 2 * pad, LANES), F32),
                        pltpu.VMEM((P_GROUPS, t, LANES), F32)],
        compiler_params=_cparams(("arbitrary",), 48),
        name="pool",
    )(y, y, _col_pool_matrices(tile), pool_w, pool_scale)


def _merge_kernel(x_ref, ym_ref, yp_ref, gm_ref, gp_ref, gate_ref, wm_ref, wp_ref, wo_ref, fw_ref, o_ref):
    slab = x_ref.shape[0] // MERGE_SLABS
    for r in range(MERGE_SLABS):
        rows = slice(r * slab, (r + 1) * slab)
        y_m = jnp.dot(ym_ref[rows, :], wm_ref[...], preferred_element_type=F32)
        y_p = jnp.dot(yp_ref[rows, :], wp_ref[...], preferred_element_type=F32)
        merged = (_sigmoid_t(gm_ref[rows, :].astype(F32)) * y_m
                  + _sigmoid_t(gp_ref[rows, :].astype(F32)) * y_p)
        upd = jnp.dot(merged.astype(BF16), wo_ref[...], preferred_element_type=F32)
        xo = x_ref[rows, :] + gate_ref[0] * upd
        o_ref[rows, :] = xo * lax.rsqrt(jnp.mean(xo * xo, axis=-1, keepdims=True) + EPS) * fw_ref[...]


def _merge(x2, ym, yp, y, mod, wm, wp, wo, final_w, rows_per_mod, g_col0):
    n, d = x2.shape
    tm = 1024
    per = rows_per_mod // tm
    pw = yp.shape[1]
    full = lambda shape: pl.BlockSpec(shape, lambda i: (0,) * len(shape))
    return pl.pallas_call(
        _merge_kernel,
        grid=(n // tm,),
        in_specs=[pl.BlockSpec((tm, d), lambda i: (i, 0)),
                  pl.BlockSpec((tm, d), lambda i: (i, 0)),
                  pl.BlockSpec((tm, pw), lambda i: (i, 0)),
                  pl.BlockSpec((tm, d), lambda i: (i, g_col0 // d)),
                  pl.BlockSpec((tm, d), lambda i: (i, g_col0 // d + 1)),
                  pl.BlockSpec((1, 1, d), lambda i: (i // per, 0, 2)),
                  full((d, d)), full((pw, d)), full((d, d)), full((1, d))],
        out_specs=pl.BlockSpec((tm, d), lambda i: (i, 0)),
        out_shape=jax.ShapeDtypeStruct((n, d), F32),
        compiler_params=_cparams(("arbitrary",), 48),
        name="merge",
    )(x2, ym, yp, y, y, mod, wm, wp, wo, final_w)


def kernel(x, c, ctx, c_ctx, norm_w, ada_w, ada_b, in_w, gate_b, head_norm_w, pool_w, pool_scale,
           branch_m_w, branch_p_w, out_w, final_norm_w):
    b, t, d = x.shape
    tc = ctx.shape[1]
    depth = norm_w.shape[0]
    assert depth == 1, "context stream update between layers is not implemented"
    mw = M_HEADS * MXU_DIM
    assert d == mw
    g0 = 5 * mw

    mod = _adaln(c, c_ctx, ada_w[0], ada_b[0])

    wt_f32 = jnp.transpose(in_w[0])
    nw = norm_w[0].reshape(1, d)

    g_q, g_k, g_v, g_o, g_z, g_p, g_gm, g_gp = range(8)
    pairs = ((g_q, g_k, "id"), (g_v, g_p, "half_silu"), (g_o, g_z, "gate"), (g_gm, g_gp, "id"))
    col = {name: i * mw for i, name in enumerate(("q", "k", "v", "p", "o", "oz", "g_m", "g_p"))}

    wt, wg, yc, gates_tc = _ctx_proj(ctx.reshape(b * tc, d), mod, b, nw, wt_f32, g0, g_k, g_v)
    y, gates_t = _in_proj(x.reshape(b * t, d), mod, nw, wt, wg, head_norm_w[0].reshape(1, mw).astype(BF16),
                          t, "in_proj", pairs)
    y = y.reshape(b, t, -1)
    yc = yc.reshape(b, tc, -1)

    pc, p, a, decay = _gates(gates_t, gates_tc, gate_b[0], b, CHUNK)
    ym = _mlstm(y, yc, p, a, pc, decay, CHUNK, col)
    yp = _pool(y, pool_w[0].astype(BF16), pool_scale[0].reshape(1, -1), col["p"])

    out = _merge(x.reshape(b * t, d), ym.reshape(b * t, mw), yp.reshape(b * t, -1), y.reshape(b * t, -1),
                 mod, branch_m_w[0].astype(BF16), branch_p_w[0].astype(BF16),
                 out_w[0].astype(BF16), final_norm_w.reshape(1, d), t, col["g_m"])
    return out.reshape(b, t, d)
```
